```python
import math
import jax
import jax.numpy as jnp
from jax import lax
import numpy as np

D_MODEL = 1024
BATCH = 8
SEQ = 2048
DEPTH = 2

DEEPNORM_ALPHA = (2 * DEPTH) ** 0.25
DEEPNORM_BETA = (8 * DEPTH) ** -0.25
LN_EPS = 1e-5
CONV_K = 4

GLA_HEADS = 4
GLA_DK = 64
GLA_DV = 128
GLA_QK = GLA_HEADS * GLA_DK
GLA_VW = GLA_HEADS * GLA_DV
GLA_GK_RANK = 16
GLA_GATE_NORM = 16.0
GLA_CHUNK = 64
GLA_NORM_EPS = 1e-5

LRU_WIDTH = 512
LRU_BLOCKS = 8
LRU_BLOCK = LRU_WIDTH // LRU_BLOCKS
LRU_C = 8.0

EVEN_SPLITS = (GLA_QK, GLA_QK, GLA_VW, GLA_VW, GLA_GK_RANK, LRU_WIDTH, LRU_WIDTH)
EVEN_IN = sum(EVEN_SPLITS)
EVEN_MIX = GLA_VW + LRU_WIDTH

SSD_HEADS = 16
SSD_HEADDIM = 64
SSD_INNER = SSD_HEADS * SSD_HEADDIM
SSD_GROUPS = 2
SSD_HEADS_PER_GROUP = SSD_HEADS // SSD_GROUPS
SSD_STATE = 128
SSD_CHUNK = 64
SSD_XBC = SSD_INNER + 2 * SSD_GROUPS * SSD_STATE
SSD_NORM_EPS = 1e-5

RWKV_HEADS = 8
RWKV_HEADDIM = 64
RWKV_WIDTH = RWKV_HEADS * RWKV_HEADDIM
RWKV_DECAY_LORA = 64
RWKV_AAA_LORA = 64
RWKV_GATE_LORA = 128
RWKV_GN_EPS = 64e-5
RWKV_SPLITS = (RWKV_WIDTH, RWKV_WIDTH, RWKV_WIDTH, RWKV_DECAY_LORA, RWKV_AAA_LORA, RWKV_GATE_LORA)
RWKV_IN = sum(RWKV_SPLITS)

ODD_SPLITS = (SSD_INNER, SSD_XBC, SSD_HEADS, RWKV_IN)
ODD_IN = sum(ODD_SPLITS)
ODD_MIX = SSD_INNER + RWKV_WIDTH

FFN_DENSE = 2816
N_EXPERTS = 8
TOP_K = 2
FFN_EXPERT = 3584

kernel_name = 'hybrid_gla_rglru_ssd_rwkv7_moe'


def split_cols(p, sizes):
    return jnp.split(p, np.cumsum(sizes)[:-1].tolist(), axis=-1)


def layer_norm(x, g, b):
    xf = x.astype(jnp.float32)
    mu = jnp.mean(xf, axis=-1, keepdims=True)
    var = jnp.mean(jnp.square(xf - mu), axis=-1, keepdims=True)
    return ((xf - mu) * lax.rsqrt(var + LN_EPS) * g + b).astype(x.dtype)


def rms_norm(x, w, eps):
    xf = x.astype(jnp.float32)
    return xf * lax.rsqrt(jnp.mean(jnp.square(xf), axis=-1, keepdims=True) + eps) * w


def causal_dwconv(x, w, b):
    y = lax.conv_general_dilated(
        x.astype(jnp.float32), w.astype(jnp.float32)[:, None, :],
        window_strides=(1,), padding=[(w.shape[0] - 1, 0)],
        dimension_numbers=('NWC', 'WIO', 'NWC'), feature_group_count=x.shape[-1])
    return y + b


def token_shift(p):
    return jnp.pad(p[:, :-1], ((0, 0), (1, 0), (0, 0)))


def scan_chunks(decay, local):
    def step(s, du):
        d, u = du
        return d * s + u, s
    _, prev = lax.scan(step, jnp.zeros_like(local[0]), (decay, local))
    return prev


def gla_mixer(q, k, v, gk, g, norm_w):
    bsz, t, _ = q.shape
    n = t // GLA_CHUNK

    def heads(z, d):
        return z.astype(jnp.float32).reshape(bsz, n, GLA_CHUNK, GLA_HEADS, d).transpose(0, 3, 1, 2, 4)

    q = heads(q, GLA_DK) * GLA_DK ** -0.5
    k = heads(k, GLA_DK)
    v = heads(v, GLA_DV)
    b = jnp.cumsum(heads(gk, GLA_DK), axis=3)
    b_last = b[:, :, :, -1:, :]
    qd = q * jnp.exp(b)
    att = jnp.einsum('bhncd,bhnmd->bhncm', qd, k * jnp.exp(-b))
    causal = jnp.tril(jnp.ones((GLA_CHUNK, GLA_CHUNK), dtype=bool))
    att = jnp.where(causal, att, 0.0)
    o = jnp.einsum('bhncm,bhnmv->bhncv', att, v)
    local = jnp.einsum('bhncd,bhncv->bhndv', k * jnp.exp(b_last - b), v)
    decay = jnp.exp(b_last[:, :, :, 0, :])[..., None]
    prev = jnp.moveaxis(scan_chunks(jnp.moveaxis(decay, 2, 0), jnp.moveaxis(local, 2, 0)), 0, 2)
    o = o + jnp.einsum('bhncd,bhndv->bhncv', qd, prev)
    o = o.transpose(0, 2, 3, 1, 4).reshape(bsz, t, GLA_HEADS, GLA_DV)
    o = rms_norm(o, norm_w, GLA_NORM_EPS)
    g = g.astype(jnp.float32).reshape(bsz, t, GLA_HEADS, GLA_DV)
    return (o * jax.nn.silu(g)).reshape(bsz, t, GLA_VW)


def rglru_mixer(xb, gate, conv_w, conv_b, wa, ba, wx, bx, lam):
    bsz, t, _ = xb.shape
    xc = causal_dwconv(xb, conv_w, conv_b)
    xblk = xc.reshape(bsz, t, LRU_BLOCKS, LRU_BLOCK)
    r = jax.nn.sigmoid(jnp.einsum('btni,nij->btnj', xblk, wa).reshape(bsz, t, LRU_WIDTH) + ba)
    i = jax.nn.sigmoid(jnp.einsum('btni,nij->btnj', xblk, wx).reshape(bsz, t, LRU_WIDTH) + bx)
    log_a = -LRU_C * r * jax.nn.softplus(-lam)
    a = jnp.exp(log_a)
    u = jnp.sqrt(-jnp.expm1(2.0 * log_a)) * (i * xc)

    def combine(lhs, rhs):
        a1, b1 = lhs
        a2, b2 = rhs
        return a1 * a2, a2 * b1 + b2

    _, h = lax.associative_scan(combine, (a, u), axis=1)
    return h * jax.nn.gelu(gate.astype(jnp.float32))


def ssd_mixer(z, xbc, dt, conv_w, conv_b, dt_bias, a_log, d_skip, norm_w):
    bsz, t, _ = z.shape
    n, l, g, hg, p, s = t // SSD_CHUNK, SSD_CHUNK, SSD_GROUPS, SSD_HEADS_PER_GROUP, SSD_HEADDIM, SSD_STATE
    xbc = jax.nn.silu(causal_dwconv(xbc, conv_w, conv_b))
    xs, bm, cm = split_cols(xbc, (SSD_INNER, g * s, g * s))
    x = xs.reshape(bsz, n, l, g, hg, p)
    bm = bm.reshape(bsz, n, l, g, s)
    cm = cm.reshape(bsz, n, l, g, s)
    dtc = jax.nn.softplus(dt.astype(jnp.float32) + dt_bias).reshape(bsz, n, l, g, hg)
    a = -jnp.exp(a_log.astype(jnp.float32)).reshape(g, hg)
    xdt = x * dtc[..., None]
    acs = jnp.cumsum(dtc * a, axis=2)
    seg = acs[:, :, :, None] - acs[:, :, None, :]
    causal = jnp.tril(jnp.ones((l, l), dtype=bool))[:, :, None, None]
    lmat = jnp.exp(jnp.where(causal, seg, -jnp.inf))
    cb = jnp.einsum('bnlgs,bnmgs->bnlmg', cm, bm)
    y = jnp.einsum('bnlmg,bnlmgh,bnmghp->bnlghp', cb, lmat, xdt)
    decay_states = jnp.exp(acs[:, :, -1:] - acs)
    local = jnp.einsum('bnlgs,bnlgh,bnlghp->bnghps', bm, decay_states, xdt)
    chunk_decay = jnp.exp(acs[:, :, -1])[..., None, None]
    prev = jnp.moveaxis(scan_chunks(jnp.moveaxis(chunk_decay, 1, 0), jnp.moveaxis(local, 1, 0)), 0, 1)
    y = y + jnp.einsum('bnlgs,bnghps,bnlgh->bnlghp', cm, prev, jnp.exp(acs))
    y = y + x * d_skip.reshape(g, hg)[..., None]
    y = y.reshape(bsz, t, SSD_INNER) * jax.nn.silu(z.astype(jnp.float32))
    y = rms_norm(y.reshape(bsz, t, g, SSD_INNER // g), norm_w.reshape(g, SSD_INNER // g), SSD_NORM_EPS)
    return y.reshape(bsz, t, SSD_INNER)


def wkv7_scan(r, w, k, v, a, b):
    def step(st, inp):
        r_t, w_t, k_t, v_t, a_t, b_t = inp
        sa = jnp.einsum('bhvk,bhk->bhv', st, a_t)
        st = st * w_t[:, :, None, :] + sa[..., None] * b_t[:, :, None, :] + v_t[..., None] * k_t[:, :, None, :]
        return st, jnp.einsum('bhvk,bhk->bhv', st, r_t)
    bsz, _, h, nd = r.shape
    s0 = jnp.zeros((bsz, h, nd, nd), jnp.float32)
    _, y = lax.scan(step, s0, tuple(jnp.moveaxis(z, 1, 0) for z in (r, w, k, v, a, b)))
    return jnp.moveaxis(y, 0, 1)


def rwkv7_mixer(p, mu, w0, w2, a0, a2, g2, k_k, k_a, r_k, ln_g, ln_b):
    bsz, t, _ = p.shape
    p = p.astype(jnp.float32)
    p = p + (token_shift(p) - p) * mu
    r, k, v, xw, xa, xg = split_cols(p, RWKV_SPLITS)
    w_log = -jax.nn.softplus(-(w0 + jnp.tanh(xw) @ w2)) - 0.5
    decay = jnp.exp(-jnp.exp(w_log))
    a = jax.nn.sigmoid(a0 + xa @ a2)
    g = jax.nn.sigmoid(xg) @ g2

    def heads(z):
        return z.reshape(bsz, t, RWKV_HEADS, RWKV_HEADDIM)

    kk = heads(k * k_k)
    kk = kk / jnp.maximum(jnp.sqrt(jnp.sum(jnp.square(kk), axis=-1, keepdims=True)), 1e-12)
    k = k * (1.0 + (a - 1.0) * k_a)
    r, k, v, decay, a = heads(r), heads(k), heads(v), heads(decay), heads(a)
    y = wkv7_scan(r, decay, k, v, -kk, kk * a)
    mu_y = jnp.mean(y, axis=-1, keepdims=True)
    var_y = jnp.mean(jnp.square(y - mu_y), axis=-1, keepdims=True)
    y = ((y - mu_y) * lax.rsqrt(var_y + RWKV_GN_EPS)).reshape(bsz, t, RWKV_WIDTH) * ln_g + ln_b
    bonus = (jnp.sum(r * k * r_k, axis=-1, keepdims=True) * v).reshape(bsz, t, RWKV_WIDTH)
    return (y + bonus) * g


def swiglu(x, w1, w3, w2):
    return (jax.nn.silu(x @ w1) * (x @ w3)) @ w2


def moe_swiglu(x, router, w1, w3, w2):
    bsz, t, d = x.shape
    xt = x.reshape(bsz * t, d)
    logits = (xt @ router).astype(jnp.float32)
    top_v, top_i = lax.top_k(logits, TOP_K)
    top_w = jax.nn.softmax(top_v, axis=-1)
    gates = jnp.sum(jax.nn.one_hot(top_i, N_EXPERTS, dtype=jnp.float32) * top_w[..., None], axis=1)
    out = jnp.zeros((bsz * t, d), jnp.float32)
    for e in range(N_EXPERTS):
        out = out + gates[:, e:e + 1] * swiglu(xt, w1[e], w3[e], w2[e])
    return out.reshape(bsz, t, d)


def even_layer(x, w_in, gk_w2, gk_b, gla_norm, conv_w, conv_b, wa, ba, wx, bx, lam,
               w_out, ln1_g, ln1_b, f_w1, f_w3, f_w2, ln2_g, ln2_b):
    p = x @ w_in
    q, k, v, g, gk_lr, xb, gate = split_cols(p, EVEN_SPLITS)
    gk = jax.nn.log_sigmoid((gk_lr @ gk_w2 + gk_b).astype(jnp.float32)) / GLA_GATE_NORM
    y_a = gla_mixer(q, k, v, gk, g, gla_norm)
    y_b = rglru_mixer(xb, gate, conv_w, conv_b, wa, ba, wx, bx, lam)
    mix = jnp.concatenate([y_a, y_b], axis=-1) @ w_out
    x = layer_norm(DEEPNORM_ALPHA * x + mix, ln1_g, ln1_b)
    x = layer_norm(DEEPNORM_ALPHA * x + swiglu(x, f_w1, f_w3, f_w2), ln2_g, ln2_b)
    return x


def odd_layer(x, w_in, conv_w, conv_b, dt_bias, a_log, d_skip, ssd_norm,
              mu, w0, w2, a0, a2, g2, k_k, k_a, r_k, rln_g, rln_b,
              w_out, ln1_g, ln1_b, router, ew1, ew3, ew2, ln2_g, ln2_b):
    p = x @ w_in
    z, xbc, dt, prw = split_cols(p, ODD_SPLITS)
    y_c = ssd_mixer(z, xbc, dt, conv_w, conv_b, dt_bias, a_log, d_skip, ssd_norm)
    y_d = rwkv7_mixer(prw, mu, w0, w2, a0, a2, g2, k_k, k_a, r_k, rln_g, rln_b)
    mix = jnp.concatenate([y_c, y_d], axis=-1) @ w_out
    x = layer_norm(DEEPNORM_ALPHA * x + mix, ln1_g, ln1_b)
    x = layer_norm(DEEPNORM_ALPHA * x + moe_swiglu(x, router, ew1, ew3, ew2), ln2_g, ln2_b)
    return x


def setup_inputs(seed: int = 0) -> dict:
    key = jax.random.key(seed)
    keys = iter(jax.random.split(key, 64))
    ne = (DEPTH + 1) // 2
    no = DEPTH // 2
    f32 = jnp.float32

    def nrm(shape, scale):
        return jax.random.normal(next(keys), shape, f32) * scale

    def unif(shape, lo, hi):
        return jax.random.uniform(next(keys), shape, f32, lo, hi)

    def gain(shape):
        return 1.0 + nrm(shape, 0.02)

    lru_a = unif((ne, LRU_WIDTH), 0.9, 0.999) ** (1.0 / LRU_C)
    dt0 = jnp.exp(unif((no, SSD_HEADS), math.log(1e-3), math.log(1e-1)))
    return {
        'x': nrm((BATCH, SEQ, D_MODEL), 1.0),
        'e_w_in': nrm((ne, D_MODEL, EVEN_IN), D_MODEL ** -0.5),
        'e_gk_w2': nrm((ne, GLA_GK_RANK, GLA_QK), GLA_GK_RANK ** -0.5),
        'e_gk_b': nrm((ne, GLA_QK), 0.1),
        'e_gla_norm': gain((ne, GLA_DV)),
        'e_conv_w': nrm((ne, CONV_K, LRU_WIDTH), CONV_K ** -0.5),
        'e_conv_b': nrm((ne, LRU_WIDTH), 0.1),
        'e_lru_wa': nrm((ne, LRU_BLOCKS, LRU_BLOCK, LRU_BLOCK), LRU_BLOCK ** -0.5),
        'e_lru_ba': nrm((ne, LRU_WIDTH), 0.1),
        'e_lru_wx': nrm((ne, LRU_BLOCKS, LRU_BLOCK, LRU_BLOCK), LRU_BLOCK ** -0.5),
        'e_lru_bx': nrm((ne, LRU_WIDTH), 0.1),
        'e_lru_lambda': jnp.log(lru_a) - jnp.log1p(-lru_a),
        'e_w_out': nrm((ne, EVEN_MIX, D_MODEL), EVEN_MIX ** -0.5 * DEEPNORM_BETA),
        'e_ln1_g': gain((ne, D_MODEL)),
        'e_ln1_b': nrm((ne, D_MODEL), 0.02),
        'e_ffn_w1': nrm((ne, D_MODEL, FFN_DENSE), D_MODEL ** -0.5),
        'e_ffn_w3': nrm((ne, D_MODEL, FFN_DENSE), D_MODEL ** -0.5),
        'e_ffn_w2': nrm((ne, FFN_DENSE, D_MODEL), FFN_DENSE ** -0.5 * DEEPNORM_BETA),
        'e_ln2_g': gain((ne, D_MODEL)),
        'e_ln2_b': nrm((ne, D_MODEL), 0.02),
        'o_w_in': nrm((no, D_MODEL, ODD_IN), D_MODEL ** -0.5),
        'o_conv_w': nrm((no, CONV_K, SSD_XBC), CONV_K ** -0.5),
        'o_conv_b': nrm((no, SSD_XBC), 0.1),
        'o_dt_bias': dt0 + jnp.log(-jnp.expm1(-dt0)),
        'o_a_log': jnp.log(unif((no, SSD_HEADS), 1.0, 16.0)),
        'o_d_skip': 1.0 + nrm((no, SSD_HEADS), 0.1),
        'o_ssd_norm': gain((no, SSD_INNER)),
        'o_rwkv_mu': unif((no, RWKV_IN), 0.0, 1.0),
        'o_rwkv_w0': unif((no, RWKV_WIDTH), -5.0, 0.0),
        'o_rwkv_w2': nrm((no, RWKV_DECAY_LORA, RWKV_WIDTH), 0.5 * RWKV_DECAY_LORA ** -0.5),
        'o_rwkv_a0': nrm((no, RWKV_WIDTH), 0.1),
        'o_rwkv_a2': nrm((no, RWKV_AAA_LORA, RWKV_WIDTH), 0.5 * RWKV_AAA_LORA ** -0.5),
        'o_rwkv_g2': nrm((no, RWKV_GATE_LORA, RWKV_WIDTH), RWKV_GATE_LORA ** -0.5),
        'o_rwkv_k_k': 0.85 + nrm((no, RWKV_WIDTH), 0.05),
        'o_rwkv_k_a': 1.0 + nrm((no, RWKV_WIDTH), 0.05),
        'o_rwkv_r_k': nrm((no, RWKV_HEADS, RWKV_HEADDIM), 0.1),
        'o_rwkv_ln_g': gain((no, RWKV_WIDTH)),
        'o_rwkv_ln_b': nrm((no, RWKV_WIDTH), 0.02),
        'o_w_out': nrm((no, ODD_MIX, D_MODEL), ODD_MIX ** -0.5 * DEEPNORM_BETA),
        'o_ln1_g': gain((no, D_MODEL)),
        'o_ln1_b': nrm((no, D_MODEL), 0.02),
        'o_router': nrm((no, D_MODEL, N_EXPERTS), D_MODEL ** -0.5),
        'o_exp_w1': nrm((no, N_EXPERTS, D_MODEL, FFN_EXPERT), D_MODEL ** -0.5),
        'o_exp_w3': nrm((no, N_EXPERTS, D_MODEL, FFN_EXPERT), D_MODEL ** -0.5),
        'o_exp_w2': nrm((no, N_EXPERTS, FFN_EXPERT, D_MODEL), FFN_EXPERT ** -0.5 * DEEPNORM_BETA),
        'o_ln2_g': gain((no, D_MODEL)),
        'o_ln2_b': nrm((no, D_MODEL), 0.02),
    }


def reference(x, e_w_in, e_gk_w2, e_gk_b, e_gla_norm, e_conv_w, e_conv_b, e_lru_wa, e_lru_ba,
              e_lru_wx, e_lru_bx, e_lru_lambda, e_w_out, e_ln1_g, e_ln1_b, e_ffn_w1, e_ffn_w3,
              e_ffn_w2, e_ln2_g, e_ln2_b, o_w_in, o_conv_w, o_conv_b, o_dt_bias, o_a_log, o_d_skip,
              o_ssd_norm, o_rwkv_mu, o_rwkv_w0, o_rwkv_w2, o_rwkv_a0, o_rwkv_a2, o_rwkv_g2,
              o_rwkv_k_k, o_rwkv_k_a, o_rwkv_r_k, o_rwkv_ln_g, o_rwkv_ln_b, o_w_out, o_ln1_g,
              o_ln1_b, o_router, o_exp_w1, o_exp_w3, o_exp_w2, o_ln2_g, o_ln2_b):
    for i in range(DEPTH):
        j = i // 2
        if i % 2 == 0:
            x = even_layer(x, e_w_in[j], e_gk_w2[j], e_gk_b[j], e_gla_norm[j], e_conv_w[j], e_conv_b[j],
                           e_lru_wa[j], e_lru_ba[j], e_lru_wx[j], e_lru_bx[j], e_lru_lambda[j],
                           e_w_out[j], e_ln1_g[j], e_ln1_b[j], e_ffn_w1[j], e_ffn_w3[j], e_ffn_w2[j],
                           e_ln2_g[j], e_ln2_b[j])
        else:
            x = odd_layer(x, o_w_in[j], o_conv_w[j], o_conv_b[j], o_dt_bias[j], o_a_log[j], o_d_skip[j],
                          o_ssd_norm[j], o_rwkv_mu[j], o_rwkv_w0[j], o_rwkv_w2[j], o_rwkv_a0[j],
                          o_rwkv_a2[j], o_rwkv_g2[j], o_rwkv_k_k[j], o_rwkv_k_a[j], o_rwkv_r_k[j],
                          o_rwkv_ln_g[j], o_rwkv_ln_b[j], o_w_out[j], o_ln1_g[j], o_ln1_b[j],
                          o_router[j], o_exp_w1[j], o_exp_w3[j], o_exp_w2[j], o_ln2_g[j], o_ln2_b[j])
    return x
```

```python
import functools
import math

import jax
import jax.numpy as jnp
from jax import lax
from jax.experimental import pallas as pl
from jax.experimental.pallas import tpu as pltpu

F32 = jnp.float32
BF16 = jnp.bfloat16
HIGHEST = lax.Precision.HIGHEST

D_MODEL = 1024
DEPTH = 2
DEEPNORM_ALPHA = (2 * DEPTH) ** 0.25
LN_EPS = 1e-5
CONV_K = 4
CHUNK = 64

GLA_HEADS = 4
GLA_DK = 64
GLA_DV = 128
GLA_QK = GLA_HEADS * GLA_DK
GLA_VW = GLA_HEADS * GLA_DV
GLA_GK_RANK = 16
GLA_GATE_NORM = 16.0
GLA_NORM_EPS = 1e-5

LRU_WIDTH = 512
LRU_BLOCKS = 8
LRU_C = 8.0

SSD_HEADS = 16
SSD_HEADDIM = 64
SSD_INNER = SSD_HEADS * SSD_HEADDIM
SSD_GROUPS = 2
SSD_STATE = 128
SSD_GROUP_WIDTH = SSD_INNER // SSD_GROUPS
SSD_NORM_EPS = 1e-5

RWKV_HEADS = 8
RWKV_HEADDIM = 64
RWKV_WIDTH = RWKV_HEADS * RWKV_HEADDIM
RWKV_DECAY_LORA = 64
RWKV_AAA_LORA = 64
RWKV_GATE_LORA = 128
RWKV_GN_EPS = 64e-5

N_EXPERTS = 8
LANES = 128
SUBLANES = 8
VMEM_LIMIT = 56 * 1024 * 1024

EVEN_COLS = dict(v=(0, 512), g=(512, 512), xb=(1024, 512), gate=(1536, 512), q=(2048, 256), k=(2304, 256),
                 gk=(2560, 128))
EVEN_WIDTH = 2688
ODD_COLS = dict(z=(0, 1024), x=(1024, 1024), r=(2048, 512), k=(2560, 512), v=(3072, 512), bm=(3584, 256),
                cm=(3840, 256), wa=(4096, 128), xg=(4224, 128), dt=(4352, 128))
ODD_WIDTH = 4480


def _mm(a, b, dims=((1,), (0,)), exact=False):
    dn = (dims, ((), ()))
    if exact:
        return lax.dot_general(a.astype(F32), b.astype(F32), dn, precision=HIGHEST, preferred_element_type=F32)
    return lax.dot_general(a.astype(BF16), b.astype(BF16), dn, preferred_element_type=F32)


_NT = ((1,), (1,))
_TN = ((0,), (0,))


def _sigmoid(x):
    return 1.0 / (1.0 + jnp.exp(-x))


def _softplus(x):
    return jnp.maximum(x, 0.0) + jnp.log(1.0 + jnp.exp(-jnp.abs(x)))


def _silu(x):
    return x * _sigmoid(x)


def _layer_norm(h, g, b):
    mu = jnp.mean(h, axis=-1, keepdims=True)
    d = h - mu
    var = jnp.mean(d * d, axis=-1, keepdims=True)
    return d * lax.rsqrt(var + LN_EPS) * g + b


def _params(*sem):
    return pltpu.CompilerParams(dimension_semantics=sem, vmem_limit_bytes=VMEM_LIMIT)


def _tri(n, strict=False):
    r = lax.broadcasted_iota(jnp.int32, (n, n), 0)
    c = lax.broadcasted_iota(jnp.int32, (n, n), 1)
    return (r > c) if strict else (r >= c)


def _shift_rows(p, tail_ref, s):
    n = p.shape[0]
    rolled = pltpu.roll(p, s, axis=0)
    head = pltpu.roll(tail_ref[...], s, axis=0)
    row = lax.broadcasted_iota(jnp.int32, (SUBLANES, p.shape[1]), 0)
    fixed = jnp.where(row < s, head, rolled[:SUBLANES])
    return jnp.concatenate([fixed, rolled[SUBLANES:]], axis=0) if n > SUBLANES else fixed


def _proj_kernel(x_ref, w_ref, o_ref):
    o_ref[...] = jnp.dot(x_ref[...].astype(BF16), w_ref[...], preferred_element_type=F32)


def _proj(x, w, tm, tn):
    m, k = x.shape
    n = w.shape[1]
    return pl.pallas_call(
        _proj_kernel,
        grid=(m // tm, n // tn),
        in_specs=[pl.BlockSpec((tm, k), lambda i, j: (i, 0)), pl.BlockSpec((k, tn), lambda i, j: (0, j))],
        out_specs=pl.BlockSpec((tm, tn), lambda i, j: (i, j)),
        out_shape=jax.ShapeDtypeStruct((m, n), F32),
        compiler_params=_params("parallel", "parallel"),
        name="in_proj",
    )(x, w)


def _outproj_ln_kernel(ya_ref, yb_ref, wa_ref, wb_ref, res_ref, g_ref, b_ref, o_ref, ob_ref):
    acc = jnp.dot(ya_ref[...].astype(BF16), wa_ref[...], preferred_element_type=F32)
    acc += jnp.dot(yb_ref[...].astype(BF16), wb_ref[...], preferred_element_type=F32)
    y = _layer_norm(DEEPNORM_ALPHA * res_ref[...] + acc, g_ref[...], b_ref[...])
    o_ref[...] = y
    ob_ref[...] = y.astype(BF16)


def _outproj_ln(ya, yb, wa, wb, res, g, b, tm):
    m = ya.shape[0]
    d = res.shape[1]
    row = lambda i: (i, 0)
    fix = lambda i: (0, 0)
    return pl.pallas_call(
        _outproj_ln_kernel,
        grid=(m // tm,),
        in_specs=[pl.BlockSpec((tm, ya.shape[1]), row), pl.BlockSpec((tm, yb.shape[1]), row),
                  pl.BlockSpec(wa.shape, fix), pl.BlockSpec(wb.shape, fix), pl.BlockSpec((tm, d), row),
                  pl.BlockSpec((1, d), fix), pl.BlockSpec((1, d), fix)],
        out_specs=[pl.BlockSpec((tm, d), row), pl.BlockSpec((tm, d), row)],
        out_shape=[jax.ShapeDtypeStruct((m, d), F32), jax.ShapeDtypeStruct((m, d), BF16)],
        compiler_params=_params("parallel"),
        name="out_proj_ln",
    )(ya, yb, wa, wb, res, g, b)


def _ffn_kernel(gated, xb_ref, w1_ref, w3_ref, w2_ref, gates_ref, res_ref, g_ref, b_ref, o_ref, ob_ref, acc_ref):
    e = pl.program_id(1)
    f = pl.program_id(2)

    @pl.when((e == 0) & (f == 0))
    def _():
        acc_ref[...] = jnp.zeros_like(acc_ref)

    x = xb_ref[...]
    h1 = jnp.dot(x, w1_ref[...], preferred_element_type=F32)
    h3 = jnp.dot(x, w3_ref[...], preferred_element_type=F32)
    h = _silu(h1) * h3
    if gated:
        gates = gates_ref[...]
        lane = lax.broadcasted_iota(jnp.int32, gates.shape, 1)
        h = h * jnp.sum(jnp.where(lane == e, gates, 0.0), axis=1, keepdims=True)
    acc_ref[...] += jnp.dot(h.astype(BF16), w2_ref[...], preferred_element_type=F32)

    @pl.when((e == pl.num_programs(1) - 1) & (f == pl.num_programs(2) - 1))
    def _():
        y = _layer_norm(DEEPNORM_ALPHA * res_ref[...] + acc_ref[...], g_ref[...], b_ref[...])
        o_ref[...] = y
        ob_ref[...] = y.astype(BF16)


def _ffn_ln(xb, w1, w3, w2, gates, res, g, b, tm, tf, gated):
    m, d = xb.shape
    ne, _, fdim = w1.shape
    row = lambda i, e, f: (i, 0)
    fix = lambda i, e, f: (0, 0)
    return pl.pallas_call(
        functools.partial(_ffn_kernel, gated),
        grid=(m // tm, ne, fdim // tf),
        in_specs=[pl.BlockSpec((tm, d), row),
                  pl.BlockSpec((None, d, tf), lambda i, e, f: (e, 0, f)),
                  pl.BlockSpec((None, d, tf), lambda i, e, f: (e, 0, f)),
                  pl.BlockSpec((None, tf, d), lambda i, e, f: (e, f, 0)),
                  pl.BlockSpec((tm, LANES), row),
                  pl.BlockSpec((tm, d), row), pl.BlockSpec((1, d), fix), pl.BlockSpec((1, d), fix)],
        out_specs=[pl.BlockSpec((tm, d), row), pl.BlockSpec((tm, d), row)],
        out_shape=[jax.ShapeDtypeStruct((m, d), F32), jax.ShapeDtypeStruct((m, d), BF16)],
        scratch_shapes=[pltpu.VMEM((tm, d), F32)],
        compiler_params=_params("parallel", "arbitrary", "arbitrary"),
        name="moe_ffn_ln" if gated else "ffn_ln",
    )(xb, w1, w3, w2, gates, res, g, b)


def _router_kernel(x_ref, w_ref, o_ref):
    logits = _mm(x_ref[...], w_ref[...], exact=True)
    lane = lax.broadcasted_iota(jnp.int32, logits.shape, 1)
    neg = jnp.float32(-jnp.inf)
    l1 = jnp.where(lane < N_EXPERTS, logits, neg)
    m1 = jnp.max(l1, axis=1, keepdims=True)
    i1 = jnp.min(jnp.where(l1 == m1, lane, LANES), axis=1, keepdims=True)
    l2 = jnp.where(lane == i1, neg, l1)
    m2 = jnp.max(l2, axis=1, keepdims=True)
    i2 = jnp.min(jnp.where(l2 == m2, lane, LANES), axis=1, keepdims=True)
    ex = jnp.exp(m2 - m1)
    w_top = 1.0 / (1.0 + ex)
    o_ref[...] = jnp.where(lane == i1, w_top, 0.0) + jnp.where(lane == i2, ex * w_top, 0.0)


def _router(x, w, tm):
    m, d = x.shape
    return pl.pallas_call(
        _router_kernel,
        grid=(m // tm,),
        in_specs=[pl.BlockSpec((tm, d), lambda i: (i, 0)), pl.BlockSpec((d, LANES), lambda i: (0, 0))],
        out_specs=pl.BlockSpec((tm, LANES), lambda i: (i, 0)),
        out_shape=jax.ShapeDtypeStruct((m, LANES), F32),
        compiler_params=_params("parallel"),
        name="router",
    )(x, w)


def _gla_kernel(q_ref, k_ref, v_ref, g_ref, gk_ref, gkw_ref, gkb_ref, nw_ref, o_ref, st_ref):
    @pl.when(pl.program_id(1) == 0)
    def _():
        st_ref[...] = jnp.zeros_like(st_ref)

    tt = q_ref.shape[0]
    tri = _tri(CHUNK)
    tri_f = tri.astype(F32)
    gkw = gkw_ref[...]
    gkb = gkb_ref[...]
    nw = nw_ref[...]

    def chunk(c, carry):
        rows = pl.ds(pl.multiple_of(c * CHUNK, CHUNK), CHUNK)
        pre = _mm(gk_ref[rows, :], gkw, exact=True) + gkb
        gk = (jnp.minimum(pre, 0.0) - jnp.log(1.0 + jnp.exp(-jnp.abs(pre)))) * (1.0 / GLA_GATE_NORM)
        bc = _mm(tri_f, gk, exact=True)
        bl = bc[CHUNK - 1:CHUNK, :]
        k = k_ref[rows, :]
        qd = q_ref[rows, :] * (GLA_DK ** -0.5) * jnp.exp(bc)
        kd = k * jnp.exp(-bc)
        kl = k * jnp.exp(bl - bc)
        dec = jnp.exp(bl)
        for h in range(GLA_HEADS):
            sk = slice(h * GLA_DK, (h + 1) * GLA_DK)
            sv = slice(h * GLA_DV, (h + 1) * GLA_DV)
            qh = qd[:, sk]
            att = jnp.where(tri, _mm(qh, kd[:, sk], _NT), 0.0)
            vh = v_ref[rows, sv]
            st = st_ref[h]
            o = _mm(att, vh) + _mm(qh, st, _NT)
            st_ref[h] = st * dec[:, sk] + _mm(vh, kl[:, sk], _TN)
            ms = jnp.mean(o * o, axis=-1, keepdims=True)
            o_ref[rows, sv] = o * lax.rsqrt(ms + GLA_NORM_EPS) * nw * _silu(g_ref[rows, sv])
        return carry

    lax.fori_loop(0, tt // CHUNK, chunk, 0)


def _gla(p, gkw, gkb, nw, tt):
    bsz, t, _ = p.shape

    def col(name):
        off, w = EVEN_COLS[name]
        return pl.BlockSpec((None, tt, w), lambda b, i, j=off // w: (b, i, j))

    fix = lambda b, i: (0, 0)
    return pl.pallas_call(
        _gla_kernel,
        grid=(bsz, t // tt),
        in_specs=[col("q"), col("k"), col("v"), col("g"), col("gk"),
                  pl.BlockSpec(gkw.shape, fix), pl.BlockSpec(gkb.shape, fix), pl.BlockSpec(nw.shape, fix)],
        out_specs=pl.BlockSpec((None, tt, GLA_VW), lambda b, i: (b, i, 0)),
        out_shape=jax.ShapeDtypeStruct((bsz, t, GLA_VW), F32),
        scratch_shapes=[pltpu.VMEM((GLA_HEADS, GLA_DV, GLA_DK), F32)],
        compiler_params=_params("parallel", "arbitrary"),
        name="gla",
    )(p, p, p, p, p, gkw, gkb, nw)


def _rglru_kernel(x_ref, gate_ref, cw_ref, cb_ref, wa_ref, ba_ref, wx_ref, bx_ref, lam_ref, o_ref,
                  tail_ref, h_ref, a_scr, u_scr):
    @pl.when(pl.program_id(1) == 0)
    def _():
        tail_ref[...] = jnp.zeros_like(tail_ref)
        h_ref[...] = jnp.zeros_like(h_ref)

    tt = x_ref.shape[0]
    x = x_ref[...]
    cw = cw_ref[...]
    xc = x * cw[CONV_K - 1:CONV_K, :] + cb_ref[...]
    for s in range(1, CONV_K):
        xc += _shift_rows(x, tail_ref, s) * cw[CONV_K - 1 - s:CONV_K - s, :]
    tail_ref[...] = x[tt - SUBLANES:, :]

    r = _sigmoid(_mm(xc, wa_ref[...]) + ba_ref[...])
    i = _sigmoid(_mm(xc, wx_ref[...]) + bx_ref[...])
    log_a = -LRU_C * r * _softplus(-lam_ref[...])
    a_scr[...] = jnp.exp(log_a)
    u_scr[...] = jnp.sqrt(1.0 - jnp.exp(2.0 * log_a)) * (i * xc)

    def step(t, h):
        h = a_scr[pl.ds(t, 1), :] * h + u_scr[pl.ds(t, 1), :]
        u_scr[pl.ds(t, 1), :] = h
        return h

    h_ref[...] = lax.fori_loop(0, tt, step, h_ref[...], unroll=8)
    gate = gate_ref[...]
    gelu = 0.5 * gate * (1.0 + jnp.tanh(math.sqrt(2.0 / math.pi) * (gate + 0.044715 * gate * gate * gate)))
    o_ref[...] = u_scr[...] * gelu


def _rglru(p, cw, cb, wa, ba, wx, bx, lam, tt):
    bsz, t, _ = p.shape
    w = LRU_WIDTH

    def col(name):
        off, _ = EVEN_COLS[name]
        return pl.BlockSpec((None, tt, w), lambda b, i, j=off // w: (b, i, j))

    fix = lambda b, i: (0, 0)
    vec = pl.BlockSpec((1, w), fix)
    return pl.pallas_call(
        _rglru_kernel,
        grid=(bsz, t // tt),
        in_specs=[col("xb"), col("gate"), pl.BlockSpec((CONV_K, w), fix), vec,
                  pl.BlockSpec((w, w), fix), vec, pl.BlockSpec((w, w), fix), vec, vec],
        out_specs=pl.BlockSpec((None, tt, w), lambda b, i: (b, i, 0)),
        out_shape=jax.ShapeDtypeStruct((bsz, t, w), F32),
        scratch_shapes=[pltpu.VMEM((SUBLANES, w), F32), pltpu.VMEM((1, w), F32),
                        pltpu.VMEM((tt, w), F32), pltpu.VMEM((tt, w), F32)],
        compiler_params=_params("parallel", "arbitrary"),
        name="rglru",
    )(p, p, cw, cb, wa, ba, wx, bx, lam)


def _ssd_kernel(z_ref, x_ref, bm_ref, cm_ref, dt_ref, cwx_ref, cbx_ref, cwb_ref, cbb_ref, cwc_ref, cbc_ref,
                dtb_ref, alog_ref, dskip_ref, nw_ref, expand_ref, o_ref,
                tx_ref, tb_ref, tc_ref, st_ref, xs_scr, bs_scr, cs_scr):
    @pl.when(pl.program_id(1) == 0)
    def _():
        tx_ref[...] = jnp.zeros_like(tx_ref)
        tb_ref[...] = jnp.zeros_like(tb_ref)
        tc_ref[...] = jnp.zeros_like(tc_ref)
        st_ref[...] = jnp.zeros_like(st_ref)

    tt = x_ref.shape[0]

    def conv_silu(src_ref, tail_ref, cw_ref, cb_ref, dst_ref):
        x = src_ref[...]
        cw = cw_ref[...]
        y = x * cw[CONV_K - 1:CONV_K, :] + cb_ref[...]
        for s in range(1, CONV_K):
            y += _shift_rows(x, tail_ref, s) * cw[CONV_K - 1 - s:CONV_K - s, :]
        tail_ref[...] = x[tt - SUBLANES:, :]
        dst_ref[...] = _silu(y)

    conv_silu(x_ref, tx_ref, cwx_ref, cbx_ref, xs_scr)
    conv_silu(bm_ref, tb_ref, cwb_ref, cbb_ref, bs_scr)
    conv_silu(cm_ref, tc_ref, cwc_ref, cbc_ref, cs_scr)

    tri = _tri(CHUNK)
    tri_f = tri.astype(F32)
    neg_a = -jnp.exp(alog_ref[...])
    dtb = dtb_ref[...]
    expand = expand_ref[...]
    dskip = dskip_ref[...]
    nw = nw_ref[...]
    hpg = SSD_HEADS // SSD_GROUPS

    def chunk(c, carry):
        rows = pl.ds(pl.multiple_of(c * CHUNK, CHUNK), CHUNK)
        dtc = _softplus(dt_ref[rows, :] + dtb)
        acs = _mm(tri_f, dtc * neg_a, exact=True)
        acs_t = acs.T
        last = acs[CHUNK - 1:CHUNK, :]
        dt_x = _mm(dtc, expand, exact=True)
        ea_x = _mm(jnp.exp(acs), expand, exact=True)
        ds_x = _mm(jnp.exp(last - acs), expand, exact=True)
        x = xs_scr[rows, :]
        xdt = x * dt_x
        xdec = xdt * ds_x
        cd_x = ea_x[CHUNK - 1:CHUNK, :]
        ys = []
        for g in range(SSD_GROUPS):
            sg = slice(g * SSD_GROUP_WIDTH, (g + 1) * SSD_GROUP_WIDTH)
            ss = slice(g * SSD_STATE, (g + 1) * SSD_STATE)
            bg = bs_scr[rows, ss]
            cg = cs_scr[rows, ss]
            cb = _mm(cg, bg, _NT)
            st = st_ref[g]
            yg = _mm(cg, st) * ea_x[:, sg]
            st_ref[g] = st * cd_x[:, sg] + _mm(bg, xdec[:, sg], _TN)
            yh = []
            for j in range(hpg):
                h = g * hpg + j
                seg = acs[:, h:h + 1] - acs_t[h:h + 1, :]
                lmat = jnp.exp(jnp.where(tri, seg, -jnp.inf))
                sh = slice(h * SSD_HEADDIM, (h + 1) * SSD_HEADDIM)
                yh.append(_mm(cb * lmat, xdt[:, sh]))
            ys.append(yg + jnp.concatenate(yh, axis=1))
        y = jnp.concatenate(ys, axis=1) + x * dskip
        y = y * _silu(z_ref[rows, :])
        outs = []
        for g in range(SSD_GROUPS):
            sg = slice(g * SSD_GROUP_WIDTH, (g + 1) * SSD_GROUP_WIDTH)
            yg = y[:, sg]
            ms = jnp.mean(yg * yg, axis=-1, keepdims=True)
            outs.append(yg * lax.rsqrt(ms + SSD_NORM_EPS))
        o_ref[rows, :] = jnp.concatenate(outs, axis=1) * nw
        return carry

    lax.fori_loop(0, tt // CHUNK, chunk, 0)


def _ssd(p, cw, cb, dtb, alog, dskip_x, nw, expand, tt):
    bsz, t, _ = p.shape

    def col(name):
        off, w = ODD_COLS[name]
        return pl.BlockSpec((None, tt, w), lambda b, i, j=off // w: (b, i, j))

    fix = lambda b, i: (0, 0)
    full = lambda a: pl.BlockSpec(a.shape, fix)
    gs = SSD_GROUPS * SSD_STATE
    cwx, cwb, cwc = cw[:, :SSD_INNER], cw[:, SSD_INNER:SSD_INNER + gs], cw[:, SSD_INNER + gs:]
    cbx, cbb, cbc = cb[:, :SSD_INNER], cb[:, SSD_INNER:SSD_INNER + gs], cb[:, SSD_INNER + gs:]
    args = (cwx, cbx, cwb, cbb, cwc, cbc, dtb, alog, dskip_x, nw, expand)
    return pl.pallas_call(
        _ssd_kernel,
        grid=(bsz, t // tt),
        in_specs=[col("z"), col("x"), col("bm"), col("cm"), col("dt")] + [full(a) for a in args],
        out_specs=pl.BlockSpec((None, tt, SSD_INNER), lambda b, i: (b, i, 0)),
        out_shape=jax.ShapeDtypeStruct((bsz, t, SSD_INNER), F32),
        scratch_shapes=[pltpu.VMEM((SUBLANES, SSD_INNER), F32), pltpu.VMEM((SUBLANES, gs), F32),
                        pltpu.VMEM((SUBLANES, gs), F32),
                        pltpu.VMEM((SSD_GROUPS, SSD_STATE, SSD_GROUP_WIDTH), F32),
                        pltpu.VMEM((tt, SSD_INNER), F32), pltpu.VMEM((tt, gs), F32), pltpu.VMEM((tt, gs), F32)],
        compiler_params=_params("parallel", "arbitrary"),
        name="ssd",
    )(p, p, p, p, p, *args)


def _rwkv_kernel(r_ref, k_ref, v_ref, wa_ref, xg_ref, mur_ref, muk_ref, muv_ref, muwa_ref, mug_ref,
                 w0_ref, w2_ref, a0_ref, a2_ref, g2_ref, kk_ref, ka_ref, rk_ref, lng_ref, lnb_ref, ones_ref,
                 o_ref, tr_ref, tk_ref, tv_ref, twa_ref, tg_ref, st_ref, y_scr,
                 r_scr, k_scr, v_scr, kk_scr, b_scr, lw_scr):
    @pl.when(pl.program_id(1) == 0)
    def _():
        for ref in (tr_ref, tk_ref, tv_ref, twa_ref, tg_ref, st_ref):
            ref[...] = jnp.zeros_like(ref)

    tt = r_ref.shape[0]
    hd = RWKV_HEADDIM

    def mix(p_ref, tail_ref, mu_ref):
        p = p_ref[...]
        prev = _shift_rows(p, tail_ref, 1)
        tail_ref[...] = p[tt - SUBLANES:, :]
        return p + (prev - p) * mu_ref[...]

    r = mix(r_ref, tr_ref, mur_ref)
    k = mix(k_ref, tk_ref, muk_ref)
    v = mix(v_ref, tv_ref, muv_ref)
    xwa = mix(wa_ref, twa_ref, muwa_ref)
    xg = mix(xg_ref, tg_ref, mug_ref)

    lane = lax.broadcasted_iota(jnp.int32, xwa.shape, 1)
    lora_in = jnp.where(lane < RWKV_DECAY_LORA, jnp.tanh(xwa), xwa)
    w_log = -_softplus(-(w0_ref[...] + _mm(lora_in, w2_ref[...], exact=True))) - 0.5
    lw = -jnp.exp(w_log)
    a_sig = _sigmoid(a0_ref[...] + _mm(lora_in, a2_ref[...], exact=True))
    gate = _mm(_sigmoid(xg), g2_ref[...], exact=True)
    ones_bd = ones_ref[...]
    kk = k * kk_ref[...]
    kk = kk / jnp.maximum(jnp.sqrt(_mm(kk * kk, ones_bd, exact=True)), 1e-12)
    k = k * (1.0 + (a_sig - 1.0) * ka_ref[...])

    tri_incl = _tri(CHUNK)
    tri_strict = _tri(CHUNK, strict=True)
    tri_f = tri_incl.astype(F32)
    eye = (tri_incl & ~tri_strict).astype(F32)
    for ref, val in ((r_scr, r), (k_scr, k), (v_scr, v), (kk_scr, kk), (b_scr, kk * a_sig), (lw_scr, lw)):
        ref[...] = val

    def chunk(c, carry):
        rows = pl.ds(pl.multiple_of(c * CHUNK, CHUNK), CHUNK)
        lwc = lw_scr[rows, :]
        kc = k_scr[rows, :]
        bc = b_scr[rows, :]
        vc = v_scr[rows, :]
        cum = _mm(tri_f, lwc, exact=True)
        last = cum[CHUNK - 1:CHUNK, :]
        at = -kk_scr[rows, :] * jnp.exp(cum - lwc)
        rt = r_scr[rows, :] * jnp.exp(cum)
        e_neg = jnp.exp(-cum)
        bt = bc * e_neg
        kt = kc * e_neg
        e_last = jnp.exp(last - cum)
        bh = bc * e_last
        kh = kc * e_last
        wc = jnp.exp(last)
        for h in range(RWKV_HEADS):
            sl = slice(h * hd, (h + 1) * hd)
            vh = vc[:, sl]
            amat = _mm(jnp.concatenate([at[:, sl], rt[:, sl]], axis=0),
                       jnp.concatenate([bt[:, sl], kt[:, sl]], axis=0), _NT)
            a_ab = jnp.where(tri_strict, amat[:CHUNK, :CHUNK], 0.0)
            a_ak = jnp.where(tri_strict, amat[:CHUNK, CHUNK:], 0.0)
            a_rb = jnp.where(tri_incl, amat[CHUNK:, :CHUNK], 0.0)
            a_rk = jnp.where(tri_incl, amat[CHUNK:, CHUNK:], 0.0)
            inv = eye + a_ab
            pw = a_ab
            for _ in range(int(math.log2(CHUNK)) - 1):
                pw = _mm(pw, pw, exact=True)
                inv = inv + _mm(inv, pw, exact=True)
            avrv = _mm(jnp.concatenate([a_ak, a_rk], axis=0), vh)
            tu = _mm(inv, jnp.concatenate([at[:, sl], avrv[:CHUNK]], axis=1), exact=True)
            qy = _mm(a_rb, tu) + jnp.concatenate([rt[:, sl], avrv[CHUNK:]], axis=1)
            zb = _mm(tu, bh[:, sl], _TN)
            st = st_ref[h]
            y_scr[rows, sl] = _mm(qy[:, :hd], st, _NT) + qy[:, hd:]
            st_ref[h] = st * wc[:, sl] + _mm(st, zb[:hd]) + zb[hd:] + _mm(vh, kh[:, sl], _TN)
        return carry

    lax.fori_loop(0, tt // CHUNK, chunk, 0)

    y = y_scr[...]
    inv_n = 1.0 / hd
    mu_y = _mm(y, ones_bd, exact=True) * inv_n
    dy = y - mu_y
    var_y = _mm(dy * dy, ones_bd, exact=True) * inv_n
    yn = dy * lax.rsqrt(var_y + RWKV_GN_EPS) * lng_ref[...] + lnb_ref[...]
    bonus = _mm(r * k * rk_ref[...], ones_bd, exact=True) * v
    o_ref[...] = (yn + bonus) * gate


def _rwkv(p, mu, w0, w2p, a0, a2p, g2, k_k, k_a, r_k, ln_g, ln_b, ones_bd, tt):
    bsz, t, _ = p.shape
    w = RWKV_WIDTH

    def col(name):
        off, wd = ODD_COLS[name]
        return pl.BlockSpec((None, tt, wd), lambda b, i, j=off // wd: (b, i, j))

    fix = lambda b, i: (0, 0)
    full = lambda a: pl.BlockSpec(a.shape, fix)
    mur, muk, muv = mu[:, :w], mu[:, w:2 * w], mu[:, 2 * w:3 * w]
    muwa, mug = mu[:, 3 * w:3 * w + LANES], mu[:, 3 * w + LANES:]
    args = (mur, muk, muv, muwa, mug, w0, w2p, a0, a2p, g2, k_k, k_a, r_k, ln_g, ln_b, ones_bd)
    return pl.pallas_call(
        _rwkv_kernel,
        grid=(bsz, t // tt),
        in_specs=[col("r"), col("k"), col("v"), col("wa"), col("xg")] + [full(a) for a in args],
        out_specs=pl.BlockSpec((None, tt, w), lambda b, i: (b, i, 0)),
        out_shape=jax.ShapeDtypeStruct((bsz, t, w), F32),
        scratch_shapes=[pltpu.VMEM((SUBLANES, w), F32), pltpu.VMEM((SUBLANES, w), F32),
                        pltpu.VMEM((SUBLANES, w), F32), pltpu.VMEM((SUBLANES, LANES), F32),
                        pltpu.VMEM((SUBLANES, LANES), F32),
                        pltpu.VMEM((RWKV_HEADS, RWKV_HEADDIM, RWKV_HEADDIM), F32)]
        + [pltpu.VMEM((tt, w), F32)] * 7,
        compiler_params=_params("parallel", "arbitrary"),
        name="rwkv7",
    )(p, p, p, p, p, *args)


def _reorder_cols(w, pieces, order, width):
    out = jnp.zeros((w.shape[0], width), w.dtype)
    for name, (off, _) in order.items():
        start, size = pieces[name]
        out = out.at[:, off:off + size].set(w[:, start:start + size])
    return out


def _block_diag(w):
    n, i, j = w.shape
    eye = jnp.eye(n, dtype=w.dtype)
    return (eye[:, None, :, None] * w[:, :, None, :]).reshape(n * i, n * j)


def _row(v, width=None):
    v = v.reshape(1, -1).astype(F32)
    if width is not None and v.shape[1] < width:
        v = jnp.pad(v, ((0, 0), (0, width - v.shape[1])))
    return v


def _even_layer(x, x_in, w_in, gk_w2, gk_b, gla_norm, conv_w, conv_b, wa, ba, wx, bx, lam,
                w_out, ln1_g, ln1_b, f_w1, f_w3, f_w2, ln2_g, ln2_b, bsz, t):
    qk, vw, lw = GLA_QK, GLA_VW, LRU_WIDTH
    pieces = dict(q=(0, qk), k=(qk, qk), v=(2 * qk, vw), g=(2 * qk + vw, vw), gk=(2 * qk + 2 * vw, GLA_GK_RANK),
                  xb=(2 * qk + 2 * vw + GLA_GK_RANK, lw), gate=(2 * qk + 2 * vw + GLA_GK_RANK + lw, lw))
    w_in_r = _reorder_cols(w_in, pieces, EVEN_COLS, EVEN_WIDTH).astype(BF16)
    p = _proj(x_in, w_in_r, tm=min(1024, x.shape[0]), tn=896).reshape(bsz, t, EVEN_WIDTH)
    tt = min(512, t)
    gkw = jnp.pad(gk_w2, ((0, LANES - GLA_GK_RANK), (0, 0)))
    y_a = _gla(p, gkw, _row(gk_b), _row(gla_norm), tt)
    y_b = _rglru(p, conv_w, _row(conv_b), _block_diag(wa).astype(BF16), _row(ba), _block_diag(wx).astype(BF16),
                 _row(bx), _row(lam), tt)
    n = bsz * t
    tm = min(512, n)
    x1, x1b = _outproj_ln(y_a.reshape(n, vw), y_b.reshape(n, lw), w_out[:vw].astype(BF16), w_out[vw:].astype(BF16),
                          x, _row(ln1_g), _row(ln1_b), tm)
    ones = jnp.ones((n, LANES), F32)
    return _ffn_ln(x1b, f_w1[None].astype(BF16), f_w3[None].astype(BF16), f_w2[None].astype(BF16), ones, x1,
                   _row(ln2_g), _row(ln2_b), tm=min(1024, n), tf=256, gated=False)


def _odd_layer(x, xb, w_in, conv_w, conv_b, dt_bias, a_log, d_skip, ssd_norm, mu, w0, w2, a0, a2, g2, k_k, k_a,
               r_k, rln_g, rln_b, w_out, ln1_g, ln1_b, router, ew1, ew3, ew2, ln2_g, ln2_b, bsz, t):
    si, gs, rw = SSD_INNER, SSD_GROUPS * SSD_STATE, RWKV_WIDTH
    o = 2 * si + 2 * gs + SSD_HEADS
    pieces = dict(z=(0, si), x=(si, si), bm=(2 * si, gs), cm=(2 * si + gs, gs), dt=(2 * si + 2 * gs, SSD_HEADS),
                  r=(o, rw), k=(o + rw, rw), v=(o + 2 * rw, rw),
                  wa=(o + 3 * rw, RWKV_DECAY_LORA + RWKV_AAA_LORA), xg=(o + 3 * rw + LANES, RWKV_GATE_LORA))
    w_in_r = _reorder_cols(w_in, pieces, ODD_COLS, ODD_WIDTH).astype(BF16)
    n = bsz * t
    p = _proj(xb, w_in_r, tm=min(1024, n), tn=896).reshape(bsz, t, ODD_WIDTH)
    expand = jnp.pad(jnp.repeat(jnp.eye(SSD_HEADS, dtype=F32), SSD_HEADDIM, axis=1), ((0, LANES - SSD_HEADS), (0, 0)))
    y_c = _ssd(p, conv_w, _row(conv_b), _row(dt_bias, LANES), _row(a_log, LANES),
               _row(jnp.repeat(d_skip, SSD_HEADDIM)), _row(ssd_norm), expand, tt=min(256, t))
    zeros = jnp.zeros((RWKV_DECAY_LORA, rw), F32)
    ones_bd = _block_diag(jnp.ones((RWKV_HEADS, RWKV_HEADDIM, RWKV_HEADDIM), F32))
    y_d = _rwkv(p, _row(mu), _row(w0), jnp.concatenate([w2, zeros]), _row(a0), jnp.concatenate([zeros, a2]), g2,
                _row(k_k), _row(k_a), _row(r_k), _row(rln_g), _row(rln_b), ones_bd, tt=min(256, t))
    tm = min(512, n)
    x1, x1b = _outproj_ln(y_c.reshape(n, si), y_d.reshape(n, rw), w_out[:si].astype(BF16), w_out[si:].astype(BF16),
                          x, _row(ln1_g), _row(ln1_b), tm)
    gates = _router(x1, jnp.pad(router, ((0, 0), (0, LANES - N_EXPERTS))), tm)
    return _ffn_ln(x1b, ew1.astype(BF16), ew3.astype(BF16), ew2.astype(BF16), gates, x1,
                   _row(ln2_g), _row(ln2_b), tm=min(1024, n), tf=512, gated=True)


def kernel(x, e_w_in, e_gk_w2, e_gk_b, e_gla_norm, e_conv_w, e_conv_b, e_lru_wa, e_lru_ba, e_lru_wx, e_lru_bx, e_lru_lambda, e_w_out, e_ln1_g, e_ln1_b, e_ffn_w1, e_ffn_w3, e_ffn_w2, e_ln2_g, e_ln2_b, o_w_in, o_conv_w, o_conv_b, o_dt_bias, o_a_log, o_d_skip, o_ssd_norm, o_rwkv_mu, o_rwkv_w0, o_rwkv_w2, o_rwkv_a0, o_rwkv_a2, o_rwkv_g2, o_rwkv_k_k, o_rwkv_k_a, o_rwkv_r_k, o_rwkv_ln_g, o_rwkv_ln_b, o_w_out, o_ln1_g, o_ln1_b, o_router, o_exp_w1, o_exp_w3, o_exp_w2, o_ln2_g, o_ln2_b):
    bsz, t, d = x.shape
    h = x.reshape(bsz * t, d)
    hb = h
    for i in range(DEPTH):
        j = i // 2
        if i % 2 == 0:
            h, hb = _even_layer(h, hb, e_w_in[j], e_gk_w2[j], e_gk_b[j], e_gla_norm[j], e_conv_w[j], e_conv_b[j],
                                e_lru_wa[j], e_lru_ba[j], e_lru_wx[j], e_lru_bx[j], e_lru_lambda[j], e_w_out[j],
                                e_ln1_g[j], e_ln1_b[j], e_ffn_w1[j], e_ffn_w3[j], e_ffn_w2[j], e_ln2_g[j],
                                e_ln2_b[j], bsz, t)
        else:
            h, hb = _odd_layer(h, hb, o_w_in[j], o_conv_w[j], o_conv_b[j], o_dt_bias[j], o_a_log[j], o_d_skip[j],
                               o_ssd_norm[j], o_rwkv_mu[j], o_rwkv_w0[j], o_rwkv_w2[j], o_rwkv_a0[j],
                               o_rwkv_a2[j], o_rwkv_g2[j], o_rwkv_k_k[j], o_rwkv_k_a[j], o_rwkv_r_k[j],
                               o_rwkv_ln_g[j], o_rwkv_ln_b[j], o_w_out[j], o_ln1_g[j], o_ln1_b[j], o_router[j],
                               o_exp_w1[j], o_exp_w3[j], o_exp_w2[j], o_ln2_g[j], o_ln2_b[j], bsz, t)
    return h.reshape(bsz, t, d)
```

```python
import functools
import math

import jax
import jax.numpy as jnp
from jax import lax
from jax.experimental import pallas as pl
from jax.experimental.pallas import tpu as pltpu

F32 = jnp.float32
BF16 = jnp.bfloat16
HIGHEST = lax.Precision.HIGHEST

D_MODEL = 1024
DEPTH = 2
DEEPNORM_ALPHA = (2 * DEPTH) ** 0.25
LN_EPS = 1e-5
CONV_K = 4
CHUNK = 64

GLA_HEADS = 4
GLA_DK = 64
GLA_DV = 128
GLA_QK = GLA_HEADS * GLA_DK
GLA_VW = GLA_HEADS * GLA_DV
GLA_GK_RANK = 16
GLA_GATE_NORM = 16.0
GLA_NORM_EPS = 1e-5

LRU_WIDTH = 512
LRU_BLOCKS = 8
LRU_C = 8.0

SSD_HEADS = 16
SSD_HEADDIM = 64
SSD_INNER = SSD_HEADS * SSD_HEADDIM
SSD_GROUPS = 2
SSD_STATE = 128
SSD_GROUP_WIDTH = SSD_INNER // SSD_GROUPS
SSD_NORM_EPS = 1e-5

RWKV_HEADS = 8
RWKV_HEADDIM = 64
RWKV_WIDTH = RWKV_HEADS * RWKV_HEADDIM
RWKV_DECAY_LORA = 64
RWKV_AAA_LORA = 64
RWKV_GATE_LORA = 128
RWKV_GN_EPS = 64e-5

N_EXPERTS = 8
LANES = 128
SUBLANES = 8
VMEM_LIMIT = 56 * 1024 * 1024

EVEN_COLS = dict(v=(0, 512), g=(512, 512), xb=(1024, 512), gate=(1536, 512), q=(2048, 256), k=(2304, 256),
                 gk=(2560, 128))
EVEN_WIDTH = 2688
ODD_COLS = dict(z=(0, 1024), x=(1024, 1024), r=(2048, 512), k=(2560, 512), v=(3072, 512), bm=(3584, 256),
                cm=(3840, 256), wa=(4096, 128), xg=(4224, 128), dt=(4352, 128))
ODD_WIDTH = 4480


def _mm(a, b, dims=((1,), (0,)), exact=False):
    dn = (dims, ((), ()))
    if exact:
        return lax.dot_general(a.astype(F32), b.astype(F32), dn, precision=HIGHEST, preferred_element_type=F32)
    return lax.dot_general(a.astype(BF16), b.astype(BF16), dn, preferred_element_type=F32)


_NT = ((1,), (1,))
_TN = ((0,), (0,))


def _mm_01(x, sel):
    hi = x.astype(BF16)
    lo = (x - hi.astype(F32)).astype(BF16)
    return (jnp.dot(hi, sel, preferred_element_type=F32) + jnp.dot(lo, sel, preferred_element_type=F32))


def _sigmoid(x):
    return 1.0 / (1.0 + jnp.exp(-x))


def _softplus(x):
    return jnp.maximum(x, 0.0) + jnp.log(1.0 + jnp.exp(-jnp.abs(x)))


def _silu(x):
    return x * _sigmoid(x)


def _layer_norm(h, g, b):
    mu = jnp.mean(h, axis=-1, keepdims=True)
    d = h - mu
    var = jnp.mean(d * d, axis=-1, keepdims=True)
    return d * lax.rsqrt(var + LN_EPS) * g + b


def _params(*sem):
    return pltpu.CompilerParams(dimension_semantics=sem, vmem_limit_bytes=VMEM_LIMIT)


def _tri(n, strict=False):
    r = lax.broadcasted_iota(jnp.int32, (n, n), 0)
    c = lax.broadcasted_iota(jnp.int32, (n, n), 1)
    return (r > c) if strict else (r >= c)


def _shift_rows(p, tail_ref, s):
    n = p.shape[0]
    rolled = pltpu.roll(p, s, axis=0)
    head = pltpu.roll(tail_ref[...], s, axis=0)
    row = lax.broadcasted_iota(jnp.int32, (SUBLANES, p.shape[1]), 0)
    fixed = jnp.where(row < s, head, rolled[:SUBLANES])
    return jnp.concatenate([fixed, rolled[SUBLANES:]], axis=0) if n > SUBLANES else fixed


def _proj_kernel(x_ref, w_ref, o_ref):
    o_ref[...] = jnp.dot(x_ref[...].astype(BF16), w_ref[...], preferred_element_type=F32)


def _proj(x, w, tm, tn):
    m, k = x.shape
    n = w.shape[1]
    return pl.pallas_call(
        _proj_kernel,
        grid=(m // tm, n // tn),
        in_specs=[pl.BlockSpec((tm, k), lambda i, j: (i, 0)), pl.BlockSpec((k, tn), lambda i, j: (0, j))],
        out_specs=pl.BlockSpec((tm, tn), lambda i, j: (i, j)),
        out_shape=jax.ShapeDtypeStruct((m, n), F32),
        compiler_params=_params("parallel", "parallel"),
        name="in_proj",
    )(x, w)


def _outproj_ln_kernel(ya_ref, yb_ref, wa_ref, wb_ref, res_ref, g_ref, b_ref, o_ref, ob_ref):
    acc = jnp.dot(ya_ref[...].astype(BF16), wa_ref[...], preferred_element_type=F32)
    acc += jnp.dot(yb_ref[...].astype(BF16), wb_ref[...], preferred_element_type=F32)
    y = _layer_norm(DEEPNORM_ALPHA * res_ref[...] + acc, g_ref[...], b_ref[...])
    o_ref[...] = y
    ob_ref[...] = y.astype(BF16)


def _outproj_ln(ya, yb, wa, wb, res, g, b, tm):
    m = ya.shape[0]
    d = res.shape[1]
    row = lambda i: (i, 0)
    fix = lambda i: (0, 0)
    return pl.pallas_call(
        _outproj_ln_kernel,
        grid=(m // tm,),
        in_specs=[pl.BlockSpec((tm, ya.shape[1]), row), pl.BlockSpec((tm, yb.shape[1]), row),
                  pl.BlockSpec(wa.shape, fix), pl.BlockSpec(wb.shape, fix), pl.BlockSpec((tm, d), row),
                  pl.BlockSpec((1, d), fix), pl.BlockSpec((1, d), fix)],
        out_specs=[pl.BlockSpec((tm, d), row), pl.BlockSpec((tm, d), row)],
        out_shape=[jax.ShapeDtypeStruct((m, d), F32), jax.ShapeDtypeStruct((m, d), BF16)],
        compiler_params=_params("parallel"),
        name="out_proj_ln",
    )(ya, yb, wa, wb, res, g, b)


def _ffn_kernel(gated, xb_ref, w1_ref, w3_ref, w2_ref, gates_ref, res_ref, g_ref, b_ref, o_ref, ob_ref, acc_ref):
    e = pl.program_id(1)
    f = pl.program_id(2)

    @pl.when((e == 0) & (f == 0))
    def _():
        acc_ref[...] = jnp.zeros_like(acc_ref)

    x = xb_ref[...]
    h1 = jnp.dot(x, w1_ref[...], preferred_element_type=F32)
    h3 = jnp.dot(x, w3_ref[...], preferred_element_type=F32)
    h = _silu(h1) * h3
    if gated:
        gates = gates_ref[...]
        lane = lax.broadcasted_iota(jnp.int32, gates.shape, 1)
        h = h * jnp.sum(jnp.where(lane == e, gates, 0.0), axis=1, keepdims=True)
    acc_ref[...] += jnp.dot(h.astype(BF16), w2_ref[...], preferred_element_type=F32)

    @pl.when((e == pl.num_programs(1) - 1) & (f == pl.num_programs(2) - 1))
    def _():
        y = _layer_norm(DEEPNORM_ALPHA * res_ref[...] + acc_ref[...], g_ref[...], b_ref[...])
        o_ref[...] = y
        ob_ref[...] = y.astype(BF16)


def _ffn_ln(xb, w1, w3, w2, gates, res, g, b, tm, tf, gated):
    m, d = xb.shape
    ne, _, fdim = w1.shape
    row = lambda i, e, f: (i, 0)
    fix = lambda i, e, f: (0, 0)
    return pl.pallas_call(
        functools.partial(_ffn_kernel, gated),
        grid=(m // tm, ne, fdim // tf),
        in_specs=[pl.BlockSpec((tm, d), row),
                  pl.BlockSpec((None, d, tf), lambda i, e, f: (e, 0, f)),
                  pl.BlockSpec((None, d, tf), lambda i, e, f: (e, 0, f)),
                  pl.BlockSpec((None, tf, d), lambda i, e, f: (e, f, 0)),
                  pl.BlockSpec((tm, LANES), row),
                  pl.BlockSpec((tm, d), row), pl.BlockSpec((1, d), fix), pl.BlockSpec((1, d), fix)],
        out_specs=[pl.BlockSpec((tm, d), row), pl.BlockSpec((tm, d), row)],
        out_shape=[jax.ShapeDtypeStruct((m, d), F32), jax.ShapeDtypeStruct((m, d), BF16)],
        scratch_shapes=[pltpu.VMEM((tm, d), F32)],
        compiler_params=_params("parallel", "arbitrary", "arbitrary"),
        name="moe_ffn_ln" if gated else "ffn_ln",
    )(xb, w1, w3, w2, gates, res, g, b)


def _router_kernel(x_ref, w_ref, o_ref):
    logits = _mm(x_ref[...], w_ref[...], exact=True)
    lane = lax.broadcasted_iota(jnp.int32, logits.shape, 1)
    neg = jnp.float32(-jnp.inf)
    l1 = jnp.where(lane < N_EXPERTS, logits, neg)
    m1 = jnp.max(l1, axis=1, keepdims=True)
    i1 = jnp.min(jnp.where(l1 == m1, lane, LANES), axis=1, keepdims=True)
    l2 = jnp.where(lane == i1, neg, l1)
    m2 = jnp.max(l2, axis=1, keepdims=True)
    i2 = jnp.min(jnp.where(l2 == m2, lane, LANES), axis=1, keepdims=True)
    ex = jnp.exp(m2 - m1)
    w_top = 1.0 / (1.0 + ex)
    o_ref[...] = jnp.where(lane == i1, w_top, 0.0) + jnp.where(lane == i2, ex * w_top, 0.0)


def _router(x, w, tm):
    m, d = x.shape
    return pl.pallas_call(
        _router_kernel,
        grid=(m // tm,),
        in_specs=[pl.BlockSpec((tm, d), lambda i: (i, 0)), pl.BlockSpec((d, LANES), lambda i: (0, 0))],
        out_specs=pl.BlockSpec((tm, LANES), lambda i: (i, 0)),
        out_shape=jax.ShapeDtypeStruct((m, LANES), F32),
        compiler_params=_params("parallel"),
        name="router",
    )(x, w)


def _gla_kernel(q_ref, k_ref, v_ref, g_ref, gk_ref, gkw_ref, gkb_ref, nw_ref, o_ref, st_ref):
    @pl.when(pl.program_id(1) == 0)
    def _():
        st_ref[...] = jnp.zeros_like(st_ref)

    tt = q_ref.shape[0]
    tri = _tri(CHUNK)
    tri_f = tri.astype(F32)
    gkw = gkw_ref[...]
    gkb = gkb_ref[...]
    nw = nw_ref[...]

    def chunk(c, carry):
        rows = pl.ds(pl.multiple_of(c * CHUNK, CHUNK), CHUNK)
        pre = _mm(gk_ref[rows, :], gkw, exact=True) + gkb
        gk = (jnp.minimum(pre, 0.0) - jnp.log(1.0 + jnp.exp(-jnp.abs(pre)))) * (1.0 / GLA_GATE_NORM)
        bc = _mm(tri_f, gk, exact=True)
        bl = bc[CHUNK - 1:CHUNK, :]
        k = k_ref[rows, :]
        qd = q_ref[rows, :] * (GLA_DK ** -0.5) * jnp.exp(bc)
        kd = k * jnp.exp(-bc)
        kl = k * jnp.exp(bl - bc)
        dec = jnp.exp(bl)
        heads = range(GLA_HEADS)
        sks = [slice(h * GLA_DK, (h + 1) * GLA_DK) for h in heads]
        svs = [slice(h * GLA_DV, (h + 1) * GLA_DV) for h in heads]
        att = [jnp.where(tri, _mm(qd[:, sk], kd[:, sk], _NT), 0.0) for sk in sks]
        vs = [v_ref[rows, sv] for sv in svs]
        sts = [st_ref[h] for h in heads]
        os_ = [_mm(att[h], vs[h]) + _mm(qd[:, sks[h]], sts[h], _NT) for h in heads]
        for h in heads:
            st_ref[h] = sts[h] * dec[:, sks[h]] + _mm(vs[h], kl[:, sks[h]], _TN)
        for h in heads:
            o = os_[h]
            ms = jnp.mean(o * o, axis=-1, keepdims=True)
            o_ref[rows, svs[h]] = o * lax.rsqrt(ms + GLA_NORM_EPS) * nw * _silu(g_ref[rows, svs[h]])
        return carry

    lax.fori_loop(0, tt // CHUNK, chunk, 0)


def _gla(p, gkw, gkb, nw, tt):
    bsz, t, _ = p.shape

    def col(name):
        off, w = EVEN_COLS[name]
        return pl.BlockSpec((None, tt, w), lambda b, i, j=off // w: (b, i, j))

    fix = lambda b, i: (0, 0)
    return pl.pallas_call(
        _gla_kernel,
        grid=(bsz, t // tt),
        in_specs=[col("q"), col("k"), col("v"), col("g"), col("gk"),
                  pl.BlockSpec(gkw.shape, fix), pl.BlockSpec(gkb.shape, fix), pl.BlockSpec(nw.shape, fix)],
        out_specs=pl.BlockSpec((None, tt, GLA_VW), lambda b, i: (b, i, 0)),
        out_shape=jax.ShapeDtypeStruct((bsz, t, GLA_VW), F32),
        scratch_shapes=[pltpu.VMEM((GLA_HEADS, GLA_DV, GLA_DK), F32)],
        compiler_params=_params("parallel", "arbitrary"),
        name="gla",
    )(p, p, p, p, p, gkw, gkb, nw)


def _rglru_kernel(x_ref, gate_ref, cw_ref, cb_ref, wa_ref, ba_ref, wx_ref, bx_ref, lam_ref, o_ref,
                  tail_ref, h_ref, a_scr, u_scr):
    @pl.when(pl.program_id(1) == 0)
    def _():
        tail_ref[...] = jnp.zeros_like(tail_ref)
        h_ref[...] = jnp.zeros_like(h_ref)

    tt = x_ref.shape[0]
    x = x_ref[...]
    cw = cw_ref[...]
    xc = x * cw[CONV_K - 1:CONV_K, :] + cb_ref[...]
    for s in range(1, CONV_K):
        xc += _shift_rows(x, tail_ref, s) * cw[CONV_K - 1 - s:CONV_K - s, :]
    tail_ref[...] = x[tt - SUBLANES:, :]

    r = _sigmoid(_mm(xc, wa_ref[...]) + ba_ref[...])
    i = _sigmoid(_mm(xc, wx_ref[...]) + bx_ref[...])
    log_a = -LRU_C * r * _softplus(-lam_ref[...])
    a_scr[...] = jnp.exp(log_a)
    u_scr[...] = jnp.sqrt(1.0 - jnp.exp(2.0 * log_a)) * (i * xc)

    def step(t, h):
        h = a_scr[pl.ds(t, 1), :] * h + u_scr[pl.ds(t, 1), :]
        u_scr[pl.ds(t, 1), :] = h
        return h

    h_ref[...] = lax.fori_loop(0, tt, step, h_ref[...], unroll=8)
    gate = gate_ref[...]
    gelu = 0.5 * gate * (1.0 + jnp.tanh(math.sqrt(2.0 / math.pi) * (gate + 0.044715 * gate * gate * gate)))
    o_ref[...] = u_scr[...] * gelu


def _rglru(p, cw, cb, wa, ba, wx, bx, lam, tt):
    bsz, t, _ = p.shape
    w = LRU_WIDTH

    def col(name):
        off, _ = EVEN_COLS[name]
        return pl.BlockSpec((None, tt, w), lambda b, i, j=off // w: (b, i, j))

    fix = lambda b, i: (0, 0)
    vec = pl.BlockSpec((1, w), fix)
    return pl.pallas_call(
        _rglru_kernel,
        grid=(bsz, t // tt),
        in_specs=[col("xb"), col("gate"), pl.BlockSpec((CONV_K, w), fix), vec,
                  pl.BlockSpec((w, w), fix), vec, pl.BlockSpec((w, w), fix), vec, vec],
        out_specs=pl.BlockSpec((None, tt, w), lambda b, i: (b, i, 0)),
        out_shape=jax.ShapeDtypeStruct((bsz, t, w), F32),
        scratch_shapes=[pltpu.VMEM((SUBLANES, w), F32), pltpu.VMEM((1, w), F32),
                        pltpu.VMEM((tt, w), F32), pltpu.VMEM((tt, w), F32)],
        compiler_params=_params("parallel", "arbitrary"),
        name="rglru",
    )(p, p, cw, cb, wa, ba, wx, bx, lam)


def _ssd_kernel(z_ref, x_ref, bm_ref, cm_ref, dt_ref, cwx_ref, cbx_ref, cwb_ref, cbb_ref, cwc_ref, cbc_ref,
                dtb_ref, alog_ref, dskip_ref, nw_ref, expand_ref, o_ref,
                tx_ref, tb_ref, tc_ref, st_ref, xs_scr, bs_scr, cs_scr):
    @pl.when(pl.program_id(1) == 0)
    def _():
        tx_ref[...] = jnp.zeros_like(tx_ref)
        tb_ref[...] = jnp.zeros_like(tb_ref)
        tc_ref[...] = jnp.zeros_like(tc_ref)
        st_ref[...] = jnp.zeros_like(st_ref)

    tt = x_ref.shape[0]

    def conv_silu(src_ref, tail_ref, cw_ref, cb_ref, dst_ref):
        x = src_ref[...]
        cw = cw_ref[...]
        y = x * cw[CONV_K - 1:CONV_K, :] + cb_ref[...]
        for s in range(1, CONV_K):
            y += _shift_rows(x, tail_ref, s) * cw[CONV_K - 1 - s:CONV_K - s, :]
        tail_ref[...] = x[tt - SUBLANES:, :]
        dst_ref[...] = _silu(y)

    conv_silu(x_ref, tx_ref, cwx_ref, cbx_ref, xs_scr)
    conv_silu(bm_ref, tb_ref, cwb_ref, cbb_ref, bs_scr)
    conv_silu(cm_ref, tc_ref, cwc_ref, cbc_ref, cs_scr)

    tri = _tri(CHUNK)
    tri_f = tri.astype(F32)
    neg_a = -jnp.exp(alog_ref[...])
    dtb = dtb_ref[...]
    expand = expand_ref[...]
    dskip = dskip_ref[...]
    nw = nw_ref[...]
    hpg = SSD_HEADS // SSD_GROUPS

    def chunk(c, carry):
        rows = pl.ds(pl.multiple_of(c * CHUNK, CHUNK), CHUNK)
        dtc = _softplus(dt_ref[rows, :] + dtb)
        acs = _mm(tri_f, dtc * neg_a, exact=True)
        acs_t = acs.T
        last = acs[CHUNK - 1:CHUNK, :]
        dt_x = _mm_01(dtc, expand)
        ea_x = _mm_01(jnp.exp(acs), expand)
        ds_x = _mm_01(jnp.exp(last - acs), expand)
        x = xs_scr[rows, :]
        xdt = x * dt_x
        xdec = xdt * ds_x
        cd_x = ea_x[CHUNK - 1:CHUNK, :]
        ys = []
        for g in range(SSD_GROUPS):
            sg = slice(g * SSD_GROUP_WIDTH, (g + 1) * SSD_GROUP_WIDTH)
            ss = slice(g * SSD_STATE, (g + 1) * SSD_STATE)
            bg = bs_scr[rows, ss]
            cg = cs_scr[rows, ss]
            cb = _mm(cg, bg, _NT)
            st = st_ref[g]
            yg = _mm(cg, st) * ea_x[:, sg]
            st_ref[g] = st * cd_x[:, sg] + _mm(bg, xdec[:, sg], _TN)
            yh = []
            for j in range(hpg):
                h = g * hpg + j
                seg = acs[:, h:h + 1] - acs_t[h:h + 1, :]
                lmat = jnp.exp(jnp.where(tri, seg, -jnp.inf))
                sh = slice(h * SSD_HEADDIM, (h + 1) * SSD_HEADDIM)
                yh.append(_mm(cb * lmat, xdt[:, sh]))
            ys.append(yg + jnp.concatenate(yh, axis=1))
        y = jnp.concatenate(ys, axis=1) + x * dskip
        y = y * _silu(z_ref[rows, :])
        outs = []
        for g in range(SSD_GROUPS):
            sg = slice(g * SSD_GROUP_WIDTH, (g + 1) * SSD_GROUP_WIDTH)
            yg = y[:, sg]
            ms = jnp.mean(yg * yg, axis=-1, keepdims=True)
            outs.append(yg * lax.rsqrt(ms + SSD_NORM_EPS))
        o_ref[rows, :] = jnp.concatenate(outs, axis=1) * nw
        return carry

    lax.fori_loop(0, tt // CHUNK, chunk, 0)


def _ssd(p, cw, cb, dtb, alog, dskip_x, nw, expand, tt):
    bsz, t, _ = p.shape

    def col(name):
        off, w = ODD_COLS[name]
        return pl.BlockSpec((None, tt, w), lambda b, i, j=off // w: (b, i, j))

    fix = lambda b, i: (0, 0)
    full = lambda a: pl.BlockSpec(a.shape, fix)
    gs = SSD_GROUPS * SSD_STATE
    cwx, cwb, cwc = cw[:, :SSD_INNER], cw[:, SSD_INNER:SSD_INNER + gs], cw[:, SSD_INNER + gs:]
    cbx, cbb, cbc = cb[:, :SSD_INNER], cb[:, SSD_INNER:SSD_INNER + gs], cb[:, SSD_INNER + gs:]
    args = (cwx, cbx, cwb, cbb, cwc, cbc, dtb, alog, dskip_x, nw, expand)
    return pl.pallas_call(
        _ssd_kernel,
        grid=(bsz, t // tt),
        in_specs=[col("z"), col("x"), col("bm"), col("cm"), col("dt")] + [full(a) for a in args],
        out_specs=pl.BlockSpec((None, tt, SSD_INNER), lambda b, i: (b, i, 0)),
        out_shape=jax.ShapeDtypeStruct((bsz, t, SSD_INNER), F32),
        scratch_shapes=[pltpu.VMEM((SUBLANES, SSD_INNER), F32), pltpu.VMEM((SUBLANES, gs), F32),
                        pltpu.VMEM((SUBLANES, gs), F32),
                        pltpu.VMEM((SSD_GROUPS, SSD_STATE, SSD_GROUP_WIDTH), F32),
                        pltpu.VMEM((tt, SSD_INNER), F32), pltpu.VMEM((tt, gs), F32), pltpu.VMEM((tt, gs), F32)],
        compiler_params=_params("parallel", "arbitrary"),
        name="ssd",
    )(p, p, p, p, p, *args)


def _rwkv_kernel(r_ref, k_ref, v_ref, wa_ref, xg_ref, mur_ref, muk_ref, muv_ref, muwa_ref, mug_ref,
                 w0_ref, w2_ref, a0_ref, a2_ref, g2_ref, kk_ref, ka_ref, rk_ref, lng_ref, lnb_ref, ones_ref,
                 o_ref, tr_ref, tk_ref, tv_ref, twa_ref, tg_ref, st_ref, y_scr,
                 r_scr, k_scr, v_scr, kk_scr, b_scr, lw_scr):
    @pl.when(pl.program_id(1) == 0)
    def _():
        for ref in (tr_ref, tk_ref, tv_ref, twa_ref, tg_ref, st_ref):
            ref[...] = jnp.zeros_like(ref)

    tt = r_ref.shape[0]
    hd = RWKV_HEADDIM

    def mix(p_ref, tail_ref, mu_ref):
        p = p_ref[...]
        prev = _shift_rows(p, tail_ref, 1)
        tail_ref[...] = p[tt - SUBLANES:, :]
        return p + (prev - p) * mu_ref[...]

    r = mix(r_ref, tr_ref, mur_ref)
    k = mix(k_ref, tk_ref, muk_ref)
    v = mix(v_ref, tv_ref, muv_ref)
    xwa = mix(wa_ref, twa_ref, muwa_ref)
    xg = mix(xg_ref, tg_ref, mug_ref)

    lane = lax.broadcasted_iota(jnp.int32, xwa.shape, 1)
    lora_in = jnp.where(lane < RWKV_DECAY_LORA, jnp.tanh(xwa), xwa)
    w_log = -_softplus(-(w0_ref[...] + _mm(lora_in, w2_ref[...], exact=True))) - 0.5
    lw = -jnp.exp(w_log)
    a_sig = _sigmoid(a0_ref[...] + _mm(lora_in, a2_ref[...], exact=True))
    gate = _mm(_sigmoid(xg), g2_ref[...], exact=True)
    ones_bd = ones_ref[...]
    kk = k * kk_ref[...]
    kk = kk / jnp.maximum(jnp.sqrt(_mm_01(kk * kk, ones_bd)), 1e-12)
    k = k * (1.0 + (a_sig - 1.0) * ka_ref[...])

    tri_incl = _tri(CHUNK)
    tri_strict = _tri(CHUNK, strict=True)
    tri_f = tri_incl.astype(F32)
    tri_col = jnp.concatenate([tri_strict, tri_incl], axis=0)
    eye = (tri_incl & ~tri_strict).astype(F32)
    for ref, val in ((r_scr, r), (k_scr, k), (v_scr, v), (kk_scr, kk), (b_scr, kk * a_sig), (lw_scr, lw)):
        ref[...] = val

    def chunk(c, carry):
        rows = pl.ds(pl.multiple_of(c * CHUNK, CHUNK), CHUNK)
        lwc = lw_scr[rows, :]
        kc = k_scr[rows, :]
        bc = b_scr[rows, :]
        vc = v_scr[rows, :]
        cum = _mm(tri_f, lwc, exact=True)
        last = cum[CHUNK - 1:CHUNK, :]
        at = -kk_scr[rows, :] * jnp.exp(cum - lwc)
        rt = r_scr[rows, :] * jnp.exp(cum)
        e_neg = jnp.exp(-cum)
        bt = bc * e_neg
        kt = kc * e_neg
        e_last = jnp.exp(last - cum)
        bh = bc * e_last
        kh = kc * e_last
        wc = jnp.exp(last)
        heads = range(RWKV_HEADS)
        sls = [slice(h * hd, (h + 1) * hd) for h in heads]
        amat = [_mm(jnp.concatenate([at[:, sl], rt[:, sl]], axis=0),
                    jnp.concatenate([bt[:, sl], kt[:, sl]], axis=0), _NT) for sl in sls]
        a_ab = [jnp.where(tri_strict, m[:CHUNK, :CHUNK], 0.0) for m in amat]
        a_rb = [jnp.where(tri_incl, m[CHUNK:, :CHUNK], 0.0) for m in amat]
        a_xk = [jnp.where(tri_col, m[:, CHUNK:], 0.0) for m in amat]
        avrv = [_mm(a_xk[h], vc[:, sls[h]]) for h in heads]
        inv = [eye + m for m in a_ab]
        pw = a_ab
        for _ in range(int(math.log2(CHUNK)) - 1):
            pw = [_mm(m, m) for m in pw]
            inv = [inv[h] + _mm(inv[h], pw[h]) for h in heads]
        tu = [_mm(inv[h], jnp.concatenate([at[:, sls[h]], avrv[h][:CHUNK]], axis=1)) for h in heads]
        qy = [_mm(a_rb[h], tu[h]) + jnp.concatenate([rt[:, sls[h]], avrv[h][CHUNK:]], axis=1)
              for h in heads]
        zb = [_mm(tu[h], bh[:, sls[h]], _TN) for h in heads]
        vk = [_mm(vc[:, sls[h]], kh[:, sls[h]], _TN) for h in heads]
        for h in heads:
            st = st_ref[h]
            y_scr[rows, sls[h]] = _mm(qy[h][:, :hd], st, _NT) + qy[h][:, hd:]
            st_ref[h] = st * wc[:, sls[h]] + _mm(st, zb[h][:hd]) + zb[h][hd:] + vk[h]
        return carry

    lax.fori_loop(0, tt // CHUNK, chunk, 0)

    y = y_scr[...]
    inv_n = 1.0 / hd
    mu_y = _mm_01(y, ones_bd) * inv_n
    dy = y - mu_y
    var_y = _mm_01(dy * dy, ones_bd) * inv_n
    yn = dy * lax.rsqrt(var_y + RWKV_GN_EPS) * lng_ref[...] + lnb_ref[...]
    bonus = _mm_01(r * k * rk_ref[...], ones_bd) * v
    o_ref[...] = (yn + bonus) * gate


def _rwkv(p, mu, w0, w2p, a0, a2p, g2, k_k, k_a, r_k, ln_g, ln_b, ones_bd, tt):
    bsz, t, _ = p.shape
    w = RWKV_WIDTH

    def col(name):
        off, wd = ODD_COLS[name]
        return pl.BlockSpec((None, tt, wd), lambda b, i, j=off // wd: (b, i, j))

    fix = lambda b, i: (0, 0)
    full = lambda a: pl.BlockSpec(a.shape, fix)
    mur, muk, muv = mu[:, :w], mu[:, w:2 * w], mu[:, 2 * w:3 * w]
    muwa, mug = mu[:, 3 * w:3 * w + LANES], mu[:, 3 * w + LANES:]
    args = (mur, muk, muv, muwa, mug, w0, w2p, a0, a2p, g2, k_k, k_a, r_k, ln_g, ln_b, ones_bd)
    return pl.pallas_call(
        _rwkv_kernel,
        grid=(bsz, t // tt),
        in_specs=[col("r"), col("k"), col("v"), col("wa"), col("xg")] + [full(a) for a in args],
        out_specs=pl.BlockSpec((None, tt, w), lambda b, i: (b, i, 0)),
        out_shape=jax.ShapeDtypeStruct((bsz, t, w), F32),
        scratch_shapes=[pltpu.VMEM((SUBLANES, w), F32), pltpu.VMEM((SUBLANES, w), F32),
                        pltpu.VMEM((SUBLANES, w), F32), pltpu.VMEM((SUBLANES, LANES), F32),
                        pltpu.VMEM((SUBLANES, LANES), F32),
                        pltpu.VMEM((RWKV_HEADS, RWKV_HEADDIM, RWKV_HEADDIM), F32)]
        + [pltpu.VMEM((tt, w), F32)] * 7,
        compiler_params=_params("parallel", "arbitrary"),
        name="rwkv7",
    )(p, p, p, p, p, *args)


def _reorder_cols(w, pieces, order, width):
    out = jnp.zeros((w.shape[0], width), w.dtype)
    for name, (off, _) in order.items():
        start, size = pieces[name]
        out = out.at[:, off:off + size].set(w[:, start:start + size])
    return out


def _block_diag(w):
    n, i, j = w.shape
    eye = jnp.eye(n, dtype=w.dtype)
    return (eye[:, None, :, None] * w[:, :, None, :]).reshape(n * i, n * j)


def _row(v, width=None):
    v = v.reshape(1, -1).astype(F32)
    if width is not None and v.shape[1] < width:
        v = jnp.pad(v, ((0, 0), (0, width - v.shape[1])))
    return v


def _even_layer(x, x_in, w_in, gk_w2, gk_b, gla_norm, conv_w, conv_b, wa, ba, wx, bx, lam,
                w_out, ln1_g, ln1_b, f_w1, f_w3, f_w2, ln2_g, ln2_b, bsz, t):
    qk, vw, lw = GLA_QK, GLA_VW, LRU_WIDTH
    pieces = dict(q=(0, qk), k=(qk, qk), v=(2 * qk, vw), g=(2 * qk + vw, vw), gk=(2 * qk + 2 * vw, GLA_GK_RANK),
                  xb=(2 * qk + 2 * vw + GLA_GK_RANK, lw), gate=(2 * qk + 2 * vw + GLA_GK_RANK + lw, lw))
    w_in_r = _reorder_cols(w_in, pieces, EVEN_COLS, EVEN_WIDTH).astype(BF16)
    p = _proj(x_in, w_in_r, tm=min(1024, x.shape[0]), tn=896).reshape(bsz, t, EVEN_WIDTH)
    tt = min(512, t)
    gkw = jnp.pad(gk_w2, ((0, LANES - GLA_GK_RANK), (0, 0)))
    y_a = _gla(p, gkw, _row(gk_b), _row(gla_norm), tt)
    y_b = _rglru(p, conv_w, _row(conv_b), _block_diag(wa).astype(BF16), _row(ba), _block_diag(wx).astype(BF16),
                 _row(bx), _row(lam), tt)
    n = bsz * t
    tm = min(512, n)
    x1, x1b = _outproj_ln(y_a.reshape(n, vw), y_b.reshape(n, lw), w_out[:vw].astype(BF16), w_out[vw:].astype(BF16),
                          x, _row(ln1_g), _row(ln1_b), tm)
    ones = jnp.ones((n, LANES), F32)
    return _ffn_ln(x1b, f_w1[None].astype(BF16), f_w3[None].astype(BF16), f_w2[None].astype(BF16), ones, x1,
                   _row(ln2_g), _row(ln2_b), tm=min(1024, n), tf=256, gated=False)


def _odd_layer(x, xb, w_in, conv_w, conv_b, dt_bias, a_log, d_skip, ssd_norm, mu, w0, w2, a0, a2, g2, k_k, k_a,
               r_k, rln_g, rln_b, w_out, ln1_g, ln1_b, router, ew1, ew3, ew2, ln2_g, ln2_b, bsz, t):
    si, gs, rw = SSD_INNER, SSD_GROUPS * SSD_STATE, RWKV_WIDTH
    o = 2 * si + 2 * gs + SSD_HEADS
    pieces = dict(z=(0, si), x=(si, si), bm=(2 * si, gs), cm=(2 * si + gs, gs), dt=(2 * si + 2 * gs, SSD_HEADS),
                  r=(o, rw), k=(o + rw, rw), v=(o + 2 * rw, rw),
                  wa=(o + 3 * rw, RWKV_DECAY_LORA + RWKV_AAA_LORA), xg=(o + 3 * rw + LANES, RWKV_GATE_LORA))
    w_in_r = _reorder_cols(w_in, pieces, ODD_COLS, ODD_WIDTH).astype(BF16)
    n = bsz * t
    p = _proj(xb, w_in_r, tm=min(1024, n), tn=896).reshape(bsz, t, ODD_WIDTH)
    expand = jnp.pad(jnp.repeat(jnp.eye(SSD_HEADS, dtype=BF16), SSD_HEADDIM, axis=1), ((0, LANES - SSD_HEADS), (0, 0)))
    y_c = _ssd(p, conv_w, _row(conv_b), _row(dt_bias, LANES), _row(a_log, LANES),
               _row(jnp.repeat(d_skip, SSD_HEADDIM)), _row(ssd_norm), expand, tt=min(256, t))
    zeros = jnp.zeros((RWKV_DECAY_LORA, rw), F32)
    ones_bd = _block_diag(jnp.ones((RWKV_HEADS, RWKV_HEADDIM, RWKV_HEADDIM), BF16))
    y_d = _rwkv(p, _row(mu), _row(w0), jnp.concatenate([w2, zeros]), _row(a0), jnp.concatenate([zeros, a2]), g2,
                _row(k_k), _row(k_a), _row(r_k), _row(rln_g), _row(rln_b), ones_bd, tt=min(256, t))
    tm = min(512, n)
    x1, x1b = _outproj_ln(y_c.reshape(n, si), y_d.reshape(n, rw), w_out[:si].astype(BF16), w_out[si:].astype(BF16),
                          x, _row(ln1_g), _row(ln1_b), tm)
    gates = _router(x1, jnp.pad(router, ((0, 0), (0, LANES - N_EXPERTS))), tm)
    return _ffn_ln(x1b, ew1.astype(BF16), ew3.astype(BF16), ew2.astype(BF16), gates, x1,
                   _row(ln2_g), _row(ln2_b), tm=min(1024, n), tf=512, gated=True)


def kernel(x, e_w_in, e_gk_w2, e_gk_b, e_gla_norm, e_conv_w, e_conv_b, e_lru_wa, e_lru_ba, e_lru_wx, e_lru_bx, e_lru_lambda, e_w_out, e_ln1_g, e_ln1_b, e_ffn_w1, e_ffn_w3, e_ffn_w2, e_ln2_g, e_ln2_b, o_w_in, o_conv_w, o_conv_b, o_dt_bias, o_a_log, o_d_skip, o_ssd_norm, o_rwkv_mu, o_rwkv_w0, o_rwkv_w2, o_rwkv_a0, o_rwkv_a2, o_rwkv_g2, o_rwkv_k_k, o_rwkv_k_a, o_rwkv_r_k, o_rwkv_ln_g, o_rwkv_ln_b, o_w_out, o_ln1_g, o_ln1_b, o_router, o_exp_w1, o_exp_w3, o_exp_w2, o_ln2_g, o_ln2_b):
    bsz, t, d = x.shape
    h = x.reshape(bsz * t, d)
    hb = h
    for i in range(DEPTH):
        j = i // 2
        if i % 2 == 0:
            h, hb = _even_layer(h, hb, e_w_in[j], e_gk_w2[j], e_gk_b[j], e_gla_norm[j], e_conv_w[j], e_conv_b[j],
                                e_lru_wa[j], e_lru_ba[j], e_lru_wx[j], e_lru_bx[j], e_lru_lambda[j], e_w_out[j],
                                e_ln1_g[j], e_ln1_b[j], e_ffn_w1[j], e_ffn_w3[j], e_ffn_w2[j], e_ln2_g[j],
                                e_ln2_b[j], bsz, t)
        else:
            h, hb = _odd_layer(h, hb, o_w_in[j], o_conv_w[j], o_conv_b[j], o_dt_bias[j], o_a_log[j], o_d_skip[j],
                               o_ssd_norm[j], o_rwkv_mu[j], o_rwkv_w0[j], o_rwkv_w2[j], o_rwkv_a0[j],
                               o_rwkv_a2[j], o_rwkv_g2[j], o_rwkv_k_k[j], o_rwkv_k_a[j], o_rwkv_r_k[j],
                               o_rwkv_ln_g[j], o_rwkv_ln_b[j], o_w_out[j], o_ln1_g[j], o_ln1_b[j], o_router[j],
                               o_exp_w1[j], o_exp_w3[j], o_exp_w2[j], o_ln2_g[j], o_ln2_b[j], bsz, t)
    return h.reshape(bsz, t, d)
```

```python
import functools
import math

import jax
import jax.numpy as jnp
from jax import lax
from jax.experimental import pallas as pl
from jax.experimental.pallas import tpu as pltpu

F32 = jnp.float32
BF16 = jnp.bfloat16
HIGHEST = lax.Precision.HIGHEST

D_MODEL = 1024
DEPTH = 2
DEEPNORM_ALPHA = (2 * DEPTH) ** 0.25
LN_EPS = 1e-5
CONV_K = 4
CHUNK = 64

GLA_HEADS = 4
GLA_DK = 64
GLA_DV = 128
GLA_QK = GLA_HEADS * GLA_DK
GLA_VW = GLA_HEADS * GLA_DV
GLA_GK_RANK = 16
GLA_GATE_NORM = 16.0
GLA_NORM_EPS = 1e-5

LRU_WIDTH = 512
LRU_BLOCKS = 8
LRU_C = 8.0

SSD_HEADS = 16
SSD_HEADDIM = 64
SSD_INNER = SSD_HEADS * SSD_HEADDIM
SSD_GROUPS = 2
SSD_STATE = 128
SSD_GROUP_WIDTH = SSD_INNER // SSD_GROUPS
SSD_NORM_EPS = 1e-5

RWKV_HEADS = 8
RWKV_HEADDIM = 64
RWKV_WIDTH = RWKV_HEADS * RWKV_HEADDIM
RWKV_DECAY_LORA = 64
RWKV_AAA_LORA = 64
RWKV_GATE_LORA = 128
RWKV_GN_EPS = 64e-5

N_EXPERTS = 8
LANES = 128
SUBLANES = 8
assert N_EXPERTS == SUBLANES
MOE_TB = 256
MOE_TR = 256
MOE_TE = 512
assert MOE_TE % MOE_TR == 0
VMEM_LIMIT = 56 * 1024 * 1024

EVEN_COLS = dict(v=(0, 512), g=(512, 512), xb=(1024, 512), gate=(1536, 512), q=(2048, 256), k=(2304, 256),
                 gk=(2560, 128))
EVEN_WIDTH = 2688
ODD_COLS = dict(z=(0, 1024), x=(1024, 1024), r=(2048, 512), k=(2560, 512), v=(3072, 512), bm=(3584, 256),
                cm=(3840, 256), wa=(4096, 128), xg=(4224, 128), dt=(4352, 128))
ODD_WIDTH = 4480


def _mm(a, b, dims=((1,), (0,)), exact=False):
    dn = (dims, ((), ()))
    if exact:
        return lax.dot_general(a.astype(F32), b.astype(F32), dn, precision=HIGHEST, preferred_element_type=F32)
    return lax.dot_general(a.astype(BF16), b.astype(BF16), dn, preferred_element_type=F32)


_NT = ((1,), (1,))
_TN = ((0,), (0,))


def _mm_01(x, sel):
    hi = x.astype(BF16)
    lo = (x - hi.astype(F32)).astype(BF16)
    return (jnp.dot(hi, sel, preferred_element_type=F32) + jnp.dot(lo, sel, preferred_element_type=F32))


def _sigmoid(x):
    return 1.0 / (1.0 + jnp.exp(-x))


def _softplus(x):
    return jnp.maximum(x, 0.0) + jnp.log(1.0 + jnp.exp(-jnp.abs(x)))


def _silu(x):
    return x * _sigmoid(x)


def _layer_norm(h, g, b):
    mu = jnp.mean(h, axis=-1, keepdims=True)
    d = h - mu
    var = jnp.mean(d * d, axis=-1, keepdims=True)
    return d * lax.rsqrt(var + LN_EPS) * g + b


def _params(*sem):
    return pltpu.CompilerParams(dimension_semantics=sem, vmem_limit_bytes=VMEM_LIMIT)


def _tri(n, strict=False):
    r = lax.broadcasted_iota(jnp.int32, (n, n), 0)
    c = lax.broadcasted_iota(jnp.int32, (n, n), 1)
    return (r > c) if strict else (r >= c)


def _shift_rows(p, tail_ref, s):
    n = p.shape[0]
    rolled = pltpu.roll(p, s, axis=0)
    head = pltpu.roll(tail_ref[...], s, axis=0)
    row = lax.broadcasted_iota(jnp.int32, (SUBLANES, p.shape[1]), 0)
    fixed = jnp.where(row < s, head, rolled[:SUBLANES])
    return jnp.concatenate([fixed, rolled[SUBLANES:]], axis=0) if n > SUBLANES else fixed


def _proj_kernel(x_ref, w_ref, o_ref):
    o_ref[...] = jnp.dot(x_ref[...].astype(BF16), w_ref[...], preferred_element_type=F32)


def _proj(x, w, tm, tn):
    m, k = x.shape
    n = w.shape[1]
    return pl.pallas_call(
        _proj_kernel,
        grid=(m // tm, n // tn),
        in_specs=[pl.BlockSpec((tm, k), lambda i, j: (i, 0)), pl.BlockSpec((k, tn), lambda i, j: (0, j))],
        out_specs=pl.BlockSpec((tm, tn), lambda i, j: (i, j)),
        out_shape=jax.ShapeDtypeStruct((m, n), F32),
        compiler_params=_params("parallel", "parallel"),
        name="in_proj",
    )(x, w)


def _outproj_ln_kernel(ya_ref, yb_ref, wa_ref, wb_ref, res_ref, g_ref, b_ref, o_ref, ob_ref):
    acc = jnp.dot(ya_ref[...].astype(BF16), wa_ref[...], preferred_element_type=F32)
    acc += jnp.dot(yb_ref[...].astype(BF16), wb_ref[...], preferred_element_type=F32)
    y = _layer_norm(DEEPNORM_ALPHA * res_ref[...] + acc, g_ref[...], b_ref[...])
    o_ref[...] = y
    ob_ref[...] = y.astype(BF16)


def _outproj_ln(ya, yb, wa, wb, res, g, b, tm):
    m = ya.shape[0]
    d = res.shape[1]
    row = lambda i: (i, 0)
    fix = lambda i: (0, 0)
    return pl.pallas_call(
        _outproj_ln_kernel,
        grid=(m // tm,),
        in_specs=[pl.BlockSpec((tm, ya.shape[1]), row), pl.BlockSpec((tm, yb.shape[1]), row),
                  pl.BlockSpec(wa.shape, fix), pl.BlockSpec(wb.shape, fix), pl.BlockSpec((tm, d), row),
                  pl.BlockSpec((1, d), fix), pl.BlockSpec((1, d), fix)],
        out_specs=[pl.BlockSpec((tm, d), row), pl.BlockSpec((tm, d), row)],
        out_shape=[jax.ShapeDtypeStruct((m, d), F32), jax.ShapeDtypeStruct((m, d), BF16)],
        compiler_params=_params("parallel"),
        name="out_proj_ln",
    )(ya, yb, wa, wb, res, g, b)


def _ffn_kernel(xb_ref, w1_ref, w3_ref, w2_ref, res_ref, g_ref, b_ref, o_ref, ob_ref, acc_ref):
    f = pl.program_id(1)

    @pl.when(f == 0)
    def _():
        acc_ref[...] = jnp.zeros_like(acc_ref)

    x = xb_ref[...]
    h1 = jnp.dot(x, w1_ref[...], preferred_element_type=F32)
    h3 = jnp.dot(x, w3_ref[...], preferred_element_type=F32)
    acc_ref[...] += jnp.dot((_silu(h1) * h3).astype(BF16), w2_ref[...], preferred_element_type=F32)

    @pl.when(f == pl.num_programs(1) - 1)
    def _():
        y = _layer_norm(DEEPNORM_ALPHA * res_ref[...] + acc_ref[...], g_ref[...], b_ref[...])
        o_ref[...] = y
        ob_ref[...] = y.astype(BF16)


def _ffn_ln(xb, w1, w3, w2, res, g, b, tm, tf):
    m, d = xb.shape
    fdim = w1.shape[1]
    row = lambda i, f: (i, 0)
    fix = lambda i, f: (0, 0)
    return pl.pallas_call(
        _ffn_kernel,
        grid=(m // tm, fdim // tf),
        in_specs=[pl.BlockSpec((tm, d), row),
                  pl.BlockSpec((d, tf), lambda i, f: (0, f)),
                  pl.BlockSpec((d, tf), lambda i, f: (0, f)),
                  pl.BlockSpec((tf, d), lambda i, f: (f, 0)),
                  pl.BlockSpec((tm, d), row), pl.BlockSpec((1, d), fix), pl.BlockSpec((1, d), fix)],
        out_specs=[pl.BlockSpec((tm, d), row), pl.BlockSpec((tm, d), row)],
        out_shape=[jax.ShapeDtypeStruct((m, d), F32), jax.ShapeDtypeStruct((m, d), BF16)],
        scratch_shapes=[pltpu.VMEM((tm, d), F32)],
        compiler_params=_params("parallel", "arbitrary"),
        name="ffn_ln",
    )(xb, w1, w3, w2, res, g, b)


def _router_kernel(x_ref, w_ref, gates_ref, rank_ref, rank_t_ref, cnt_ref, carry_ref):
    @pl.when(pl.program_id(0) == 0)
    def _():
        carry_ref[...] = jnp.zeros_like(carry_ref)

    logits = _mm(x_ref[...], w_ref[...], exact=True)
    lane = lax.broadcasted_iota(jnp.int32, logits.shape, 1)
    neg = jnp.float32(-jnp.inf)
    l1 = jnp.where(lane < N_EXPERTS, logits, neg)
    m1 = jnp.max(l1, axis=1, keepdims=True)
    i1 = jnp.min(jnp.where(l1 == m1, lane, LANES), axis=1, keepdims=True)
    l2 = jnp.where(lane == i1, neg, l1)
    m2 = jnp.max(l2, axis=1, keepdims=True)
    i2 = jnp.min(jnp.where(l2 == m2, lane, LANES), axis=1, keepdims=True)
    ex = jnp.exp(m2 - m1)
    w_top = 1.0 / (1.0 + ex)
    gates_ref[...] = jnp.where(lane == i1, w_top, 0.0) + jnp.where(lane == i2, ex * w_top, 0.0)

    sel = jnp.where(lane == i1, 1.0, 0.0) + jnp.where(lane == i2, 1.0, 0.0)
    tb = sel.shape[0]
    before = _mm(_tri(tb, strict=True).astype(F32), sel)
    carry = carry_ref[...]
    rank = jnp.where(sel > 0.0, carry + before, -1.0)
    rank_ref[...] = rank
    rank_t_ref[...] = rank.T[:SUBLANES, :]
    carry = carry + jnp.sum(sel, axis=0, keepdims=True)
    carry_ref[...] = carry
    cnt_ref[...] = jnp.broadcast_to(carry, cnt_ref.shape)


def _router(x, w):
    m, d = x.shape
    nblk = m // MOE_TB
    return pl.pallas_call(
        _router_kernel,
        grid=(nblk,),
        in_specs=[pl.BlockSpec((MOE_TB, d), lambda i: (i, 0)), pl.BlockSpec((d, LANES), lambda i: (0, 0))],
        out_specs=[pl.BlockSpec((MOE_TB, LANES), lambda i: (i, 0)), pl.BlockSpec((MOE_TB, LANES), lambda i: (i, 0)),
                   pl.BlockSpec((SUBLANES, MOE_TB), lambda i: (0, i)), pl.BlockSpec((SUBLANES, LANES), lambda i: (i, 0))],
        out_shape=[jax.ShapeDtypeStruct((m, LANES), F32), jax.ShapeDtypeStruct((m, LANES), F32),
                   jax.ShapeDtypeStruct((SUBLANES, m), F32), jax.ShapeDtypeStruct((nblk * SUBLANES, LANES), F32)],
        scratch_shapes=[pltpu.VMEM((1, LANES), F32)],
        compiler_params=_params("arbitrary"),
        name="router",
    )(x, w)


def _moe_plan(cnt, n_tok):
    i32 = jnp.int32
    nblk = n_tok // MOE_TB
    c_inc = cnt.reshape(nblk, SUBLANES, LANES)[:, 0, :N_EXPERTS].astype(i32)
    c_exc = jnp.concatenate([jnp.zeros((1, N_EXPERTS), i32), c_inc[:-1]], axis=0)
    gsz = (c_inc[-1] + MOE_TE - 1) // MOE_TE * MOE_TE
    gend = jnp.cumsum(gsz)
    start = gend - gsz
    n_et = _moe_rows(n_tok) // MOE_TE
    n_act = gend[-1] // MOE_TE
    et = jnp.minimum(jnp.arange(n_et, dtype=i32), n_act - 1)
    et_expert = jnp.sum((gend[None, :] <= (et * MOE_TE)[:, None]).astype(i32), axis=1)
    et_active = (jnp.arange(n_et, dtype=i32) < n_act).astype(i32)

    lo = start[None, :] + c_exc
    hi = start[None, :] + c_inc
    first_tile = lo // MOE_TR
    n_items = jnp.where(hi > lo, (hi - 1) // MOE_TR - first_tile + 1, 0)
    wmax = _moe_rows(n_tok) // MOE_TR + N_EXPERTS * nblk
    blk_id = jnp.broadcast_to(jnp.arange(nblk, dtype=i32)[:, None], (nblk, N_EXPERTS))
    exp_id = jnp.broadcast_to(jnp.arange(N_EXPERTS, dtype=i32)[None, :], (nblk, N_EXPERTS))

    def worklist(expert_major):
        flat = (lambda a: a.T.reshape(-1)) if expert_major else (lambda a: a.reshape(-1))
        n, ft, blk, exp = flat(n_items), flat(first_tile), flat(blk_id), flat(exp_id)
        inc = jnp.cumsum(n)
        total = inc[-1]
        w = jnp.arange(wmax, dtype=i32)
        wc = jnp.minimum(w, total - 1)
        idx = jnp.sum((inc[None, :] <= wc[:, None]).astype(i32), axis=1)
        tile = ft[idx] + wc - (inc - n)[idx]
        valid = w < total
        key = tile if expert_major else blk[idx]
        first = valid & ((w == 0) | (key != jnp.roll(key, 1)))
        last = valid & ((w == total - 1) | (key != jnp.roll(key, -1)))
        return tile, blk[idx], exp[idx], valid.astype(i32), first.astype(i32), last.astype(i32)

    return start, (et, et_expert, et_active), worklist(True), worklist(False)


def _moe_rows(n_tok):
    return 2 * n_tok + N_EXPERTS * MOE_TE


def _one_hot_rows(dest, tile):
    row = lax.broadcasted_iota(jnp.int32, (MOE_TR, MOE_TB), 0) + tile * MOE_TR
    return jnp.where(dest == row.astype(F32), 1.0, 0.0).astype(BF16)


def _dispatch_kernel(tile_ref, blk_ref, exp_ref, valid_ref, first_ref, x_ref, gates_ref, dest_ref, xs_ref, gs_ref):
    w = pl.program_id(0)

    @pl.when(valid_ref[w] == 1)
    def _():
        onehot = _one_hot_rows(dest_ref[pl.ds(exp_ref[w], 1), :], tile_ref[w])
        xg = jnp.dot(onehot, x_ref[...], preferred_element_type=F32).astype(BF16)
        gg = _mm_01_left(onehot, gates_ref[...])

        @pl.when(first_ref[w] == 1)
        def _():
            xs_ref[...] = xg
            gs_ref[...] = gg

        @pl.when(first_ref[w] == 0)
        def _():
            xs_ref[...] += xg
            gs_ref[...] += gg


def _mm_01_left(sel, x):
    hi = x.astype(BF16)
    lo = (x - hi.astype(F32)).astype(BF16)
    return jnp.dot(sel, hi, preferred_element_type=F32) + jnp.dot(sel, lo, preferred_element_type=F32)


def _dispatch(plan, xb, gates, dest_t):
    tile, blk, exp, valid, first, _ = plan
    n, d = xb.shape
    rows = _moe_rows(n)
    grid_spec = pltpu.PrefetchScalarGridSpec(
        num_scalar_prefetch=5,
        grid=(tile.shape[0],),
        in_specs=[pl.BlockSpec((MOE_TB, d), lambda w, t, b, e, v, f: (b[w], 0)),
                  pl.BlockSpec((MOE_TB, LANES), lambda w, t, b, e, v, f: (b[w], 0)),
                  pl.BlockSpec((SUBLANES, MOE_TB), lambda w, t, b, e, v, f: (0, b[w]))],
        out_specs=[pl.BlockSpec((MOE_TR, d), lambda w, t, b, e, v, f: (t[w], 0)),
                   pl.BlockSpec((MOE_TR, LANES), lambda w, t, b, e, v, f: (t[w], 0))],
    )
    return pl.pallas_call(
        _dispatch_kernel,
        grid_spec=grid_spec,
        out_shape=[jax.ShapeDtypeStruct((rows, d), BF16), jax.ShapeDtypeStruct((rows, LANES), F32)],
        compiler_params=_params("arbitrary"),
        name="moe_dispatch",
    )(tile, blk, exp, valid, first, xb, gates, dest_t)


def _expert_kernel(et_ref, ee_ref, ea_ref, xs_ref, w1_ref, w3_ref, w2_ref, gs_ref, ys_ref, acc_ref):
    j = pl.program_id(0)
    f = pl.program_id(1)

    @pl.when(ea_ref[j] == 1)
    def _():
        @pl.when(f == 0)
        def _():
            acc_ref[...] = jnp.zeros_like(acc_ref)

        x = xs_ref[...]
        h1 = jnp.dot(x, w1_ref[...], preferred_element_type=F32)
        h3 = jnp.dot(x, w3_ref[...], preferred_element_type=F32)
        gs = gs_ref[...]
        lane = lax.broadcasted_iota(jnp.int32, gs.shape, 1)
        gate = jnp.sum(jnp.where(lane == ee_ref[j], gs, 0.0), axis=1, keepdims=True)
        acc_ref[...] += jnp.dot((_silu(h1) * h3 * gate).astype(BF16), w2_ref[...], preferred_element_type=F32)

        @pl.when(f == pl.num_programs(1) - 1)
        def _():
            ys_ref[...] = acc_ref[...].astype(BF16)


def _experts(plan, xs, gs, w1, w3, w2, tf):
    et, ee, ea = plan
    rows, d = xs.shape
    nf = w1.shape[2] // tf
    fidx = lambda f, a, j: f * a[j] + (nf - 1) * (1 - a[j])
    grid_spec = pltpu.PrefetchScalarGridSpec(
        num_scalar_prefetch=3,
        grid=(et.shape[0], nf),
        in_specs=[pl.BlockSpec((MOE_TE, d), lambda j, f, t, e, a: (t[j], 0)),
                  pl.BlockSpec((None, d, tf), lambda j, f, t, e, a: (e[j], 0, fidx(f, a, j))),
                  pl.BlockSpec((None, d, tf), lambda j, f, t, e, a: (e[j], 0, fidx(f, a, j))),
                  pl.BlockSpec((None, tf, d), lambda j, f, t, e, a: (e[j], fidx(f, a, j), 0)),
                  pl.BlockSpec((MOE_TE, LANES), lambda j, f, t, e, a: (t[j], 0))],
        out_specs=pl.BlockSpec((MOE_TE, d), lambda j, f, t, e, a: (t[j], 0)),
        scratch_shapes=[pltpu.VMEM((MOE_TE, d), F32)],
    )
    return pl.pallas_call(
        _expert_kernel,
        grid_spec=grid_spec,
        out_shape=jax.ShapeDtypeStruct((rows, d), BF16),
        compiler_params=_params("arbitrary", "arbitrary"),
        name="moe_experts",
    )(et, ee, ea, xs, w1, w3, w2, gs)


def _combine_kernel(tile_ref, blk_ref, exp_ref, valid_ref, first_ref, last_ref, ys_ref, dest_ref, res_ref, g_ref,
                    b_ref, o_ref, acc_ref):
    w = pl.program_id(0)

    @pl.when(valid_ref[w] == 1)
    def _():
        dest = dest_ref[...]
        lane = lax.broadcasted_iota(jnp.int32, dest.shape, 1)
        dcol = jnp.sum(jnp.where(lane == exp_ref[w], dest, 0.0), axis=1, keepdims=True)
        col = lax.broadcasted_iota(jnp.int32, (MOE_TB, MOE_TR), 1) + tile_ref[w] * MOE_TR
        onehot = jnp.where(dcol == col.astype(F32), 1.0, 0.0).astype(BF16)
        part = jnp.dot(onehot, ys_ref[...], preferred_element_type=F32)

        @pl.when(first_ref[w] == 1)
        def _():
            acc_ref[...] = part

        @pl.when(first_ref[w] == 0)
        def _():
            acc_ref[...] += part

        @pl.when(last_ref[w] == 1)
        def _():
            o_ref[...] = _layer_norm(DEEPNORM_ALPHA * res_ref[...] + acc_ref[...], g_ref[...], b_ref[...])


def _combine_ln(plan, ys, dest, res, g, b):
    tile, blk, exp, valid, first, last = plan
    n, d = res.shape
    tok = lambda w, t, b_, e, v, f, l: (b_[w], 0)
    fix = lambda w, t, b_, e, v, f, l: (0, 0)
    grid_spec = pltpu.PrefetchScalarGridSpec(
        num_scalar_prefetch=6,
        grid=(tile.shape[0],),
        in_specs=[pl.BlockSpec((MOE_TR, d), lambda w, t, b_, e, v, f, l: (t[w], 0)),
                  pl.BlockSpec((MOE_TB, LANES), tok), pl.BlockSpec((MOE_TB, d), tok),
                  pl.BlockSpec((1, d), fix), pl.BlockSpec((1, d), fix)],
        out_specs=pl.BlockSpec((MOE_TB, d), tok),
        scratch_shapes=[pltpu.VMEM((MOE_TB, d), F32)],
    )
    return pl.pallas_call(
        _combine_kernel,
        grid_spec=grid_spec,
        out_shape=jax.ShapeDtypeStruct((n, d), F32),
        compiler_params=_params("arbitrary"),
        name="moe_combine_ln",
    )(tile, blk, exp, valid, first, last, ys, dest, res, g, b)


def _moe_ln(x, xb, router_w, w1, w3, w2, g, b):
    n = x.shape[0]
    gates, rank, rank_t, cnt = _router(x, router_w)
    start, expert_tiles, by_tile, by_block = _moe_plan(cnt, n)
    startf = start.astype(F32)
    dest = jnp.where(rank >= 0.0, rank + jnp.pad(startf, (0, LANES - N_EXPERTS))[None, :], -1.0)
    dest_t = jnp.where(rank_t >= 0.0, rank_t + startf[:, None], -1.0)
    xs, gs = _dispatch(by_tile, xb, gates, dest_t)
    ys = _experts(expert_tiles, xs, gs, w1, w3, w2, tf=512)
    return _combine_ln(by_block, ys, dest, x, g, b)


def _gla_kernel(q_ref, k_ref, v_ref, g_ref, gk_ref, gkw_ref, gkb_ref, nw_ref, o_ref, st_ref):
    @pl.when(pl.program_id(1) == 0)
    def _():
        st_ref[...] = jnp.zeros_like(st_ref)

    tt = q_ref.shape[0]
    tri = _tri(CHUNK)
    tri_f = tri.astype(F32)
    gkw = gkw_ref[...]
    gkb = gkb_ref[...]
    nw = nw_ref[...]

    def chunk(c, carry):
        rows = pl.ds(pl.multiple_of(c * CHUNK, CHUNK), CHUNK)
        pre = _mm(gk_ref[rows, :], gkw, exact=True) + gkb
        gk = (jnp.minimum(pre, 0.0) - jnp.log(1.0 + jnp.exp(-jnp.abs(pre)))) * (1.0 / GLA_GATE_NORM)
        bc = _mm(tri_f, gk, exact=True)
        bl = bc[CHUNK - 1:CHUNK, :]
        k = k_ref[rows, :]
        qd = q_ref[rows, :] * (GLA_DK ** -0.5) * jnp.exp(bc)
        kd = k * jnp.exp(-bc)
        kl = k * jnp.exp(bl - bc)
        dec = jnp.exp(bl)
        heads = range(GLA_HEADS)
        sks = [slice(h * GLA_DK, (h + 1) * GLA_DK) for h in heads]
        svs = [slice(h * GLA_DV, (h + 1) * GLA_DV) for h in heads]
        att = [jnp.where(tri, _mm(qd[:, sk], kd[:, sk], _NT), 0.0) for sk in sks]
        vs = [v_ref[rows, sv] for sv in svs]
        sts = [st_ref[h] for h in heads]
        os_ = [_mm(att[h], vs[h]) + _mm(qd[:, sks[h]], sts[h], _NT) for h in heads]
        for h in heads:
            st_ref[h] = sts[h] * dec[:, sks[h]] + _mm(vs[h], kl[:, sks[h]], _TN)
        for h in heads:
            o = os_[h]
            ms = jnp.mean(o * o, axis=-1, keepdims=True)
            o_ref[rows, svs[h]] = o * lax.rsqrt(ms + GLA_NORM_EPS) * nw * _silu(g_ref[rows, svs[h]])
        return carry

    lax.fori_loop(0, tt // CHUNK, chunk, 0)


def _gla(p, gkw, gkb, nw, tt):
    bsz, t, _ = p.shape

    def col(name):
        off, w = EVEN_COLS[name]
        return pl.BlockSpec((None, tt, w), lambda b, i, j=off // w: (b, i, j))

    fix = lambda b, i: (0, 0)
    return pl.pallas_call(
        _gla_kernel,
        grid=(bsz, t // tt),
        in_specs=[col("q"), col("k"), col("v"), col("g"), col("gk"),
                  pl.BlockSpec(gkw.shape, fix), pl.BlockSpec(gkb.shape, fix), pl.BlockSpec(nw.shape, fix)],
        out_specs=pl.BlockSpec((None, tt, GLA_VW), lambda b, i: (b, i, 0)),
        out_shape=jax.ShapeDtypeStruct((bsz, t, GLA_VW), F32),
        scratch_shapes=[pltpu.VMEM((GLA_HEADS, GLA_DV, GLA_DK), F32)],
        compiler_params=_params("parallel", "arbitrary"),
        name="gla",
    )(p, p, p, p, p, gkw, gkb, nw)


def _rglru_kernel(x_ref, gate_ref, cw_ref, cb_ref, wa_ref, ba_ref, wx_ref, bx_ref, lam_ref, o_ref,
                  tail_ref, h_ref, a_scr, u_scr):
    @pl.when(pl.program_id(1) == 0)
    def _():
        tail_ref[...] = jnp.zeros_like(tail_ref)
        h_ref[...] = jnp.zeros_like(h_ref)

    tt = x_ref.shape[0]
    x = x_ref[...]
    cw = cw_ref[...]
    xc = x * cw[CONV_K - 1:CONV_K, :] + cb_ref[...]
    for s in range(1, CONV_K):
        xc += _shift_rows(x, tail_ref, s) * cw[CONV_K - 1 - s:CONV_K - s, :]
    tail_ref[...] = x[tt - SUBLANES:, :]

    r = _sigmoid(_mm(xc, wa_ref[...]) + ba_ref[...])
    i = _sigmoid(_mm(xc, wx_ref[...]) + bx_ref[...])
    log_a = -LRU_C * r * _softplus(-lam_ref[...])
    a_scr[...] = jnp.exp(log_a)
    u_scr[...] = jnp.sqrt(1.0 - jnp.exp(2.0 * log_a)) * (i * xc)

    def step(t, h):
        h = a_scr[pl.ds(t, 1), :] * h + u_scr[pl.ds(t, 1), :]
        u_scr[pl.ds(t, 1), :] = h
        return h

    h_ref[...] = lax.fori_loop(0, tt, step, h_ref[...], unroll=8)
    gate = gate_ref[...]
    gelu = 0.5 * gate * (1.0 + jnp.tanh(math.sqrt(2.0 / math.pi) * (gate + 0.044715 * gate * gate * gate)))
    o_ref[...] = u_scr[...] * gelu


def _rglru(p, cw, cb, wa, ba, wx, bx, lam, tt):
    bsz, t, _ = p.shape
    w = LRU_WIDTH

    def col(name):
        off, _ = EVEN_COLS[name]
        return pl.BlockSpec((None, tt, w), lambda b, i, j=off // w: (b, i, j))

    fix = lambda b, i: (0, 0)
    vec = pl.BlockSpec((1, w), fix)
    return pl.pallas_call(
        _rglru_kernel,
        grid=(bsz, t // tt),
        in_specs=[col("xb"), col("gate"), pl.BlockSpec((CONV_K, w), fix), vec,
                  pl.BlockSpec((w, w), fix), vec, pl.BlockSpec((w, w), fix), vec, vec],
        out_specs=pl.BlockSpec((None, tt, w), lambda b, i: (b, i, 0)),
        out_shape=jax.ShapeDtypeStruct((bsz, t, w), F32),
        scratch_shapes=[pltpu.VMEM((SUBLANES, w), F32), pltpu.VMEM((1, w), F32),
                        pltpu.VMEM((tt, w), F32), pltpu.VMEM((tt, w), F32)],
        compiler_params=_params("parallel", "arbitrary"),
        name="rglru",
    )(p, p, cw, cb, wa, ba, wx, bx, lam)


def _ssd_kernel(z_ref, x_ref, bm_ref, cm_ref, dt_ref, cwx_ref, cbx_ref, cwb_ref, cbb_ref, cwc_ref, cbc_ref,
                dtb_ref, alog_ref, dskip_ref, nw_ref, expand_ref, o_ref,
                tx_ref, tb_ref, tc_ref, st_ref, xs_scr, bs_scr, cs_scr):
    @pl.when(pl.program_id(1) == 0)
    def _():
        tx_ref[...] = jnp.zeros_like(tx_ref)
        tb_ref[...] = jnp.zeros_like(tb_ref)
        tc_ref[...] = jnp.zeros_like(tc_ref)
        st_ref[...] = jnp.zeros_like(st_ref)

    tt = x_ref.shape[0]

    def conv_silu(src_ref, tail_ref, cw_ref, cb_ref, dst_ref):
        x = src_ref[...]
        cw = cw_ref[...]
        y = x * cw[CONV_K - 1:CONV_K, :] + cb_ref[...]
        for s in range(1, CONV_K):
            y += _shift_rows(x, tail_ref, s) * cw[CONV_K - 1 - s:CONV_K - s, :]
        tail_ref[...] = x[tt - SUBLANES:, :]
        dst_ref[...] = _silu(y)

    conv_silu(x_ref, tx_ref, cwx_ref, cbx_ref, xs_scr)
    conv_silu(bm_ref, tb_ref, cwb_ref, cbb_ref, bs_scr)
    conv_silu(cm_ref, tc_ref, cwc_ref, cbc_ref, cs_scr)

    tri = _tri(CHUNK)
    tri_f = tri.astype(F32)
    neg_a = -jnp.exp(alog_ref[...])
    dtb = dtb_ref[...]
    expand = expand_ref[...]
    dskip = dskip_ref[...]
    nw = nw_ref[...]
    hpg = SSD_HEADS // SSD_GROUPS

    def chunk(c, carry):
        rows = pl.ds(pl.multiple_of(c * CHUNK, CHUNK), CHUNK)
        dtc = _softplus(dt_ref[rows, :] + dtb)
        acs = _mm(tri_f, dtc * neg_a, exact=True)
        acs_t = acs.T
        last = acs[CHUNK - 1:CHUNK, :]
        dt_x = _mm_01(dtc, expand)
        ea_x = _mm_01(jnp.exp(acs), expand)
        ds_x = _mm_01(jnp.exp(last - acs), expand)
        x = xs_scr[rows, :]
        xdt = x * dt_x
        xdec = xdt * ds_x
        cd_x = ea_x[CHUNK - 1:CHUNK, :]
        ys = []
        for g in range(SSD_GROUPS):
            sg = slice(g * SSD_GROUP_WIDTH, (g + 1) * SSD_GROUP_WIDTH)
            ss = slice(g * SSD_STATE, (g + 1) * SSD_STATE)
            bg = bs_scr[rows, ss]
            cg = cs_scr[rows, ss]
            cb = _mm(cg, bg, _NT)
            st = st_ref[g]
            yg = _mm(cg, st) * ea_x[:, sg]
            st_ref[g] = st * cd_x[:, sg] + _mm(bg, xdec[:, sg], _TN)
            yh = []
            for j in range(hpg):
                h = g * hpg + j
                seg = acs[:, h:h + 1] - acs_t[h:h + 1, :]
                lmat = jnp.exp(jnp.where(tri, seg, -jnp.inf))
                sh = slice(h * SSD_HEADDIM, (h + 1) * SSD_HEADDIM)
                yh.append(_mm(cb * lmat, xdt[:, sh]))
            ys.append(yg + jnp.concatenate(yh, axis=1))
        y = jnp.concatenate(ys, axis=1) + x * dskip
        y = y * _silu(z_ref[rows, :])
        outs = []
        for g in range(SSD_GROUPS):
            sg = slice(g * SSD_GROUP_WIDTH, (g + 1) * SSD_GROUP_WIDTH)
            yg = y[:, sg]
            ms = jnp.mean(yg * yg, axis=-1, keepdims=True)
            outs.append(yg * lax.rsqrt(ms + SSD_NORM_EPS))
        o_ref[rows, :] = jnp.concatenate(outs, axis=1) * nw
        return carry

    lax.fori_loop(0, tt // CHUNK, chunk, 0)


def _ssd(p, cw, cb, dtb, alog, dskip_x, nw, expand, tt):
    bsz, t, _ = p.shape

    def col(name):
        off, w = ODD_COLS[name]
        return pl.BlockSpec((None, tt, w), lambda b, i, j=off // w: (b, i, j))

    fix = lambda b, i: (0, 0)
    full = lambda a: pl.BlockSpec(a.shape, fix)
    gs = SSD_GROUPS * SSD_STATE
    cwx, cwb, cwc = cw[:, :SSD_INNER], cw[:, SSD_INNER:SSD_INNER + gs], cw[:, SSD_INNER + gs:]
    cbx, cbb, cbc = cb[:, :SSD_INNER], cb[:, SSD_INNER:SSD_INNER + gs], cb[:, SSD_INNER + gs:]
    args = (cwx, cbx, cwb, cbb, cwc, cbc, dtb, alog, dskip_x, nw, expand)
    return pl.pallas_call(
        _ssd_kernel,
        grid=(bsz, t // tt),
        in_specs=[col("z"), col("x"), col("bm"), col("cm"), col("dt")] + [full(a) for a in args],
        out_specs=pl.BlockSpec((None, tt, SSD_INNER), lambda b, i: (b, i, 0)),
        out_shape=jax.ShapeDtypeStruct((bsz, t, SSD_INNER), F32),
        scratch_shapes=[pltpu.VMEM((SUBLANES, SSD_INNER), F32), pltpu.VMEM((SUBLANES, gs), F32),
                        pltpu.VMEM((SUBLANES, gs), F32),
                        pltpu.VMEM((SSD_GROUPS, SSD_STATE, SSD_GROUP_WIDTH), F32),
                        pltpu.VMEM((tt, SSD_INNER), F32), pltpu.VMEM((tt, gs), F32), pltpu.VMEM((tt, gs), F32)],
        compiler_params=_params("parallel", "arbitrary"),
        name="ssd",
    )(p, p, p, p, p, *args)


def _rwkv_kernel(r_ref, k_ref, v_ref, wa_ref, xg_ref, mur_ref, muk_ref, muv_ref, muwa_ref, mug_ref,
                 w0_ref, w2_ref, a0_ref, a2_ref, g2_ref, kk_ref, ka_ref, rk_ref, lng_ref, lnb_ref, ones_ref,
                 o_ref, tr_ref, tk_ref, tv_ref, twa_ref, tg_ref, st_ref, y_scr,
                 r_scr, k_scr, v_scr, kk_scr, b_scr, lw_scr):
    @pl.when(pl.program_id(1) == 0)
    def _():
        for ref in (tr_ref, tk_ref, tv_ref, twa_ref, tg_ref, st_ref):
            ref[...] = jnp.zeros_like(ref)

    tt = r_ref.shape[0]
    hd = RWKV_HEADDIM

    def mix(p_ref, tail_ref, mu_ref):
        p = p_ref[...]
        prev = _shift_rows(p, tail_ref, 1)
        tail_ref[...] = p[tt - SUBLANES:, :]
        return p + (prev - p) * mu_ref[...]

    r = mix(r_ref, tr_ref, mur_ref)
    k = mix(k_ref, tk_ref, muk_ref)
    v = mix(v_ref, tv_ref, muv_ref)
    xwa = mix(wa_ref, twa_ref, muwa_ref)
    xg = mix(xg_ref, tg_ref, mug_ref)

    lane = lax.broadcasted_iota(jnp.int32, xwa.shape, 1)
    lora_in = jnp.where(lane < RWKV_DECAY_LORA, jnp.tanh(xwa), xwa)
    w_log = -_softplus(-(w0_ref[...] + _mm(lora_in, w2_ref[...], exact=True))) - 0.5
    lw = -jnp.exp(w_log)
    a_sig = _sigmoid(a0_ref[...] + _mm(lora_in, a2_ref[...], exact=True))
    gate = _mm(_sigmoid(xg), g2_ref[...], exact=True)
    ones_bd = ones_ref[...]
    kk = k * kk_ref[...]
    kk = kk / jnp.maximum(jnp.sqrt(_mm_01(kk * kk, ones_bd)), 1e-12)
    k = k * (1.0 + (a_sig - 1.0) * ka_ref[...])

    tri_incl = _tri(CHUNK)
    tri_strict = _tri(CHUNK, strict=True)
    tri_f = tri_incl.astype(F32)
    tri_col = jnp.concatenate([tri_strict, tri_incl], axis=0)
    eye = (tri_incl & ~tri_strict).astype(F32)
    for ref, val in ((r_scr, r), (k_scr, k), (v_scr, v), (kk_scr, kk), (b_scr, kk * a_sig), (lw_scr, lw)):
        ref[...] = val

    def chunk(c, carry):
        rows = pl.ds(pl.multiple_of(c * CHUNK, CHUNK), CHUNK)
        lwc = lw_scr[rows, :]
        kc = k_scr[rows, :]
        bc = b_scr[rows, :]
        vc = v_scr[rows, :]
        cum = _mm(tri_f, lwc, exact=True)
        last = cum[CHUNK - 1:CHUNK, :]
        at = -kk_scr[rows, :] * jnp.exp(cum - lwc)
        rt = r_scr[rows, :] * jnp.exp(cum)
        e_neg = jnp.exp(-cum)
        bt = bc * e_neg
        kt = kc * e_neg
        e_last = jnp.exp(last - cum)
        bh = bc * e_last
        kh = kc * e_last
        wc = jnp.exp(last)
        heads = range(RWKV_HEADS)
        sls = [slice(h * hd, (h + 1) * hd) for h in heads]
        amat = [_mm(jnp.concatenate([at[:, sl], rt[:, sl]], axis=0),
                    jnp.concatenate([bt[:, sl], kt[:, sl]], axis=0), _NT) for sl in sls]
        a_ab = [jnp.where(tri_strict, m[:CHUNK, :CHUNK], 0.0) for m in amat]
        a_rb = [jnp.where(tri_incl, m[CHUNK:, :CHUNK], 0.0) for m in amat]
        a_xk = [jnp.where(tri_col, m[:, CHUNK:], 0.0) for m in amat]
        avrv = [_mm(a_xk[h], vc[:, sls[h]]) for h in heads]
        inv = [eye + m for m in a_ab]
        pw = a_ab
        for _ in range(int(math.log2(CHUNK)) - 1):
            pw = [_mm(m, m) for m in pw]
            inv = [inv[h] + _mm(inv[h], pw[h]) for h in heads]
        tu = [_mm(inv[h], jnp.concatenate([at[:, sls[h]], avrv[h][:CHUNK]], axis=1)) for h in heads]
        qy = [_mm(a_rb[h], tu[h]) + jnp.concatenate([rt[:, sls[h]], avrv[h][CHUNK:]], axis=1)
              for h in heads]
        zb = [_mm(tu[h], bh[:, sls[h]], _TN) for h in heads]
        vk = [_mm(vc[:, sls[h]], kh[:, sls[h]], _TN) for h in heads]
        for h in heads:
            st = st_ref[h]
            y_scr[rows, sls[h]] = _mm(qy[h][:, :hd], st, _NT) + qy[h][:, hd:]
            st_ref[h] = st * wc[:, sls[h]] + _mm(st, zb[h][:hd]) + zb[h][hd:] + vk[h]
        return carry

    lax.fori_loop(0, tt // CHUNK, chunk, 0)

    y = y_scr[...]
    inv_n = 1.0 / hd
    mu_y = _mm_01(y, ones_bd) * inv_n
    dy = y - mu_y
    var_y = _mm_01(dy * dy, ones_bd) * inv_n
    yn = dy * lax.rsqrt(var_y + RWKV_GN_EPS) * lng_ref[...] + lnb_ref[...]
    bonus = _mm_01(r * k * rk_ref[...], ones_bd) * v
    o_ref[...] = (yn + bonus) * gate


def _rwkv(p, mu, w0, w2p, a0, a2p, g2, k_k, k_a, r_k, ln_g, ln_b, ones_bd, tt):
    bsz, t, _ = p.shape
    w = RWKV_WIDTH

    def col(name):
        off, wd = ODD_COLS[name]
        return pl.BlockSpec((None, tt, wd), lambda b, i, j=off // wd: (b, i, j))

    fix = lambda b, i: (0, 0)
    full = lambda a: pl.BlockSpec(a.shape, fix)
    mur, muk, muv = mu[:, :w], mu[:, w:2 * w], mu[:, 2 * w:3 * w]
    muwa, mug = mu[:, 3 * w:3 * w + LANES], mu[:, 3 * w + LANES:]
    args = (mur, muk, muv, muwa, mug, w0, w2p, a0, a2p, g2, k_k, k_a, r_k, ln_g, ln_b, ones_bd)
    return pl.pallas_call(
        _rwkv_kernel,
        grid=(bsz, t // tt),
        in_specs=[col("r"), col("k"), col("v"), col("wa"), col("xg")] + [full(a) for a in args],
        out_specs=pl.BlockSpec((None, tt, w), lambda b, i: (b, i, 0)),
        out_shape=jax.ShapeDtypeStruct((bsz, t, w), F32),
        scratch_shapes=[pltpu.VMEM((SUBLANES, w), F32), pltpu.VMEM((SUBLANES, w), F32),
                        pltpu.VMEM((SUBLANES, w), F32), pltpu.VMEM((SUBLANES, LANES), F32),
                        pltpu.VMEM((SUBLANES, LANES), F32),
                        pltpu.VMEM((RWKV_HEADS, RWKV_HEADDIM, RWKV_HEADDIM), F32)]
        + [pltpu.VMEM((tt, w), F32)] * 7,
        compiler_params=_params("parallel", "arbitrary"),
        name="rwkv7",
    )(p, p, p, p, p, *args)


def _reorder_cols(w, pieces, order, width):
    out = jnp.zeros((w.shape[0], width), w.dtype)
    for name, (off, _) in order.items():
        start, size = pieces[name]
        out = out.at[:, off:off + size].set(w[:, start:start + size])
    return out


def _block_diag(w):
    n, i, j = w.shape
    eye = jnp.eye(n, dtype=w.dtype)
    return (eye[:, None, :, None] * w[:, :, None, :]).reshape(n * i, n * j)


def _row(v, width=None):
    v = v.reshape(1, -1).astype(F32)
    if width is not None and v.shape[1] < width:
        v = jnp.pad(v, ((0, 0), (0, width - v.shape[1])))
    return v


def _even_layer(x, x_in, w_in, gk_w2, gk_b, gla_norm, conv_w, conv_b, wa, ba, wx, bx, lam,
                w_out, ln1_g, ln1_b, f_w1, f_w3, f_w2, ln2_g, ln2_b, bsz, t):
    qk, vw, lw = GLA_QK, GLA_VW, LRU_WIDTH
    pieces = dict(q=(0, qk), k=(qk, qk), v=(2 * qk, vw), g=(2 * qk + vw, vw), gk=(2 * qk + 2 * vw, GLA_GK_RANK),
                  xb=(2 * qk + 2 * vw + GLA_GK_RANK, lw), gate=(2 * qk + 2 * vw + GLA_GK_RANK + lw, lw))
    w_in_r = _reorder_cols(w_in, pieces, EVEN_COLS, EVEN_WIDTH).astype(BF16)
    p = _proj(x_in, w_in_r, tm=min(1024, x.shape[0]), tn=896).reshape(bsz, t, EVEN_WIDTH)
    tt = min(512, t)
    gkw = jnp.pad(gk_w2, ((0, LANES - GLA_GK_RANK), (0, 0)))
    y_a = _gla(p, gkw, _row(gk_b), _row(gla_norm), tt)
    y_b = _rglru(p, conv_w, _row(conv_b), _block_diag(wa).astype(BF16), _row(ba), _block_diag(wx).astype(BF16),
                 _row(bx), _row(lam), tt)
    n = bsz * t
    tm = min(512, n)
    x1, x1b = _outproj_ln(y_a.reshape(n, vw), y_b.reshape(n, lw), w_out[:vw].astype(BF16), w_out[vw:].astype(BF16),
                          x, _row(ln1_g), _row(ln1_b), tm)
    return _ffn_ln(x1b, f_w1.astype(BF16), f_w3.astype(BF16), f_w2.astype(BF16), x1,
                   _row(ln2_g), _row(ln2_b), tm=min(1024, n), tf=256)


def _odd_layer(x, xb, w_in, conv_w, conv_b, dt_bias, a_log, d_skip, ssd_norm, mu, w0, w2, a0, a2, g2, k_k, k_a,
               r_k, rln_g, rln_b, w_out, ln1_g, ln1_b, router, ew1, ew3, ew2, ln2_g, ln2_b, bsz, t):
    si, gs, rw = SSD_INNER, SSD_GROUPS * SSD_STATE, RWKV_WIDTH
    o = 2 * si + 2 * gs + SSD_HEADS
    pieces = dict(z=(0, si), x=(si, si), bm=(2 * si, gs), cm=(2 * si + gs, gs), dt=(2 * si + 2 * gs, SSD_HEADS),
                  r=(o, rw), k=(o + rw, rw), v=(o + 2 * rw, rw),
                  wa=(o + 3 * rw, RWKV_DECAY_LORA + RWKV_AAA_LORA), xg=(o + 3 * rw + LANES, RWKV_GATE_LORA))
    w_in_r = _reorder_cols(w_in, pieces, ODD_COLS, ODD_WIDTH).astype(BF16)
    n = bsz * t
    p = _proj(xb, w_in_r, tm=min(1024, n), tn=896).reshape(bsz, t, ODD_WIDTH)
    expand = jnp.pad(jnp.repeat(jnp.eye(SSD_HEADS, dtype=BF16), SSD_HEADDIM, axis=1), ((0, LANES - SSD_HEADS), (0, 0)))
    y_c = _ssd(p, conv_w, _row(conv_b), _row(dt_bias, LANES), _row(a_log, LANES),
               _row(jnp.repeat(d_skip, SSD_HEADDIM)), _row(ssd_norm), expand, tt=min(256, t))
    zeros = jnp.zeros((RWKV_DECAY_LORA, rw), F32)
    ones_bd = _block_diag(jnp.ones((RWKV_HEADS, RWKV_HEADDIM, RWKV_HEADDIM), BF16))
    y_d = _rwkv(p, _row(mu), _row(w0), jnp.concatenate([w2, zeros]), _row(a0), jnp.concatenate([zeros, a2]), g2,
                _row(k_k), _row(k_a), _row(r_k), _row(rln_g), _row(rln_b), ones_bd, tt=min(256, t))
    tm = min(512, n)
    x1, x1b = _outproj_ln(y_c.reshape(n, si), y_d.reshape(n, rw), w_out[:si].astype(BF16), w_out[si:].astype(BF16),
                          x, _row(ln1_g), _row(ln1_b), tm)
    y = _moe_ln(x1, x1b, jnp.pad(router, ((0, 0), (0, LANES - N_EXPERTS))), ew1.astype(BF16), ew3.astype(BF16),
                ew2.astype(BF16), _row(ln2_g), _row(ln2_b))
    return y, None


def kernel(x, e_w_in, e_gk_w2, e_gk_b, e_gla_norm, e_conv_w, e_conv_b, e_lru_wa, e_lru_ba, e_lru_wx, e_lru_bx, e_lru_lambda, e_w_out, e_ln1_g, e_ln1_b, e_ffn_w1, e_ffn_w3, e_ffn_w2, e_ln2_g, e_ln2_b, o_w_in, o_conv_w, o_conv_b, o_dt_bias, o_a_log, o_d_skip, o_ssd_norm, o_rwkv_mu, o_rwkv_w0, o_rwkv_w2, o_rwkv_a0, o_rwkv_a2, o_rwkv_g2, o_rwkv_k_k, o_rwkv_k_a, o_rwkv_r_k, o_rwkv_ln_g, o_rwkv_ln_b, o_w_out, o_ln1_g, o_ln1_b, o_router, o_exp_w1, o_exp_w3, o_exp_w2, o_ln2_g, o_ln2_b):
    bsz, t, d = x.shape
    h = x.reshape(bsz * t, d)
    hb = h
    for i in range(DEPTH):
        j = i // 2
        if i % 2 == 0:
            h, hb = _even_layer(h, hb, e_w_in[j], e_gk_w2[j], e_gk_b[j], e_gla_norm[j], e_conv_w[j], e_conv_b[j],
                                e_lru_wa[j], e_lru_ba[j], e_lru_wx[j], e_lru_bx[j], e_lru_lambda[j], e_w_out[j],
                                e_ln1_g[j], e_ln1_b[j], e_ffn_w1[j], e_ffn_w3[j], e_ffn_w2[j], e_ln2_g[j],
                                e_ln2_b[j], bsz, t)
        else:
            h, hb = _odd_layer(h, hb, o_w_in[j], o_conv_w[j], o_conv_b[j], o_dt_bias[j], o_a_log[j], o_d_skip[j],
                               o_ssd_norm[j], o_rwkv_mu[j], o_rwkv_w0[j], o_rwkv_w2[j], o_rwkv_a0[j],
                               o_rwkv_a2[j], o_rwkv_g2[j], o_rwkv_k_k[j], o_rwkv_k_a[j], o_rwkv_r_k[j],
                               o_rwkv_ln_g[j], o_rwkv_ln_b[j], o_w_out[j], o_ln1_g[j], o_ln1_b[j], o_router[j],
                               o_exp_w1[j], o_exp_w3[j], o_exp_w2[j], o_ln2_g[j], o_ln2_b[j], bsz, t)
    return h.reshape(bsz, t, d)
```

```python
import functools
import math

import jax
import jax.numpy as jnp
from jax import lax
from jax.experimental import pallas as pl
from jax.experimental.pallas import tpu as pltpu

F32 = jnp.float32
BF16 = jnp.bfloat16
HIGHEST = lax.Precision.HIGHEST

D_MODEL = 1024
DEPTH = 2
DEEPNORM_ALPHA = (2 * DEPTH) ** 0.25
LN_EPS = 1e-5
CONV_K = 4
CHUNK = 64

GLA_HEADS = 4
GLA_DK = 64
GLA_DV = 128
GLA_QK = GLA_HEADS * GLA_DK
GLA_VW = GLA_HEADS * GLA_DV
GLA_GK_RANK = 16
GLA_GATE_NORM = 16.0
GLA_NORM_EPS = 1e-5

LRU_WIDTH = 512
LRU_BLOCKS = 8
LRU_C = 8.0

SSD_HEADS = 16
SSD_HEADDIM = 64
SSD_INNER = SSD_HEADS * SSD_HEADDIM
SSD_GROUPS = 2
SSD_STATE = 128
SSD_GROUP_WIDTH = SSD_INNER // SSD_GROUPS
SSD_NORM_EPS = 1e-5

RWKV_HEADS = 8
RWKV_HEADDIM = 64
RWKV_WIDTH = RWKV_HEADS * RWKV_HEADDIM
RWKV_DECAY_LORA = 64
RWKV_AAA_LORA = 64
RWKV_GATE_LORA = 128
RWKV_GN_EPS = 64e-5
RWKV_CHUNK = 128

N_EXPERTS = 8
LANES = 128
SUBLANES = 8
assert N_EXPERTS == SUBLANES
MOE_TB = 512
MOE_TR = 512
MOE_TE = 512
assert MOE_TE % MOE_TR == 0
VMEM_LIMIT = 56 * 1024 * 1024

EVEN_COLS = dict(v=(0, 512), g=(512, 512), xb=(1024, 512), gate=(1536, 512), q=(2048, 256), k=(2304, 256),
                 gk=(2560, 128))
EVEN_WIDTH = 2688
ODD_COLS = dict(z=(0, 1024), x=(1024, 1024), r=(2048, 512), k=(2560, 512), v=(3072, 512), bm=(3584, 256),
                cm=(3840, 256), wa=(4096, 128), xg=(4224, 128), dt=(4352, 128))
ODD_WIDTH = 4480


def _mm(a, b, dims=((1,), (0,)), exact=False):
    dn = (dims, ((), ()))
    if exact:
        return lax.dot_general(a.astype(F32), b.astype(F32), dn, precision=HIGHEST, preferred_element_type=F32)
    return lax.dot_general(a.astype(BF16), b.astype(BF16), dn, preferred_element_type=F32)


_NT = ((1,), (1,))
_TN = ((0,), (0,))


def _mm_01(x, sel):
    hi = x.astype(BF16)
    lo = (x - hi.astype(F32)).astype(BF16)
    return (jnp.dot(hi, sel, preferred_element_type=F32) + jnp.dot(lo, sel, preferred_element_type=F32))


def _sigmoid(x):
    return 1.0 / (1.0 + jnp.exp(-x))


def _softplus(x):
    return jnp.maximum(x, 0.0) + jnp.log(1.0 + jnp.exp(-jnp.abs(x)))


def _silu(x):
    return x * _sigmoid(x)


def _layer_norm(h, g, b):
    mu = jnp.mean(h, axis=-1, keepdims=True)
    d = h - mu
    var = jnp.mean(d * d, axis=-1, keepdims=True)
    return d * lax.rsqrt(var + LN_EPS) * g + b


def _params(*sem):
    return pltpu.CompilerParams(dimension_semantics=sem, vmem_limit_bytes=VMEM_LIMIT)


def _tri(n, strict=False):
    r = lax.broadcasted_iota(jnp.int32, (n, n), 0)
    c = lax.broadcasted_iota(jnp.int32, (n, n), 1)
    return (r > c) if strict else (r >= c)


def _shift_rows(p, tail_ref, s):
    n = p.shape[0]
    rolled = pltpu.roll(p, s, axis=0)
    head = pltpu.roll(tail_ref[...], s, axis=0)
    row = lax.broadcasted_iota(jnp.int32, (SUBLANES, p.shape[1]), 0)
    fixed = jnp.where(row < s, head, rolled[:SUBLANES])
    return jnp.concatenate([fixed, rolled[SUBLANES:]], axis=0) if n > SUBLANES else fixed


def _proj_kernel(x_ref, w_ref, o_ref):
    o_ref[...] = jnp.dot(x_ref[...].astype(BF16), w_ref[...], preferred_element_type=F32)


def _proj(x, w, tm, tn):
    m, k = x.shape
    n = w.shape[1]
    return pl.pallas_call(
        _proj_kernel,
        grid=(m // tm, n // tn),
        in_specs=[pl.BlockSpec((tm, k), lambda i, j: (i, 0)), pl.BlockSpec((k, tn), lambda i, j: (0, j))],
        out_specs=pl.BlockSpec((tm, tn), lambda i, j: (i, j)),
        out_shape=jax.ShapeDtypeStruct((m, n), F32),
        compiler_params=_params("parallel", "parallel"),
        name="in_proj",
    )(x, w)


def _outproj_ln_kernel(ya_ref, yb_ref, wa_ref, wb_ref, res_ref, g_ref, b_ref, o_ref, ob_ref):
    acc = jnp.dot(ya_ref[...].astype(BF16), wa_ref[...], preferred_element_type=F32)
    acc += jnp.dot(yb_ref[...].astype(BF16), wb_ref[...], preferred_element_type=F32)
    y = _layer_norm(DEEPNORM_ALPHA * res_ref[...] + acc, g_ref[...], b_ref[...])
    o_ref[...] = y
    ob_ref[...] = y.astype(BF16)


def _outproj_ln(ya, yb, wa, wb, res, g, b, tm):
    m = ya.shape[0]
    d = res.shape[1]
    row = lambda i: (i, 0)
    fix = lambda i: (0, 0)
    return pl.pallas_call(
        _outproj_ln_kernel,
        grid=(m // tm,),
        in_specs=[pl.BlockSpec((tm, ya.shape[1]), row), pl.BlockSpec((tm, yb.shape[1]), row),
                  pl.BlockSpec(wa.shape, fix), pl.BlockSpec(wb.shape, fix), pl.BlockSpec((tm, d), row),
                  pl.BlockSpec((1, d), fix), pl.BlockSpec((1, d), fix)],
        out_specs=[pl.BlockSpec((tm, d), row), pl.BlockSpec((tm, d), row)],
        out_shape=[jax.ShapeDtypeStruct((m, d), F32), jax.ShapeDtypeStruct((m, d), BF16)],
        compiler_params=_params("parallel"),
        name="out_proj_ln",
    )(ya, yb, wa, wb, res, g, b)


def _ffn_kernel(xb_ref, w1_ref, w3_ref, w2_ref, res_ref, g_ref, b_ref, o_ref, ob_ref, acc_ref):
    f = pl.program_id(1)

    @pl.when(f == 0)
    def _():
        acc_ref[...] = jnp.zeros_like(acc_ref)

    x = xb_ref[...]
    h1 = jnp.dot(x, w1_ref[...], preferred_element_type=F32)
    h3 = jnp.dot(x, w3_ref[...], preferred_element_type=F32)
    acc_ref[...] += jnp.dot((_silu(h1) * h3).astype(BF16), w2_ref[...], preferred_element_type=F32)

    @pl.when(f == pl.num_programs(1) - 1)
    def _():
        y = _layer_norm(DEEPNORM_ALPHA * res_ref[...] + acc_ref[...], g_ref[...], b_ref[...])
        o_ref[...] = y
        ob_ref[...] = y.astype(BF16)


def _ffn_ln(xb, w1, w3, w2, res, g, b, tm, tf):
    m, d = xb.shape
    fdim = w1.shape[1]
    row = lambda i, f: (i, 0)
    fix = lambda i, f: (0, 0)
    return pl.pallas_call(
        _ffn_kernel,
        grid=(m // tm, fdim // tf),
        in_specs=[pl.BlockSpec((tm, d), row),
                  pl.BlockSpec((d, tf), lambda i, f: (0, f)),
                  pl.BlockSpec((d, tf), lambda i, f: (0, f)),
                  pl.BlockSpec((tf, d), lambda i, f: (f, 0)),
                  pl.BlockSpec((tm, d), row), pl.BlockSpec((1, d), fix), pl.BlockSpec((1, d), fix)],
        out_specs=[pl.BlockSpec((tm, d), row), pl.BlockSpec((tm, d), row)],
        out_shape=[jax.ShapeDtypeStruct((m, d), F32), jax.ShapeDtypeStruct((m, d), BF16)],
        scratch_shapes=[pltpu.VMEM((tm, d), F32)],
        compiler_params=_params("parallel", "arbitrary"),
        name="ffn_ln",
    )(xb, w1, w3, w2, res, g, b)


def _router_kernel(x_ref, w_ref, gates_ref, rank_ref, rank_t_ref, cnt_ref, carry_ref):
    @pl.when(pl.program_id(0) == 0)
    def _():
        carry_ref[...] = jnp.zeros_like(carry_ref)

    logits = _mm(x_ref[...], w_ref[...], exact=True)
    lane = lax.broadcasted_iota(jnp.int32, logits.shape, 1)
    neg = jnp.float32(-jnp.inf)
    l1 = jnp.where(lane < N_EXPERTS, logits, neg)
    m1 = jnp.max(l1, axis=1, keepdims=True)
    i1 = jnp.min(jnp.where(l1 == m1, lane, LANES), axis=1, keepdims=True)
    l2 = jnp.where(lane == i1, neg, l1)
    m2 = jnp.max(l2, axis=1, keepdims=True)
    i2 = jnp.min(jnp.where(l2 == m2, lane, LANES), axis=1, keepdims=True)
    ex = jnp.exp(m2 - m1)
    w_top = 1.0 / (1.0 + ex)
    gates_ref[...] = jnp.where(lane == i1, w_top, 0.0) + jnp.where(lane == i2, ex * w_top, 0.0)

    sel = jnp.where(lane == i1, 1.0, 0.0) + jnp.where(lane == i2, 1.0, 0.0)
    tb = sel.shape[0]
    before = _mm(_tri(tb, strict=True).astype(F32), sel)
    carry = carry_ref[...]
    rank = jnp.where(sel > 0.0, carry + before, -1.0)
    rank_ref[...] = rank
    rank_t_ref[...] = rank.T[:SUBLANES, :]
    carry = carry + jnp.sum(sel, axis=0, keepdims=True)
    carry_ref[...] = carry
    cnt_ref[...] = jnp.broadcast_to(carry, cnt_ref.shape)


def _router(x, w):
    m, d = x.shape
    nblk = m // MOE_TB
    return pl.pallas_call(
        _router_kernel,
        grid=(nblk,),
        in_specs=[pl.BlockSpec((MOE_TB, d), lambda i: (i, 0)), pl.BlockSpec((d, LANES), lambda i: (0, 0))],
        out_specs=[pl.BlockSpec((MOE_TB, LANES), lambda i: (i, 0)), pl.BlockSpec((MOE_TB, LANES), lambda i: (i, 0)),
                   pl.BlockSpec((SUBLANES, MOE_TB), lambda i: (0, i)), pl.BlockSpec((SUBLANES, LANES), lambda i: (i, 0))],
        out_shape=[jax.ShapeDtypeStruct((m, LANES), F32), jax.ShapeDtypeStruct((m, LANES), F32),
                   jax.ShapeDtypeStruct((SUBLANES, m), F32), jax.ShapeDtypeStruct((nblk * SUBLANES, LANES), F32)],
        scratch_shapes=[pltpu.VMEM((1, LANES), F32)],
        compiler_params=_params("arbitrary"),
        name="router",
    )(x, w)


def _moe_plan(cnt, n_tok):
    i32 = jnp.int32
    nblk = n_tok // MOE_TB
    c_inc = cnt.reshape(nblk, SUBLANES, LANES)[:, 0, :N_EXPERTS].astype(i32)
    c_exc = jnp.concatenate([jnp.zeros((1, N_EXPERTS), i32), c_inc[:-1]], axis=0)
    gsz = (c_inc[-1] + MOE_TE - 1) // MOE_TE * MOE_TE
    gend = jnp.cumsum(gsz)
    start = gend - gsz
    n_et = _moe_rows(n_tok) // MOE_TE
    n_act = gend[-1] // MOE_TE
    et = jnp.minimum(jnp.arange(n_et, dtype=i32), jnp.maximum(n_act - 1, 0))
    et_expert = jnp.sum((gend[None, :] <= (et * MOE_TE)[:, None]).astype(i32), axis=1)
    et_active = (jnp.arange(n_et, dtype=i32) < n_act).astype(i32)

    lo = start[None, :] + c_exc
    hi = start[None, :] + c_inc
    first_tile = lo // MOE_TR
    n_items = jnp.where(hi > lo, (hi - 1) // MOE_TR - first_tile + 1, 0)
    wmax = _moe_rows(n_tok) // MOE_TR + N_EXPERTS * nblk
    blk_id = jnp.broadcast_to(jnp.arange(nblk, dtype=i32)[:, None], (nblk, N_EXPERTS))
    exp_id = jnp.broadcast_to(jnp.arange(N_EXPERTS, dtype=i32)[None, :], (nblk, N_EXPERTS))

    def worklist(expert_major):
        flat = (lambda a: a.T.reshape(-1)) if expert_major else (lambda a: a.reshape(-1))
        n, ft, blk, exp = flat(n_items), flat(first_tile), flat(blk_id), flat(exp_id)
        inc = jnp.cumsum(n)
        total = inc[-1]
        w = jnp.arange(wmax, dtype=i32)
        wc = jnp.minimum(w, jnp.maximum(total - 1, 0))
        idx = jnp.sum((inc[None, :] <= wc[:, None]).astype(i32), axis=1)
        tile = ft[idx] + wc - (inc - n)[idx]
        valid = w < total
        key = tile if expert_major else blk[idx]
        first = valid & ((w == 0) | (key != jnp.roll(key, 1)))
        last = valid & ((w == total - 1) | (key != jnp.roll(key, -1)))
        return tile, blk[idx], exp[idx], valid.astype(i32), first.astype(i32), last.astype(i32)

    return start, (et, et_expert, et_active), worklist(True), worklist(False)


def _moe_rows(n_tok):
    return 2 * n_tok + N_EXPERTS * MOE_TE


def _one_hot_rows(dest, tile):
    row = lax.broadcasted_iota(jnp.int32, (MOE_TR, MOE_TB), 0) + tile * MOE_TR
    return jnp.where(dest == row.astype(F32), 1.0, 0.0).astype(BF16)


def _dispatch_kernel(tile_ref, blk_ref, exp_ref, valid_ref, first_ref, x_ref, dest_ref, xs_ref):
    w = pl.program_id(0)

    @pl.when(valid_ref[w] == 1)
    def _():
        onehot = _one_hot_rows(dest_ref[pl.ds(exp_ref[w], 1), :], tile_ref[w])
        xg = jnp.dot(onehot, x_ref[...], preferred_element_type=F32).astype(BF16)

        @pl.when(first_ref[w] == 1)
        def _():
            xs_ref[...] = xg

        @pl.when(first_ref[w] == 0)
        def _():
            xs_ref[...] += xg


def _dispatch(plan, xb, dest_t):
    tile, blk, exp, valid, first, _ = plan
    n, d = xb.shape
    rows = _moe_rows(n)
    grid_spec = pltpu.PrefetchScalarGridSpec(
        num_scalar_prefetch=5,
        grid=(tile.shape[0],),
        in_specs=[pl.BlockSpec((MOE_TB, d), lambda w, t, b, e, v, f: (b[w], 0)),
                  pl.BlockSpec((SUBLANES, MOE_TB), lambda w, t, b, e, v, f: (0, b[w]))],
        out_specs=pl.BlockSpec((MOE_TR, d), lambda w, t, b, e, v, f: (t[w], 0)),
    )
    return pl.pallas_call(
        _dispatch_kernel,
        grid_spec=grid_spec,
        out_shape=jax.ShapeDtypeStruct((rows, d), BF16),
        compiler_params=_params("arbitrary"),
        name="moe_dispatch",
    )(tile, blk, exp, valid, first, xb, dest_t)


def _expert_kernel(et_ref, ee_ref, ea_ref, xs_ref, w1_ref, w3_ref, w2_ref, ys_ref, acc_ref):
    j = pl.program_id(0)
    f = pl.program_id(1)

    @pl.when(ea_ref[j] == 1)
    def _():
        @pl.when(f == 0)
        def _():
            acc_ref[...] = jnp.zeros_like(acc_ref)

        x = xs_ref[...]
        h1 = jnp.dot(x, w1_ref[...], preferred_element_type=F32)
        h3 = jnp.dot(x, w3_ref[...], preferred_element_type=F32)
        acc_ref[...] += jnp.dot((_silu(h1) * h3).astype(BF16), w2_ref[...], preferred_element_type=F32)

        @pl.when(f == pl.num_programs(1) - 1)
        def _():
            ys_ref[...] = acc_ref[...].astype(BF16)


def _experts(plan, xs, w1, w3, w2, tf):
    et, ee, ea = plan
    rows, d = xs.shape
    nf = w1.shape[2] // tf
    fidx = lambda f, a, j: f * a[j] + (nf - 1) * (1 - a[j])
    grid_spec = pltpu.PrefetchScalarGridSpec(
        num_scalar_prefetch=3,
        grid=(et.shape[0], nf),
        in_specs=[pl.BlockSpec((MOE_TE, d), lambda j, f, t, e, a: (t[j], 0)),
                  pl.BlockSpec((None, d, tf), lambda j, f, t, e, a: (e[j], 0, fidx(f, a, j))),
                  pl.BlockSpec((None, d, tf), lambda j, f, t, e, a: (e[j], 0, fidx(f, a, j))),
                  pl.BlockSpec((None, tf, d), lambda j, f, t, e, a: (e[j], fidx(f, a, j), 0))],
        out_specs=pl.BlockSpec((MOE_TE, d), lambda j, f, t, e, a: (t[j], 0)),
        scratch_shapes=[pltpu.VMEM((MOE_TE, d), F32)],
    )
    return pl.pallas_call(
        _expert_kernel,
        grid_spec=grid_spec,
        out_shape=jax.ShapeDtypeStruct((rows, d), BF16),
        compiler_params=_params("arbitrary", "arbitrary"),
        name="moe_experts",
    )(et, ee, ea, xs, w1, w3, w2)


def _combine_kernel(tile_ref, blk_ref, exp_ref, valid_ref, first_ref, last_ref, ys_ref, dest_ref, gates_ref,
                    res_ref, g_ref, b_ref, o_ref, acc_ref):
    w = pl.program_id(0)

    @pl.when(valid_ref[w] == 1)
    def _():
        dest = dest_ref[...]
        lane = lax.broadcasted_iota(jnp.int32, dest.shape, 1)
        mine = lane == exp_ref[w]
        dcol = jnp.sum(jnp.where(mine, dest, 0.0), axis=1, keepdims=True)
        gcol = jnp.sum(jnp.where(mine, gates_ref[...], 0.0), axis=1, keepdims=True)
        col = lax.broadcasted_iota(jnp.int32, (MOE_TB, MOE_TR), 1) + tile_ref[w] * MOE_TR
        onehot = jnp.where(dcol == col.astype(F32), 1.0, 0.0).astype(BF16)
        part = jnp.dot(onehot, ys_ref[...], preferred_element_type=F32) * gcol

        @pl.when(first_ref[w] == 1)
        def _():
            acc_ref[...] = part

        @pl.when(first_ref[w] == 0)
        def _():
            acc_ref[...] += part

        @pl.when(last_ref[w] == 1)
        def _():
            o_ref[...] = _layer_norm(DEEPNORM_ALPHA * res_ref[...] + acc_ref[...], g_ref[...], b_ref[...])


def _combine_ln(plan, ys, dest, gates, res, g, b):
    tile, blk, exp, valid, first, last = plan
    n, d = res.shape
    tok = lambda w, t, b_, e, v, f, l: (b_[w], 0)
    fix = lambda w, t, b_, e, v, f, l: (0, 0)
    grid_spec = pltpu.PrefetchScalarGridSpec(
        num_scalar_prefetch=6,
        grid=(tile.shape[0],),
        in_specs=[pl.BlockSpec((MOE_TR, d), lambda w, t, b_, e, v, f, l: (t[w], 0)),
                  pl.BlockSpec((MOE_TB, LANES), tok), pl.BlockSpec((MOE_TB, LANES), tok), pl.BlockSpec((MOE_TB, d), tok),
                  pl.BlockSpec((1, d), fix), pl.BlockSpec((1, d), fix)],
        out_specs=pl.BlockSpec((MOE_TB, d), tok),
        scratch_shapes=[pltpu.VMEM((MOE_TB, d), F32)],
    )
    return pl.pallas_call(
        _combine_kernel,
        grid_spec=grid_spec,
        out_shape=jax.ShapeDtypeStruct((n, d), F32),
        compiler_params=_params("arbitrary"),
        name="moe_combine_ln",
    )(tile, blk, exp, valid, first, last, ys, dest, gates, res, g, b)


def _moe_ln(x, xb, router_w, w1, w3, w2, g, b):
    n = x.shape[0]
    gates, rank, rank_t, cnt = _router(x, router_w)
    start, expert_tiles, by_tile, by_block = _moe_plan(cnt, n)
    startf = start.astype(F32)
    dest = jnp.where(rank >= 0.0, rank + jnp.pad(startf, (0, LANES - N_EXPERTS))[None, :], -1.0)
    dest_t = jnp.where(rank_t >= 0.0, rank_t + startf[:, None], -1.0)
    xs = _dispatch(by_tile, xb, dest_t)
    ys = _experts(expert_tiles, xs, w1, w3, w2, tf=1792)
    return _combine_ln(by_block, ys, dest, gates, x, g, b)


def _gla_kernel(q_ref, k_ref, v_ref, g_ref, gk_ref, gkw_ref, gkb_ref, nw_ref, o_ref, st_ref):
    @pl.when(pl.program_id(1) == 0)
    def _():
        st_ref[...] = jnp.zeros_like(st_ref)

    tt = q_ref.shape[0]
    tri = _tri(CHUNK)
    tri_f = tri.astype(F32)
    gkw = gkw_ref[...]
    gkb = gkb_ref[...]
    nw = nw_ref[...]

    def chunk(c, carry):
        rows = pl.ds(pl.multiple_of(c * CHUNK, CHUNK), CHUNK)
        pre = _mm(gk_ref[rows, :], gkw, exact=True) + gkb
        gk = (jnp.minimum(pre, 0.0) - jnp.log(1.0 + jnp.exp(-jnp.abs(pre)))) * (1.0 / GLA_GATE_NORM)
        bc = _mm(tri_f, gk, exact=True)
        bl = bc[CHUNK - 1:CHUNK, :]
        k = k_ref[rows, :]
        qd = q_ref[rows, :] * (GLA_DK ** -0.5) * jnp.exp(bc)
        kd = k * jnp.exp(-bc)
        kl = k * jnp.exp(bl - bc)
        dec = jnp.exp(bl)
        heads = range(GLA_HEADS)
        sks = [slice(h * GLA_DK, (h + 1) * GLA_DK) for h in heads]
        svs = [slice(h * GLA_DV, (h + 1) * GLA_DV) for h in heads]
        att = [jnp.where(tri, _mm(qd[:, sk], kd[:, sk], _NT), 0.0) for sk in sks]
        vs = [v_ref[rows, sv] for sv in svs]
        sts = [st_ref[h] for h in heads]
        os_ = [_mm(att[h], vs[h]) + _mm(qd[:, sks[h]], sts[h], _NT) for h in heads]
        for h in heads:
            st_ref[h] = sts[h] * dec[:, sks[h]] + _mm(vs[h], kl[:, sks[h]], _TN)
        for h in heads:
            o = os_[h]
            ms = jnp.mean(o * o, axis=-1, keepdims=True)
            o_ref[rows, svs[h]] = o * lax.rsqrt(ms + GLA_NORM_EPS) * nw * _silu(g_ref[rows, svs[h]])
        return carry

    lax.fori_loop(0, tt // CHUNK, chunk, 0, unroll=2)


def _gla(p, gkw, gkb, nw, tt):
    bsz, t, _ = p.shape

    def col(name):
        off, w = EVEN_COLS[name]
        return pl.BlockSpec((None, tt, w), lambda b, i, j=off // w: (b, i, j))

    fix = lambda b, i: (0, 0)
    return pl.pallas_call(
        _gla_kernel,
        grid=(bsz, t // tt),
        in_specs=[col("q"), col("k"), col("v"), col("g"), col("gk"),
                  pl.BlockSpec(gkw.shape, fix), pl.BlockSpec(gkb.shape, fix), pl.BlockSpec(nw.shape, fix)],
        out_specs=pl.BlockSpec((None, tt, GLA_VW), lambda b, i: (b, i, 0)),
        out_shape=jax.ShapeDtypeStruct((bsz, t, GLA_VW), F32),
        scratch_shapes=[pltpu.VMEM((GLA_HEADS, GLA_DV, GLA_DK), F32)],
        compiler_params=_params("parallel", "arbitrary"),
        name="gla",
    )(p, p, p, p, p, gkw, gkb, nw)


def _rglru_kernel(x_ref, gate_ref, cw_ref, cb_ref, wa_ref, ba_ref, wx_ref, bx_ref, lam_ref, o_ref,
                  tail_ref, h_ref, a_scr, u_scr):
    @pl.when(pl.program_id(1) == 0)
    def _():
        tail_ref[...] = jnp.zeros_like(tail_ref)
        h_ref[...] = jnp.zeros_like(h_ref)

    tt = x_ref.shape[0]
    x = x_ref[...]
    cw = cw_ref[...]
    xc = x * cw[CONV_K - 1:CONV_K, :] + cb_ref[...]
    for s in range(1, CONV_K):
        xc += _shift_rows(x, tail_ref, s) * cw[CONV_K - 1 - s:CONV_K - s, :]
    tail_ref[...] = x[tt - SUBLANES:, :]

    r = _sigmoid(_mm(xc, wa_ref[...]) + ba_ref[...])
    i = _sigmoid(_mm(xc, wx_ref[...]) + bx_ref[...])
    log_a = -LRU_C * r * _softplus(-lam_ref[...])
    a_scr[...] = jnp.exp(log_a)
    u_scr[...] = jnp.sqrt(1.0 - jnp.exp(2.0 * log_a)) * (i * xc)

    def step(t, h):
        h = a_scr[pl.ds(t, 1), :] * h + u_scr[pl.ds(t, 1), :]
        u_scr[pl.ds(t, 1), :] = h
        return h

    h_ref[...] = lax.fori_loop(0, tt, step, h_ref[...], unroll=8)
    gate = gate_ref[...]
    gelu = 0.5 * gate * (1.0 + jnp.tanh(math.sqrt(2.0 / math.pi) * (gate + 0.044715 * gate * gate * gate)))
    o_ref[...] = u_scr[...] * gelu


def _rglru(p, cw, cb, wa, ba, wx, bx, lam, tt):
    bsz, t, _ = p.shape
    w = LRU_WIDTH

    def col(name):
        off, _ = EVEN_COLS[name]
        return pl.BlockSpec((None, tt, w), lambda b, i, j=off // w: (b, i, j))

    fix = lambda b, i: (0, 0)
    vec = pl.BlockSpec((1, w), fix)
    return pl.pallas_call(
        _rglru_kernel,
        grid=(bsz, t // tt),
        in_specs=[col("xb"), col("gate"), pl.BlockSpec((CONV_K, w), fix), vec,
                  pl.BlockSpec((w, w), fix), vec, pl.BlockSpec((w, w), fix), vec, vec],
        out_specs=pl.BlockSpec((None, tt, w), lambda b, i: (b, i, 0)),
        out_shape=jax.ShapeDtypeStruct((bsz, t, w), F32),
        scratch_shapes=[pltpu.VMEM((SUBLANES, w), F32), pltpu.VMEM((1, w), F32),
                        pltpu.VMEM((tt, w), F32), pltpu.VMEM((tt, w), F32)],
        compiler_params=_params("parallel", "arbitrary"),
        name="rglru",
    )(p, p, cw, cb, wa, ba, wx, bx, lam)


def _ssd_kernel(z_ref, x_ref, bm_ref, cm_ref, dt_ref, cwx_ref, cbx_ref, cwb_ref, cbb_ref, cwc_ref, cbc_ref,
                dtb_ref, alog_ref, dskip_ref, nw_ref, expand_ref, o_ref,
                tx_ref, tb_ref, tc_ref, st_ref, xs_scr, bs_scr, cs_scr):
    @pl.when(pl.program_id(1) == 0)
    def _():
        tx_ref[...] = jnp.zeros_like(tx_ref)
        tb_ref[...] = jnp.zeros_like(tb_ref)
        tc_ref[...] = jnp.zeros_like(tc_ref)
        st_ref[...] = jnp.zeros_like(st_ref)

    tt = x_ref.shape[0]

    def conv_silu(src_ref, tail_ref, cw_ref, cb_ref, dst_ref):
        x = src_ref[...]
        cw = cw_ref[...]
        y = x * cw[CONV_K - 1:CONV_K, :] + cb_ref[...]
        for s in range(1, CONV_K):
            y += _shift_rows(x, tail_ref, s) * cw[CONV_K - 1 - s:CONV_K - s, :]
        tail_ref[...] = x[tt - SUBLANES:, :]
        dst_ref[...] = _silu(y)

    conv_silu(x_ref, tx_ref, cwx_ref, cbx_ref, xs_scr)
    conv_silu(bm_ref, tb_ref, cwb_ref, cbb_ref, bs_scr)
    conv_silu(cm_ref, tc_ref, cwc_ref, cbc_ref, cs_scr)

    tri = _tri(CHUNK)
    tri_f = tri.astype(F32)
    neg_a = -jnp.exp(alog_ref[...])
    dtb = dtb_ref[...]
    expand = expand_ref[...]
    dskip = dskip_ref[...]
    nw = nw_ref[...]
    hpg = SSD_HEADS // SSD_GROUPS

    def chunk(c, carry):
        rows = pl.ds(pl.multiple_of(c * CHUNK, CHUNK), CHUNK)
        dtc = _softplus(dt_ref[rows, :] + dtb)
        acs = _mm(tri_f, dtc * neg_a, exact=True)
        acs_t = acs.T
        last = acs[CHUNK - 1:CHUNK, :]
        dt_x = _mm_01(dtc, expand)
        ea_x = _mm_01(jnp.exp(acs), expand)
        ds_x = _mm_01(jnp.exp(last - acs), expand)
        x = xs_scr[rows, :]
        xdt = x * dt_x
        xdec = xdt * ds_x
        cd_x = ea_x[CHUNK - 1:CHUNK, :]
        ys = []
        for g in range(SSD_GROUPS):
            sg = slice(g * SSD_GROUP_WIDTH, (g + 1) * SSD_GROUP_WIDTH)
            ss = slice(g * SSD_STATE, (g + 1) * SSD_STATE)
            bg = bs_scr[rows, ss]
            cg = cs_scr[rows, ss]
            cb = _mm(cg, bg, _NT)
            st = st_ref[g]
            yg = _mm(cg, st) * ea_x[:, sg]
            st_ref[g] = st * cd_x[:, sg] + _mm(bg, xdec[:, sg], _TN)
            yh = []
            for j in range(hpg):
                h = g * hpg + j
                seg = acs[:, h:h + 1] - acs_t[h:h + 1, :]
                lmat = jnp.exp(jnp.where(tri, seg, -jnp.inf))
                sh = slice(h * SSD_HEADDIM, (h + 1) * SSD_HEADDIM)
                yh.append(_mm(cb * lmat, xdt[:, sh]))
            ys.append(yg + jnp.concatenate(yh, axis=1))
        y = jnp.concatenate(ys, axis=1) + x * dskip
        y = y * _silu(z_ref[rows, :])
        outs = []
        for g in range(SSD_GROUPS):
            sg = slice(g * SSD_GROUP_WIDTH, (g + 1) * SSD_GROUP_WIDTH)
            yg = y[:, sg]
            ms = jnp.mean(yg * yg, axis=-1, keepdims=True)
            outs.append(yg * lax.rsqrt(ms + SSD_NORM_EPS))
        o_ref[rows, :] = jnp.concatenate(outs, axis=1) * nw
        return carry

    lax.fori_loop(0, tt // CHUNK, chunk, 0, unroll=2)


def _ssd(p, cw, cb, dtb, alog, dskip_x, nw, expand, tt):
    bsz, t, _ = p.shape

    def col(name):
        off, w = ODD_COLS[name]
        return pl.BlockSpec((None, tt, w), lambda b, i, j=off // w: (b, i, j))

    fix = lambda b, i: (0, 0)
    full = lambda a: pl.BlockSpec(a.shape, fix)
    gs = SSD_GROUPS * SSD_STATE
    cwx, cwb, cwc = cw[:, :SSD_INNER], cw[:, SSD_INNER:SSD_INNER + gs], cw[:, SSD_INNER + gs:]
    cbx, cbb, cbc = cb[:, :SSD_INNER], cb[:, SSD_INNER:SSD_INNER + gs], cb[:, SSD_INNER + gs:]
    args = (cwx, cbx, cwb, cbb, cwc, cbc, dtb, alog, dskip_x, nw, expand)
    return pl.pallas_call(
        _ssd_kernel,
        grid=(bsz, t // tt),
        in_specs=[col("z"), col("x"), col("bm"), col("cm"), col("dt")] + [full(a) for a in args],
        out_specs=pl.BlockSpec((None, tt, SSD_INNER), lambda b, i: (b, i, 0)),
        out_shape=jax.ShapeDtypeStruct((bsz, t, SSD_INNER), F32),
        scratch_shapes=[pltpu.VMEM((SUBLANES, SSD_INNER), F32), pltpu.VMEM((SUBLANES, gs), F32),
                        pltpu.VMEM((SUBLANES, gs), F32),
                        pltpu.VMEM((SSD_GROUPS, SSD_STATE, SSD_GROUP_WIDTH), F32),
                        pltpu.VMEM((tt, SSD_INNER), F32), pltpu.VMEM((tt, gs), F32), pltpu.VMEM((tt, gs), F32)],
        compiler_params=_params("parallel", "arbitrary"),
        name="ssd",
    )(p, p, p, p, p, *args)


def _rwkv_kernel(r_ref, k_ref, v_ref, wa_ref, xg_ref, mur_ref, muk_ref, muv_ref, muwa_ref, mug_ref,
                 w0_ref, w2_ref, a0_ref, a2_ref, g2_ref, kk_ref, ka_ref, rk_ref, lng_ref, lnb_ref, ones_ref,
                 o_ref, tr_ref, tk_ref, tv_ref, twa_ref, tg_ref, st_ref, y_scr,
                 r_scr, k_scr, v_scr, kk_scr, b_scr, lw_scr):
    @pl.when(pl.program_id(1) == 0)
    def _():
        for ref in (tr_ref, tk_ref, tv_ref, twa_ref, tg_ref, st_ref):
            ref[...] = jnp.zeros_like(ref)

    tt = r_ref.shape[0]
    hd = RWKV_HEADDIM

    def mix(p_ref, tail_ref, mu_ref):
        p = p_ref[...]
        prev = _shift_rows(p, tail_ref, 1)
        tail_ref[...] = p[tt - SUBLANES:, :]
        return p + (prev - p) * mu_ref[...]

    r = mix(r_ref, tr_ref, mur_ref)
    k = mix(k_ref, tk_ref, muk_ref)
    v = mix(v_ref, tv_ref, muv_ref)
    xwa = mix(wa_ref, twa_ref, muwa_ref)
    xg = mix(xg_ref, tg_ref, mug_ref)

    lane = lax.broadcasted_iota(jnp.int32, xwa.shape, 1)
    lora_in = jnp.where(lane < RWKV_DECAY_LORA, jnp.tanh(xwa), xwa)
    w_log = -_softplus(-(w0_ref[...] + _mm(lora_in, w2_ref[...], exact=True))) - 0.5
    lw = -jnp.exp(w_log)
    a_sig = _sigmoid(a0_ref[...] + _mm(lora_in, a2_ref[...], exact=True))
    gate = _mm(_sigmoid(xg), g2_ref[...], exact=True)
    ones_bd = ones_ref[...]
    kk = k * kk_ref[...]
    kk = kk / jnp.maximum(jnp.sqrt(_mm_01(kk * kk, ones_bd)), 1e-12)
    k = k * (1.0 + (a_sig - 1.0) * ka_ref[...])

    cs = RWKV_CHUNK
    tri_incl = _tri(cs)
    tri_strict = _tri(cs, strict=True)
    tri_f = tri_incl.astype(F32)
    tri_col = jnp.concatenate([tri_strict, tri_incl], axis=0)
    eye = (tri_incl & ~tri_strict).astype(F32)
    for ref, val in ((r_scr, r), (k_scr, k), (v_scr, v), (kk_scr, kk), (b_scr, kk * a_sig), (lw_scr, lw)):
        ref[...] = val

    def chunk(c, carry):
        rows = pl.ds(pl.multiple_of(c * cs, cs), cs)
        lwc = lw_scr[rows, :]
        kc = k_scr[rows, :]
        bc = b_scr[rows, :]
        vc = v_scr[rows, :]
        cum = _mm(tri_f, lwc, exact=True)
        last = cum[cs - 1:cs, :]
        at = -kk_scr[rows, :] * jnp.exp(cum - lwc)
        rt = r_scr[rows, :] * jnp.exp(cum)
        e_neg = jnp.exp(-cum)
        bt = bc * e_neg
        kt = kc * e_neg
        e_last = jnp.exp(last - cum)
        bh = bc * e_last
        kh = kc * e_last
        wc = jnp.exp(last)
        heads = range(RWKV_HEADS)
        sls = [slice(h * hd, (h + 1) * hd) for h in heads]
        amat = [_mm(jnp.concatenate([at[:, sl], rt[:, sl]], axis=0),
                    jnp.concatenate([bt[:, sl], kt[:, sl]], axis=0), _NT) for sl in sls]
        a_ab = [jnp.where(tri_strict, m[:cs, :cs], 0.0) for m in amat]
        a_rb = [jnp.where(tri_incl, m[cs:, :cs], 0.0) for m in amat]
        a_xk = [jnp.where(tri_col, m[:, cs:], 0.0) for m in amat]
        avrv = [_mm(a_xk[h], vc[:, sls[h]]) for h in heads]
        inv = [eye + m for m in a_ab]
        pw = a_ab
        for _ in range(int(math.log2(cs)) - 1):
            pw = [_mm(m, m) for m in pw]
            inv = [inv[h] + _mm(inv[h], pw[h]) for h in heads]
        tu = [_mm(inv[h], jnp.concatenate([at[:, sls[h]], avrv[h][:cs]], axis=1)) for h in heads]
        qy = [_mm(a_rb[h], tu[h]) + jnp.concatenate([rt[:, sls[h]], avrv[h][cs:]], axis=1)
              for h in heads]
        zb = [_mm(tu[h], bh[:, sls[h]], _TN) for h in heads]
        vk = [_mm(vc[:, sls[h]], kh[:, sls[h]], _TN) for h in heads]
        for h in heads:
            st = st_ref[h]
            y_scr[rows, sls[h]] = _mm(qy[h][:, :hd], st, _NT) + qy[h][:, hd:]
            st_ref[h] = st * wc[:, sls[h]] + _mm(st, zb[h][:hd]) + zb[h][hd:] + vk[h]
        return carry

    lax.fori_loop(0, tt // cs, chunk, 0, unroll=2)

    y = y_scr[...]
    inv_n = 1.0 / hd
    mu_y = _mm_01(y, ones_bd) * inv_n
    dy = y - mu_y
    var_y = _mm_01(dy * dy, ones_bd) * inv_n
    yn = dy * lax.rsqrt(var_y + RWKV_GN_EPS) * lng_ref[...] + lnb_ref[...]
    bonus = _mm_01(r * k * rk_ref[...], ones_bd) * v
    o_ref[...] = (yn + bonus) * gate


def _rwkv(p, mu, w0, w2p, a0, a2p, g2, k_k, k_a, r_k, ln_g, ln_b, ones_bd, tt):
    bsz, t, _ = p.shape
    w = RWKV_WIDTH

    def col(name):
        off, wd = ODD_COLS[name]
        return pl.BlockSpec((None, tt, wd), lambda b, i, j=off // wd: (b, i, j))

    fix = lambda b, i: (0, 0)
    full = lambda a: pl.BlockSpec(a.shape, fix)
    mur, muk, muv = mu[:, :w], mu[:, w:2 * w], mu[:, 2 * w:3 * w]
    muwa, mug = mu[:, 3 * w:3 * w + LANES], mu[:, 3 * w + LANES:]
    args = (mur, muk, muv, muwa, mug, w0, w2p, a0, a2p, g2, k_k, k_a, r_k, ln_g, ln_b, ones_bd)
    return pl.pallas_call(
        _rwkv_kernel,
        grid=(bsz, t // tt),
        in_specs=[col("r"), col("k"), col("v"), col("wa"), col("xg")] + [full(a) for a in args],
        out_specs=pl.BlockSpec((None, tt, w), lambda b, i: (b, i, 0)),
        out_shape=jax.ShapeDtypeStruct((bsz, t, w), F32),
        scratch_shapes=[pltpu.VMEM((SUBLANES, w), F32), pltpu.VMEM((SUBLANES, w), F32),
                        pltpu.VMEM((SUBLANES, w), F32), pltpu.VMEM((SUBLANES, LANES), F32),
                        pltpu.VMEM((SUBLANES, LANES), F32),
                        pltpu.VMEM((RWKV_HEADS, RWKV_HEADDIM, RWKV_HEADDIM), F32)]
        + [pltpu.VMEM((tt, w), F32)] * 7,
        compiler_params=_params("parallel", "arbitrary"),
        name="rwkv7",
    )(p, p, p, p, p, *args)


def _reorder_cols(w, pieces, order, width):
    out = jnp.zeros((w.shape[0], width), w.dtype)
    for name, (off, _) in order.items():
        start, size = pieces[name]
        out = out.at[:, off:off + size].set(w[:, start:start + size])
    return out


def _block_diag(w):
    n, i, j = w.shape
    eye = jnp.eye(n, dtype=w.dtype)
    return (eye[:, None, :, None] * w[:, :, None, :]).reshape(n * i, n * j)


def _row(v, width=None):
    v = v.reshape(1, -1).astype(F32)
    if width is not None and v.shape[1] < width:
        v = jnp.pad(v, ((0, 0), (0, width - v.shape[1])))
    return v


def _even_layer(x, x_in, w_in, gk_w2, gk_b, gla_norm, conv_w, conv_b, wa, ba, wx, bx, lam,
                w_out, ln1_g, ln1_b, f_w1, f_w3, f_w2, ln2_g, ln2_b, bsz, t):
    qk, vw, lw = GLA_QK, GLA_VW, LRU_WIDTH
    pieces = dict(q=(0, qk), k=(qk, qk), v=(2 * qk, vw), g=(2 * qk + vw, vw), gk=(2 * qk + 2 * vw, GLA_GK_RANK),
                  xb=(2 * qk + 2 * vw + GLA_GK_RANK, lw), gate=(2 * qk + 2 * vw + GLA_GK_RANK + lw, lw))
    w_in_r = _reorder_cols(w_in, pieces, EVEN_COLS, EVEN_WIDTH).astype(BF16)
    p = _proj(x_in, w_in_r, tm=min(1024, x.shape[0]), tn=896).reshape(bsz, t, EVEN_WIDTH)
    tt = min(512, t)
    gkw = jnp.pad(gk_w2, ((0, LANES - GLA_GK_RANK), (0, 0)))
    y_a = _gla(p, gkw, _row(gk_b), _row(gla_norm), tt)
    y_b = _rglru(p, conv_w, _row(conv_b), _block_diag(wa).astype(BF16), _row(ba), _block_diag(wx).astype(BF16),
                 _row(bx), _row(lam), tt)
    n = bsz * t
    tm = min(512, n)
    x1, x1b = _outproj_ln(y_a.reshape(n, vw), y_b.reshape(n, lw), w_out[:vw].astype(BF16), w_out[vw:].astype(BF16),
                          x, _row(ln1_g), _row(ln1_b), tm)
    return _ffn_ln(x1b, f_w1.astype(BF16), f_w3.astype(BF16), f_w2.astype(BF16), x1,
                   _row(ln2_g), _row(ln2_b), tm=min(512, n), tf=1408)


def _odd_layer(x, xb, w_in, conv_w, conv_b, dt_bias, a_log, d_skip, ssd_norm, mu, w0, w2, a0, a2, g2, k_k, k_a,
               r_k, rln_g, rln_b, w_out, ln1_g, ln1_b, router, ew1, ew3, ew2, ln2_g, ln2_b, bsz, t):
    si, gs, rw = SSD_INNER, SSD_GROUPS * SSD_STATE, RWKV_WIDTH
    o = 2 * si + 2 * gs + SSD_HEADS
    pieces = dict(z=(0, si), x=(si, si), bm=(2 * si, gs), cm=(2 * si + gs, gs), dt=(2 * si + 2 * gs, SSD_HEADS),
                  r=(o, rw), k=(o + rw, rw), v=(o + 2 * rw, rw),
                  wa=(o + 3 * rw, RWKV_DECAY_LORA + RWKV_AAA_LORA), xg=(o + 3 * rw + LANES, RWKV_GATE_LORA))
    w_in_r = _reorder_cols(w_in, pieces, ODD_COLS, ODD_WIDTH).astype(BF16)
    n = bsz * t
    p = _proj(xb, w_in_r, tm=min(1024, n), tn=896).reshape(bsz, t, ODD_WIDTH)
    expand = jnp.pad(jnp.repeat(jnp.eye(SSD_HEADS, dtype=BF16), SSD_HEADDIM, axis=1), ((0, LANES - SSD_HEADS), (0, 0)))
    y_c = _ssd(p, conv_w, _row(conv_b), _row(dt_bias, LANES), _row(a_log, LANES),
               _row(jnp.repeat(d_skip, SSD_HEADDIM)), _row(ssd_norm), expand, tt=min(256, t))
    zeros = jnp.zeros((RWKV_DECAY_LORA, rw), F32)
    ones_bd = _block_diag(jnp.ones((RWKV_HEADS, RWKV_HEADDIM, RWKV_HEADDIM), BF16))
    y_d = _rwkv(p, _row(mu), _row(w0), jnp.concatenate([w2, zeros]), _row(a0), jnp.concatenate([zeros, a2]), g2,
                _row(k_k), _row(k_a), _row(r_k), _row(rln_g), _row(rln_b), ones_bd, tt=min(256, t))
    tm = min(512, n)
    x1, x1b = _outproj_ln(y_c.reshape(n, si), y_d.reshape(n, rw), w_out[:si].astype(BF16), w_out[si:].astype(BF16),
                          x, _row(ln1_g), _row(ln1_b), tm)
    y = _moe_ln(x1, x1b, jnp.pad(router, ((0, 0), (0, LANES - N_EXPERTS))), ew1.astype(BF16), ew3.astype(BF16),
                ew2.astype(BF16), _row(ln2_g), _row(ln2_b))
    return y, None


def kernel(x, e_w_in, e_gk_w2, e_gk_b, e_gla_norm, e_conv_w, e_conv_b, e_lru_wa, e_lru_ba, e_lru_wx, e_lru_bx, e_lru_lambda, e_w_out, e_ln1_g, e_ln1_b, e_ffn_w1, e_ffn_w3, e_ffn_w2, e_ln2_g, e_ln2_b, o_w_in, o_conv_w, o_conv_b, o_dt_bias, o_a_log, o_d_skip, o_ssd_norm, o_rwkv_mu, o_rwkv_w0, o_rwkv_w2, o_rwkv_a0, o_rwkv_a2, o_rwkv_g2, o_rwkv_k_k, o_rwkv_k_a, o_rwkv_r_k, o_rwkv_ln_g, o_rwkv_ln_b, o_w_out, o_ln1_g, o_ln1_b, o_router, o_exp_w1, o_exp_w3, o_exp_w2, o_ln2_g, o_ln2_b):
    bsz, t, d = x.shape
    h = x.reshape(bsz * t, d)
    hb = h
    for i in range(DEPTH):
        j = i // 2
        if i % 2 == 0:
            h, hb = _even_layer(h, hb, e_w_in[j], e_gk_w2[j], e_gk_b[j], e_gla_norm[j], e_conv_w[j], e_conv_b[j],
                                e_lru_wa[j], e_lru_ba[j], e_lru_wx[j], e_lru_bx[j], e_lru_lambda[j], e_w_out[j],
                                e_ln1_g[j], e_ln1_b[j], e_ffn_w1[j], e_ffn_w3[j], e_ffn_w2[j], e_ln2_g[j],
                                e_ln2_b[j], bsz, t)
        else:
            h, hb = _odd_layer(h, hb, o_w_in[j], o_conv_w[j], o_conv_b[j], o_dt_bias[j], o_a_log[j], o_d_skip[j],
                               o_ssd_norm[j], o_rwkv_mu[j], o_rwkv_w0[j], o_rwkv_w2[j], o_rwkv_a0[j],
                               o_rwkv_a2[j], o_rwkv_g2[j], o_rwkv_k_k[j], o_rwkv_k_a[j], o_rwkv_r_k[j],
                               o_rwkv_ln_g[j], o_rwkv_ln_b[j], o_w_out[j], o_ln1_g[j], o_ln1_b[j], o_router[j],
                               o_exp_w1[j], o_exp_w3[j], o_exp_w2[j], o_ln2_g[j], o_ln2_b[j], bsz, t)
    return h.reshape(bsz, t, d)
```

```python
import functools
import math

import jax
import jax.numpy as jnp
from jax import lax
from jax.experimental import pallas as pl
from jax.experimental.pallas import tpu as pltpu

F32 = jnp.float32
BF16 = jnp.bfloat16
HIGHEST = lax.Precision.HIGHEST

D_MODEL = 1024
DEPTH = 2
DEEPNORM_ALPHA = (2 * DEPTH) ** 0.25
LN_EPS = 1e-5
CONV_K = 4
CHUNK = 64

GLA_HEADS = 4
GLA_DK = 64
GLA_DV = 128
GLA_QK = GLA_HEADS * GLA_DK
GLA_VW = GLA_HEADS * GLA_DV
GLA_GK_RANK = 16
GLA_GATE_NORM = 16.0
GLA_NORM_EPS = 1e-5

LRU_WIDTH = 512
LRU_BLOCKS = 8
LRU_C = 8.0

SSD_HEADS = 16
SSD_HEADDIM = 64
SSD_INNER = SSD_HEADS * SSD_HEADDIM
SSD_GROUPS = 2
SSD_STATE = 128
SSD_GROUP_WIDTH = SSD_INNER // SSD_GROUPS
SSD_NORM_EPS = 1e-5

RWKV_HEADS = 8
RWKV_HEADDIM = 64
RWKV_WIDTH = RWKV_HEADS * RWKV_HEADDIM
RWKV_DECAY_LORA = 64
RWKV_AAA_LORA = 64
RWKV_GATE_LORA = 128
RWKV_GN_EPS = 64e-5
RWKV_CHUNK = 128

N_EXPERTS = 8
LANES = 128
SUBLANES = 8
assert N_EXPERTS == SUBLANES
MOE_TB = 512
MOE_TR = 512
MOE_TE = 512
assert MOE_TE % MOE_TR == 0
VMEM_LIMIT = 56 * 1024 * 1024

EVEN_COLS = dict(v=(0, 512), g=(512, 512), xb=(1024, 512), gate=(1536, 512), q=(2048, 256), k=(2304, 256),
                 gk=(2560, 128))
EVEN_WIDTH = 2688
ODD_COLS = dict(z=(0, 1024), x=(1024, 1024), r=(2048, 512), k=(2560, 512), v=(3072, 512), bm=(3584, 256),
                cm=(3840, 256), wa=(4096, 128), xg=(4224, 128), dt=(4352, 128))
ODD_WIDTH = 4480


def _mm(a, b, dims=((1,), (0,)), exact=False):
    dn = (dims, ((), ()))
    if exact:
        return lax.dot_general(a.astype(F32), b.astype(F32), dn, precision=HIGHEST, preferred_element_type=F32)
    return lax.dot_general(a.astype(BF16), b.astype(BF16), dn, preferred_element_type=F32)


_NT = ((1,), (1,))
_TN = ((0,), (0,))


def _mm_01(x, sel):
    hi = x.astype(BF16)
    lo = (x - hi.astype(F32)).astype(BF16)
    return (jnp.dot(hi, sel, preferred_element_type=F32) + jnp.dot(lo, sel, preferred_element_type=F32))


def _sigmoid(x):
    return 1.0 / (1.0 + jnp.exp(-x))


def _softplus(x):
    return jnp.maximum(x, 0.0) + jnp.log(1.0 + jnp.exp(-jnp.abs(x)))


def _silu(x):
    return x * _sigmoid(x)


def _layer_norm(h, g, b):
    mu = jnp.mean(h, axis=-1, keepdims=True)
    d = h - mu
    var = jnp.mean(d * d, axis=-1, keepdims=True)
    return d * lax.rsqrt(var + LN_EPS) * g + b


def _params(*sem):
    return pltpu.CompilerParams(dimension_semantics=sem, vmem_limit_bytes=VMEM_LIMIT)


def _tri(n, strict=False):
    r = lax.broadcasted_iota(jnp.int32, (n, n), 0)
    c = lax.broadcasted_iota(jnp.int32, (n, n), 1)
    return (r > c) if strict else (r >= c)


def _chunk_sel(n, chunk, kind="incl"):
    r = jnp.arange(n)[:, None]
    c = jnp.arange(n)[None, :]
    first = r - r % chunk
    upper = r if kind == "incl" else first + (chunk - 1)
    return ((c >= first) & (c <= upper)).astype(BF16)


def _mm_01_left(sel, x):
    hi = x.astype(BF16)
    lo = (x - hi.astype(F32)).astype(BF16)
    return jnp.dot(sel, hi, preferred_element_type=F32) + jnp.dot(sel, lo, preferred_element_type=F32)


def _shift_rows(p, tail_ref, s):
    n = p.shape[0]
    rolled = pltpu.roll(p, s, axis=0)
    head = pltpu.roll(tail_ref[...], s, axis=0)
    row = lax.broadcasted_iota(jnp.int32, (SUBLANES, p.shape[1]), 0)
    fixed = jnp.where(row < s, head, rolled[:SUBLANES])
    return jnp.concatenate([fixed, rolled[SUBLANES:]], axis=0) if n > SUBLANES else fixed


def _proj_kernel(x_ref, w_ref, o_ref):
    o_ref[...] = jnp.dot(x_ref[...].astype(BF16), w_ref[...], preferred_element_type=F32)


def _resident(shape):
    return pl.BlockSpec(shape, lambda *_: (0,) * len(shape), pipeline_mode=pl.Buffered(1))


def _proj(x, w, tm):
    m, k = x.shape
    n = w.shape[1]
    return pl.pallas_call(
        _proj_kernel,
        grid=(m // tm,),
        in_specs=[pl.BlockSpec((tm, k), lambda i: (i, 0)), _resident((k, n))],
        out_specs=pl.BlockSpec((tm, n), lambda i: (i, 0)),
        out_shape=jax.ShapeDtypeStruct((m, n), F32),
        compiler_params=_params("parallel"),
        name="in_proj",
    )(x, w)


def _outproj_ln_kernel(ya_ref, yb_ref, wa_ref, wb_ref, res_ref, g_ref, b_ref, o_ref, ob_ref):
    acc = jnp.dot(ya_ref[...].astype(BF16), wa_ref[...], preferred_element_type=F32)
    acc += jnp.dot(yb_ref[...].astype(BF16), wb_ref[...], preferred_element_type=F32)
    y = _layer_norm(DEEPNORM_ALPHA * res_ref[...] + acc, g_ref[...], b_ref[...])
    o_ref[...] = y
    ob_ref[...] = y.astype(BF16)


def _outproj_ln(ya, yb, wa, wb, res, g, b, tm):
    m = ya.shape[0]
    d = res.shape[1]
    row = lambda i: (i, 0)
    fix = lambda i: (0, 0)
    return pl.pallas_call(
        _outproj_ln_kernel,
        grid=(m // tm,),
        in_specs=[pl.BlockSpec((tm, ya.shape[1]), row), pl.BlockSpec((tm, yb.shape[1]), row),
                  pl.BlockSpec(wa.shape, fix), pl.BlockSpec(wb.shape, fix), pl.BlockSpec((tm, d), row),
                  pl.BlockSpec((1, d), fix), pl.BlockSpec((1, d), fix)],
        out_specs=[pl.BlockSpec((tm, d), row), pl.BlockSpec((tm, d), row)],
        out_shape=[jax.ShapeDtypeStruct((m, d), F32), jax.ShapeDtypeStruct((m, d), BF16)],
        compiler_params=_params("parallel"),
        name="out_proj_ln",
    )(ya, yb, wa, wb, res, g, b)


def _ffn_kernel(xb_ref, w1_ref, w3_ref, w2_ref, res_ref, g_ref, b_ref, o_ref, ob_ref):
    x = xb_ref[...]
    h1 = jnp.dot(x, w1_ref[...], preferred_element_type=F32)
    h3 = jnp.dot(x, w3_ref[...], preferred_element_type=F32)
    ff = jnp.dot((_silu(h1) * h3).astype(BF16), w2_ref[...], preferred_element_type=F32)
    y = _layer_norm(DEEPNORM_ALPHA * res_ref[...] + ff, g_ref[...], b_ref[...])
    o_ref[...] = y
    ob_ref[...] = y.astype(BF16)


def _ffn_ln(xb, w1, w3, w2, res, g, b, tm):
    m, d = xb.shape
    row = lambda i: (i, 0)
    return pl.pallas_call(
        _ffn_kernel,
        grid=(m // tm,),
        in_specs=[pl.BlockSpec((tm, d), row), _resident(w1.shape), _resident(w3.shape), _resident(w2.shape),
                  pl.BlockSpec((tm, d), row), _resident((1, d)), _resident((1, d))],
        out_specs=[pl.BlockSpec((tm, d), row), pl.BlockSpec((tm, d), row)],
        out_shape=[jax.ShapeDtypeStruct((m, d), F32), jax.ShapeDtypeStruct((m, d), BF16)],
        compiler_params=_params("parallel"),
        name="ffn_ln",
    )(xb, w1, w3, w2, res, g, b)


def _router_kernel(x_ref, w_ref, gates_ref, rank_ref, rank_t_ref, cnt_ref, carry_ref):
    @pl.when(pl.program_id(0) == 0)
    def _():
        carry_ref[...] = jnp.zeros_like(carry_ref)

    x = x_ref[...]
    w = w_ref[...]
    x_hi = x.astype(BF16)
    w_hi = w.astype(BF16)
    x_lo = (x - x_hi.astype(F32)).astype(BF16)
    w_lo = (w - w_hi.astype(F32)).astype(BF16)
    logits = (jnp.dot(x_hi, w_hi, preferred_element_type=F32) + jnp.dot(x_lo, w_hi, preferred_element_type=F32)
              + jnp.dot(x_hi, w_lo, preferred_element_type=F32))
    lane = lax.broadcasted_iota(jnp.int32, logits.shape, 1)
    neg = jnp.float32(-jnp.inf)
    l1 = jnp.where(lane < N_EXPERTS, logits, neg)
    m1 = jnp.max(l1, axis=1, keepdims=True)
    i1 = jnp.min(jnp.where(l1 == m1, lane, LANES), axis=1, keepdims=True)
    l2 = jnp.where(lane == i1, neg, l1)
    m2 = jnp.max(l2, axis=1, keepdims=True)
    i2 = jnp.min(jnp.where(l2 == m2, lane, LANES), axis=1, keepdims=True)
    ex = jnp.exp(m2 - m1)
    w_top = 1.0 / (1.0 + ex)
    gates_ref[...] = jnp.where(lane == i1, w_top, 0.0) + jnp.where(lane == i2, ex * w_top, 0.0)

    sel = jnp.where(lane == i1, 1.0, 0.0) + jnp.where(lane == i2, 1.0, 0.0)
    tb = sel.shape[0]
    before = _mm(_tri(tb, strict=True).astype(F32), sel)
    carry = carry_ref[...]
    rank = jnp.where(sel > 0.0, carry + before, -1.0)
    rank_ref[...] = rank
    rank_t_ref[...] = rank.T[:SUBLANES, :]
    carry = carry + jnp.sum(sel, axis=0, keepdims=True)
    carry_ref[...] = carry
    cnt_ref[...] = jnp.broadcast_to(carry, cnt_ref.shape)


def _router(x, w):
    m, d = x.shape
    nblk = m // MOE_TB
    return pl.pallas_call(
        _router_kernel,
        grid=(nblk,),
        in_specs=[pl.BlockSpec((MOE_TB, d), lambda i: (i, 0)), pl.BlockSpec((d, LANES), lambda i: (0, 0))],
        out_specs=[pl.BlockSpec((MOE_TB, LANES), lambda i: (i, 0)), pl.BlockSpec((MOE_TB, LANES), lambda i: (i, 0)),
                   pl.BlockSpec((SUBLANES, MOE_TB), lambda i: (0, i)), pl.BlockSpec((SUBLANES, LANES), lambda i: (i, 0))],
        out_shape=[jax.ShapeDtypeStruct((m, LANES), F32), jax.ShapeDtypeStruct((m, LANES), F32),
                   jax.ShapeDtypeStruct((SUBLANES, m), F32), jax.ShapeDtypeStruct((nblk * SUBLANES, LANES), F32)],
        scratch_shapes=[pltpu.VMEM((1, LANES), F32)],
        compiler_params=_params("arbitrary"),
        name="router",
    )(x, w)


def _moe_plan(cnt, n_tok):
    i32 = jnp.int32
    nblk = n_tok // MOE_TB
    c_inc = cnt.reshape(nblk, SUBLANES, LANES)[:, 0, :N_EXPERTS].astype(i32)
    c_exc = jnp.concatenate([jnp.zeros((1, N_EXPERTS), i32), c_inc[:-1]], axis=0)
    gsz = (c_inc[-1] + MOE_TE - 1) // MOE_TE * MOE_TE
    gend = jnp.cumsum(gsz)
    start = gend - gsz
    n_et = _moe_rows(n_tok) // MOE_TE
    n_act = gend[-1] // MOE_TE
    et = jnp.minimum(jnp.arange(n_et, dtype=i32), jnp.maximum(n_act - 1, 0))
    et_expert = jnp.sum((gend[None, :] <= (et * MOE_TE)[:, None]).astype(i32), axis=1)
    et_active = (jnp.arange(n_et, dtype=i32) < n_act).astype(i32)

    lo = start[None, :] + c_exc
    hi = start[None, :] + c_inc
    first_tile = lo // MOE_TR
    n_items = jnp.where(hi > lo, (hi - 1) // MOE_TR - first_tile + 1, 0)
    wmax = _moe_rows(n_tok) // MOE_TR + N_EXPERTS * nblk
    blk_id = jnp.broadcast_to(jnp.arange(nblk, dtype=i32)[:, None], (nblk, N_EXPERTS))
    exp_id = jnp.broadcast_to(jnp.arange(N_EXPERTS, dtype=i32)[None, :], (nblk, N_EXPERTS))

    def worklist(expert_major):
        flat = (lambda a: a.T.reshape(-1)) if expert_major else (lambda a: a.reshape(-1))
        n, ft, blk, exp = flat(n_items), flat(first_tile), flat(blk_id), flat(exp_id)
        inc = jnp.cumsum(n)
        total = inc[-1]
        w = jnp.arange(wmax, dtype=i32)
        wc = jnp.minimum(w, jnp.maximum(total - 1, 0))
        idx = jnp.sum((inc[None, :] <= wc[:, None]).astype(i32), axis=1)
        tile = ft[idx] + wc - (inc - n)[idx]
        valid = w < total
        key = tile if expert_major else blk[idx]
        first = valid & ((w == 0) | (key != jnp.roll(key, 1)))
        last = valid & ((w == total - 1) | (key != jnp.roll(key, -1)))
        return tile, blk[idx], exp[idx], valid.astype(i32), first.astype(i32), last.astype(i32)

    return start, (et, et_expert, et_active), worklist(True), worklist(False)


def _moe_rows(n_tok):
    return 2 * n_tok + N_EXPERTS * MOE_TE


def _one_hot_rows(dest, tile):
    row = lax.broadcasted_iota(jnp.int32, (MOE_TR, MOE_TB), 0) + tile * MOE_TR
    return jnp.where(dest == row.astype(F32), 1.0, 0.0).astype(BF16)


def _dispatch_kernel(tile_ref, blk_ref, exp_ref, valid_ref, first_ref, x_ref, dest_ref, xs_ref):
    w = pl.program_id(0)

    @pl.when(valid_ref[w] == 1)
    def _():
        onehot = _one_hot_rows(dest_ref[pl.ds(exp_ref[w], 1), :], tile_ref[w])
        xg = jnp.dot(onehot, x_ref[...], preferred_element_type=F32).astype(BF16)

        @pl.when(first_ref[w] == 1)
        def _():
            xs_ref[...] = xg

        @pl.when(first_ref[w] == 0)
        def _():
            xs_ref[...] += xg


def _dispatch(plan, xb, dest_t):
    tile, blk, exp, valid, first, _ = plan
    n, d = xb.shape
    rows = _moe_rows(n)
    grid_spec = pltpu.PrefetchScalarGridSpec(
        num_scalar_prefetch=5,
        grid=(tile.shape[0],),
        in_specs=[pl.BlockSpec((MOE_TB, d), lambda w, t, b, e, v, f: (b[w], 0)),
                  pl.BlockSpec((SUBLANES, MOE_TB), lambda w, t, b, e, v, f: (0, b[w]))],
        out_specs=pl.BlockSpec((MOE_TR, d), lambda w, t, b, e, v, f: (t[w], 0)),
    )
    return pl.pallas_call(
        _dispatch_kernel,
        grid_spec=grid_spec,
        out_shape=jax.ShapeDtypeStruct((rows, d), BF16),
        compiler_params=_params("arbitrary"),
        name="moe_dispatch",
    )(tile, blk, exp, valid, first, xb, dest_t)


def _expert_kernel(et_ref, ee_ref, ea_ref, xs_ref, w1_ref, w3_ref, w2_ref, ys_ref, acc_ref):
    j = pl.program_id(0)
    f = pl.program_id(1)

    @pl.when(ea_ref[j] == 1)
    def _():
        @pl.when(f == 0)
        def _():
            acc_ref[...] = jnp.zeros_like(acc_ref)

        x = xs_ref[...]
        h1 = jnp.dot(x, w1_ref[...], preferred_element_type=F32)
        h3 = jnp.dot(x, w3_ref[...], preferred_element_type=F32)
        acc_ref[...] += jnp.dot((_silu(h1) * h3).astype(BF16), w2_ref[...], preferred_element_type=F32)

        @pl.when(f == pl.num_programs(1) - 1)
        def _():
            ys_ref[...] = acc_ref[...].astype(BF16)


def _experts(plan, xs, w1, w3, w2, tf):
    et, ee, ea = plan
    rows, d = xs.shape
    nf = w1.shape[2] // tf
    fidx = lambda f, a, j: f * a[j] + (nf - 1) * (1 - a[j])
    grid_spec = pltpu.PrefetchScalarGridSpec(
        num_scalar_prefetch=3,
        grid=(et.shape[0], nf),
        in_specs=[pl.BlockSpec((MOE_TE, d), lambda j, f, t, e, a: (t[j], 0)),
                  pl.BlockSpec((None, d, tf), lambda j, f, t, e, a: (e[j], 0, fidx(f, a, j))),
                  pl.BlockSpec((None, d, tf), lambda j, f, t, e, a: (e[j], 0, fidx(f, a, j))),
                  pl.BlockSpec((None, tf, d), lambda j, f, t, e, a: (e[j], fidx(f, a, j), 0))],
        out_specs=pl.BlockSpec((MOE_TE, d), lambda j, f, t, e, a: (t[j], 0)),
        scratch_shapes=[pltpu.VMEM((MOE_TE, d), F32)],
    )
    return pl.pallas_call(
        _expert_kernel,
        grid_spec=grid_spec,
        out_shape=jax.ShapeDtypeStruct((rows, d), BF16),
        compiler_params=_params("arbitrary", "arbitrary"),
        name="moe_experts",
    )(et, ee, ea, xs, w1, w3, w2)


def _combine_kernel(tile_ref, blk_ref, exp_ref, valid_ref, first_ref, last_ref, ys_ref, dest_ref, gates_ref,
                    res_ref, g_ref, b_ref, o_ref, acc_ref):
    w = pl.program_id(0)

    @pl.when(valid_ref[w] == 1)
    def _():
        dest = dest_ref[...]
        lane = lax.broadcasted_iota(jnp.int32, dest.shape, 1)
        mine = lane == exp_ref[w]
        dcol = jnp.sum(jnp.where(mine, dest, 0.0), axis=1, keepdims=True)
        gcol = jnp.sum(jnp.where(mine, gates_ref[...], 0.0), axis=1, keepdims=True)
        col = lax.broadcasted_iota(jnp.int32, (MOE_TB, MOE_TR), 1) + tile_ref[w] * MOE_TR
        onehot = jnp.where(dcol == col.astype(F32), 1.0, 0.0).astype(BF16)
        part = jnp.dot(onehot, ys_ref[...], preferred_element_type=F32) * gcol

        @pl.when(first_ref[w] == 1)
        def _():
            acc_ref[...] = part

        @pl.when(first_ref[w] == 0)
        def _():
            acc_ref[...] += part

        @pl.when(last_ref[w] == 1)
        def _():
            o_ref[...] = _layer_norm(DEEPNORM_ALPHA * res_ref[...] + acc_ref[...], g_ref[...], b_ref[...])


def _combine_ln(plan, ys, dest, gates, res, g, b):
    tile, blk, exp, valid, first, last = plan
    n, d = res.shape
    tok = lambda w, t, b_, e, v, f, l: (b_[w], 0)
    fix = lambda w, t, b_, e, v, f, l: (0, 0)
    grid_spec = pltpu.PrefetchScalarGridSpec(
        num_scalar_prefetch=6,
        grid=(tile.shape[0],),
        in_specs=[pl.BlockSpec((MOE_TR, d), lambda w, t, b_, e, v, f, l: (t[w], 0)),
                  pl.BlockSpec((MOE_TB, LANES), tok), pl.BlockSpec((MOE_TB, LANES), tok), pl.BlockSpec((MOE_TB, d), tok),
                  pl.BlockSpec((1, d), fix), pl.BlockSpec((1, d), fix)],
        out_specs=pl.BlockSpec((MOE_TB, d), tok),
        scratch_shapes=[pltpu.VMEM((MOE_TB, d), F32)],
    )
    return pl.pallas_call(
        _combine_kernel,
        grid_spec=grid_spec,
        out_shape=jax.ShapeDtypeStruct((n, d), F32),
        compiler_params=_params("arbitrary"),
        name="moe_combine_ln",
    )(tile, blk, exp, valid, first, last, ys, dest, gates, res, g, b)


def _moe_ln(x, xb, router_w, w1, w3, w2, g, b):
    n = x.shape[0]
    gates, rank, rank_t, cnt = _router(x, router_w)
    start, expert_tiles, by_tile, by_block = _moe_plan(cnt, n)
    startf = start.astype(F32)
    dest = jnp.where(rank >= 0.0, rank + jnp.pad(startf, (0, LANES - N_EXPERTS))[None, :], -1.0)
    dest_t = jnp.where(rank_t >= 0.0, rank_t + startf[:, None], -1.0)
    xs = _dispatch(by_tile, xb, dest_t)
    ys = _experts(expert_tiles, xs, w1, w3, w2, tf=1792)
    return _combine_ln(by_block, ys, dest, gates, x, g, b)


def _gla_kernel(q_ref, k_ref, v_ref, g_ref, gk_ref, gkw_ref, gkb_ref, nw_ref, sel_ref, o_ref, st_ref, bc_scr,
                qd_scr, kd_scr):
    @pl.when(pl.program_id(1) == 0)
    def _():
        st_ref[...] = jnp.zeros_like(st_ref)

    tt = q_ref.shape[0]
    tri = _tri(CHUNK)
    nw = nw_ref[...]

    pre = _mm(gk_ref[...], gkw_ref[...], exact=True) + gkb_ref[...]
    gk = (jnp.minimum(pre, 0.0) - jnp.log(1.0 + jnp.exp(-jnp.abs(pre)))) * (1.0 / GLA_GATE_NORM)
    bc_all = _mm_01_left(sel_ref[...], gk)
    bc_scr[...] = bc_all
    qd_scr[...] = q_ref[...] * (GLA_DK ** -0.5) * jnp.exp(bc_all)
    kd_scr[...] = k_ref[...] * jnp.exp(-bc_all)

    def chunk(c, carry):
        rows = pl.ds(pl.multiple_of(c * CHUNK, CHUNK), CHUNK)
        bc = bc_scr[rows, :]
        bl = bc[CHUNK - 1:CHUNK, :]
        qd = qd_scr[rows, :]
        kd = kd_scr[rows, :]
        kl = k_ref[rows, :] * jnp.exp(bl - bc)
        dec = jnp.exp(bl)
        heads = range(GLA_HEADS)
        sks = [slice(h * GLA_DK, (h + 1) * GLA_DK) for h in heads]
        svs = [slice(h * GLA_DV, (h + 1) * GLA_DV) for h in heads]
        att = [jnp.where(tri, _mm(qd[:, sk], kd[:, sk], _NT), 0.0) for sk in sks]
        vs = [v_ref[rows, sv] for sv in svs]
        sts = [st_ref[h] for h in heads]
        os_ = [_mm(att[h], vs[h]) + _mm(qd[:, sks[h]], sts[h], _NT) for h in heads]
        for h in heads:
            st_ref[h] = sts[h] * dec[:, sks[h]] + _mm(vs[h], kl[:, sks[h]], _TN)
        for h in heads:
            o = os_[h]
            ms = jnp.mean(o * o, axis=-1, keepdims=True)
            o_ref[rows, svs[h]] = o * lax.rsqrt(ms + GLA_NORM_EPS) * nw * _silu(g_ref[rows, svs[h]])
        return carry

    lax.fori_loop(0, tt // CHUNK, chunk, 0, unroll=2)


def _gla(p, gkw, gkb, nw, tt):
    bsz, t, _ = p.shape

    def col(name):
        off, w = EVEN_COLS[name]
        return pl.BlockSpec((None, tt, w), lambda b, i, j=off // w: (b, i, j))

    fix = lambda b, i: (0, 0)
    return pl.pallas_call(
        _gla_kernel,
        grid=(bsz, t // tt),
        in_specs=[col("q"), col("k"), col("v"), col("g"), col("gk"),
                  pl.BlockSpec(gkw.shape, fix), pl.BlockSpec(gkb.shape, fix), pl.BlockSpec(nw.shape, fix),
                  pl.BlockSpec((tt, tt), fix)],
        out_specs=pl.BlockSpec((None, tt, GLA_VW), lambda b, i: (b, i, 0)),
        out_shape=jax.ShapeDtypeStruct((bsz, t, GLA_VW), F32),
        scratch_shapes=[pltpu.VMEM((GLA_HEADS, GLA_DV, GLA_DK), F32)] + [pltpu.VMEM((tt, GLA_QK), F32)] * 3,
        compiler_params=_params("parallel", "arbitrary"),
        name="gla",
    )(p, p, p, p, p, gkw, gkb, nw, _chunk_sel(tt, CHUNK))


def _rglru_kernel(x_ref, gate_ref, cw_ref, cb_ref, wa_ref, ba_ref, wx_ref, bx_ref, lam_ref, o_ref,
                  tail_ref, h_ref, a_scr, u_scr):
    @pl.when(pl.program_id(1) == 0)
    def _():
        tail_ref[...] = jnp.zeros_like(tail_ref)
        h_ref[...] = jnp.zeros_like(h_ref)

    tt = x_ref.shape[0]
    x = x_ref[...]
    cw = cw_ref[...]
    xc = x * cw[CONV_K - 1:CONV_K, :] + cb_ref[...]
    for s in range(1, CONV_K):
        xc += _shift_rows(x, tail_ref, s) * cw[CONV_K - 1 - s:CONV_K - s, :]
    tail_ref[...] = x[tt - SUBLANES:, :]

    r = _sigmoid(_mm(xc, wa_ref[...]) + ba_ref[...])
    i = _sigmoid(_mm(xc, wx_ref[...]) + bx_ref[...])
    log_a = -LRU_C * r * _softplus(-lam_ref[...])
    a_scr[...] = jnp.exp(log_a)
    u_scr[...] = jnp.sqrt(1.0 - jnp.exp(2.0 * log_a)) * (i * xc)

    def step(t, h):
        h = a_scr[pl.ds(t, 1), :] * h + u_scr[pl.ds(t, 1), :]
        u_scr[pl.ds(t, 1), :] = h
        return h

    h_ref[...] = lax.fori_loop(0, tt, step, h_ref[...], unroll=8)
    gate = gate_ref[...]
    gelu = 0.5 * gate * (1.0 + jnp.tanh(math.sqrt(2.0 / math.pi) * (gate + 0.044715 * gate * gate * gate)))
    o_ref[...] = u_scr[...] * gelu


def _rglru(p, cw, cb, wa, ba, wx, bx, lam, tt):
    bsz, t, _ = p.shape
    w = LRU_WIDTH

    def col(name):
        off, _ = EVEN_COLS[name]
        return pl.BlockSpec((None, tt, w), lambda b, i, j=off // w: (b, i, j))

    fix = lambda b, i: (0, 0)
    vec = pl.BlockSpec((1, w), fix)
    return pl.pallas_call(
        _rglru_kernel,
        grid=(bsz, t // tt),
        in_specs=[col("xb"), col("gate"), pl.BlockSpec((CONV_K, w), fix), vec,
                  pl.BlockSpec((w, w), fix), vec, pl.BlockSpec((w, w), fix), vec, vec],
        out_specs=pl.BlockSpec((None, tt, w), lambda b, i: (b, i, 0)),
        out_shape=jax.ShapeDtypeStruct((bsz, t, w), F32),
        scratch_shapes=[pltpu.VMEM((SUBLANES, w), F32), pltpu.VMEM((1, w), F32),
                        pltpu.VMEM((tt, w), F32), pltpu.VMEM((tt, w), F32)],
        compiler_params=_params("parallel", "arbitrary"),
        name="rglru",
    )(p, p, cw, cb, wa, ba, wx, bx, lam)


def _ssd_kernel(z_ref, x_ref, bm_ref, cm_ref, dt_ref, cwx_ref, cbx_ref, cwb_ref, cbb_ref, cwc_ref, cbc_ref,
                dtb_ref, alog_ref, dskip_ref, nw_ref, expand_ref, sel_ref, o_ref,
                tx_ref, tb_ref, tc_ref, st_ref, xs_scr, bs_scr, cs_scr, acs_scr, xdt_scr, xdec_scr, ea_scr):
    @pl.when(pl.program_id(1) == 0)
    def _():
        tx_ref[...] = jnp.zeros_like(tx_ref)
        tb_ref[...] = jnp.zeros_like(tb_ref)
        tc_ref[...] = jnp.zeros_like(tc_ref)
        st_ref[...] = jnp.zeros_like(st_ref)

    tt = x_ref.shape[0]

    def conv_silu(src_ref, tail_ref, cw_ref, cb_ref, dst_ref):
        x = src_ref[...]
        cw = cw_ref[...]
        y = x * cw[CONV_K - 1:CONV_K, :] + cb_ref[...]
        for s in range(1, CONV_K):
            y += _shift_rows(x, tail_ref, s) * cw[CONV_K - 1 - s:CONV_K - s, :]
        tail_ref[...] = x[tt - SUBLANES:, :]
        dst_ref[...] = _silu(y)

    conv_silu(x_ref, tx_ref, cwx_ref, cbx_ref, xs_scr)
    conv_silu(bm_ref, tb_ref, cwb_ref, cbb_ref, bs_scr)
    conv_silu(cm_ref, tc_ref, cwc_ref, cbc_ref, cs_scr)

    tri = _tri(CHUNK)
    expand = expand_ref[...]
    dskip = dskip_ref[...]
    nw = nw_ref[...]
    hpg = SSD_HEADS // SSD_GROUPS

    dtc = _softplus(dt_ref[...] + dtb_ref[...])
    da = dtc * -jnp.exp(alog_ref[...])
    sums = _mm_01_left(sel_ref[...], da)
    acs_all = sums[:tt]
    tot = sums[tt:]
    acs_scr[...] = acs_all
    xdt_all = xs_scr[...] * _mm_01(dtc, expand)
    xdt_scr[...] = xdt_all
    xdec_scr[...] = xdt_all * _mm_01(jnp.exp(tot - acs_all), expand)
    ea_scr[...] = _mm_01(jnp.exp(acs_all), expand)

    def chunk(c, carry):
        rows = pl.ds(pl.multiple_of(c * CHUNK, CHUNK), CHUNK)
        acs = acs_scr[rows, :]
        acs_t = acs.T
        ea_x = ea_scr[rows, :]
        x = xs_scr[rows, :]
        xdt = xdt_scr[rows, :]
        xdec = xdec_scr[rows, :]
        cd_x = ea_x[CHUNK - 1:CHUNK, :]
        ys = []
        for g in range(SSD_GROUPS):
            sg = slice(g * SSD_GROUP_WIDTH, (g + 1) * SSD_GROUP_WIDTH)
            ss = slice(g * SSD_STATE, (g + 1) * SSD_STATE)
            bg = bs_scr[rows, ss]
            cg = cs_scr[rows, ss]
            cb = _mm(cg, bg, _NT)
            st = st_ref[g]
            yg = _mm(cg, st) * ea_x[:, sg]
            st_ref[g] = st * cd_x[:, sg] + _mm(bg, xdec[:, sg], _TN)
            yh = []
            for j in range(hpg):
                h = g * hpg + j
                seg = acs[:, h:h + 1] - acs_t[h:h + 1, :]
                lmat = jnp.exp(jnp.where(tri, seg, -jnp.inf))
                sh = slice(h * SSD_HEADDIM, (h + 1) * SSD_HEADDIM)
                yh.append(_mm(cb * lmat, xdt[:, sh]))
            ys.append(yg + jnp.concatenate(yh, axis=1))
        y = jnp.concatenate(ys, axis=1) + x * dskip
        y = y * _silu(z_ref[rows, :])
        outs = []
        for g in range(SSD_GROUPS):
            sg = slice(g * SSD_GROUP_WIDTH, (g + 1) * SSD_GROUP_WIDTH)
            yg = y[:, sg]
            ms = jnp.mean(yg * yg, axis=-1, keepdims=True)
            outs.append(yg * lax.rsqrt(ms + SSD_NORM_EPS))
        o_ref[rows, :] = jnp.concatenate(outs, axis=1) * nw
        return carry

    lax.fori_loop(0, tt // CHUNK, chunk, 0, unroll=2)


def _ssd(p, cw, cb, dtb, alog, dskip_x, nw, expand, tt):
    bsz, t, _ = p.shape

    def col(name):
        off, w = ODD_COLS[name]
        return pl.BlockSpec((None, tt, w), lambda b, i, j=off // w: (b, i, j))

    fix = lambda b, i: (0, 0)
    full = lambda a: pl.BlockSpec(a.shape, fix)
    gs = SSD_GROUPS * SSD_STATE
    cwx, cwb, cwc = cw[:, :SSD_INNER], cw[:, SSD_INNER:SSD_INNER + gs], cw[:, SSD_INNER + gs:]
    cbx, cbb, cbc = cb[:, :SSD_INNER], cb[:, SSD_INNER:SSD_INNER + gs], cb[:, SSD_INNER + gs:]
    sel = jnp.concatenate([_chunk_sel(tt, CHUNK), _chunk_sel(tt, CHUNK, "all")], axis=0)
    args = (cwx, cbx, cwb, cbb, cwc, cbc, dtb, alog, dskip_x, nw, expand, sel)
    return pl.pallas_call(
        _ssd_kernel,
        grid=(bsz, t // tt),
        in_specs=[col("z"), col("x"), col("bm"), col("cm"), col("dt")] + [full(a) for a in args],
        out_specs=pl.BlockSpec((None, tt, SSD_INNER), lambda b, i: (b, i, 0)),
        out_shape=jax.ShapeDtypeStruct((bsz, t, SSD_INNER), F32),
        scratch_shapes=[pltpu.VMEM((SUBLANES, SSD_INNER), F32), pltpu.VMEM((SUBLANES, gs), F32),
                        pltpu.VMEM((SUBLANES, gs), F32),
                        pltpu.VMEM((SSD_GROUPS, SSD_STATE, SSD_GROUP_WIDTH), F32),
                        pltpu.VMEM((tt, SSD_INNER), F32), pltpu.VMEM((tt, gs), F32), pltpu.VMEM((tt, gs), F32),
                        pltpu.VMEM((tt, LANES), F32)] + [pltpu.VMEM((tt, SSD_INNER), F32)] * 3,
        compiler_params=_params("parallel", "arbitrary"),
        name="ssd",
    )(p, p, p, p, p, *args)


def _rwkv_kernel(r_ref, k_ref, v_ref, wa_ref, xg_ref, mur_ref, muk_ref, muv_ref, muwa_ref, mug_ref,
                 w0_ref, w2_ref, a0_ref, a2_ref, g2_ref, kk_ref, ka_ref, rk_ref, lng_ref, lnb_ref, ones_ref, sel_ref,
                 o_ref, tr_ref, tk_ref, tv_ref, twa_ref, tg_ref, st_ref, y_scr,
                 at_scr, rt_scr, bt_scr, kt_scr, bh_scr, kh_scr, v_scr, wc_scr):
    @pl.when(pl.program_id(1) == 0)
    def _():
        for ref in (tr_ref, tk_ref, tv_ref, twa_ref, tg_ref, st_ref):
            ref[...] = jnp.zeros_like(ref)

    tt = r_ref.shape[0]
    hd = RWKV_HEADDIM

    def mix(p_ref, tail_ref, mu_ref):
        p = p_ref[...]
        prev = _shift_rows(p, tail_ref, 1)
        tail_ref[...] = p[tt - SUBLANES:, :]
        return p + (prev - p) * mu_ref[...]

    r = mix(r_ref, tr_ref, mur_ref)
    k = mix(k_ref, tk_ref, muk_ref)
    v = mix(v_ref, tv_ref, muv_ref)
    xwa = mix(wa_ref, twa_ref, muwa_ref)
    xg = mix(xg_ref, tg_ref, mug_ref)

    lane = lax.broadcasted_iota(jnp.int32, xwa.shape, 1)
    lora_in = jnp.where(lane < RWKV_DECAY_LORA, jnp.tanh(xwa), xwa)
    w_log = -_softplus(-(w0_ref[...] + _mm(lora_in, w2_ref[...], exact=True))) - 0.5
    lw = -jnp.exp(w_log)
    a_sig = _sigmoid(a0_ref[...] + _mm(lora_in, a2_ref[...], exact=True))
    gate = _mm(_sigmoid(xg), g2_ref[...], exact=True)
    ones_bd = ones_ref[...]
    kk = k * kk_ref[...]
    kk = kk / jnp.maximum(jnp.sqrt(_mm_01(kk * kk, ones_bd)), 1e-12)
    k = k * (1.0 + (a_sig - 1.0) * ka_ref[...])

    cs = RWKV_CHUNK
    tri_incl = _tri(cs)
    tri_strict = _tri(cs, strict=True)
    tri_col = jnp.concatenate([tri_strict, tri_incl], axis=0)
    eye = (tri_incl & ~tri_strict).astype(F32)

    sums = _mm_01_left(sel_ref[...], lw)
    cum = sums[:tt]
    tot = sums[tt:]
    bvec = kk * a_sig
    e_neg = jnp.exp(-cum)
    e_last = jnp.exp(tot - cum)
    for ref, val in ((at_scr, -kk * jnp.exp(cum - lw)), (rt_scr, r * jnp.exp(cum)), (bt_scr, bvec * e_neg),
                     (kt_scr, k * e_neg), (bh_scr, bvec * e_last), (kh_scr, k * e_last), (v_scr, v),
                     (wc_scr, jnp.exp(tot))):
        ref[...] = val

    def chunk(c, carry):
        rows = pl.ds(pl.multiple_of(c * cs, cs), cs)
        at = at_scr[rows, :]
        rt = rt_scr[rows, :]
        bt = bt_scr[rows, :]
        kt = kt_scr[rows, :]
        bh = bh_scr[rows, :]
        kh = kh_scr[rows, :]
        vc = v_scr[rows, :]
        wc = wc_scr[pl.ds(pl.multiple_of(c * cs, cs), 1), :]
        heads = range(RWKV_HEADS)
        sls = [slice(h * hd, (h + 1) * hd) for h in heads]
        amat = [_mm(jnp.concatenate([at[:, sl], rt[:, sl]], axis=0),
                    jnp.concatenate([bt[:, sl], kt[:, sl]], axis=0), _NT) for sl in sls]
        a_ab = [jnp.where(tri_strict, m[:cs, :cs], 0.0) for m in amat]
        a_rb = [jnp.where(tri_incl, m[cs:, :cs], 0.0) for m in amat]
        a_xk = [jnp.where(tri_col, m[:, cs:], 0.0) for m in amat]
        avrv = [_mm(a_xk[h], vc[:, sls[h]]) for h in heads]
        inv = [eye + m for m in a_ab]
        pw = a_ab
        for _ in range(int(math.log2(cs)) - 1):
            pw = [_mm(m, m) for m in pw]
            inv = [inv[h] + _mm(inv[h], pw[h]) for h in heads]
        tu = [_mm(inv[h], jnp.concatenate([at[:, sls[h]], avrv[h][:cs]], axis=1)) for h in heads]
        qy = [_mm(a_rb[h], tu[h]) + jnp.concatenate([rt[:, sls[h]], avrv[h][cs:]], axis=1)
              for h in heads]
        zb = [_mm(tu[h], bh[:, sls[h]], _TN) for h in heads]
        vk = [_mm(vc[:, sls[h]], kh[:, sls[h]], _TN) for h in heads]
        for h in heads:
            st = st_ref[h]
            y_scr[rows, sls[h]] = _mm(qy[h][:, :hd], st, _NT) + qy[h][:, hd:]
            st_ref[h] = st * wc[:, sls[h]] + _mm(st, zb[h][:hd]) + zb[h][hd:] + vk[h]
        return carry

    lax.fori_loop(0, tt // cs, chunk, 0, unroll=2)

    y = y_scr[...]
    inv_n = 1.0 / hd
    mu_y = _mm_01(y, ones_bd) * inv_n
    dy = y - mu_y
    var_y = _mm_01(dy * dy, ones_bd) * inv_n
    yn = dy * lax.rsqrt(var_y + RWKV_GN_EPS) * lng_ref[...] + lnb_ref[...]
    bonus = _mm_01(r * k * rk_ref[...], ones_bd) * v
    o_ref[...] = (yn + bonus) * gate


def _rwkv(p, mu, w0, w2p, a0, a2p, g2, k_k, k_a, r_k, ln_g, ln_b, ones_bd, tt):
    bsz, t, _ = p.shape
    w = RWKV_WIDTH

    def col(name):
        off, wd = ODD_COLS[name]
        return pl.BlockSpec((None, tt, wd), lambda b, i, j=off // wd: (b, i, j))

    fix = lambda b, i: (0, 0)
    full = lambda a: pl.BlockSpec(a.shape, fix)
    mur, muk, muv = mu[:, :w], mu[:, w:2 * w], mu[:, 2 * w:3 * w]
    muwa, mug = mu[:, 3 * w:3 * w + LANES], mu[:, 3 * w + LANES:]
    sel = jnp.concatenate([_chunk_sel(tt, RWKV_CHUNK), _chunk_sel(tt, RWKV_CHUNK, "all")], axis=0)
    args = (mur, muk, muv, muwa, mug, w0, w2p, a0, a2p, g2, k_k, k_a, r_k, ln_g, ln_b, ones_bd, sel)
    return pl.pallas_call(
        _rwkv_kernel,
        grid=(bsz, t // tt),
        in_specs=[col("r"), col("k"), col("v"), col("wa"), col("xg")] + [full(a) for a in args],
        out_specs=pl.BlockSpec((None, tt, w), lambda b, i: (b, i, 0)),
        out_shape=jax.ShapeDtypeStruct((bsz, t, w), F32),
        scratch_shapes=[pltpu.VMEM((SUBLANES, w), F32), pltpu.VMEM((SUBLANES, w), F32),
                        pltpu.VMEM((SUBLANES, w), F32), pltpu.VMEM((SUBLANES, LANES), F32),
                        pltpu.VMEM((SUBLANES, LANES), F32),
                        pltpu.VMEM((RWKV_HEADS, RWKV_HEADDIM, RWKV_HEADDIM), F32)]
        + [pltpu.VMEM((tt, w), F32)] * 9,
        compiler_params=_params("parallel", "arbitrary"),
        name="rwkv7",
    )(p, p, p, p, p, *args)


def _reorder_cols(w, pieces, order, width):
    out = jnp.zeros((w.shape[0], width), w.dtype)
    for name, (off, _) in order.items():
        start, size = pieces[name]
        out = out.at[:, off:off + size].set(w[:, start:start + size])
    return out


def _block_diag(w):
    n, i, j = w.shape
    eye = jnp.eye(n, dtype=w.dtype)
    return (eye[:, None, :, None] * w[:, :, None, :]).reshape(n * i, n * j)


def _row(v, width=None):
    v = v.reshape(1, -1).astype(F32)
    if width is not None and v.shape[1] < width:
        v = jnp.pad(v, ((0, 0), (0, width - v.shape[1])))
    return v


def _even_layer(x, x_in, w_in, gk_w2, gk_b, gla_norm, conv_w, conv_b, wa, ba, wx, bx, lam,
                w_out, ln1_g, ln1_b, f_w1, f_w3, f_w2, ln2_g, ln2_b, bsz, t):
    qk, vw, lw = GLA_QK, GLA_VW, LRU_WIDTH
    pieces = dict(q=(0, qk), k=(qk, qk), v=(2 * qk, vw), g=(2 * qk + vw, vw), gk=(2 * qk + 2 * vw, GLA_GK_RANK),
                  xb=(2 * qk + 2 * vw + GLA_GK_RANK, lw), gate=(2 * qk + 2 * vw + GLA_GK_RANK + lw, lw))
    w_in_r = _reorder_cols(w_in, pieces, EVEN_COLS, EVEN_WIDTH).astype(BF16)
    p = _proj(x_in, w_in_r, tm=min(512, x.shape[0])).reshape(bsz, t, EVEN_WIDTH)
    tt = min(512, t)
    gkw = jnp.pad(gk_w2, ((0, LANES - GLA_GK_RANK), (0, 0)))
    y_a = _gla(p, gkw, _row(gk_b), _row(gla_norm), tt)
    y_b = _rglru(p, conv_w, _row(conv_b), _block_diag(wa).astype(BF16), _row(ba), _block_diag(wx).astype(BF16),
                 _row(bx), _row(lam), tt)
    n = bsz * t
    tm = min(512, n)
    x1, x1b = _outproj_ln(y_a.reshape(n, vw), y_b.reshape(n, lw), w_out[:vw].astype(BF16), w_out[vw:].astype(BF16),
                          x, _row(ln1_g), _row(ln1_b), tm)
    return _ffn_ln(x1b, f_w1.astype(BF16), f_w3.astype(BF16), f_w2.astype(BF16), x1,
                   _row(ln2_g), _row(ln2_b), tm=min(512, n))


def _odd_layer(x, xb, w_in, conv_w, conv_b, dt_bias, a_log, d_skip, ssd_norm, mu, w0, w2, a0, a2, g2, k_k, k_a,
               r_k, rln_g, rln_b, w_out, ln1_g, ln1_b, router, ew1, ew3, ew2, ln2_g, ln2_b, bsz, t):
    si, gs, rw = SSD_INNER, SSD_GROUPS * SSD_STATE, RWKV_WIDTH
    o = 2 * si + 2 * gs + SSD_HEADS
    pieces = dict(z=(0, si), x=(si, si), bm=(2 * si, gs), cm=(2 * si + gs, gs), dt=(2 * si + 2 * gs, SSD_HEADS),
                  r=(o, rw), k=(o + rw, rw), v=(o + 2 * rw, rw),
                  wa=(o + 3 * rw, RWKV_DECAY_LORA + RWKV_AAA_LORA), xg=(o + 3 * rw + LANES, RWKV_GATE_LORA))
    w_in_r = _reorder_cols(w_in, pieces, ODD_COLS, ODD_WIDTH).astype(BF16)
    n = bsz * t
    p = _proj(xb, w_in_r, tm=min(512, n)).reshape(bsz, t, ODD_WIDTH)
    expand = jnp.pad(jnp.repeat(jnp.eye(SSD_HEADS, dtype=BF16), SSD_HEADDIM, axis=1), ((0, LANES - SSD_HEADS), (0, 0)))
    y_c = _ssd(p, conv_w, _row(conv_b), _row(dt_bias, LANES), _row(a_log, LANES),
               _row(jnp.repeat(d_skip, SSD_HEADDIM)), _row(ssd_norm), expand, tt=min(256, t))
    zeros = jnp.zeros((RWKV_DECAY_LORA, rw), F32)
    ones_bd = _block_diag(jnp.ones((RWKV_HEADS, RWKV_HEADDIM, RWKV_HEADDIM), BF16))
    y_d = _rwkv(p, _row(mu), _row(w0), jnp.concatenate([w2, zeros]), _row(a0), jnp.concatenate([zeros, a2]), g2,
                _row(k_k), _row(k_a), _row(r_k), _row(rln_g), _row(rln_b), ones_bd, tt=min(256, t))
    tm = min(512, n)
    x1, x1b = _outproj_ln(y_c.reshape(n, si), y_d.reshape(n, rw), w_out[:si].astype(BF16), w_out[si:].astype(BF16),
                          x, _row(ln1_g), _row(ln1_b), tm)
    y = _moe_ln(x1, x1b, jnp.pad(router, ((0, 0), (0, LANES - N_EXPERTS))), ew1.astype(BF16), ew3.astype(BF16),
                ew2.astype(BF16), _row(ln2_g), _row(ln2_b))
    return y, None


def kernel(x, e_w_in, e_gk_w2, e_gk_b, e_gla_norm, e_conv_w, e_conv_b, e_lru_wa, e_lru_ba, e_lru_wx, e_lru_bx, e_lru_lambda, e_w_out, e_ln1_g, e_ln1_b, e_ffn_w1, e_ffn_w3, e_ffn_w2, e_ln2_g, e_ln2_b, o_w_in, o_conv_w, o_conv_b, o_dt_bias, o_a_log, o_d_skip, o_ssd_norm, o_rwkv_mu, o_rwkv_w0, o_rwkv_w2, o_rwkv_a0, o_rwkv_a2, o_rwkv_g2, o_rwkv_k_k, o_rwkv_k_a, o_rwkv_r_k, o_rwkv_ln_g, o_rwkv_ln_b, o_w_out, o_ln1_g, o_ln1_b, o_router, o_exp_w1, o_exp_w3, o_exp_w2, o_ln2_g, o_ln2_b):
    bsz, t, d = x.shape
    h = x.reshape(bsz * t, d)
    hb = h
    for i in range(DEPTH):
        j = i // 2
        if i % 2 == 0:
            h, hb = _even_layer(h, hb, e_w_in[j], e_gk_w2[j], e_gk_b[j], e_gla_norm[j], e_conv_w[j], e_conv_b[j],
                                e_lru_wa[j], e_lru_ba[j], e_lru_wx[j], e_lru_bx[j], e_lru_lambda[j], e_w_out[j],
                                e_ln1_g[j], e_ln1_b[j], e_ffn_w1[j], e_ffn_w3[j], e_ffn_w2[j], e_ln2_g[j],
                                e_ln2_b[j], bsz, t)
        else:
            h, hb = _odd_layer(h, hb, o_w_in[j], o_conv_w[j], o_conv_b[j], o_dt_bias[j], o_a_log[j], o_d_skip[j],
                               o_ssd_norm[j], o_rwkv_mu[j], o_rwkv_w0[j], o_rwkv_w2[j], o_rwkv_a0[j],
                               o_rwkv_a2[j], o_rwkv_g2[j], o_rwkv_k_k[j], o_rwkv_k_a[j], o_rwkv_r_k[j],
                               o_rwkv_ln_g[j], o_rwkv_ln_b[j], o_w_out[j], o_ln1_g[j], o_ln1_b[j], o_router[j],
                               o_exp_w1[j], o_exp_w3[j], o_exp_w2[j], o_ln2_g[j], o_ln2_b[j], bsz, t)
    return h.reshape(bsz, t, d)
```

```python
import functools
import math

import jax
import jax.numpy as jnp
from jax import lax
from jax.experimental import pallas as pl
from jax.experimental.pallas import tpu as pltpu

F32 = jnp.float32
BF16 = jnp.bfloat16
HIGHEST = lax.Precision.HIGHEST

D_MODEL = 1024
DEPTH = 2
DEEPNORM_ALPHA = (2 * DEPTH) ** 0.25
LN_EPS = 1e-5
CONV_K = 4
CHUNK = 64

GLA_HEADS = 4
GLA_DK = 64
GLA_DV = 128
GLA_QK = GLA_HEADS * GLA_DK
GLA_VW = GLA_HEADS * GLA_DV
GLA_GK_RANK = 16
GLA_GATE_NORM = 16.0
GLA_NORM_EPS = 1e-5

LRU_WIDTH = 512
LRU_BLOCKS = 8
LRU_C = 8.0

SSD_HEADS = 16
SSD_HEADDIM = 64
SSD_INNER = SSD_HEADS * SSD_HEADDIM
SSD_GROUPS = 2
SSD_STATE = 128
SSD_GROUP_WIDTH = SSD_INNER // SSD_GROUPS
SSD_NORM_EPS = 1e-5

RWKV_HEADS = 8
RWKV_HEADDIM = 64
RWKV_WIDTH = RWKV_HEADS * RWKV_HEADDIM
RWKV_DECAY_LORA = 64
RWKV_AAA_LORA = 64
RWKV_GATE_LORA = 128
RWKV_GN_EPS = 64e-5
RWKV_CHUNK = 128

N_EXPERTS = 8
LANES = 128
SUBLANES = 8
assert N_EXPERTS == SUBLANES
MOE_TB = 512
MOE_TR = 512
MOE_TE = 512
assert MOE_TE % MOE_TR == 0
VMEM_LIMIT = 56 * 1024 * 1024

EVEN_COLS = dict(v=(0, 512), g=(512, 512), xb=(1024, 512), gate=(1536, 512), q=(2048, 256), k=(2304, 256),
                 gk=(2560, 128))
EVEN_WIDTH = 2688
ODD_COLS = dict(z=(0, 1024), x=(1024, 1024), r=(2048, 512), k=(2560, 512), v=(3072, 512), bm=(3584, 256),
                cm=(3840, 256), wa=(4096, 128), xg=(4224, 128), dt=(4352, 128))
ODD_WIDTH = 4480


def _mm(a, b, dims=((1,), (0,)), exact=False):
    dn = (dims, ((), ()))
    if exact:
        return lax.dot_general(a.astype(F32), b.astype(F32), dn, precision=HIGHEST, preferred_element_type=F32)
    return lax.dot_general(a.astype(BF16), b.astype(BF16), dn, preferred_element_type=F32)


_NT = ((1,), (1,))
_TN = ((0,), (0,))


def _mm_01(x, sel):
    hi = x.astype(BF16)
    lo = (x - hi.astype(F32)).astype(BF16)
    return (jnp.dot(hi, sel, preferred_element_type=F32) + jnp.dot(lo, sel, preferred_element_type=F32))


def _sigmoid(x):
    return 1.0 / (1.0 + jnp.exp(-x))


def _softplus(x):
    return jnp.maximum(x, 0.0) + jnp.log(1.0 + jnp.exp(-jnp.abs(x)))


def _silu(x):
    return x * _sigmoid(x)


def _layer_norm(h, g, b):
    mu = jnp.mean(h, axis=-1, keepdims=True)
    d = h - mu
    var = jnp.mean(d * d, axis=-1, keepdims=True)
    return d * lax.rsqrt(var + LN_EPS) * g + b


def _params(*sem):
    return pltpu.CompilerParams(dimension_semantics=sem, vmem_limit_bytes=VMEM_LIMIT)


def _tri(n, strict=False):
    r = lax.broadcasted_iota(jnp.int32, (n, n), 0)
    c = lax.broadcasted_iota(jnp.int32, (n, n), 1)
    return (r > c) if strict else (r >= c)


def _chunk_sel(n, chunk, kind="incl"):
    r = jnp.arange(n)[:, None]
    c = jnp.arange(n)[None, :]
    first = r - r % chunk
    upper = r if kind == "incl" else first + (chunk - 1)
    return ((c >= first) & (c <= upper)).astype(BF16)


def _mm_01_left(sel, x):
    hi = x.astype(BF16)
    lo = (x - hi.astype(F32)).astype(BF16)
    return jnp.dot(sel, hi, preferred_element_type=F32) + jnp.dot(sel, lo, preferred_element_type=F32)


def _shift_rows(p, tail_ref, s):
    n = p.shape[0]
    rolled = pltpu.roll(p, s, axis=0)
    head = pltpu.roll(tail_ref[...], s, axis=0)
    row = lax.broadcasted_iota(jnp.int32, (SUBLANES, p.shape[1]), 0)
    fixed = jnp.where(row < s, head, rolled[:SUBLANES])
    return jnp.concatenate([fixed, rolled[SUBLANES:]], axis=0) if n > SUBLANES else fixed


def _proj_kernel(x_ref, w_ref, o_ref):
    o_ref[...] = jnp.dot(x_ref[...].astype(BF16), w_ref[...], preferred_element_type=F32)


def _resident(shape):
    return pl.BlockSpec(shape, lambda *_: (0,) * len(shape), pipeline_mode=pl.Buffered(1))


def _proj(x, w, tm):
    m, k = x.shape
    n = w.shape[1]
    return pl.pallas_call(
        _proj_kernel,
        grid=(m // tm,),
        in_specs=[pl.BlockSpec((tm, k), lambda i: (i, 0)), _resident((k, n))],
        out_specs=pl.BlockSpec((tm, n), lambda i: (i, 0)),
        out_shape=jax.ShapeDtypeStruct((m, n), F32),
        compiler_params=_params("parallel"),
        name="in_proj",
    )(x, w)


def _outproj_ln_kernel(ya_ref, yb_ref, wa_ref, wb_ref, res_ref, g_ref, b_ref, o_ref, ob_ref):
    acc = jnp.dot(ya_ref[...].astype(BF16), wa_ref[...], preferred_element_type=F32)
    acc += jnp.dot(yb_ref[...].astype(BF16), wb_ref[...], preferred_element_type=F32)
    y = _layer_norm(DEEPNORM_ALPHA * res_ref[...] + acc, g_ref[...], b_ref[...])
    o_ref[...] = y
    ob_ref[...] = y.astype(BF16)


def _outproj_ln(ya, yb, wa, wb, res, g, b, tm):
    m = ya.shape[0]
    d = res.shape[1]
    row = lambda i: (i, 0)
    fix = lambda i: (0, 0)
    return pl.pallas_call(
        _outproj_ln_kernel,
        grid=(m // tm,),
        in_specs=[pl.BlockSpec((tm, ya.shape[1]), row), pl.BlockSpec((tm, yb.shape[1]), row),
                  pl.BlockSpec(wa.shape, fix), pl.BlockSpec(wb.shape, fix), pl.BlockSpec((tm, d), row),
                  pl.BlockSpec((1, d), fix), pl.BlockSpec((1, d), fix)],
        out_specs=[pl.BlockSpec((tm, d), row), pl.BlockSpec((tm, d), row)],
        out_shape=[jax.ShapeDtypeStruct((m, d), F32), jax.ShapeDtypeStruct((m, d), BF16)],
        compiler_params=_params("parallel"),
        name="out_proj_ln",
    )(ya, yb, wa, wb, res, g, b)


def _ffn_kernel(xb_ref, w1_ref, w3_ref, w2_ref, res_ref, g_ref, b_ref, o_ref, ob_ref):
    x = xb_ref[...]
    h1 = jnp.dot(x, w1_ref[...], preferred_element_type=F32)
    h3 = jnp.dot(x, w3_ref[...], preferred_element_type=F32)
    ff = jnp.dot((_silu(h1) * h3).astype(BF16), w2_ref[...], preferred_element_type=F32)
    y = _layer_norm(DEEPNORM_ALPHA * res_ref[...] + ff, g_ref[...], b_ref[...])
    o_ref[...] = y
    ob_ref[...] = y.astype(BF16)


def _ffn_ln(xb, w1, w3, w2, res, g, b, tm):
    m, d = xb.shape
    row = lambda i: (i, 0)
    return pl.pallas_call(
        _ffn_kernel,
        grid=(m // tm,),
        in_specs=[pl.BlockSpec((tm, d), row), _resident(w1.shape), _resident(w3.shape), _resident(w2.shape),
                  pl.BlockSpec((tm, d), row), _resident((1, d)), _resident((1, d))],
        out_specs=[pl.BlockSpec((tm, d), row), pl.BlockSpec((tm, d), row)],
        out_shape=[jax.ShapeDtypeStruct((m, d), F32), jax.ShapeDtypeStruct((m, d), BF16)],
        compiler_params=_params("parallel"),
        name="ffn_ln",
    )(xb, w1, w3, w2, res, g, b)


def _router_kernel(x_ref, w_ref, gates_ref, rank_ref, rank_t_ref, cnt_ref, carry_ref):
    @pl.when(pl.program_id(0) == 0)
    def _():
        carry_ref[...] = jnp.zeros_like(carry_ref)

    x = x_ref[...]
    w = w_ref[...]
    x_hi = x.astype(BF16)
    w_hi = w.astype(BF16)
    x_lo = (x - x_hi.astype(F32)).astype(BF16)
    w_lo = (w - w_hi.astype(F32)).astype(BF16)
    logits = (jnp.dot(x_hi, w_hi, preferred_element_type=F32) + jnp.dot(x_lo, w_hi, preferred_element_type=F32)
              + jnp.dot(x_hi, w_lo, preferred_element_type=F32))
    lane = lax.broadcasted_iota(jnp.int32, logits.shape, 1)
    neg = jnp.float32(-jnp.inf)
    l1 = jnp.where(lane < N_EXPERTS, logits, neg)
    m1 = jnp.max(l1, axis=1, keepdims=True)
    i1 = jnp.min(jnp.where(l1 == m1, lane, LANES), axis=1, keepdims=True)
    l2 = jnp.where(lane == i1, neg, l1)
    m2 = jnp.max(l2, axis=1, keepdims=True)
    i2 = jnp.min(jnp.where(l2 == m2, lane, LANES), axis=1, keepdims=True)
    ex = jnp.exp(m2 - m1)
    w_top = 1.0 / (1.0 + ex)
    gates_ref[...] = jnp.where(lane == i1, w_top, 0.0) + jnp.where(lane == i2, ex * w_top, 0.0)

    sel = jnp.where(lane == i1, 1.0, 0.0) + jnp.where(lane == i2, 1.0, 0.0)
    tb = sel.shape[0]
    before = _mm(_tri(tb, strict=True).astype(F32), sel)
    carry = carry_ref[...]
    rank = jnp.where(sel > 0.0, carry + before, -1.0)
    rank_ref[...] = rank
    rank_t_ref[...] = rank.T[:SUBLANES, :]
    carry = carry + jnp.sum(sel, axis=0, keepdims=True)
    carry_ref[...] = carry
    cnt_ref[...] = jnp.broadcast_to(carry, cnt_ref.shape)


def _router(x, w):
    m, d = x.shape
    nblk = m // MOE_TB
    return pl.pallas_call(
        _router_kernel,
        grid=(nblk,),
        in_specs=[pl.BlockSpec((MOE_TB, d), lambda i: (i, 0)), pl.BlockSpec((d, LANES), lambda i: (0, 0))],
        out_specs=[pl.BlockSpec((MOE_TB, LANES), lambda i: (i, 0)), pl.BlockSpec((MOE_TB, LANES), lambda i: (i, 0)),
                   pl.BlockSpec((SUBLANES, MOE_TB), lambda i: (0, i)), pl.BlockSpec((SUBLANES, LANES), lambda i: (i, 0))],
        out_shape=[jax.ShapeDtypeStruct((m, LANES), F32), jax.ShapeDtypeStruct((m, LANES), F32),
                   jax.ShapeDtypeStruct((SUBLANES, m), F32), jax.ShapeDtypeStruct((nblk * SUBLANES, LANES), F32)],
        scratch_shapes=[pltpu.VMEM((1, LANES), F32)],
        compiler_params=_params("arbitrary"),
        name="router",
    )(x, w)


def _moe_plan(cnt, n_tok):
    i32 = jnp.int32
    nblk = n_tok // MOE_TB
    c_inc = cnt.reshape(nblk, SUBLANES, LANES)[:, 0, :N_EXPERTS].astype(i32)
    c_exc = jnp.concatenate([jnp.zeros((1, N_EXPERTS), i32), c_inc[:-1]], axis=0)
    gsz = (c_inc[-1] + MOE_TE - 1) // MOE_TE * MOE_TE
    gend = jnp.cumsum(gsz)
    start = gend - gsz
    n_et = _moe_rows(n_tok) // MOE_TE
    n_act = gend[-1] // MOE_TE
    et = jnp.minimum(jnp.arange(n_et, dtype=i32), jnp.maximum(n_act - 1, 0))
    et_expert = jnp.sum((gend[None, :] <= (et * MOE_TE)[:, None]).astype(i32), axis=1)
    et_active = (jnp.arange(n_et, dtype=i32) < n_act).astype(i32)

    lo = start[None, :] + c_exc
    hi = start[None, :] + c_inc
    first_tile = lo // MOE_TR
    n_items = jnp.where(hi > lo, (hi - 1) // MOE_TR - first_tile + 1, 0)
    wmax = _moe_rows(n_tok) // MOE_TR + N_EXPERTS * nblk
    blk_id = jnp.broadcast_to(jnp.arange(nblk, dtype=i32)[:, None], (nblk, N_EXPERTS))
    exp_id = jnp.broadcast_to(jnp.arange(N_EXPERTS, dtype=i32)[None, :], (nblk, N_EXPERTS))

    def worklist(expert_major):
        flat = (lambda a: a.T.reshape(-1)) if expert_major else (lambda a: a.reshape(-1))
        n, ft, blk, exp = flat(n_items), flat(first_tile), flat(blk_id), flat(exp_id)
        inc = jnp.cumsum(n)
        total = inc[-1]
        w = jnp.arange(wmax, dtype=i32)
        wc = jnp.minimum(w, jnp.maximum(total - 1, 0))
        idx = jnp.sum((inc[None, :] <= wc[:, None]).astype(i32), axis=1)
        tile = ft[idx] + wc - (inc - n)[idx]
        valid = w < total
        key = tile if expert_major else blk[idx]
        first = valid & ((w == 0) | (key != jnp.roll(key, 1)))
        last = valid & ((w == total - 1) | (key != jnp.roll(key, -1)))
        return tile, blk[idx], exp[idx], valid.astype(i32), first.astype(i32), last.astype(i32)

    return start, (et, et_expert, et_active), worklist(True), worklist(False)


def _moe_rows(n_tok):
    return 2 * n_tok + N_EXPERTS * MOE_TE


def _one_hot_rows(dest, tile):
    row = lax.broadcasted_iota(jnp.int32, (MOE_TR, MOE_TB), 0) + tile * MOE_TR
    return jnp.where(dest == row.astype(F32), 1.0, 0.0).astype(BF16)


def _dispatch_kernel(tile_ref, blk_ref, exp_ref, valid_ref, first_ref, x_ref, dest_ref, xs_ref):
    w = pl.program_id(0)

    @pl.when(valid_ref[w] == 1)
    def _():
        onehot = _one_hot_rows(dest_ref[pl.ds(exp_ref[w], 1), :], tile_ref[w])
        xg = jnp.dot(onehot, x_ref[...], preferred_element_type=F32).astype(BF16)

        @pl.when(first_ref[w] == 1)
        def _():
            xs_ref[...] = xg

        @pl.when(first_ref[w] == 0)
        def _():
            xs_ref[...] += xg


def _dispatch(plan, xb, dest_t):
    tile, blk, exp, valid, first, _ = plan
    n, d = xb.shape
    rows = _moe_rows(n)
    grid_spec = pltpu.PrefetchScalarGridSpec(
        num_scalar_prefetch=5,
        grid=(tile.shape[0],),
        in_specs=[pl.BlockSpec((MOE_TB, d), lambda w, t, b, e, v, f: (b[w], 0)),
                  pl.BlockSpec((SUBLANES, MOE_TB), lambda w, t, b, e, v, f: (0, b[w]))],
        out_specs=pl.BlockSpec((MOE_TR, d), lambda w, t, b, e, v, f: (t[w], 0)),
    )
    return pl.pallas_call(
        _dispatch_kernel,
        grid_spec=grid_spec,
        out_shape=jax.ShapeDtypeStruct((rows, d), BF16),
        compiler_params=_params("arbitrary"),
        name="moe_dispatch",
    )(tile, blk, exp, valid, first, xb, dest_t)


def _expert_kernel(et_ref, ee_ref, ea_ref, xs_ref, w1_ref, w3_ref, w2_ref, ys_ref, acc_ref):
    j = pl.program_id(0)
    f = pl.program_id(1)

    @pl.when(ea_ref[j] == 1)
    def _():
        @pl.when(f == 0)
        def _():
            acc_ref[...] = jnp.zeros_like(acc_ref)

        x = xs_ref[...]
        h1 = jnp.dot(x, w1_ref[...], preferred_element_type=F32)
        h3 = jnp.dot(x, w3_ref[...], preferred_element_type=F32)
        acc_ref[...] += jnp.dot((_silu(h1) * h3).astype(BF16), w2_ref[...], preferred_element_type=F32)

        @pl.when(f == pl.num_programs(1) - 1)
        def _():
            ys_ref[...] = acc_ref[...].astype(BF16)


def _experts(plan, xs, w1, w3, w2, tf):
    et, ee, ea = plan
    rows, d = xs.shape
    nf = w1.shape[2] // tf
    fidx = lambda f, a, j: f * a[j] + (nf - 1) * (1 - a[j])
    grid_spec = pltpu.PrefetchScalarGridSpec(
        num_scalar_prefetch=3,
        grid=(et.shape[0], nf),
        in_specs=[pl.BlockSpec((MOE_TE, d), lambda j, f, t, e, a: (t[j], 0)),
                  pl.BlockSpec((None, d, tf), lambda j, f, t, e, a: (e[j], 0, fidx(f, a, j))),
                  pl.BlockSpec((None, d, tf), lambda j, f, t, e, a: (e[j], 0, fidx(f, a, j))),
                  pl.BlockSpec((None, tf, d), lambda j, f, t, e, a: (e[j], fidx(f, a, j), 0))],
        out_specs=pl.BlockSpec((MOE_TE, d), lambda j, f, t, e, a: (t[j], 0)),
        scratch_shapes=[pltpu.VMEM((MOE_TE, d), F32)],
    )
    return pl.pallas_call(
        _expert_kernel,
        grid_spec=grid_spec,
        out_shape=jax.ShapeDtypeStruct((rows, d), BF16),
        compiler_params=_params("arbitrary", "arbitrary"),
        name="moe_experts",
    )(et, ee, ea, xs, w1, w3, w2)


def _combine_kernel(tile_ref, blk_ref, exp_ref, valid_ref, first_ref, last_ref, ys_ref, dest_ref, gates_ref,
                    res_ref, g_ref, b_ref, o_ref, acc_ref):
    w = pl.program_id(0)

    @pl.when(valid_ref[w] == 1)
    def _():
        dest = dest_ref[...]
        lane = lax.broadcasted_iota(jnp.int32, dest.shape, 1)
        mine = lane == exp_ref[w]
        dcol = jnp.sum(jnp.where(mine, dest, 0.0), axis=1, keepdims=True)
        gcol = jnp.sum(jnp.where(mine, gates_ref[...], 0.0), axis=1, keepdims=True)
        col = lax.broadcasted_iota(jnp.int32, (MOE_TB, MOE_TR), 1) + tile_ref[w] * MOE_TR
        onehot = jnp.where(dcol == col.astype(F32), 1.0, 0.0).astype(BF16)
        part = jnp.dot(onehot, ys_ref[...], preferred_element_type=F32) * gcol

        @pl.when(first_ref[w] == 1)
        def _():
            acc_ref[...] = part

        @pl.when(first_ref[w] == 0)
        def _():
            acc_ref[...] += part

        @pl.when(last_ref[w] == 1)
        def _():
            o_ref[...] = _layer_norm(DEEPNORM_ALPHA * res_ref[...] + acc_ref[...], g_ref[...], b_ref[...])


def _combine_ln(plan, ys, dest, gates, res, g, b):
    tile, blk, exp, valid, first, last = plan
    n, d = res.shape
    tok = lambda w, t, b_, e, v, f, l: (b_[w], 0)
    fix = lambda w, t, b_, e, v, f, l: (0, 0)
    grid_spec = pltpu.PrefetchScalarGridSpec(
        num_scalar_prefetch=6,
        grid=(tile.shape[0],),
        in_specs=[pl.BlockSpec((MOE_TR, d), lambda w, t, b_, e, v, f, l: (t[w], 0)),
                  pl.BlockSpec((MOE_TB, LANES), tok), pl.BlockSpec((MOE_TB, LANES), tok), pl.BlockSpec((MOE_TB, d), tok),
                  pl.BlockSpec((1, d), fix), pl.BlockSpec((1, d), fix)],
        out_specs=pl.BlockSpec((MOE_TB, d), tok),
        scratch_shapes=[pltpu.VMEM((MOE_TB, d), F32)],
    )
    return pl.pallas_call(
        _combine_kernel,
        grid_spec=grid_spec,
        out_shape=jax.ShapeDtypeStruct((n, d), F32),
        compiler_params=_params("arbitrary"),
        name="moe_combine_ln",
    )(tile, blk, exp, valid, first, last, ys, dest, gates, res, g, b)


def _moe_ln(x, xb, router_w, w1, w3, w2, g, b):
    n = x.shape[0]
    gates, rank, rank_t, cnt = _router(x, router_w)
    start, expert_tiles, by_tile, by_block = _moe_plan(cnt, n)
    startf = start.astype(F32)
    dest = jnp.where(rank >= 0.0, rank + jnp.pad(startf, (0, LANES - N_EXPERTS))[None, :], -1.0)
    dest_t = jnp.where(rank_t >= 0.0, rank_t + startf[:, None], -1.0)
    xs = _dispatch(by_tile, xb, dest_t)
    ys = _experts(expert_tiles, xs, w1, w3, w2, tf=1792)
    return _combine_ln(by_block, ys, dest, gates, x, g, b)


def _gla_kernel(q_ref, k_ref, v_ref, g_ref, gk_ref, gkw_ref, gkb_ref, nw_ref, sel_ref, o_ref, st_ref, bc_scr,
                qd_scr, kd_scr):
    @pl.when(pl.program_id(1) == 0)
    def _():
        st_ref[...] = jnp.zeros_like(st_ref)

    tt = q_ref.shape[0]
    tri = _tri(CHUNK)
    nw = nw_ref[...]

    pre = _mm(gk_ref[...], gkw_ref[...], exact=True) + gkb_ref[...]
    gk = (jnp.minimum(pre, 0.0) - jnp.log(1.0 + jnp.exp(-jnp.abs(pre)))) * (1.0 / GLA_GATE_NORM)
    bc_all = _mm_01_left(sel_ref[...], gk)
    bc_scr[...] = bc_all
    qd_scr[...] = q_ref[...] * (GLA_DK ** -0.5) * jnp.exp(bc_all)
    kd_scr[...] = k_ref[...] * jnp.exp(-bc_all)

    def chunk(c, carry):
        rows = pl.ds(pl.multiple_of(c * CHUNK, CHUNK), CHUNK)
        bc = bc_scr[rows, :]
        bl = bc[CHUNK - 1:CHUNK, :]
        qd = qd_scr[rows, :]
        kd = kd_scr[rows, :]
        kl = k_ref[rows, :] * jnp.exp(bl - bc)
        dec = jnp.exp(bl)
        heads = range(GLA_HEADS)
        sks = [slice(h * GLA_DK, (h + 1) * GLA_DK) for h in heads]
        svs = [slice(h * GLA_DV, (h + 1) * GLA_DV) for h in heads]
        att = [jnp.where(tri, _mm(qd[:, sk], kd[:, sk], _NT), 0.0) for sk in sks]
        vs = [v_ref[rows, sv] for sv in svs]
        sts = [st_ref[h] for h in heads]
        os_ = [_mm(att[h], vs[h]) + _mm(qd[:, sks[h]], sts[h], _NT) for h in heads]
        for h in heads:
            st_ref[h] = sts[h] * dec[:, sks[h]] + _mm(vs[h], kl[:, sks[h]], _TN)
        for h in heads:
            o = os_[h]
            ms = jnp.mean(o * o, axis=-1, keepdims=True)
            y = o * lax.rsqrt(ms + GLA_NORM_EPS) * nw * _silu(g_ref[rows, svs[h]])
            o_ref[rows, svs[h]] = y.astype(o_ref.dtype)
        return carry

    lax.fori_loop(0, tt // CHUNK, chunk, 0, unroll=2)


def _gla(p, gkw, gkb, nw, tt):
    bsz, t, _ = p.shape

    def col(name):
        off, w = EVEN_COLS[name]
        return pl.BlockSpec((None, tt, w), lambda b, i, j=off // w: (b, i, j))

    fix = lambda b, i: (0, 0)
    return pl.pallas_call(
        _gla_kernel,
        grid=(bsz, t // tt),
        in_specs=[col("q"), col("k"), col("v"), col("g"), col("gk"),
                  pl.BlockSpec(gkw.shape, fix), pl.BlockSpec(gkb.shape, fix), pl.BlockSpec(nw.shape, fix),
                  pl.BlockSpec((tt, tt), fix)],
        out_specs=pl.BlockSpec((None, tt, GLA_VW), lambda b, i: (b, i, 0)),
        out_shape=jax.ShapeDtypeStruct((bsz, t, GLA_VW), BF16),
        scratch_shapes=[pltpu.VMEM((GLA_HEADS, GLA_DV, GLA_DK), F32)] + [pltpu.VMEM((tt, GLA_QK), F32)] * 3,
        compiler_params=_params("parallel", "arbitrary"),
        name="gla",
    )(p, p, p, p, p, gkw, gkb, nw, _chunk_sel(tt, CHUNK))


def _rglru_kernel(x_ref, gate_ref, cw_ref, cb_ref, wa_ref, ba_ref, wx_ref, bx_ref, lam_ref, o_ref,
                  tail_ref, h_ref, a_scr, u_scr):
    @pl.when(pl.program_id(1) == 0)
    def _():
        tail_ref[...] = jnp.zeros_like(tail_ref)
        h_ref[...] = jnp.zeros_like(h_ref)

    tt = x_ref.shape[0]
    x = x_ref[...]
    cw = cw_ref[...]
    xc = x * cw[CONV_K - 1:CONV_K, :] + cb_ref[...]
    for s in range(1, CONV_K):
        xc += _shift_rows(x, tail_ref, s) * cw[CONV_K - 1 - s:CONV_K - s, :]
    tail_ref[...] = x[tt - SUBLANES:, :]

    r = _sigmoid(_mm(xc, wa_ref[...]) + ba_ref[...])
    i = _sigmoid(_mm(xc, wx_ref[...]) + bx_ref[...])
    log_a = -LRU_C * r * _softplus(-lam_ref[...])
    a_scr[...] = jnp.exp(log_a)
    u_scr[...] = jnp.sqrt(1.0 - jnp.exp(2.0 * log_a)) * (i * xc)

    row = lax.broadcasted_iota(jnp.int32, (SUBLANES, x.shape[1]), 0)

    def step(i, h):
        rows = pl.ds(pl.multiple_of(i * SUBLANES, SUBLANES), SUBLANES)
        a = a_scr[rows, :]
        u = u_scr[rows, :]
        for s in (1, 2, 4):
            u = u + a * jnp.where(row >= s, pltpu.roll(u, s, axis=0), 0.0)
            a = a * jnp.where(row >= s, pltpu.roll(a, s, axis=0), 1.0)
        hb = u + a * h
        u_scr[rows, :] = hb
        return hb[SUBLANES - 1:, :]

    h_ref[...] = lax.fori_loop(0, tt // SUBLANES, step, h_ref[...], unroll=4)
    gate = gate_ref[...]
    gelu = 0.5 * gate * (1.0 + jnp.tanh(math.sqrt(2.0 / math.pi) * (gate + 0.044715 * gate * gate * gate)))
    o_ref[...] = (u_scr[...] * gelu).astype(o_ref.dtype)


def _rglru(p, cw, cb, wa, ba, wx, bx, lam, tt):
    bsz, t, _ = p.shape
    w = LRU_WIDTH

    def col(name):
        off, _ = EVEN_COLS[name]
        return pl.BlockSpec((None, tt, w), lambda b, i, j=off // w: (b, i, j))

    fix = lambda b, i: (0, 0)
    vec = pl.BlockSpec((1, w), fix)
    return pl.pallas_call(
        _rglru_kernel,
        grid=(bsz, t // tt),
        in_specs=[col("xb"), col("gate"), pl.BlockSpec((CONV_K, w), fix), vec,
                  pl.BlockSpec((w, w), fix), vec, pl.BlockSpec((w, w), fix), vec, vec],
        out_specs=pl.BlockSpec((None, tt, w), lambda b, i: (b, i, 0)),
        out_shape=jax.ShapeDtypeStruct((bsz, t, w), BF16),
        scratch_shapes=[pltpu.VMEM((SUBLANES, w), F32), pltpu.VMEM((1, w), F32),
                        pltpu.VMEM((tt, w), F32), pltpu.VMEM((tt, w), F32)],
        compiler_params=_params("parallel", "arbitrary"),
        name="rglru",
    )(p, p, cw, cb, wa, ba, wx, bx, lam)


def _ssd_kernel(z_ref, x_ref, bm_ref, cm_ref, dt_ref, cwx_ref, cbx_ref, cwb_ref, cbb_ref, cwc_ref, cbc_ref,
                dtb_ref, alog_ref, dskip_ref, nw_ref, expand_ref, sel_ref, o_ref,
                tx_ref, tb_ref, tc_ref, st_ref, xs_scr, bs_scr, cs_scr, acs_scr, xdt_scr, xdec_scr, ea_scr):
    @pl.when(pl.program_id(1) == 0)
    def _():
        tx_ref[...] = jnp.zeros_like(tx_ref)
        tb_ref[...] = jnp.zeros_like(tb_ref)
        tc_ref[...] = jnp.zeros_like(tc_ref)
        st_ref[...] = jnp.zeros_like(st_ref)

    tt = x_ref.shape[0]

    def conv_silu(src_ref, tail_ref, cw_ref, cb_ref, dst_ref):
        x = src_ref[...]
        cw = cw_ref[...]
        y = x * cw[CONV_K - 1:CONV_K, :] + cb_ref[...]
        for s in range(1, CONV_K):
            y += _shift_rows(x, tail_ref, s) * cw[CONV_K - 1 - s:CONV_K - s, :]
        tail_ref[...] = x[tt - SUBLANES:, :]
        dst_ref[...] = _silu(y)

    conv_silu(x_ref, tx_ref, cwx_ref, cbx_ref, xs_scr)
    conv_silu(bm_ref, tb_ref, cwb_ref, cbb_ref, bs_scr)
    conv_silu(cm_ref, tc_ref, cwc_ref, cbc_ref, cs_scr)

    tri = _tri(CHUNK)
    expand = expand_ref[...]
    dskip = dskip_ref[...]
    nw = nw_ref[...]
    hpg = SSD_HEADS // SSD_GROUPS

    dtc = _softplus(dt_ref[...] + dtb_ref[...])
    da = dtc * -jnp.exp(alog_ref[...])
    sums = _mm_01_left(sel_ref[...], da)
    acs_all = sums[:tt]
    tot = sums[tt:]
    acs_scr[...] = acs_all
    xdt_all = xs_scr[...] * _mm_01(dtc, expand)
    xdt_scr[...] = xdt_all
    xdec_scr[...] = xdt_all * _mm_01(jnp.exp(tot - acs_all), expand)
    ea_scr[...] = _mm_01(jnp.exp(acs_all), expand)

    def chunk(c, carry):
        rows = pl.ds(pl.multiple_of(c * CHUNK, CHUNK), CHUNK)
        acs = acs_scr[rows, :]
        acs_t = acs.T
        ea_x = ea_scr[rows, :]
        x = xs_scr[rows, :]
        xdt = xdt_scr[rows, :]
        xdec = xdec_scr[rows, :]
        cd_x = ea_x[CHUNK - 1:CHUNK, :]
        ys = []
        for g in range(SSD_GROUPS):
            sg = slice(g * SSD_GROUP_WIDTH, (g + 1) * SSD_GROUP_WIDTH)
            ss = slice(g * SSD_STATE, (g + 1) * SSD_STATE)
            bg = bs_scr[rows, ss]
            cg = cs_scr[rows, ss]
            cb = _mm(cg, bg, _NT)
            st = st_ref[g]
            yg = _mm(cg, st) * ea_x[:, sg]
            st_ref[g] = st * cd_x[:, sg] + _mm(bg, xdec[:, sg], _TN)
            yh = []
            for j in range(hpg):
                h = g * hpg + j
                seg = acs[:, h:h + 1] - acs_t[h:h + 1, :]
                lmat = jnp.exp(jnp.where(tri, seg, -jnp.inf))
                sh = slice(h * SSD_HEADDIM, (h + 1) * SSD_HEADDIM)
                yh.append(_mm(cb * lmat, xdt[:, sh]))
            ys.append(yg + jnp.concatenate(yh, axis=1))
        y = jnp.concatenate(ys, axis=1) + x * dskip
        y = y * _silu(z_ref[rows, :])
        outs = []
        for g in range(SSD_GROUPS):
            sg = slice(g * SSD_GROUP_WIDTH, (g + 1) * SSD_GROUP_WIDTH)
            yg = y[:, sg]
            ms = jnp.mean(yg * yg, axis=-1, keepdims=True)
            outs.append(yg * lax.rsqrt(ms + SSD_NORM_EPS))
        o_ref[rows, :] = (jnp.concatenate(outs, axis=1) * nw).astype(o_ref.dtype)
        return carry

    lax.fori_loop(0, tt // CHUNK, chunk, 0, unroll=2)


def _ssd(p, cw, cb, dtb, alog, dskip_x, nw, expand, tt):
    bsz, t, _ = p.shape

    def col(name):
        off, w = ODD_COLS[name]
        return pl.BlockSpec((None, tt, w), lambda b, i, j=off // w: (b, i, j))

    fix = lambda b, i: (0, 0)
    full = lambda a: pl.BlockSpec(a.shape, fix)
    gs = SSD_GROUPS * SSD_STATE
    cwx, cwb, cwc = cw[:, :SSD_INNER], cw[:, SSD_INNER:SSD_INNER + gs], cw[:, SSD_INNER + gs:]
    cbx, cbb, cbc = cb[:, :SSD_INNER], cb[:, SSD_INNER:SSD_INNER + gs], cb[:, SSD_INNER + gs:]
    sel = jnp.concatenate([_chunk_sel(tt, CHUNK), _chunk_sel(tt, CHUNK, "all")], axis=0)
    args = (cwx, cbx, cwb, cbb, cwc, cbc, dtb, alog, dskip_x, nw, expand, sel)
    return pl.pallas_call(
        _ssd_kernel,
        grid=(bsz, t // tt),
        in_specs=[col("z"), col("x"), col("bm"), col("cm"), col("dt")] + [full(a) for a in args],
        out_specs=pl.BlockSpec((None, tt, SSD_INNER), lambda b, i: (b, i, 0)),
        out_shape=jax.ShapeDtypeStruct((bsz, t, SSD_INNER), BF16),
        scratch_shapes=[pltpu.VMEM((SUBLANES, SSD_INNER), F32), pltpu.VMEM((SUBLANES, gs), F32),
                        pltpu.VMEM((SUBLANES, gs), F32),
                        pltpu.VMEM((SSD_GROUPS, SSD_STATE, SSD_GROUP_WIDTH), F32),
                        pltpu.VMEM((tt, SSD_INNER), F32), pltpu.VMEM((tt, gs), F32), pltpu.VMEM((tt, gs), F32),
                        pltpu.VMEM((tt, LANES), F32)] + [pltpu.VMEM((tt, SSD_INNER), F32)] * 3,
        compiler_params=_params("parallel", "arbitrary"),
        name="ssd",
    )(p, p, p, p, p, *args)


def _rwkv_kernel(r_ref, k_ref, v_ref, wa_ref, xg_ref, mur_ref, muk_ref, muv_ref, muwa_ref, mug_ref,
                 w0_ref, w2_ref, a0_ref, a2_ref, g2_ref, kk_ref, ka_ref, rk_ref, lng_ref, lnb_ref, ones_ref, sel_ref,
                 o_ref, tr_ref, tk_ref, tv_ref, twa_ref, tg_ref, st_ref, y_scr,
                 at_scr, rt_scr, bt_scr, kt_scr, bh_scr, kh_scr, v_scr, wc_scr):
    @pl.when(pl.program_id(1) == 0)
    def _():
        for ref in (tr_ref, tk_ref, tv_ref, twa_ref, tg_ref, st_ref):
            ref[...] = jnp.zeros_like(ref)

    tt = r_ref.shape[0]
    hd = RWKV_HEADDIM

    def mix(p_ref, tail_ref, mu_ref):
        p = p_ref[...]
        prev = _shift_rows(p, tail_ref, 1)
        tail_ref[...] = p[tt - SUBLANES:, :]
        return p + (prev - p) * mu_ref[...]

    r = mix(r_ref, tr_ref, mur_ref)
    k = mix(k_ref, tk_ref, muk_ref)
    v = mix(v_ref, tv_ref, muv_ref)
    xwa = mix(wa_ref, twa_ref, muwa_ref)
    xg = mix(xg_ref, tg_ref, mug_ref)

    lane = lax.broadcasted_iota(jnp.int32, xwa.shape, 1)
    lora_in = jnp.where(lane < RWKV_DECAY_LORA, jnp.tanh(xwa), xwa)
    w_log = -_softplus(-(w0_ref[...] + _mm(lora_in, w2_ref[...], exact=True))) - 0.5
    lw = -jnp.exp(w_log)
    a_sig = _sigmoid(a0_ref[...] + _mm(lora_in, a2_ref[...], exact=True))
    gate = _mm(_sigmoid(xg), g2_ref[...], exact=True)
    ones_bd = ones_ref[...]
    kk = k * kk_ref[...]
    kk = kk / jnp.maximum(jnp.sqrt(_mm_01(kk * kk, ones_bd)), 1e-12)
    k = k * (1.0 + (a_sig - 1.0) * ka_ref[...])

    cs = RWKV_CHUNK
    tri_incl = _tri(cs)
    tri_strict = _tri(cs, strict=True)
    tri_col = jnp.concatenate([tri_strict, tri_incl], axis=0)
    eye = (tri_incl & ~tri_strict).astype(F32)

    sums = _mm_01_left(sel_ref[...], lw)
    cum = sums[:tt]
    tot = sums[tt:]
    bvec = kk * a_sig
    e_neg = jnp.exp(-cum)
    e_last = jnp.exp(tot - cum)
    for ref, val in ((at_scr, -kk * jnp.exp(cum - lw)), (rt_scr, r * jnp.exp(cum)), (bt_scr, bvec * e_neg),
                     (kt_scr, k * e_neg), (bh_scr, bvec * e_last), (kh_scr, k * e_last), (v_scr, v),
                     (wc_scr, jnp.exp(tot))):
        ref[...] = val

    def chunk(c, carry):
        rows = pl.ds(pl.multiple_of(c * cs, cs), cs)
        at = at_scr[rows, :]
        rt = rt_scr[rows, :]
        bt = bt_scr[rows, :]
        kt = kt_scr[rows, :]
        bh = bh_scr[rows, :]
        kh = kh_scr[rows, :]
        vc = v_scr[rows, :]
        wc = wc_scr[pl.ds(pl.multiple_of(c * cs, cs), 1), :]
        heads = range(RWKV_HEADS)
        sls = [slice(h * hd, (h + 1) * hd) for h in heads]
        amat = [_mm(jnp.concatenate([at[:, sl], rt[:, sl]], axis=0),
                    jnp.concatenate([bt[:, sl], kt[:, sl]], axis=0), _NT) for sl in sls]
        a_ab = [jnp.where(tri_strict, m[:cs, :cs], 0.0) for m in amat]
        a_rb = [jnp.where(tri_incl, m[cs:, :cs], 0.0) for m in amat]
        a_xk = [jnp.where(tri_col, m[:, cs:], 0.0) for m in amat]
        avrv = [_mm(a_xk[h], vc[:, sls[h]]) for h in heads]
        inv = [eye + m for m in a_ab]
        pw = [_mm(m, m) for m in a_ab]
        for _ in range(int(math.log2(cs)) - 2):
            both = [_mm(jnp.concatenate([inv[h], pw[h]], axis=0), pw[h]) for h in heads]
            inv = [inv[h] + both[h][:cs] for h in heads]
            pw = [m[cs:] for m in both]
        inv = [inv[h] + _mm(inv[h], pw[h]) for h in heads]
        tu = [_mm(inv[h], jnp.concatenate([at[:, sls[h]], avrv[h][:cs]], axis=1)) for h in heads]
        qy = [_mm(a_rb[h], tu[h]) + jnp.concatenate([rt[:, sls[h]], avrv[h][cs:]], axis=1)
              for h in heads]
        zb =[_mm(tu[h], bh[:, sls[h]], _TN) for h in heads]
        vk = [_mm(vc[:, sls[h]], kh[:, sls[h]], _TN) for h in heads]
        for h in heads:
            st = st_ref[h]
            y_scr[rows, sls[h]] = _mm(qy[h][:, :hd], st, _NT) + qy[h][:, hd:]
            st_ref[h] = st * wc[:, sls[h]] + _mm(st, zb[h][:hd]) + zb[h][hd:] + vk[h]
        return carry

    lax.fori_loop(0, tt // cs, chunk, 0, unroll=2)

    y = y_scr[...]
    inv_n = 1.0 / hd
    mu_y = _mm_01(y, ones_bd) * inv_n
    dy = y - mu_y
    var_y = _mm_01(dy * dy, ones_bd) * inv_n
    yn = dy * lax.rsqrt(var_y + RWKV_GN_EPS) * lng_ref[...] + lnb_ref[...]
    bonus = _mm_01(r * k * rk_ref[...], ones_bd) * v
    o_ref[...] = ((yn + bonus) * gate).astype(o_ref.dtype)


def _rwkv(p, mu, w0, w2p, a0, a2p, g2, k_k, k_a, r_k, ln_g, ln_b, ones_bd, tt):
    bsz, t, _ = p.shape
    w = RWKV_WIDTH

    def col(name):
        off, wd = ODD_COLS[name]
        return pl.BlockSpec((None, tt, wd), lambda b, i, j=off // wd: (b, i, j))

    fix = lambda b, i: (0, 0)
    full = lambda a: pl.BlockSpec(a.shape, fix)
    mur, muk, muv = mu[:, :w], mu[:, w:2 * w], mu[:, 2 * w:3 * w]
    muwa, mug = mu[:, 3 * w:3 * w + LANES], mu[:, 3 * w + LANES:]
    sel = jnp.concatenate([_chunk_sel(tt, RWKV_CHUNK), _chunk_sel(tt, RWKV_CHUNK, "all")], axis=0)
    args = (mur, muk, muv, muwa, mug, w0, w2p, a0, a2p, g2, k_k, k_a, r_k, ln_g, ln_b, ones_bd, sel)
    return pl.pallas_call(
        _rwkv_kernel,
        grid=(bsz, t // tt),
        in_specs=[col("r"), col("k"), col("v"), col("wa"), col("xg")] + [full(a) for a in args],
        out_specs=pl.BlockSpec((None, tt, w), lambda b, i: (b, i, 0)),
        out_shape=jax.ShapeDtypeStruct((bsz, t, w), BF16),
        scratch_shapes=[pltpu.VMEM((SUBLANES, w), F32), pltpu.VMEM((SUBLANES, w), F32),
                        pltpu.VMEM((SUBLANES, w), F32), pltpu.VMEM((SUBLANES, LANES), F32),
                        pltpu.VMEM((SUBLANES, LANES), F32),
                        pltpu.VMEM((RWKV_HEADS, RWKV_HEADDIM, RWKV_HEADDIM), F32)]
        + [pltpu.VMEM((tt, w), F32)] * 9,
        compiler_params=_params("parallel", "arbitrary"),
        name="rwkv7",
    )(p, p, p, p, p, *args)


def _reorder_cols(w, pieces, order, width):
    parts, pos = [], 0
    for name, (off, slot) in order.items():
        assert off == pos, "slots must be listed in order and contiguous"
        start, size = pieces[name]
        parts.append(w[:, start:start + size])
        if size < slot:
            parts.append(jnp.zeros((w.shape[0], slot - size), w.dtype))
        pos += slot
    assert pos == width
    return jnp.concatenate(parts, axis=1)


def _block_diag(w):
    n, i, j = w.shape
    eye = jnp.eye(n, dtype=w.dtype)
    return (eye[:, None, :, None] * w[:, :, None, :]).reshape(n * i, n * j)


def _row(v, width=None):
    v = v.reshape(1, -1).astype(F32)
    if width is not None and v.shape[1] < width:
        v = jnp.pad(v, ((0, 0), (0, width - v.shape[1])))
    return v


def _even_layer(x, x_in, w_in, gk_w2, gk_b, gla_norm, conv_w, conv_b, wa, ba, wx, bx, lam,
                w_out, ln1_g, ln1_b, f_w1, f_w3, f_w2, ln2_g, ln2_b, bsz, t):
    qk, vw, lw = GLA_QK, GLA_VW, LRU_WIDTH
    pieces = dict(q=(0, qk), k=(qk, qk), v=(2 * qk, vw), g=(2 * qk + vw, vw), gk=(2 * qk + 2 * vw, GLA_GK_RANK),
                  xb=(2 * qk + 2 * vw + GLA_GK_RANK, lw), gate=(2 * qk + 2 * vw + GLA_GK_RANK + lw, lw))
    w_in_r = _reorder_cols(w_in, pieces, EVEN_COLS, EVEN_WIDTH).astype(BF16)
    p = _proj(x_in, w_in_r, tm=min(512, x.shape[0])).reshape(bsz, t, EVEN_WIDTH)
    tt = min(512, t)
    gkw = jnp.pad(gk_w2, ((0, LANES - GLA_GK_RANK), (0, 0)))
    y_a = _gla(p, gkw, _row(gk_b), _row(gla_norm), tt)
    y_b = _rglru(p, conv_w, _row(conv_b), _block_diag(wa).astype(BF16), _row(ba), _block_diag(wx).astype(BF16),
                 _row(bx), _row(lam), tt)
    n = bsz * t
    tm = min(512, n)
    x1, x1b = _outproj_ln(y_a.reshape(n, vw), y_b.reshape(n, lw), w_out[:vw].astype(BF16), w_out[vw:].astype(BF16),
                          x, _row(ln1_g), _row(ln1_b), tm)
    return _ffn_ln(x1b, f_w1.astype(BF16), f_w3.astype(BF16), f_w2.astype(BF16), x1,
                   _row(ln2_g), _row(ln2_b), tm=min(512, n))


def _odd_layer(x, xb, w_in, conv_w, conv_b, dt_bias, a_log, d_skip, ssd_norm, mu, w0, w2, a0, a2, g2, k_k, k_a,
               r_k, rln_g, rln_b, w_out, ln1_g, ln1_b, router, ew1, ew3, ew2, ln2_g, ln2_b, bsz, t):
    si, gs, rw = SSD_INNER, SSD_GROUPS * SSD_STATE, RWKV_WIDTH
    o = 2 * si + 2 * gs + SSD_HEADS
    pieces = dict(z=(0, si), x=(si, si), bm=(2 * si, gs), cm=(2 * si + gs, gs), dt=(2 * si + 2 * gs, SSD_HEADS),
                  r=(o, rw), k=(o + rw, rw), v=(o + 2 * rw, rw),
                  wa=(o + 3 * rw, RWKV_DECAY_LORA + RWKV_AAA_LORA), xg=(o + 3 * rw + LANES, RWKV_GATE_LORA))
    w_in_r = _reorder_cols(w_in, pieces, ODD_COLS, ODD_WIDTH).astype(BF16)
    n = bsz * t
    p = _proj(xb, w_in_r, tm=min(512, n)).reshape(bsz, t, ODD_WIDTH)
    expand = jnp.pad(jnp.repeat(jnp.eye(SSD_HEADS, dtype=BF16), SSD_HEADDIM, axis=1), ((0, LANES - SSD_HEADS), (0, 0)))
    y_c = _ssd(p, conv_w, _row(conv_b), _row(dt_bias, LANES), _row(a_log, LANES),
               _row(jnp.repeat(d_skip, SSD_HEADDIM)), _row(ssd_norm), expand, tt=min(256, t))
    zeros = jnp.zeros((RWKV_DECAY_LORA, rw), F32)
    ones_bd = _block_diag(jnp.ones((RWKV_HEADS, RWKV_HEADDIM, RWKV_HEADDIM), BF16))
    y_d = _rwkv(p, _row(mu), _row(w0), jnp.concatenate([w2, zeros]), _row(a0), jnp.concatenate([zeros, a2]), g2,
                _row(k_k), _row(k_a), _row(r_k), _row(rln_g), _row(rln_b), ones_bd, tt=min(256, t))
    tm = min(512, n)
    x1, x1b = _outproj_ln(y_c.reshape(n, si), y_d.reshape(n, rw), w_out[:si].astype(BF16), w_out[si:].astype(BF16),
                          x, _row(ln1_g), _row(ln1_b), tm)
    y = _moe_ln(x1, x1b, jnp.pad(router, ((0, 0), (0, LANES - N_EXPERTS))), ew1.astype(BF16), ew3.astype(BF16),
                ew2.astype(BF16), _row(ln2_g), _row(ln2_b))
    return y, None


def kernel(x, e_w_in, e_gk_w2, e_gk_b, e_gla_norm, e_conv_w, e_conv_b, e_lru_wa, e_lru_ba, e_lru_wx, e_lru_bx, e_lru_lambda, e_w_out, e_ln1_g, e_ln1_b, e_ffn_w1, e_ffn_w3, e_ffn_w2, e_ln2_g, e_ln2_b, o_w_in, o_conv_w, o_conv_b, o_dt_bias, o_a_log, o_d_skip, o_ssd_norm, o_rwkv_mu, o_rwkv_w0, o_rwkv_w2, o_rwkv_a0, o_rwkv_a2, o_rwkv_g2, o_rwkv_k_k, o_rwkv_k_a, o_rwkv_r_k, o_rwkv_ln_g, o_rwkv_ln_b, o_w_out, o_ln1_g, o_ln1_b, o_router, o_exp_w1, o_exp_w3, o_exp_w2, o_ln2_g, o_ln2_b):
    bsz, t, d = x.shape
    h = x.reshape(bsz * t, d)
    hb = h
    for i in range(DEPTH):
        j = i // 2
        if i % 2 == 0:
            h, hb = _even_layer(h, hb, e_w_in[j], e_gk_w2[j], e_gk_b[j], e_gla_norm[j], e_conv_w[j], e_conv_b[j],
                                e_lru_wa[j], e_lru_ba[j], e_lru_wx[j], e_lru_bx[j], e_lru_lambda[j], e_w_out[j],
                                e_ln1_g[j], e_ln1_b[j], e_ffn_w1[j], e_ffn_w3[j], e_ffn_w2[j], e_ln2_g[j],
                                e_ln2_b[j], bsz, t)
        else:
            h, hb = _odd_layer(h, hb, o_w_in[j], o_conv_w[j], o_conv_b[j], o_dt_bias[j], o_a_log[j], o_d_skip[j],
                               o_ssd_norm[j], o_rwkv_mu[j], o_rwkv_w0[j], o_rwkv_w2[j], o_rwkv_a0[j],
                               o_rwkv_a2[j], o_rwkv_g2[j], o_rwkv_k_k[j], o_rwkv_k_a[j], o_rwkv_r_k[j],
                               o_rwkv_ln_g[j], o_rwkv_ln_b[j], o_w_out[j], o_ln1_g[j], o_ln1_b[j], o_router[j],
                               o_exp_w1[j], o_exp_w3[j], o_exp_w2[j], o_ln2_g[j], o_ln2_b[j], bsz, t)
    return h.reshape(bsz, t, d)
```

```python
import functools
import math

import jax
import jax.numpy as jnp
from jax import lax
from jax.experimental import pallas as pl
from jax.experimental.pallas import tpu as pltpu

F32 = jnp.float32
BF16 = jnp.bfloat16
HIGHEST = lax.Precision.HIGHEST

D_MODEL = 1024
DEPTH = 2
DEEPNORM_ALPHA = (2 * DEPTH) ** 0.25
LN_EPS = 1e-5
CONV_K = 4
CHUNK = 64

GLA_HEADS = 4
GLA_DK = 64
GLA_DV = 128
GLA_QK = GLA_HEADS * GLA_DK
GLA_VW = GLA_HEADS * GLA_DV
GLA_GK_RANK = 16
GLA_GATE_NORM = 16.0
GLA_NORM_EPS = 1e-5

LRU_WIDTH = 512
LRU_BLOCKS = 8
LRU_C = 8.0

SSD_HEADS = 16
SSD_HEADDIM = 64
SSD_INNER = SSD_HEADS * SSD_HEADDIM
SSD_GROUPS = 2
SSD_STATE = 128
SSD_GROUP_WIDTH = SSD_INNER // SSD_GROUPS
SSD_NORM_EPS = 1e-5

RWKV_HEADS = 8
RWKV_HEADDIM = 64
RWKV_WIDTH = RWKV_HEADS * RWKV_HEADDIM
RWKV_DECAY_LORA = 64
RWKV_AAA_LORA = 64
RWKV_GATE_LORA = 128
RWKV_GN_EPS = 64e-5
RWKV_CHUNK = 128

N_EXPERTS = 8
LANES = 128
SUBLANES = 8
assert N_EXPERTS == SUBLANES
MOE_TB = 512
MOE_TR = 256
MOE_TE = 512
assert MOE_TE % MOE_TR == 0
VMEM_LIMIT = 56 * 1024 * 1024

EVEN_COLS = dict(v=(0, 512), g=(512, 512), xb=(1024, 512), gate=(1536, 512), q=(2048, 256), k=(2304, 256),
                 gk=(2560, 128))
EVEN_WIDTH = 2688
ODD_COLS = dict(z=(0, 1024), x=(1024, 1024), r=(2048, 512), k=(2560, 512), v=(3072, 512), bm=(3584, 256),
                cm=(3840, 256), wa=(4096, 128), xg=(4224, 128), dt=(4352, 128))
ODD_WIDTH = 4480


def _mm(a, b, dims=((1,), (0,)), exact=False):
    dn = (dims, ((), ()))
    if exact:
        return lax.dot_general(a.astype(F32), b.astype(F32), dn, precision=HIGHEST, preferred_element_type=F32)
    return lax.dot_general(a.astype(BF16), b.astype(BF16), dn, preferred_element_type=F32)


_NT = ((1,), (1,))
_TN = ((0,), (0,))


def _mm_01(x, sel):
    hi = x.astype(BF16)
    lo = (x - hi.astype(F32)).astype(BF16)
    return (jnp.dot(hi, sel, preferred_element_type=F32) + jnp.dot(lo, sel, preferred_element_type=F32))


def _sigmoid(x):
    return 1.0 / (1.0 + jnp.exp(-x))


def _softplus(x):
    return jnp.maximum(x, 0.0) + jnp.log(1.0 + jnp.exp(-jnp.abs(x)))


def _silu(x):
    return x * _sigmoid(x)


def _layer_norm(h, g, b):
    mu = jnp.mean(h, axis=-1, keepdims=True)
    d = h - mu
    var = jnp.mean(d * d, axis=-1, keepdims=True)
    return d * lax.rsqrt(var + LN_EPS) * g + b


def _params(*sem):
    return pltpu.CompilerParams(dimension_semantics=sem, vmem_limit_bytes=VMEM_LIMIT)


def _tri(n, strict=False):
    r = lax.broadcasted_iota(jnp.int32, (n, n), 0)
    c = lax.broadcasted_iota(jnp.int32, (n, n), 1)
    return (r > c) if strict else (r >= c)


def _chunk_sel(n, chunk, kind="incl"):
    r = jnp.arange(n)[:, None]
    c = jnp.arange(n)[None, :]
    first = r - r % chunk
    upper = r if kind == "incl" else first + (chunk - 1)
    return ((c >= first) & (c <= upper)).astype(BF16)


def _mm_01_left(sel, x):
    hi = x.astype(BF16)
    lo = (x - hi.astype(F32)).astype(BF16)
    return jnp.dot(sel, hi, preferred_element_type=F32) + jnp.dot(sel, lo, preferred_element_type=F32)


def _shift_rows(p, tail_ref, s):
    n = p.shape[0]
    rolled = pltpu.roll(p, s, axis=0)
    head = pltpu.roll(tail_ref[...], s, axis=0)
    row = lax.broadcasted_iota(jnp.int32, (SUBLANES, p.shape[1]), 0)
    fixed = jnp.where(row < s, head, rolled[:SUBLANES])
    return jnp.concatenate([fixed, rolled[SUBLANES:]], axis=0) if n > SUBLANES else fixed


def _proj_kernel(x_ref, w_ref, o_ref):
    o_ref[...] = jnp.dot(x_ref[...].astype(BF16), w_ref[...], preferred_element_type=F32)


def _resident(shape):
    return pl.BlockSpec(shape, lambda *_: (0,) * len(shape), pipeline_mode=pl.Buffered(1))


def _proj(x, w, tm):
    m, k = x.shape
    n = w.shape[1]
    return pl.pallas_call(
        _proj_kernel,
        grid=(m // tm,),
        in_specs=[pl.BlockSpec((tm, k), lambda i: (i, 0)), _resident((k, n))],
        out_specs=pl.BlockSpec((tm, n), lambda i: (i, 0)),
        out_shape=jax.ShapeDtypeStruct((m, n), F32),
        compiler_params=_params("parallel"),
        name="in_proj",
    )(x, w)


def _outproj_ln_kernel(ya_ref, yb_ref, wa_ref, wb_ref, res_ref, g_ref, b_ref, o_ref, ob_ref):
    acc = jnp.dot(ya_ref[...].astype(BF16), wa_ref[...], preferred_element_type=F32)
    acc += jnp.dot(yb_ref[...].astype(BF16), wb_ref[...], preferred_element_type=F32)
    y = _layer_norm(DEEPNORM_ALPHA * res_ref[...] + acc, g_ref[...], b_ref[...])
    o_ref[...] = y
    ob_ref[...] = y.astype(BF16)


def _outproj_ln(ya, yb, wa, wb, res, g, b, tm):
    m = ya.shape[0]
    d = res.shape[1]
    row = lambda i: (i, 0)
    fix = lambda i: (0, 0)
    return pl.pallas_call(
        _outproj_ln_kernel,
        grid=(m // tm,),
        in_specs=[pl.BlockSpec((tm, ya.shape[1]), row), pl.BlockSpec((tm, yb.shape[1]), row),
                  pl.BlockSpec(wa.shape, fix), pl.BlockSpec(wb.shape, fix), pl.BlockSpec((tm, d), row),
                  pl.BlockSpec((1, d), fix), pl.BlockSpec((1, d), fix)],
        out_specs=[pl.BlockSpec((tm, d), row), pl.BlockSpec((tm, d), row)],
        out_shape=[jax.ShapeDtypeStruct((m, d), F32), jax.ShapeDtypeStruct((m, d), BF16)],
        compiler_params=_params("parallel"),
        name="out_proj_ln",
    )(ya, yb, wa, wb, res, g, b)


def _ffn_kernel(xb_ref, w1_ref, w3_ref, w2_ref, res_ref, g_ref, b_ref, o_ref, ob_ref):
    x = xb_ref[...]
    h1 = jnp.dot(x, w1_ref[...], preferred_element_type=F32)
    h3 = jnp.dot(x, w3_ref[...], preferred_element_type=F32)
    ff = jnp.dot((_silu(h1) * h3).astype(BF16), w2_ref[...], preferred_element_type=F32)
    y = _layer_norm(DEEPNORM_ALPHA * res_ref[...] + ff, g_ref[...], b_ref[...])
    o_ref[...] = y
    ob_ref[...] = y.astype(BF16)


def _ffn_ln(xb, w1, w3, w2, res, g, b, tm):
    m, d = xb.shape
    row = lambda i: (i, 0)
    return pl.pallas_call(
        _ffn_kernel,
        grid=(m // tm,),
        in_specs=[pl.BlockSpec((tm, d), row), _resident(w1.shape), _resident(w3.shape), _resident(w2.shape),
                  pl.BlockSpec((tm, d), row), _resident((1, d)), _resident((1, d))],
        out_specs=[pl.BlockSpec((tm, d), row), pl.BlockSpec((tm, d), row)],
        out_shape=[jax.ShapeDtypeStruct((m, d), F32), jax.ShapeDtypeStruct((m, d), BF16)],
        compiler_params=_params("parallel"),
        name="ffn_ln",
    )(xb, w1, w3, w2, res, g, b)


def _router_kernel(x_ref, w_ref, gates_ref, rank_ref, rank_t_ref, cnt_ref, carry_ref):
    @pl.when(pl.program_id(0) == 0)
    def _():
        carry_ref[...] = jnp.zeros_like(carry_ref)

    x = x_ref[...]
    w = w_ref[...]
    x_hi = x.astype(BF16)
    w_hi = w.astype(BF16)
    x_lo = (x - x_hi.astype(F32)).astype(BF16)
    w_lo = (w - w_hi.astype(F32)).astype(BF16)
    logits = (jnp.dot(x_hi, w_hi, preferred_element_type=F32) + jnp.dot(x_lo, w_hi, preferred_element_type=F32)
              + jnp.dot(x_hi, w_lo, preferred_element_type=F32))
    lane = lax.broadcasted_iota(jnp.int32, logits.shape, 1)
    neg = jnp.float32(-jnp.inf)
    l1 = jnp.where(lane < N_EXPERTS, logits, neg)
    m1 = jnp.max(l1, axis=1, keepdims=True)
    i1 = jnp.min(jnp.where(l1 == m1, lane, LANES), axis=1, keepdims=True)
    l2 = jnp.where(lane == i1, neg, l1)
    m2 = jnp.max(l2, axis=1, keepdims=True)
    i2 = jnp.min(jnp.where(l2 == m2, lane, LANES), axis=1, keepdims=True)
    ex = jnp.exp(m2 - m1)
    w_top = 1.0 / (1.0 + ex)
    gates_ref[...] = jnp.where(lane == i1, w_top, 0.0) + jnp.where(lane == i2, ex * w_top, 0.0)

    sel = jnp.where(lane == i1, 1.0, 0.0) + jnp.where(lane == i2, 1.0, 0.0)
    tb = sel.shape[0]
    before = _mm(_tri(tb, strict=True).astype(F32), sel)
    carry = carry_ref[...]
    rank = jnp.where(sel > 0.0, carry + before, -1.0)
    rank_ref[...] = rank
    rank_t_ref[...] = rank.T[:SUBLANES, :]
    carry = carry + jnp.sum(sel, axis=0, keepdims=True)
    carry_ref[...] = carry
    cnt_ref[...] = jnp.broadcast_to(carry, cnt_ref.shape)


def _router(x, w):
    m, d = x.shape
    nblk = m // MOE_TB
    return pl.pallas_call(
        _router_kernel,
        grid=(nblk,),
        in_specs=[pl.BlockSpec((MOE_TB, d), lambda i: (i, 0)), pl.BlockSpec((d, LANES), lambda i: (0, 0))],
        out_specs=[pl.BlockSpec((MOE_TB, LANES), lambda i: (i, 0)), pl.BlockSpec((MOE_TB, LANES), lambda i: (i, 0)),
                   pl.BlockSpec((SUBLANES, MOE_TB), lambda i: (0, i)), pl.BlockSpec((SUBLANES, LANES), lambda i: (i, 0))],
        out_shape=[jax.ShapeDtypeStruct((m, LANES), F32), jax.ShapeDtypeStruct((m, LANES), F32),
                   jax.ShapeDtypeStruct((SUBLANES, m), F32), jax.ShapeDtypeStruct((nblk * SUBLANES, LANES), F32)],
        scratch_shapes=[pltpu.VMEM((1, LANES), F32)],
        compiler_params=_params("arbitrary"),
        name="router",
    )(x, w)


def _moe_plan(cnt, n_tok):
    i32 = jnp.int32
    nblk = n_tok // MOE_TB
    c_inc = cnt.reshape(nblk, SUBLANES, LANES)[:, 0, :N_EXPERTS].astype(i32)
    c_exc = jnp.concatenate([jnp.zeros((1, N_EXPERTS), i32), c_inc[:-1]], axis=0)
    gsz = (c_inc[-1] + MOE_TE - 1) // MOE_TE * MOE_TE
    gend = jnp.cumsum(gsz)
    start = gend - gsz
    n_et = _moe_rows(n_tok) // MOE_TE
    n_act = gend[-1] // MOE_TE
    et = jnp.minimum(jnp.arange(n_et, dtype=i32), jnp.maximum(n_act - 1, 0))
    et_expert = jnp.sum((gend[None, :] <= (et * MOE_TE)[:, None]).astype(i32), axis=1)
    et_active = (jnp.arange(n_et, dtype=i32) < n_act).astype(i32)

    lo = start[None, :] + c_exc
    hi = start[None, :] + c_inc
    first_tile = lo // MOE_TR
    n_items = jnp.where(hi > lo, (hi - 1) // MOE_TR - first_tile + 1, 0)
    wmax = _moe_rows(n_tok) // MOE_TR + N_EXPERTS * nblk
    blk_id = jnp.broadcast_to(jnp.arange(nblk, dtype=i32)[:, None], (nblk, N_EXPERTS))
    exp_id = jnp.broadcast_to(jnp.arange(N_EXPERTS, dtype=i32)[None, :], (nblk, N_EXPERTS))

    def worklist(expert_major):
        flat = (lambda a: a.T.reshape(-1)) if expert_major else (lambda a: a.reshape(-1))
        n, ft, blk, exp = flat(n_items), flat(first_tile), flat(blk_id), flat(exp_id)
        inc = jnp.cumsum(n)
        total = inc[-1]
        w = jnp.arange(wmax, dtype=i32)
        wc = jnp.minimum(w, jnp.maximum(total - 1, 0))
        idx = jnp.sum((inc[None, :] <= wc[:, None]).astype(i32), axis=1)
        tile = ft[idx] + wc - (inc - n)[idx]
        valid = w < total
        key = tile if expert_major else blk[idx]
        first = valid & ((w == 0) | (key != jnp.roll(key, 1)))
        last = valid & ((w == total - 1) | (key != jnp.roll(key, -1)))
        kind = valid.astype(i32)
        if expert_major:
            tiles = jnp.arange(_moe_rows(n_tok) // MOE_TR, dtype=i32)
            used_end = start + c_inc[-1]
            group = jnp.minimum(jnp.sum((gend[None, :] <= (tiles * MOE_TR)[:, None]).astype(i32), axis=1),
                                N_EXPERTS - 1)
            empty = (tiles * MOE_TR >= used_end[group]) | (tiles * MOE_TR >= gend[-1])
            order = jnp.cumsum(empty.astype(i32)) - 1
            n_empty = order[-1] + 1
            k = jnp.clip(w - total, 0, jnp.maximum(n_empty - 1, 0))
            spare = jnp.sum(jnp.where(empty[None, :] & (order[None, :] == k[:, None]), tiles[None, :], 0), axis=1)
            fill = (~valid) & (w - total < n_empty)
            tile = jnp.where(valid | (n_empty == 0), tile, spare)
            kind = jnp.where(valid, 1, jnp.where(fill, 2, 0)).astype(i32)
        return tile, blk[idx], exp[idx], kind, first.astype(i32), last.astype(i32)

    return start, (et, et_expert, et_active), worklist(True), worklist(False)


def _moe_rows(n_tok):
    return 2 * n_tok + N_EXPERTS * MOE_TE


def _one_hot_rows(dest, tile):
    row = lax.broadcasted_iota(jnp.int32, (MOE_TR, MOE_TB), 0) + tile * MOE_TR
    return jnp.where(dest == row.astype(F32), 1.0, 0.0).astype(BF16)


def _dispatch_kernel(tile_ref, blk_ref, exp_ref, kind_ref, first_ref, x_ref, dest_ref, xs_ref):
    w = pl.program_id(0)

    @pl.when(kind_ref[w] == 2)
    def _():
        xs_ref[...] = jnp.zeros_like(xs_ref)

    @pl.when(kind_ref[w] == 1)
    def _():
        onehot = _one_hot_rows(dest_ref[pl.ds(exp_ref[w], 1), :], tile_ref[w])
        xg = jnp.dot(onehot, x_ref[...], preferred_element_type=F32).astype(BF16)

        @pl.when(first_ref[w] == 1)
        def _():
            xs_ref[...] = xg

        @pl.when(first_ref[w] == 0)
        def _():
            xs_ref[...] += xg


def _dispatch(plan, xb, dest_t):
    tile, blk, exp, valid, first, _ = plan
    n, d = xb.shape
    rows = _moe_rows(n)
    grid_spec = pltpu.PrefetchScalarGridSpec(
        num_scalar_prefetch=5,
        grid=(tile.shape[0],),
        in_specs=[pl.BlockSpec((MOE_TB, d), lambda w, t, b, e, v, f: (b[w], 0)),
                  pl.BlockSpec((SUBLANES, MOE_TB), lambda w, t, b, e, v, f: (0, b[w]))],
        out_specs=pl.BlockSpec((MOE_TR, d), lambda w, t, b, e, v, f: (t[w], 0)),
    )
    return pl.pallas_call(
        _dispatch_kernel,
        grid_spec=grid_spec,
        out_shape=jax.ShapeDtypeStruct((rows, d), BF16),
        compiler_params=_params("arbitrary"),
        name="moe_dispatch",
    )(tile, blk, exp, valid, first, xb, dest_t)


def _expert_kernel(et_ref, ee_ref, ea_ref, xs_ref, w1_ref, w3_ref, w2_ref, ys_ref, acc_ref):
    j = pl.program_id(0)
    f = pl.program_id(1)

    @pl.when((ea_ref[j] == 0) & (f == 0))
    def _():
        ys_ref[...] = jnp.zeros_like(ys_ref)

    @pl.when(ea_ref[j] == 1)
    def _():
        @pl.when(f == 0)
        def _():
            acc_ref[...] = jnp.zeros_like(acc_ref)

        x = xs_ref[...]
        h1 = jnp.dot(x, w1_ref[...], preferred_element_type=F32)
        h3 = jnp.dot(x, w3_ref[...], preferred_element_type=F32)
        acc_ref[...] += jnp.dot((_silu(h1) * h3).astype(BF16), w2_ref[...], preferred_element_type=F32)

        @pl.when(f == pl.num_programs(1) - 1)
        def _():
            ys_ref[...] = acc_ref[...].astype(BF16)


def _experts(plan, xs, w1, w3, w2, tf):
    et, ee, ea = plan
    rows, d = xs.shape
    nf = w1.shape[2] // tf
    fidx = lambda f, a, j: f * a[j] + (nf - 1) * (1 - a[j])
    grid_spec = pltpu.PrefetchScalarGridSpec(
        num_scalar_prefetch=3,
        grid=(et.shape[0], nf),
        in_specs=[pl.BlockSpec((MOE_TE, d), lambda j, f, t, e, a: (t[j], 0)),
                  pl.BlockSpec((None, d, tf), lambda j, f, t, e, a: (e[j], 0, fidx(f, a, j))),
                  pl.BlockSpec((None, d, tf), lambda j, f, t, e, a: (e[j], 0, fidx(f, a, j))),
                  pl.BlockSpec((None, tf, d), lambda j, f, t, e, a: (e[j], fidx(f, a, j), 0))],
        out_specs=pl.BlockSpec((MOE_TE, d), lambda j, f, t, e, a: (j, 0)),
        scratch_shapes=[pltpu.VMEM((MOE_TE, d), F32)],
    )
    return pl.pallas_call(
        _expert_kernel,
        grid_spec=grid_spec,
        out_shape=jax.ShapeDtypeStruct((rows, d), BF16),
        compiler_params=_params("arbitrary", "arbitrary"),
        name="moe_experts",
    )(et, ee, ea, xs, w1, w3, w2)


def _combine_kernel(tile_ref, blk_ref, exp_ref, valid_ref, first_ref, last_ref, ys_ref, dest_ref, gates_ref,
                    res_ref, g_ref, b_ref, o_ref, acc_ref):
    w = pl.program_id(0)

    @pl.when(valid_ref[w] == 1)
    def _():
        dest = dest_ref[...]
        lane = lax.broadcasted_iota(jnp.int32, dest.shape, 1)
        mine = lane == exp_ref[w]
        dcol = jnp.sum(jnp.where(mine, dest, 0.0), axis=1, keepdims=True)
        gcol = jnp.sum(jnp.where(mine, gates_ref[...], 0.0), axis=1, keepdims=True)
        col = lax.broadcasted_iota(jnp.int32, (MOE_TB, MOE_TR), 1) + tile_ref[w] * MOE_TR
        onehot = jnp.where(dcol == col.astype(F32), 1.0, 0.0).astype(BF16)
        part = jnp.dot(onehot, ys_ref[...], preferred_element_type=F32) * gcol

        @pl.when(first_ref[w] == 1)
        def _():
            acc_ref[...] = part

        @pl.when(first_ref[w] == 0)
        def _():
            acc_ref[...] += part

        @pl.when(last_ref[w] == 1)
        def _():
            o_ref[...] = _layer_norm(DEEPNORM_ALPHA * res_ref[...] + acc_ref[...], g_ref[...], b_ref[...])


def _combine_ln(plan, ys, dest, gates, res, g, b):
    tile, blk, exp, valid, first, last = plan
    n, d = res.shape
    tok = lambda w, t, b_, e, v, f, l: (b_[w], 0)
    fix = lambda w, t, b_, e, v, f, l: (0, 0)
    grid_spec = pltpu.PrefetchScalarGridSpec(
        num_scalar_prefetch=6,
        grid=(tile.shape[0],),
        in_specs=[pl.BlockSpec((MOE_TR, d), lambda w, t, b_, e, v, f, l: (t[w], 0)),
                  pl.BlockSpec((MOE_TB, LANES), tok), pl.BlockSpec((MOE_TB, LANES), tok), pl.BlockSpec((MOE_TB, d), tok),
                  pl.BlockSpec((1, d), fix), pl.BlockSpec((1, d), fix)],
        out_specs=pl.BlockSpec((MOE_TB, d), tok),
        scratch_shapes=[pltpu.VMEM((MOE_TB, d), F32)],
    )
    return pl.pallas_call(
        _combine_kernel,
        grid_spec=grid_spec,
        out_shape=jax.ShapeDtypeStruct((n, d), F32),
        compiler_params=_params("arbitrary"),
        name="moe_combine_ln",
    )(tile, blk, exp, valid, first, last, ys, dest, gates, res, g, b)


def _moe_ln(x, xb, router_w, w1, w3, w2, g, b):
    n = x.shape[0]
    gates, rank, rank_t, cnt = _router(x, router_w)
    start, expert_tiles, by_tile, by_block = _moe_plan(cnt, n)
    startf = start.astype(F32)
    dest = jnp.where(rank >= 0.0, rank + jnp.pad(startf, (0, LANES - N_EXPERTS))[None, :], -1.0)
    dest_t = jnp.where(rank_t >= 0.0, rank_t + startf[:, None], -1.0)
    xs = _dispatch(by_tile, xb, dest_t)
    ys = _experts(expert_tiles, xs, w1, w3, w2, tf=1792)
    return _combine_ln(by_block, ys, dest, gates, x, g, b)


def _gla_kernel(q_ref, k_ref, v_ref, g_ref, gk_ref, gkw_ref, gkb_ref, nw_ref, sel_ref, o_ref, st_ref, bc_scr,
                qd_scr, kd_scr):
    @pl.when(pl.program_id(1) == 0)
    def _():
        st_ref[...] = jnp.zeros_like(st_ref)

    tt = q_ref.shape[0]
    tri = _tri(CHUNK)
    nw = nw_ref[...]

    pre = _mm(gk_ref[...], gkw_ref[...], exact=True) + gkb_ref[...]
    gk = (jnp.minimum(pre, 0.0) - jnp.log(1.0 + jnp.exp(-jnp.abs(pre)))) * (1.0 / GLA_GATE_NORM)
    bc_all = _mm_01_left(sel_ref[...], gk)
    bc_scr[...] = bc_all
    qd_scr[...] = q_ref[...] * (GLA_DK ** -0.5) * jnp.exp(bc_all)
    kd_scr[...] = k_ref[...] * jnp.exp(-bc_all)

    def chunk(c, carry):
        rows = pl.ds(pl.multiple_of(c * CHUNK, CHUNK), CHUNK)
        bc = bc_scr[rows, :]
        bl = bc[CHUNK - 1:CHUNK, :]
        qd = qd_scr[rows, :]
        kd = kd_scr[rows, :]
        kl = k_ref[rows, :] * jnp.exp(bl - bc)
        dec = jnp.exp(bl)
        heads = range(GLA_HEADS)
        sks = [slice(h * GLA_DK, (h + 1) * GLA_DK) for h in heads]
        svs = [slice(h * GLA_DV, (h + 1) * GLA_DV) for h in heads]
        att = [jnp.where(tri, _mm(qd[:, sk], kd[:, sk], _NT), 0.0) for sk in sks]
        vs = [v_ref[rows, sv] for sv in svs]
        sts = [st_ref[h] for h in heads]
        os_ = [_mm(att[h], vs[h]) + _mm(qd[:, sks[h]], sts[h], _NT) for h in heads]
        for h in heads:
            st_ref[h] = sts[h] * dec[:, sks[h]] + _mm(vs[h], kl[:, sks[h]], _TN)
        for h in heads:
            o = os_[h]
            ms = jnp.mean(o * o, axis=-1, keepdims=True)
            y = o * lax.rsqrt(ms + GLA_NORM_EPS) * nw * _silu(g_ref[rows, svs[h]])
            o_ref[rows, svs[h]] = y.astype(o_ref.dtype)
        return carry

    lax.fori_loop(0, tt // CHUNK, chunk, 0, unroll=2)


def _gla(p, gkw, gkb, nw, tt):
    bsz, t, _ = p.shape

    def col(name):
        off, w = EVEN_COLS[name]
        return pl.BlockSpec((None, tt, w), lambda b, i, j=off // w: (b, i, j))

    fix = lambda b, i: (0, 0)
    return pl.pallas_call(
        _gla_kernel,
        grid=(bsz, t // tt),
        in_specs=[col("q"), col("k"), col("v"), col("g"), col("gk"),
                  pl.BlockSpec(gkw.shape, fix), pl.BlockSpec(gkb.shape, fix), pl.BlockSpec(nw.shape, fix),
                  pl.BlockSpec((tt, tt), fix)],
        out_specs=pl.BlockSpec((None, tt, GLA_VW), lambda b, i: (b, i, 0)),
        out_shape=jax.ShapeDtypeStruct((bsz, t, GLA_VW), BF16),
        scratch_shapes=[pltpu.VMEM((GLA_HEADS, GLA_DV, GLA_DK), F32)] + [pltpu.VMEM((tt, GLA_QK), F32)] * 3,
        compiler_params=_params("parallel", "arbitrary"),
        name="gla",
    )(p, p, p, p, p, gkw, gkb, nw, _chunk_sel(tt, CHUNK))


def _rglru_kernel(x_ref, gate_ref, cw_ref, cb_ref, wa_ref, ba_ref, wx_ref, bx_ref, lam_ref, o_ref,
                  tail_ref, h_ref, a_scr, u_scr):
    @pl.when(pl.program_id(1) == 0)
    def _():
        tail_ref[...] = jnp.zeros_like(tail_ref)
        h_ref[...] = jnp.zeros_like(h_ref)

    tt = x_ref.shape[0]
    x = x_ref[...]
    cw = cw_ref[...]
    xc = x * cw[CONV_K - 1:CONV_K, :] + cb_ref[...]
    for s in range(1, CONV_K):
        xc += _shift_rows(x, tail_ref, s) * cw[CONV_K - 1 - s:CONV_K - s, :]
    tail_ref[...] = x[tt - SUBLANES:, :]

    r = _sigmoid(_mm(xc, wa_ref[...]) + ba_ref[...])
    i = _sigmoid(_mm(xc, wx_ref[...]) + bx_ref[...])
    log_a = -LRU_C * r * _softplus(-lam_ref[...])
    a_scr[...] = jnp.exp(log_a)
    u_scr[...] = jnp.sqrt(1.0 - jnp.exp(2.0 * log_a)) * (i * xc)

    row = lax.broadcasted_iota(jnp.int32, (SUBLANES, x.shape[1]), 0)

    def step(i, h):
        rows = pl.ds(pl.multiple_of(i * SUBLANES, SUBLANES), SUBLANES)
        a = a_scr[rows, :]
        u = u_scr[rows, :]
        for s in (1, 2, 4):
            u = u + a * jnp.where(row >= s, pltpu.roll(u, s, axis=0), 0.0)
            a = a * jnp.where(row >= s, pltpu.roll(a, s, axis=0), 1.0)
        hb = u + a * h
        u_scr[rows, :] = hb
        return hb[SUBLANES - 1:, :]

    h_ref[...] = lax.fori_loop(0, tt // SUBLANES, step, h_ref[...], unroll=4)
    gate = gate_ref[...]
    gelu = 0.5 * gate * (1.0 + jnp.tanh(math.sqrt(2.0 / math.pi) * (gate + 0.044715 * gate * gate * gate)))
    o_ref[...] = (u_scr[...] * gelu).astype(o_ref.dtype)


def _rglru(p, cw, cb, wa, ba, wx, bx, lam, tt):
    bsz, t, _ = p.shape
    w = LRU_WIDTH

    def col(name):
        off, _ = EVEN_COLS[name]
        return pl.BlockSpec((None, tt, w), lambda b, i, j=off // w: (b, i, j))

    fix = lambda b, i: (0, 0)
    vec = pl.BlockSpec((1, w), fix)
    return pl.pallas_call(
        _rglru_kernel,
        grid=(bsz, t // tt),
        in_specs=[col("xb"), col("gate"), pl.BlockSpec((CONV_K, w), fix), vec,
                  pl.BlockSpec((w, w), fix), vec, pl.BlockSpec((w, w), fix), vec, vec],
        out_specs=pl.BlockSpec((None, tt, w), lambda b, i: (b, i, 0)),
        out_shape=jax.ShapeDtypeStruct((bsz, t, w), BF16),
        scratch_shapes=[pltpu.VMEM((SUBLANES, w), F32), pltpu.VMEM((1, w), F32),
                        pltpu.VMEM((tt, w), F32), pltpu.VMEM((tt, w), F32)],
        compiler_params=_params("parallel", "arbitrary"),
        name="rglru",
    )(p, p, cw, cb, wa, ba, wx, bx, lam)


def _ssd_kernel(z_ref, x_ref, bm_ref, cm_ref, dt_ref, cwx_ref, cbx_ref, cwb_ref, cbb_ref, cwc_ref, cbc_ref,
                dtb_ref, alog_ref, dskip_ref, nw_ref, expand_ref, sel_ref, o_ref,
                tx_ref, tb_ref, tc_ref, st_ref, xs_scr, bs_scr, cs_scr, acs_scr, xdt_scr, xdec_scr, ea_scr):
    @pl.when(pl.program_id(1) == 0)
    def _():
        tx_ref[...] = jnp.zeros_like(tx_ref)
        tb_ref[...] = jnp.zeros_like(tb_ref)
        tc_ref[...] = jnp.zeros_like(tc_ref)
        st_ref[...] = jnp.zeros_like(st_ref)

    tt = x_ref.shape[0]

    def conv_silu(src_ref, tail_ref, cw_ref, cb_ref, dst_ref):
        x = src_ref[...]
        cw = cw_ref[...]
        y = x * cw[CONV_K - 1:CONV_K, :] + cb_ref[...]
        for s in range(1, CONV_K):
            y += _shift_rows(x, tail_ref, s) * cw[CONV_K - 1 - s:CONV_K - s, :]
        tail_ref[...] = x[tt - SUBLANES:, :]
        dst_ref[...] = _silu(y)

    conv_silu(x_ref, tx_ref, cwx_ref, cbx_ref, xs_scr)
    conv_silu(bm_ref, tb_ref, cwb_ref, cbb_ref, bs_scr)
    conv_silu(cm_ref, tc_ref, cwc_ref, cbc_ref, cs_scr)

    tri = _tri(CHUNK)
    expand = expand_ref[...]
    dskip = dskip_ref[...]
    nw = nw_ref[...]
    hpg = SSD_HEADS // SSD_GROUPS

    dtc = _softplus(dt_ref[...] + dtb_ref[...])
    da = dtc * -jnp.exp(alog_ref[...])
    sums = _mm_01_left(sel_ref[...], da)
    acs_all = sums[:tt]
    tot = sums[tt:]
    acs_scr[...] = acs_all
    xdt_all = xs_scr[...] * _mm_01(dtc, expand)
    xdt_scr[...] = xdt_all
    xdec_scr[...] = xdt_all * _mm_01(jnp.exp(tot - acs_all), expand)
    ea_scr[...] = _mm_01(jnp.exp(acs_all), expand)

    def chunk(c, carry):
        rows = pl.ds(pl.multiple_of(c * CHUNK, CHUNK), CHUNK)
        acs = acs_scr[rows, :]
        acs_t = acs.T
        ea_x = ea_scr[rows, :]
        x = xs_scr[rows, :]
        xdt = xdt_scr[rows, :]
        xdec = xdec_scr[rows, :]
        cd_x = ea_x[CHUNK - 1:CHUNK, :]
        ys = []
        for g in range(SSD_GROUPS):
            sg = slice(g * SSD_GROUP_WIDTH, (g + 1) * SSD_GROUP_WIDTH)
            ss = slice(g * SSD_STATE, (g + 1) * SSD_STATE)
            bg = bs_scr[rows, ss]
            cg = cs_scr[rows, ss]
            cb = _mm(cg, bg, _NT)
            st = st_ref[g]
            yg = _mm(cg, st) * ea_x[:, sg]
            st_ref[g] = st * cd_x[:, sg] + _mm(bg, xdec[:, sg], _TN)
            yh = []
            for j in range(hpg):
                h = g * hpg + j
                seg = acs[:, h:h + 1] - acs_t[h:h + 1, :]
                lmat = jnp.exp(jnp.where(tri, seg, -jnp.inf))
                sh = slice(h * SSD_HEADDIM, (h + 1) * SSD_HEADDIM)
                yh.append(_mm(cb * lmat, xdt[:, sh]))
            ys.append(yg + jnp.concatenate(yh, axis=1))
        y = jnp.concatenate(ys, axis=1) + x * dskip
        y = y * _silu(z_ref[rows, :])
        outs = []
        for g in range(SSD_GROUPS):
            sg = slice(g * SSD_GROUP_WIDTH, (g + 1) * SSD_GROUP_WIDTH)
            yg = y[:, sg]
            ms = jnp.mean(yg * yg, axis=-1, keepdims=True)
            outs.append(yg * lax.rsqrt(ms + SSD_NORM_EPS))
        o_ref[rows, :] = (jnp.concatenate(outs, axis=1) * nw).astype(o_ref.dtype)
        return carry

    lax.fori_loop(0, tt // CHUNK, chunk, 0, unroll=2)


def _ssd(p, cw, cb, dtb, alog, dskip_x, nw, expand, tt):
    bsz, t, _ = p.shape

    def col(name):
        off, w = ODD_COLS[name]
        return pl.BlockSpec((None, tt, w), lambda b, i, j=off // w: (b, i, j))

    fix = lambda b, i: (0, 0)
    full = lambda a: pl.BlockSpec(a.shape, fix)
    gs = SSD_GROUPS * SSD_STATE
    cwx, cwb, cwc = cw[:, :SSD_INNER], cw[:, SSD_INNER:SSD_INNER + gs], cw[:, SSD_INNER + gs:]
    cbx, cbb, cbc = cb[:, :SSD_INNER], cb[:, SSD_INNER:SSD_INNER + gs], cb[:, SSD_INNER + gs:]
    sel = jnp.concatenate([_chunk_sel(tt, CHUNK), _chunk_sel(tt, CHUNK, "all")], axis=0)
    args = (cwx, cbx, cwb, cbb, cwc, cbc, dtb, alog, dskip_x, nw, expand, sel)
    return pl.pallas_call(
        _ssd_kernel,
        grid=(bsz, t // tt),
        in_specs=[col("z"), col("x"), col("bm"), col("cm"), col("dt")] + [full(a) for a in args],
        out_specs=pl.BlockSpec((None, tt, SSD_INNER), lambda b, i: (b, i, 0)),
        out_shape=jax.ShapeDtypeStruct((bsz, t, SSD_INNER), BF16),
        scratch_shapes=[pltpu.VMEM((SUBLANES, SSD_INNER), F32), pltpu.VMEM((SUBLANES, gs), F32),
                        pltpu.VMEM((SUBLANES, gs), F32),
                        pltpu.VMEM((SSD_GROUPS, SSD_STATE, SSD_GROUP_WIDTH), F32),
                        pltpu.VMEM((tt, SSD_INNER), F32), pltpu.VMEM((tt, gs), F32), pltpu.VMEM((tt, gs), F32),
                        pltpu.VMEM((tt, LANES), F32)] + [pltpu.VMEM((tt, SSD_INNER), F32)] * 3,
        compiler_params=_params("parallel", "arbitrary"),
        name="ssd",
    )(p, p, p, p, p, *args)


def _rwkv_kernel(r_ref, k_ref, v_ref, wa_ref, xg_ref, mur_ref, muk_ref, muv_ref, muwa_ref, mug_ref,
                 w0_ref, w2_ref, a0_ref, a2_ref, g2_ref, kk_ref, ka_ref, rk_ref, lng_ref, lnb_ref, ones_ref, sel_ref,
                 o_ref, tr_ref, tk_ref, tv_ref, twa_ref, tg_ref, st_ref, y_scr,
                 at_scr, rt_scr, bt_scr, kt_scr, bh_scr, kh_scr, v_scr, wc_scr):
    @pl.when(pl.program_id(1) == 0)
    def _():
        for ref in (tr_ref, tk_ref, tv_ref, twa_ref, tg_ref, st_ref):
            ref[...] = jnp.zeros_like(ref)

    tt = r_ref.shape[0]
    hd = RWKV_HEADDIM

    def mix(p_ref, tail_ref, mu_ref):
        p = p_ref[...]
        prev = _shift_rows(p, tail_ref, 1)
        tail_ref[...] = p[tt - SUBLANES:, :]
        return p + (prev - p) * mu_ref[...]

    r = mix(r_ref, tr_ref, mur_ref)
    k = mix(k_ref, tk_ref, muk_ref)
    v = mix(v_ref, tv_ref, muv_ref)
    xwa = mix(wa_ref, twa_ref, muwa_ref)
    xg = mix(xg_ref, tg_ref, mug_ref)

    lane = lax.broadcasted_iota(jnp.int32, xwa.shape, 1)
    lora_in = jnp.where(lane < RWKV_DECAY_LORA, jnp.tanh(xwa), xwa)
    w_log = -_softplus(-(w0_ref[...] + _mm(lora_in, w2_ref[...], exact=True))) - 0.5
    lw = -jnp.exp(w_log)
    a_sig = _sigmoid(a0_ref[...] + _mm(lora_in, a2_ref[...], exact=True))
    gate = _mm(_sigmoid(xg), g2_ref[...], exact=True)
    ones_bd = ones_ref[...]
    kk = k * kk_ref[...]
    kk = kk / jnp.maximum(jnp.sqrt(_mm_01(kk * kk, ones_bd)), 1e-12)
    k = k * (1.0 + (a_sig - 1.0) * ka_ref[...])

    cs = RWKV_CHUNK
    tri_incl = _tri(cs)
    tri_strict = _tri(cs, strict=True)
    tri_col = jnp.concatenate([tri_strict, tri_incl], axis=0)
    eye = (tri_incl & ~tri_strict).astype(F32)

    sums = _mm_01_left(sel_ref[...], lw)
    cum = sums[:tt]
    tot = sums[tt:]
    bvec = kk * a_sig
    e_neg = jnp.exp(-cum)
    e_last = jnp.exp(tot - cum)
    for ref, val in ((at_scr, -kk * jnp.exp(cum - lw)), (rt_scr, r * jnp.exp(cum)), (bt_scr, bvec * e_neg),
                     (kt_scr, k * e_neg), (bh_scr, bvec * e_last), (kh_scr, k * e_last), (v_scr, v),
                     (wc_scr, jnp.exp(tot))):
        ref[...] = val

    def chunk(c, carry):
        rows = pl.ds(pl.multiple_of(c * cs, cs), cs)
        at = at_scr[rows, :]
        rt = rt_scr[rows, :]
        bt = bt_scr[rows, :]
        kt = kt_scr[rows, :]
        bh = bh_scr[rows, :]
        kh = kh_scr[rows, :]
        vc = v_scr[rows, :]
        wc = wc_scr[pl.ds(pl.multiple_of(c * cs, cs), 1), :]
        heads = range(RWKV_HEADS)
        sls = [slice(h * hd, (h + 1) * hd) for h in heads]
        amat = [_mm(jnp.concatenate([at[:, sl], rt[:, sl]], axis=0),
                    jnp.concatenate([bt[:, sl], kt[:, sl]], axis=0), _NT) for sl in sls]
        a_ab = [jnp.where(tri_strict, m[:cs, :cs], 0.0) for m in amat]
        a_rb = [jnp.where(tri_incl, m[cs:, :cs], 0.0) for m in amat]
        a_xk = [jnp.where(tri_col, m[:, cs:], 0.0) for m in amat]
        avrv = [_mm(a_xk[h], vc[:, sls[h]]) for h in heads]
        inv = [eye + m for m in a_ab]
        pw = [_mm(m, m) for m in a_ab]
        for _ in range(int(math.log2(cs)) - 2):
            both = [_mm(jnp.concatenate([inv[h], pw[h]], axis=0), pw[h]) for h in heads]
            inv = [inv[h] + both[h][:cs] for h in heads]
            pw = [m[cs:] for m in both]
        inv = [inv[h] + _mm(inv[h], pw[h]) for h in heads]
        tu = [_mm(inv[h], jnp.concatenate([at[:, sls[h]], avrv[h][:cs]], axis=1)) for h in heads]
        qy = [_mm(a_rb[h], tu[h]) + jnp.concatenate([rt[:, sls[h]], avrv[h][cs:]], axis=1)
              for h in heads]
        zb =[_mm(tu[h], bh[:, sls[h]], _TN) for h in heads]
        vk = [_mm(vc[:, sls[h]], kh[:, sls[h]], _TN) for h in heads]
        for h in heads:
            st = st_ref[h]
            y_scr[rows, sls[h]] = _mm(qy[h][:, :hd], st, _NT) + qy[h][:, hd:]
            st_ref[h] = st * wc[:, sls[h]] + _mm(st, zb[h][:hd]) + zb[h][hd:] + vk[h]
        return carry

    lax.fori_loop(0, tt // cs, chunk, 0, unroll=2)

    y = y_scr[...]
    inv_n = 1.0 / hd
    mu_y = _mm_01(y, ones_bd) * inv_n
    dy = y - mu_y
    var_y = _mm_01(dy * dy, ones_bd) * inv_n
    yn = dy * lax.rsqrt(var_y + RWKV_GN_EPS) * lng_ref[...] + lnb_ref[...]
    bonus = _mm_01(r * k * rk_ref[...], ones_bd) * v
    o_ref[...] = ((yn + bonus) * gate).astype(o_ref.dtype)


def _rwkv(p, mu, w0, w2p, a0, a2p, g2, k_k, k_a, r_k, ln_g, ln_b, ones_bd, tt):
    bsz, t, _ = p.shape
    w = RWKV_WIDTH

    def col(name):
        off, wd = ODD_COLS[name]
        return pl.BlockSpec((None, tt, wd), lambda b, i, j=off // wd: (b, i, j))

    fix = lambda b, i: (0, 0)
    full = lambda a: pl.BlockSpec(a.shape, fix)
    mur, muk, muv = mu[:, :w], mu[:, w:2 * w], mu[:, 2 * w:3 * w]
    muwa, mug = mu[:, 3 * w:3 * w + LANES], mu[:, 3 * w + LANES:]
    sel = jnp.concatenate([_chunk_sel(tt, RWKV_CHUNK), _chunk_sel(tt, RWKV_CHUNK, "all")], axis=0)
    args = (mur, muk, muv, muwa, mug, w0, w2p, a0, a2p, g2, k_k, k_a, r_k, ln_g, ln_b, ones_bd, sel)
    return pl.pallas_call(
        _rwkv_kernel,
        grid=(bsz, t // tt),
        in_specs=[col("r"), col("k"), col("v"), col("wa"), col("xg")] + [full(a) for a in args],
        out_specs=pl.BlockSpec((None, tt, w), lambda b, i: (b, i, 0)),
        out_shape=jax.ShapeDtypeStruct((bsz, t, w), BF16),
        scratch_shapes=[pltpu.VMEM((SUBLANES, w), F32), pltpu.VMEM((SUBLANES, w), F32),
                        pltpu.VMEM((SUBLANES, w), F32), pltpu.VMEM((SUBLANES, LANES), F32),
                        pltpu.VMEM((SUBLANES, LANES), F32),
                        pltpu.VMEM((RWKV_HEADS, RWKV_HEADDIM, RWKV_HEADDIM), F32)]
        + [pltpu.VMEM((tt, w), F32)] * 9,
        compiler_params=_params("parallel", "arbitrary"),
        name="rwkv7",
    )(p, p, p, p, p, *args)


def _reorder_cols(w, pieces, order, width):
    parts, pos = [], 0
    for name, (off, slot) in order.items():
        assert off == pos, "slots must be listed in order and contiguous"
        start, size = pieces[name]
        parts.append(w[:, start:start + size])
        if size < slot:
            parts.append(jnp.zeros((w.shape[0], slot - size), w.dtype))
        pos += slot
    assert pos == width
    return jnp.concatenate(parts, axis=1)


def _block_diag(w):
    n, i, j = w.shape
    eye = jnp.eye(n, dtype=w.dtype)
    return (eye[:, None, :, None] * w[:, :, None, :]).reshape(n * i, n * j)


def _row(v, width=None):
    v = v.reshape(1, -1).astype(F32)
    if width is not None and v.shape[1] < width:
        v = jnp.pad(v, ((0, 0), (0, width - v.shape[1])))
    return v


def _even_layer(x, x_in, w_in, gk_w2, gk_b, gla_norm, conv_w, conv_b, wa, ba, wx, bx, lam,
                w_out, ln1_g, ln1_b, f_w1, f_w3, f_w2, ln2_g, ln2_b, bsz, t):
    qk, vw, lw = GLA_QK, GLA_VW, LRU_WIDTH
    pieces = dict(q=(0, qk), k=(qk, qk), v=(2 * qk, vw), g=(2 * qk + vw, vw), gk=(2 * qk + 2 * vw, GLA_GK_RANK),
                  xb=(2 * qk + 2 * vw + GLA_GK_RANK, lw), gate=(2 * qk + 2 * vw + GLA_GK_RANK + lw, lw))
    w_in_r = _reorder_cols(w_in, pieces, EVEN_COLS, EVEN_WIDTH).astype(BF16)
    p = _proj(x_in, w_in_r, tm=min(512, x.shape[0])).reshape(bsz, t, EVEN_WIDTH)
    tt = min(512, t)
    gkw = jnp.pad(gk_w2, ((0, LANES - GLA_GK_RANK), (0, 0)))
    y_a = _gla(p, gkw, _row(gk_b), _row(gla_norm), tt)
    y_b = _rglru(p, conv_w, _row(conv_b), _block_diag(wa).astype(BF16), _row(ba), _block_diag(wx).astype(BF16),
                 _row(bx), _row(lam), tt)
    n = bsz * t
    tm = min(512, n)
    x1, x1b = _outproj_ln(y_a.reshape(n, vw), y_b.reshape(n, lw), w_out[:vw].astype(BF16), w_out[vw:].astype(BF16),
                          x, _row(ln1_g), _row(ln1_b), tm)
    return _ffn_ln(x1b, f_w1.astype(BF16), f_w3.astype(BF16), f_w2.astype(BF16), x1,
                   _row(ln2_g), _row(ln2_b), tm=min(512, n))


def _odd_layer(x, xb, w_in, conv_w, conv_b, dt_bias, a_log, d_skip, ssd_norm, mu, w0, w2, a0, a2, g2, k_k, k_a,
               r_k, rln_g, rln_b, w_out, ln1_g, ln1_b, router, ew1, ew3, ew2, ln2_g, ln2_b, bsz, t):
    si, gs, rw = SSD_INNER, SSD_GROUPS * SSD_STATE, RWKV_WIDTH
    o = 2 * si + 2 * gs + SSD_HEADS
    pieces = dict(z=(0, si), x=(si, si), bm=(2 * si, gs), cm=(2 * si + gs, gs), dt=(2 * si + 2 * gs, SSD_HEADS),
                  r=(o, rw), k=(o + rw, rw), v=(o + 2 * rw, rw),
                  wa=(o + 3 * rw, RWKV_DECAY_LORA + RWKV_AAA_LORA), xg=(o + 3 * rw + LANES, RWKV_GATE_LORA))
    w_in_r = _reorder_cols(w_in, pieces, ODD_COLS, ODD_WIDTH).astype(BF16)
    n = bsz * t
    p = _proj(xb, w_in_r, tm=min(512, n)).reshape(bsz, t, ODD_WIDTH)
    expand = jnp.pad(jnp.repeat(jnp.eye(SSD_HEADS, dtype=BF16), SSD_HEADDIM, axis=1), ((0, LANES - SSD_HEADS), (0, 0)))
    y_c = _ssd(p, conv_w, _row(conv_b), _row(dt_bias, LANES), _row(a_log, LANES),
               _row(jnp.repeat(d_skip, SSD_HEADDIM)), _row(ssd_norm), expand, tt=min(256, t))
    zeros = jnp.zeros((RWKV_DECAY_LORA, rw), F32)
    ones_bd = _block_diag(jnp.ones((RWKV_HEADS, RWKV_HEADDIM, RWKV_HEADDIM), BF16))
    y_d = _rwkv(p, _row(mu), _row(w0), jnp.concatenate([w2, zeros]), _row(a0), jnp.concatenate([zeros, a2]), g2,
                _row(k_k), _row(k_a), _row(r_k), _row(rln_g), _row(rln_b), ones_bd, tt=min(256, t))
    tm = min(512, n)
    x1, x1b = _outproj_ln(y_c.reshape(n, si), y_d.reshape(n, rw), w_out[:si].astype(BF16), w_out[si:].astype(BF16),
                          x, _row(ln1_g), _row(ln1_b), tm)
    y = _moe_ln(x1, x1b, jnp.pad(router, ((0, 0), (0, LANES - N_EXPERTS))), ew1.astype(BF16), ew3.astype(BF16),
                ew2.astype(BF16), _row(ln2_g), _row(ln2_b))
    return y, None


def kernel(x, e_w_in, e_gk_w2, e_gk_b, e_gla_norm, e_conv_w, e_conv_b, e_lru_wa, e_lru_ba, e_lru_wx, e_lru_bx, e_lru_lambda, e_w_out, e_ln1_g, e_ln1_b, e_ffn_w1, e_ffn_w3, e_ffn_w2, e_ln2_g, e_ln2_b, o_w_in, o_conv_w, o_conv_b, o_dt_bias, o_a_log, o_d_skip, o_ssd_norm, o_rwkv_mu, o_rwkv_w0, o_rwkv_w2, o_rwkv_a0, o_rwkv_a2, o_rwkv_g2, o_rwkv_k_k, o_rwkv_k_a, o_rwkv_r_k, o_rwkv_ln_g, o_rwkv_ln_b, o_w_out, o_ln1_g, o_ln1_b, o_router, o_exp_w1, o_exp_w3, o_exp_w2, o_ln2_g, o_ln2_b):
    bsz, t, d = x.shape
    h = x.reshape(bsz * t, d)
    hb = h
    for i in range(DEPTH):
        j = i // 2
        if i % 2 == 0:
            h, hb = _even_layer(h, hb, e_w_in[j], e_gk_w2[j], e_gk_b[j], e_gla_norm[j], e_conv_w[j], e_conv_b[j],
                                e_lru_wa[j], e_lru_ba[j], e_lru_wx[j], e_lru_bx[j], e_lru_lambda[j], e_w_out[j],
                                e_ln1_g[j], e_ln1_b[j], e_ffn_w1[j], e_ffn_w3[j], e_ffn_w2[j], e_ln2_g[j],
                                e_ln2_b[j], bsz, t)
        else:
            h, hb = _odd_layer(h, hb, o_w_in[j], o_conv_w[j], o_conv_b[j], o_dt_bias[j], o_a_log[j], o_d_skip[j],
                               o_ssd_norm[j], o_rwkv_mu[j], o_rwkv_w0[j], o_rwkv_w2[j], o_rwkv_a0[j],
                               o_rwkv_a2[j], o_rwkv_g2[j], o_rwkv_k_k[j], o_rwkv_k_a[j], o_rwkv_r_k[j],
                               o_rwkv_ln_g[j], o_rwkv_ln_b[j], o_w_out[j], o_ln1_g[j], o_ln1_b[j], o_router[j],
                               o_exp_w1[j], o_exp_w3[j], o_exp_w2[j], o_ln2_g[j], o_ln2_b[j], bsz, t)
    return h.reshape(bsz, t, d)
```

```python
import functools
import math

import jax
import jax.numpy as jnp
from jax import lax
from jax.experimental import pallas as pl
from jax.experimental.pallas import tpu as pltpu

F32 = jnp.float32
BF16 = jnp.bfloat16
HIGHEST = lax.Precision.HIGHEST

D_MODEL = 1024
DEPTH = 2
DEEPNORM_ALPHA = (2 * DEPTH) ** 0.25
LN_EPS = 1e-5
CONV_K = 4
CHUNK = 64

GLA_HEADS = 4
GLA_DK = 64
GLA_DV = 128
GLA_QK = GLA_HEADS * GLA_DK
GLA_VW = GLA_HEADS * GLA_DV
GLA_GK_RANK = 16
GLA_GATE_NORM = 16.0
GLA_NORM_EPS = 1e-5

LRU_WIDTH = 512
LRU_BLOCKS = 8
LRU_C = 8.0

SSD_HEADS = 16
SSD_HEADDIM = 64
SSD_INNER = SSD_HEADS * SSD_HEADDIM
SSD_GROUPS = 2
SSD_STATE = 128
SSD_GROUP_WIDTH = SSD_INNER // SSD_GROUPS
SSD_NORM_EPS = 1e-5

RWKV_HEADS = 8
RWKV_HEADDIM = 64
RWKV_WIDTH = RWKV_HEADS * RWKV_HEADDIM
RWKV_DECAY_LORA = 64
RWKV_AAA_LORA = 64
RWKV_GATE_LORA = 128
RWKV_GN_EPS = 64e-5
RWKV_CHUNK = 128

N_EXPERTS = 8
LANES = 128
SUBLANES = 8
assert N_EXPERTS == SUBLANES
MOE_TB = 512
MOE_TR = 256
MOE_TE = 512
assert MOE_TE % MOE_TR == 0
VMEM_LIMIT = 56 * 1024 * 1024

EVEN_COLS = dict(v=(0, 512), g=(512, 512), xb=(1024, 512), gate=(1536, 512), q=(2048, 256), k=(2304, 256),
                 gk=(2560, 128))
EVEN_WIDTH = 2688
ODD_COLS = dict(z=(0, 1024), x=(1024, 1024), r=(2048, 512), k=(2560, 512), v=(3072, 512), bm=(3584, 256),
                cm=(3840, 256), wa=(4096, 128), xg=(4224, 128), dt=(4352, 128))
ODD_WIDTH = 4480


def _mm(a, b, dims=((1,), (0,)), exact=False):
    dn = (dims, ((), ()))
    if exact:
        return lax.dot_general(a.astype(F32), b.astype(F32), dn, precision=HIGHEST, preferred_element_type=F32)
    return lax.dot_general(a.astype(BF16), b.astype(BF16), dn, preferred_element_type=F32)


_NT = ((1,), (1,))
_TN = ((0,), (0,))


def _mm_01(x, sel):
    hi = x.astype(BF16)
    lo = (x - hi.astype(F32)).astype(BF16)
    return (jnp.dot(hi, sel, preferred_element_type=F32) + jnp.dot(lo, sel, preferred_element_type=F32))


def _sigmoid(x):
    return 1.0 / (1.0 + jnp.exp(-x))


def _softplus(x):
    return jnp.maximum(x, 0.0) + jnp.log(1.0 + jnp.exp(-jnp.abs(x)))


def _silu(x):
    return x * _sigmoid(x)


def _layer_norm(h, g, b):
    mu = jnp.mean(h, axis=-1, keepdims=True)
    d = h - mu
    var = jnp.mean(d * d, axis=-1, keepdims=True)
    return d * lax.rsqrt(var + LN_EPS) * g + b


def _params(*sem):
    return pltpu.CompilerParams(dimension_semantics=sem, vmem_limit_bytes=VMEM_LIMIT)


def _tri(n, strict=False):
    r = lax.broadcasted_iota(jnp.int32, (n, n), 0)
    c = lax.broadcasted_iota(jnp.int32, (n, n), 1)
    return (r > c) if strict else (r >= c)


def _chunk_sel(n, chunk, kind="incl"):
    r = jnp.arange(n)[:, None]
    c = jnp.arange(n)[None, :]
    first = r - r % chunk
    upper = r if kind == "incl" else first + (chunk - 1)
    return ((c >= first) & (c <= upper)).astype(BF16)


def _mm_01_left(sel, x):
    hi = x.astype(BF16)
    lo = (x - hi.astype(F32)).astype(BF16)
    return jnp.dot(sel, hi, preferred_element_type=F32) + jnp.dot(sel, lo, preferred_element_type=F32)


def _shift_rows(p, tail_ref, s):
    n = p.shape[0]
    rolled = pltpu.roll(p, s, axis=0)
    head = pltpu.roll(tail_ref[...], s, axis=0)
    row = lax.broadcasted_iota(jnp.int32, (SUBLANES, p.shape[1]), 0)
    fixed = jnp.where(row < s, head, rolled[:SUBLANES])
    return jnp.concatenate([fixed, rolled[SUBLANES:]], axis=0) if n > SUBLANES else fixed


def _proj_kernel(x_ref, w_ref, o_ref):
    o_ref[...] = jnp.dot(x_ref[...].astype(BF16), w_ref[...], preferred_element_type=F32)


def _resident(shape):
    return pl.BlockSpec(shape, lambda *_: (0,) * len(shape), pipeline_mode=pl.Buffered(1))


def _proj(x, w, tm):
    m, k = x.shape
    n = w.shape[1]
    return pl.pallas_call(
        _proj_kernel,
        grid=(m // tm,),
        in_specs=[pl.BlockSpec((tm, k), lambda i: (i, 0)), _resident((k, n))],
        out_specs=pl.BlockSpec((tm, n), lambda i: (i, 0)),
        out_shape=jax.ShapeDtypeStruct((m, n), F32),
        compiler_params=_params("parallel"),
        name="in_proj",
    )(x, w)


def _outproj_ln_kernel(ya_ref, yb_ref, wa_ref, wb_ref, res_ref, g_ref, b_ref, o_ref, ob_ref):
    acc = jnp.dot(ya_ref[...].astype(BF16), wa_ref[...], preferred_element_type=F32)
    acc += jnp.dot(yb_ref[...].astype(BF16), wb_ref[...], preferred_element_type=F32)
    y = _layer_norm(DEEPNORM_ALPHA * res_ref[...] + acc, g_ref[...], b_ref[...])
    o_ref[...] = y
    ob_ref[...] = y.astype(BF16)


def _outproj_ln(ya, yb, wa, wb, res, g, b, tm):
    m = ya.shape[0]
    d = res.shape[1]
    row = lambda i: (i, 0)
    fix = lambda i: (0, 0)
    return pl.pallas_call(
        _outproj_ln_kernel,
        grid=(m // tm,),
        in_specs=[pl.BlockSpec((tm, ya.shape[1]), row), pl.BlockSpec((tm, yb.shape[1]), row),
                  pl.BlockSpec(wa.shape, fix), pl.BlockSpec(wb.shape, fix), pl.BlockSpec((tm, d), row),
                  pl.BlockSpec((1, d), fix), pl.BlockSpec((1, d), fix)],
        out_specs=[pl.BlockSpec((tm, d), row), pl.BlockSpec((tm, d), row)],
        out_shape=[jax.ShapeDtypeStruct((m, d), F32), jax.ShapeDtypeStruct((m, d), BF16)],
        compiler_params=_params("parallel"),
        name="out_proj_ln",
    )(ya, yb, wa, wb, res, g, b)


def _ffn_kernel(xb_ref, w1_ref, w3_ref, w2_ref, res_ref, g_ref, b_ref, o_ref, ob_ref):
    x = xb_ref[...]
    h1 = jnp.dot(x, w1_ref[...], preferred_element_type=F32)
    h3 = jnp.dot(x, w3_ref[...], preferred_element_type=F32)
    ff = jnp.dot((_silu(h1) * h3).astype(BF16), w2_ref[...], preferred_element_type=F32)
    y = _layer_norm(DEEPNORM_ALPHA * res_ref[...] + ff, g_ref[...], b_ref[...])
    o_ref[...] = y
    ob_ref[...] = y.astype(BF16)


def _ffn_ln(xb, w1, w3, w2, res, g, b, tm):
    m, d = xb.shape
    row = lambda i: (i, 0)
    return pl.pallas_call(
        _ffn_kernel,
        grid=(m // tm,),
        in_specs=[pl.BlockSpec((tm, d), row), _resident(w1.shape), _resident(w3.shape), _resident(w2.shape),
                  pl.BlockSpec((tm, d), row), _resident((1, d)), _resident((1, d))],
        out_specs=[pl.BlockSpec((tm, d), row), pl.BlockSpec((tm, d), row)],
        out_shape=[jax.ShapeDtypeStruct((m, d), F32), jax.ShapeDtypeStruct((m, d), BF16)],
        compiler_params=_params("parallel"),
        name="ffn_ln",
    )(xb, w1, w3, w2, res, g, b)


def _router_kernel(x_ref, w_ref, gates_ref, rank_ref, rank_t_ref, cnt_ref, carry_ref):
    @pl.when(pl.program_id(0) == 0)
    def _():
        carry_ref[...] = jnp.zeros_like(carry_ref)

    x = x_ref[...]
    w = w_ref[...]
    x_hi = x.astype(BF16)
    w_hi = w.astype(BF16)
    x_lo = (x - x_hi.astype(F32)).astype(BF16)
    w_lo = (w - w_hi.astype(F32)).astype(BF16)
    logits = (jnp.dot(x_hi, w_hi, preferred_element_type=F32) + jnp.dot(x_lo, w_hi, preferred_element_type=F32)
              + jnp.dot(x_hi, w_lo, preferred_element_type=F32))
    lane = lax.broadcasted_iota(jnp.int32, logits.shape, 1)
    neg = jnp.float32(-jnp.inf)
    l1 = jnp.where(lane < N_EXPERTS, logits, neg)
    m1 = jnp.max(l1, axis=1, keepdims=True)
    i1 = jnp.min(jnp.where(l1 == m1, lane, LANES), axis=1, keepdims=True)
    l2 = jnp.where(lane == i1, neg, l1)
    m2 = jnp.max(l2, axis=1, keepdims=True)
    i2 = jnp.min(jnp.where(l2 == m2, lane, LANES), axis=1, keepdims=True)
    ex = jnp.exp(m2 - m1)
    w_top = 1.0 / (1.0 + ex)
    gates_ref[...] = jnp.where(lane == i1, w_top, 0.0) + jnp.where(lane == i2, ex * w_top, 0.0)

    sel = jnp.where(lane == i1, 1.0, 0.0) + jnp.where(lane == i2, 1.0, 0.0)
    tb = sel.shape[0]
    before = _mm(_tri(tb, strict=True).astype(F32), sel)
    carry = carry_ref[...]
    rank = jnp.where(sel > 0.0, carry + before, -1.0)
    rank_ref[...] = rank
    rank_t_ref[...] = rank.T[:SUBLANES, :]
    carry = carry + jnp.sum(sel, axis=0, keepdims=True)
    carry_ref[...] = carry
    cnt_ref[...] = jnp.broadcast_to(carry, cnt_ref.shape)


def _router(x, w):
    m, d = x.shape
    nblk = m // MOE_TB
    return pl.pallas_call(
        _router_kernel,
        grid=(nblk,),
        in_specs=[pl.BlockSpec((MOE_TB, d), lambda i: (i, 0)), pl.BlockSpec((d, LANES), lambda i: (0, 0))],
        out_specs=[pl.BlockSpec((MOE_TB, LANES), lambda i: (i, 0)), pl.BlockSpec((MOE_TB, LANES), lambda i: (i, 0)),
                   pl.BlockSpec((SUBLANES, MOE_TB), lambda i: (0, i)), pl.BlockSpec((SUBLANES, LANES), lambda i: (i, 0))],
        out_shape=[jax.ShapeDtypeStruct((m, LANES), F32), jax.ShapeDtypeStruct((m, LANES), F32),
                   jax.ShapeDtypeStruct((SUBLANES, m), F32), jax.ShapeDtypeStruct((nblk * SUBLANES, LANES), F32)],
        scratch_shapes=[pltpu.VMEM((1, LANES), F32)],
        compiler_params=_params("arbitrary"),
        name="router",
    )(x, w)


def _moe_plan(cnt, n_tok):
    i32 = jnp.int32
    nblk = n_tok // MOE_TB
    c_inc = cnt.reshape(nblk, SUBLANES, LANES)[:, 0, :N_EXPERTS].astype(i32)
    c_exc = jnp.concatenate([jnp.zeros((1, N_EXPERTS), i32), c_inc[:-1]], axis=0)
    gsz = (c_inc[-1] + MOE_TE - 1) // MOE_TE * MOE_TE
    gend = jnp.cumsum(gsz)
    start = gend - gsz
    n_et = _moe_rows(n_tok) // MOE_TE
    n_act = gend[-1] // MOE_TE
    et = jnp.minimum(jnp.arange(n_et, dtype=i32), jnp.maximum(n_act - 1, 0))
    et_expert = jnp.sum((gend[None, :] <= (et * MOE_TE)[:, None]).astype(i32), axis=1)
    et_active = (jnp.arange(n_et, dtype=i32) < n_act).astype(i32)

    lo = start[None, :] + c_exc
    hi = start[None, :] + c_inc
    first_tile = lo // MOE_TR
    n_items = jnp.where(hi > lo, (hi - 1) // MOE_TR - first_tile + 1, 0)
    wmax = _moe_rows(n_tok) // MOE_TR + N_EXPERTS * nblk
    exp_id = jnp.broadcast_to(jnp.arange(N_EXPERTS, dtype=i32)[None, :], (nblk, N_EXPERTS))

    n, ft, exp = (a.reshape(-1) for a in (n_items, first_tile, exp_id))
    inc = jnp.cumsum(n)
    wc = jnp.minimum(jnp.arange(wmax, dtype=i32), jnp.maximum(inc[-1] - 1, 0))
    idx = jnp.sum((inc[None, :] <= wc[:, None]).astype(i32), axis=1)
    per_block = jnp.sum(n_items, axis=1)
    by_block = (ft[idx] + wc - (inc - n)[idx], exp[idx], jnp.cumsum(per_block) - per_block, per_block)

    row0 = jnp.arange(_moe_rows(n_tok) // MOE_TR, dtype=i32) * MOE_TR
    group = jnp.minimum(jnp.sum((gend[None, :] <= row0[:, None]).astype(i32), axis=1), N_EXPERTS - 1)
    rank_lo = row0 - start[group]
    count = c_inc[-1][group]
    rank_hi = jnp.minimum(rank_lo + MOE_TR, count)
    holds = (rank_lo < count) & (row0 < gend[-1])
    blk_lo = jnp.sum((c_inc[:, group].T <= rank_lo[:, None]).astype(i32), axis=1)
    blk_hi = jnp.sum((c_exc[:, group].T < rank_hi[:, None]).astype(i32), axis=1)
    by_tile = (group, jnp.minimum(blk_lo, nblk - 1), jnp.where(holds, blk_hi - blk_lo, 0))

    return start, (et, et_expert, et_active), by_tile, by_block


def _moe_rows(n_tok):
    return 2 * n_tok + N_EXPERTS * MOE_TE


def _one_hot_rows(dest, tile):
    row = lax.broadcasted_iota(jnp.int32, (MOE_TR, MOE_TB), 0) + tile * MOE_TR
    return jnp.where(dest == row.astype(F32), 1.0, 0.0).astype(BF16)


def _dispatch_kernel(exp_ref, blk0_ref, nblk_ref, x_ref, dest_ref, xs_ref, acc_ref):
    j = pl.program_id(0)
    acc_ref[...] = jnp.zeros_like(acc_ref)

    def block(i, carry):
        b = blk0_ref[j] + i
        onehot = _one_hot_rows(dest_ref[b, pl.ds(exp_ref[j], 1), :], j)
        xblk = x_ref[pl.ds(pl.multiple_of(b * MOE_TB, MOE_TB), MOE_TB), :]
        acc_ref[...] += jnp.dot(onehot, xblk, preferred_element_type=F32)
        return carry

    lax.fori_loop(0, nblk_ref[j], block, 0)
    xs_ref[...] = acc_ref[...].astype(BF16)


def _dispatch(plan, xb, dest_t):
    exp, blk0, nblk = plan
    n, d = xb.shape
    rows = _moe_rows(n)
    dest3 = dest_t.reshape(N_EXPERTS, n // MOE_TB, MOE_TB).transpose(1, 0, 2)
    grid_spec = pltpu.PrefetchScalarGridSpec(
        num_scalar_prefetch=3,
        grid=(rows // MOE_TR,),
        in_specs=[_resident(xb.shape), _resident(dest3.shape)],
        out_specs=pl.BlockSpec((MOE_TR, d), lambda j, e, b, c: (j, 0)),
        scratch_shapes=[pltpu.VMEM((MOE_TR, d), F32)],
    )
    return pl.pallas_call(
        _dispatch_kernel,
        grid_spec=grid_spec,
        out_shape=jax.ShapeDtypeStruct((rows, d), BF16),
        compiler_params=_params("arbitrary"),
        name="moe_dispatch",
    )(exp, blk0, nblk, xb, dest3)


def _expert_kernel(et_ref, ee_ref, ea_ref, xs_ref, w1_ref, w3_ref, w2_ref, ys_ref, acc_ref):
    j = pl.program_id(0)
    f = pl.program_id(1)

    @pl.when((ea_ref[j] == 0) & (f == 0))
    def _():
        ys_ref[...] = jnp.zeros_like(ys_ref)

    @pl.when(ea_ref[j] == 1)
    def _():
        @pl.when(f == 0)
        def _():
            acc_ref[...] = jnp.zeros_like(acc_ref)

        x = xs_ref[...]
        h1 = jnp.dot(x, w1_ref[...], preferred_element_type=F32)
        h3 = jnp.dot(x, w3_ref[...], preferred_element_type=F32)
        acc_ref[...] += jnp.dot((_silu(h1) * h3).astype(BF16), w2_ref[...], preferred_element_type=F32)

        @pl.when(f == pl.num_programs(1) - 1)
        def _():
            ys_ref[...] = acc_ref[...].astype(BF16)


def _experts(plan, xs, w1, w3, w2, tf):
    et, ee, ea = plan
    rows, d = xs.shape
    nf = w1.shape[2] // tf
    fidx = lambda f, a, j: f * a[j] + (nf - 1) * (1 - a[j])
    grid_spec = pltpu.PrefetchScalarGridSpec(
        num_scalar_prefetch=3,
        grid=(et.shape[0], nf),
        in_specs=[pl.BlockSpec((MOE_TE, d), lambda j, f, t, e, a: (t[j], 0)),
                  pl.BlockSpec((None, d, tf), lambda j, f, t, e, a: (e[j], 0, fidx(f, a, j))),
                  pl.BlockSpec((None, d, tf), lambda j, f, t, e, a: (e[j], 0, fidx(f, a, j))),
                  pl.BlockSpec((None, tf, d), lambda j, f, t, e, a: (e[j], fidx(f, a, j), 0))],
        out_specs=pl.BlockSpec((MOE_TE, d), lambda j, f, t, e, a: (j, 0)),
        scratch_shapes=[pltpu.VMEM((MOE_TE, d), F32)],
    )
    return pl.pallas_call(
        _expert_kernel,
        grid_spec=grid_spec,
        out_shape=jax.ShapeDtypeStruct((rows, d), BF16),
        compiler_params=_params("arbitrary", "arbitrary"),
        name="moe_experts",
    )(et, ee, ea, xs, w1, w3, w2)


def _combine_kernel(tile_ref, exp_ref, off_ref, cnt_ref, ys_hbm, dest_ref, gates_ref, res_ref, g_ref, b_ref, o_ref,
                    buf_ref, sem_ref, acc_ref):
    j = pl.program_id(0)
    base = off_ref[j]
    n = cnt_ref[j]

    def fetch(i, slot):
        row0 = pl.multiple_of(tile_ref[base + i] * MOE_TR, MOE_TR)
        return pltpu.make_async_copy(ys_hbm.at[pl.ds(row0, MOE_TR), :], buf_ref.at[slot], sem_ref.at[slot])

    @pl.when(n > 0)
    def _():
        fetch(0, 0).start()

    acc_ref[...] = jnp.zeros_like(acc_ref)
    dest = dest_ref[...]
    gates = gates_ref[...]
    lane = lax.broadcasted_iota(jnp.int32, dest.shape, 1)

    def item(i, carry):
        slot = lax.rem(i, 2)

        @pl.when(i + 1 < n)
        def _():
            fetch(i + 1, 1 - slot).start()

        mine = lane == exp_ref[base + i]
        dcol = jnp.sum(jnp.where(mine, dest, 0.0), axis=1, keepdims=True)
        gcol = jnp.sum(jnp.where(mine, gates, 0.0), axis=1, keepdims=True)
        col = lax.broadcasted_iota(jnp.int32, (MOE_TB, MOE_TR), 1) + tile_ref[base + i] * MOE_TR
        onehot = jnp.where(dcol == col.astype(F32), 1.0, 0.0).astype(BF16)
        fetch(i, slot).wait()
        acc_ref[...] += jnp.dot(onehot, buf_ref[slot], preferred_element_type=F32) * gcol
        return carry

    lax.fori_loop(0, n, item, 0)
    o_ref[...] = _layer_norm(DEEPNORM_ALPHA * res_ref[...] + acc_ref[...], g_ref[...], b_ref[...])


def _combine_ln(plan, ys, dest, gates, res, g, b):
    tile, exp, off, cnt = plan
    n, d = res.shape
    tok = lambda j, *_: (j, 0)
    grid_spec = pltpu.PrefetchScalarGridSpec(
        num_scalar_prefetch=4,
        grid=(n // MOE_TB,),
        in_specs=[pl.BlockSpec(memory_space=pl.ANY),
                  pl.BlockSpec((MOE_TB, LANES), tok), pl.BlockSpec((MOE_TB, LANES), tok), pl.BlockSpec((MOE_TB, d), tok),
                  _resident((1, d)), _resident((1, d))],
        out_specs=pl.BlockSpec((MOE_TB, d), tok),
        scratch_shapes=[pltpu.VMEM((2, MOE_TR, d), BF16), pltpu.SemaphoreType.DMA((2,)),
                        pltpu.VMEM((MOE_TB, d), F32)],
    )
    return pl.pallas_call(
        _combine_kernel,
        grid_spec=grid_spec,
        out_shape=jax.ShapeDtypeStruct((n, d), F32),
        compiler_params=_params("arbitrary"),
        name="moe_combine_ln",
    )(tile, exp, off, cnt, ys, dest, gates, res, g, b)


def _moe_ln(x, xb, router_w, w1, w3, w2, g, b):
    n = x.shape[0]
    gates, rank, rank_t, cnt = _router(x, router_w)
    start, expert_tiles, by_tile, by_block = _moe_plan(cnt, n)
    startf = start.astype(F32)
    dest = jnp.where(rank >= 0.0, rank + jnp.pad(startf, (0, LANES - N_EXPERTS))[None, :], -1.0)
    dest_t = jnp.where(rank_t >= 0.0, rank_t + startf[:, None], -1.0)
    xs = _dispatch(by_tile, xb, dest_t)
    ys = _experts(expert_tiles, xs, w1, w3, w2, tf=1792)
    return _combine_ln(by_block, ys, dest, gates, x, g, b)


def _gla_kernel(q_ref, k_ref, v_ref, g_ref, gk_ref, gkw_ref, gkb_ref, nw_ref, sel_ref, o_ref, st_ref, bc_scr,
                qd_scr, kd_scr):
    @pl.when(pl.program_id(1) == 0)
    def _():
        st_ref[...] = jnp.zeros_like(st_ref)

    tt = q_ref.shape[0]
    tri = _tri(CHUNK)
    nw = nw_ref[...]

    pre = _mm(gk_ref[...], gkw_ref[...], exact=True) + gkb_ref[...]
    gk = (jnp.minimum(pre, 0.0) - jnp.log(1.0 + jnp.exp(-jnp.abs(pre)))) * (1.0 / GLA_GATE_NORM)
    bc_all = _mm_01_left(sel_ref[...], gk)
    bc_scr[...] = bc_all
    qd_scr[...] = q_ref[...] * (GLA_DK ** -0.5) * jnp.exp(bc_all)
    kd_scr[...] = k_ref[...] * jnp.exp(-bc_all)

    def chunk(c, carry):
        rows = pl.ds(pl.multiple_of(c * CHUNK, CHUNK), CHUNK)
        bc = bc_scr[rows, :]
        bl = bc[CHUNK - 1:CHUNK, :]
        qd = qd_scr[rows, :]
        kd = kd_scr[rows, :]
        kl = k_ref[rows, :] * jnp.exp(bl - bc)
        dec = jnp.exp(bl)
        heads = range(GLA_HEADS)
        sks = [slice(h * GLA_DK, (h + 1) * GLA_DK) for h in heads]
        svs = [slice(h * GLA_DV, (h + 1) * GLA_DV) for h in heads]
        att = [jnp.where(tri, _mm(qd[:, sk], kd[:, sk], _NT), 0.0) for sk in sks]
        vs = [v_ref[rows, sv] for sv in svs]
        sts = [st_ref[h] for h in heads]
        os_ = [_mm(att[h], vs[h]) + _mm(qd[:, sks[h]], sts[h], _NT) for h in heads]
        for h in heads:
            st_ref[h] = sts[h] * dec[:, sks[h]] + _mm(vs[h], kl[:, sks[h]], _TN)
        for h in heads:
            o = os_[h]
            ms = jnp.mean(o * o, axis=-1, keepdims=True)
            y = o * lax.rsqrt(ms + GLA_NORM_EPS) * nw * _silu(g_ref[rows, svs[h]])
            o_ref[rows, svs[h]] = y.astype(o_ref.dtype)
        return carry

    lax.fori_loop(0, tt // CHUNK, chunk, 0, unroll=2)


def _gla(p, gkw, gkb, nw, tt):
    bsz, t, _ = p.shape

    def col(name):
        off, w = EVEN_COLS[name]
        return pl.BlockSpec((None, tt, w), lambda b, i, j=off // w: (b, i, j))

    fix = lambda b, i: (0, 0)
    return pl.pallas_call(
        _gla_kernel,
        grid=(bsz, t // tt),
        in_specs=[col("q"), col("k"), col("v"), col("g"), col("gk"),
                  pl.BlockSpec(gkw.shape, fix), pl.BlockSpec(gkb.shape, fix), pl.BlockSpec(nw.shape, fix),
                  pl.BlockSpec((tt, tt), fix)],
        out_specs=pl.BlockSpec((None, tt, GLA_VW), lambda b, i: (b, i, 0)),
        out_shape=jax.ShapeDtypeStruct((bsz, t, GLA_VW), BF16),
        scratch_shapes=[pltpu.VMEM((GLA_HEADS, GLA_DV, GLA_DK), F32)] + [pltpu.VMEM((tt, GLA_QK), F32)] * 3,
        compiler_params=_params("parallel", "arbitrary"),
        name="gla",
    )(p, p, p, p, p, gkw, gkb, nw, _chunk_sel(tt, CHUNK))


def _rglru_kernel(x_ref, gate_ref, cw_ref, cb_ref, wa_ref, ba_ref, wx_ref, bx_ref, lam_ref, o_ref,
                  tail_ref, h_ref, a_scr, u_scr):
    @pl.when(pl.program_id(1) == 0)
    def _():
        tail_ref[...] = jnp.zeros_like(tail_ref)
        h_ref[...] = jnp.zeros_like(h_ref)

    tt = x_ref.shape[0]
    x = x_ref[...]
    cw = cw_ref[...]
    xc = x * cw[CONV_K - 1:CONV_K, :] + cb_ref[...]
    for s in range(1, CONV_K):
        xc += _shift_rows(x, tail_ref, s) * cw[CONV_K - 1 - s:CONV_K - s, :]
    tail_ref[...] = x[tt - SUBLANES:, :]

    r = _sigmoid(_mm(xc, wa_ref[...]) + ba_ref[...])
    i = _sigmoid(_mm(xc, wx_ref[...]) + bx_ref[...])
    log_a = -LRU_C * r * _softplus(-lam_ref[...])
    a_scr[...] = jnp.exp(log_a)
    u_scr[...] = jnp.sqrt(1.0 - jnp.exp(2.0 * log_a)) * (i * xc)

    row = lax.broadcasted_iota(jnp.int32, (SUBLANES, x.shape[1]), 0)

    def step(i, h):
        rows = pl.ds(pl.multiple_of(i * SUBLANES, SUBLANES), SUBLANES)
        a = a_scr[rows, :]
        u = u_scr[rows, :]
        for s in (1, 2, 4):
            u = u + a * jnp.where(row >= s, pltpu.roll(u, s, axis=0), 0.0)
            a = a * jnp.where(row >= s, pltpu.roll(a, s, axis=0), 1.0)
        hb = u + a * h
        u_scr[rows, :] = hb
        return hb[SUBLANES - 1:, :]

    h_ref[...] = lax.fori_loop(0, tt // SUBLANES, step, h_ref[...], unroll=4)
    gate = gate_ref[...]
    gelu = 0.5 * gate * (1.0 + jnp.tanh(math.sqrt(2.0 / math.pi) * (gate + 0.044715 * gate * gate * gate)))
    o_ref[...] = (u_scr[...] * gelu).astype(o_ref.dtype)


def _rglru(p, cw, cb, wa, ba, wx, bx, lam, tt):
    bsz, t, _ = p.shape
    w = LRU_WIDTH

    def col(name):
        off, _ = EVEN_COLS[name]
        return pl.BlockSpec((None, tt, w), lambda b, i, j=off // w: (b, i, j))

    fix = lambda b, i: (0, 0)
    vec = pl.BlockSpec((1, w), fix)
    return pl.pallas_call(
        _rglru_kernel,
        grid=(bsz, t // tt),
        in_specs=[col("xb"), col("gate"), pl.BlockSpec((CONV_K, w), fix), vec,
                  pl.BlockSpec((w, w), fix), vec, pl.BlockSpec((w, w), fix), vec, vec],
        out_specs=pl.BlockSpec((None, tt, w), lambda b, i: (b, i, 0)),
        out_shape=jax.ShapeDtypeStruct((bsz, t, w), BF16),
        scratch_shapes=[pltpu.VMEM((SUBLANES, w), F32), pltpu.VMEM((1, w), F32),
                        pltpu.VMEM((tt, w), F32), pltpu.VMEM((tt, w), F32)],
        compiler_params=_params("parallel", "arbitrary"),
        name="rglru",
    )(p, p, cw, cb, wa, ba, wx, bx, lam)


def _ssd_kernel(z_ref, x_ref, bm_ref, cm_ref, dt_ref, cwx_ref, cbx_ref, cwb_ref, cbb_ref, cwc_ref, cbc_ref,
                dtb_ref, alog_ref, dskip_ref, nw_ref, expand_ref, sel_ref, o_ref,
                tx_ref, tb_ref, tc_ref, st_ref, xs_scr, bs_scr, cs_scr, acs_scr, xdt_scr, xdec_scr, ea_scr):
    @pl.when(pl.program_id(1) == 0)
    def _():
        tx_ref[...] = jnp.zeros_like(tx_ref)
        tb_ref[...] = jnp.zeros_like(tb_ref)
        tc_ref[...] = jnp.zeros_like(tc_ref)
        st_ref[...] = jnp.zeros_like(st_ref)

    tt = x_ref.shape[0]

    def conv_silu(src_ref, tail_ref, cw_ref, cb_ref, dst_ref):
        x = src_ref[...]
        cw = cw_ref[...]
        y = x * cw[CONV_K - 1:CONV_K, :] + cb_ref[...]
        for s in range(1, CONV_K):
            y += _shift_rows(x, tail_ref, s) * cw[CONV_K - 1 - s:CONV_K - s, :]
        tail_ref[...] = x[tt - SUBLANES:, :]
        dst_ref[...] = _silu(y)

    conv_silu(x_ref, tx_ref, cwx_ref, cbx_ref, xs_scr)
    conv_silu(bm_ref, tb_ref, cwb_ref, cbb_ref, bs_scr)
    conv_silu(cm_ref, tc_ref, cwc_ref, cbc_ref, cs_scr)

    tri = _tri(CHUNK)
    expand = expand_ref[...]
    dskip = dskip_ref[...]
    nw = nw_ref[...]
    hpg = SSD_HEADS // SSD_GROUPS

    dtc = _softplus(dt_ref[...] + dtb_ref[...])
    da = dtc * -jnp.exp(alog_ref[...])
    sums = _mm_01_left(sel_ref[...], da)
    acs_all = sums[:tt]
    tot = sums[tt:]
    acs_scr[...] = acs_all
    xdt_all = xs_scr[...] * _mm_01(dtc, expand)
    xdt_scr[...] = xdt_all
    xdec_scr[...] = xdt_all * _mm_01(jnp.exp(tot - acs_all), expand)
    ea_scr[...] = _mm_01(jnp.exp(acs_all), expand)

    def chunk(c, carry):
        rows = pl.ds(pl.multiple_of(c * CHUNK, CHUNK), CHUNK)
        acs = acs_scr[rows, :]
        acs_t = acs.T
        ea_x = ea_scr[rows, :]
        x = xs_scr[rows, :]
        xdt = xdt_scr[rows, :]
        xdec = xdec_scr[rows, :]
        cd_x = ea_x[CHUNK - 1:CHUNK, :]
        ys = []
        for g in range(SSD_GROUPS):
            sg = slice(g * SSD_GROUP_WIDTH, (g + 1) * SSD_GROUP_WIDTH)
            ss = slice(g * SSD_STATE, (g + 1) * SSD_STATE)
            bg = bs_scr[rows, ss]
            cg = cs_scr[rows, ss]
            cb = _mm(cg, bg, _NT)
            st = st_ref[g]
            yg = _mm(cg, st) * ea_x[:, sg]
            st_ref[g] = st * cd_x[:, sg] + _mm(bg, xdec[:, sg], _TN)
            yh = []
            for j in range(hpg):
                h = g * hpg + j
                seg = acs[:, h:h + 1] - acs_t[h:h + 1, :]
                lmat = jnp.exp(jnp.where(tri, seg, -jnp.inf))
                sh = slice(h * SSD_HEADDIM, (h + 1) * SSD_HEADDIM)
                yh.append(_mm(cb * lmat, xdt[:, sh]))
            ys.append(yg + jnp.concatenate(yh, axis=1))
        y = jnp.concatenate(ys, axis=1) + x * dskip
        y = y * _silu(z_ref[rows, :])
        outs = []
        for g in range(SSD_GROUPS):
            sg = slice(g * SSD_GROUP_WIDTH, (g + 1) * SSD_GROUP_WIDTH)
            yg = y[:, sg]
            ms = jnp.mean(yg * yg, axis=-1, keepdims=True)
            outs.append(yg * lax.rsqrt(ms + SSD_NORM_EPS))
        o_ref[rows, :] = (jnp.concatenate(outs, axis=1) * nw).astype(o_ref.dtype)
        return carry

    lax.fori_loop(0, tt // CHUNK, chunk, 0, unroll=2)


def _ssd(p, cw, cb, dtb, alog, dskip_x, nw, expand, tt):
    bsz, t, _ = p.shape

    def col(name):
        off, w = ODD_COLS[name]
        return pl.BlockSpec((None, tt, w), lambda b, i, j=off // w: (b, i, j))

    fix = lambda b, i: (0, 0)
    full = lambda a: pl.BlockSpec(a.shape, fix)
    gs = SSD_GROUPS * SSD_STATE
    cwx, cwb, cwc = cw[:, :SSD_INNER], cw[:, SSD_INNER:SSD_INNER + gs], cw[:, SSD_INNER + gs:]
    cbx, cbb, cbc = cb[:, :SSD_INNER], cb[:, SSD_INNER:SSD_INNER + gs], cb[:, SSD_INNER + gs:]
    sel = jnp.concatenate([_chunk_sel(tt, CHUNK), _chunk_sel(tt, CHUNK, "all")], axis=0)
    args = (cwx, cbx, cwb, cbb, cwc, cbc, dtb, alog, dskip_x, nw, expand, sel)
    return pl.pallas_call(
        _ssd_kernel,
        grid=(bsz, t // tt),
        in_specs=[col("z"), col("x"), col("bm"), col("cm"), col("dt")] + [full(a) for a in args],
        out_specs=pl.BlockSpec((None, tt, SSD_INNER), lambda b, i: (b, i, 0)),
        out_shape=jax.ShapeDtypeStruct((bsz, t, SSD_INNER), BF16),
        scratch_shapes=[pltpu.VMEM((SUBLANES, SSD_INNER), F32), pltpu.VMEM((SUBLANES, gs), F32),
                        pltpu.VMEM((SUBLANES, gs), F32),
                        pltpu.VMEM((SSD_GROUPS, SSD_STATE, SSD_GROUP_WIDTH), F32),
                        pltpu.VMEM((tt, SSD_INNER), F32), pltpu.VMEM((tt, gs), F32), pltpu.VMEM((tt, gs), F32),
                        pltpu.VMEM((tt, LANES), F32)] + [pltpu.VMEM((tt, SSD_INNER), F32)] * 3,
        compiler_params=_params("parallel", "arbitrary"),
        name="ssd",
    )(p, p, p, p, p, *args)


def _rwkv_kernel(r_ref, k_ref, v_ref, wa_ref, xg_ref, mur_ref, muk_ref, muv_ref, muwa_ref, mug_ref,
                 w0_ref, w2_ref, a0_ref, a2_ref, g2_ref, kk_ref, ka_ref, rk_ref, lng_ref, lnb_ref, ones_ref, sel_ref,
                 o_ref, tr_ref, tk_ref, tv_ref, twa_ref, tg_ref, st_ref, y_scr,
                 at_scr, rt_scr, bt_scr, kt_scr, bh_scr, kh_scr, v_scr, wc_scr):
    @pl.when(pl.program_id(1) == 0)
    def _():
        for ref in (tr_ref, tk_ref, tv_ref, twa_ref, tg_ref, st_ref):
            ref[...] = jnp.zeros_like(ref)

    tt = r_ref.shape[0]
    hd = RWKV_HEADDIM

    def mix(p_ref, tail_ref, mu_ref):
        p = p_ref[...]
        prev = _shift_rows(p, tail_ref, 1)
        tail_ref[...] = p[tt - SUBLANES:, :]
        return p + (prev - p) * mu_ref[...]

    r = mix(r_ref, tr_ref, mur_ref)
    k = mix(k_ref, tk_ref, muk_ref)
    v = mix(v_ref, tv_ref, muv_ref)
    xwa = mix(wa_ref, twa_ref, muwa_ref)
    xg = mix(xg_ref, tg_ref, mug_ref)

    lane = lax.broadcasted_iota(jnp.int32, xwa.shape, 1)
    lora_in = jnp.where(lane < RWKV_DECAY_LORA, jnp.tanh(xwa), xwa)
    w_log = -_softplus(-(w0_ref[...] + _mm(lora_in, w2_ref[...], exact=True))) - 0.5
    lw = -jnp.exp(w_log)
    a_sig = _sigmoid(a0_ref[...] + _mm(lora_in, a2_ref[...], exact=True))
    gate = _mm(_sigmoid(xg), g2_ref[...], exact=True)
    ones_bd = ones_ref[...]
    kk = k * kk_ref[...]
    kk = kk / jnp.maximum(jnp.sqrt(_mm_01(kk * kk, ones_bd)), 1e-12)
    k = k * (1.0 + (a_sig - 1.0) * ka_ref[...])

    cs = RWKV_CHUNK
    tri_incl = _tri(cs)
    tri_strict = _tri(cs, strict=True)
    tri_col = jnp.concatenate([tri_strict, tri_incl], axis=0)
    eye = (tri_incl & ~tri_strict).astype(F32)

    sums = _mm_01_left(sel_ref[...], lw)
    cum = sums[:tt]
    tot = sums[tt:]
    bvec = kk * a_sig
    e_neg = jnp.exp(-cum)
    e_last = jnp.exp(tot - cum)
    for ref, val in ((at_scr, -kk * jnp.exp(cum - lw)), (rt_scr, r * jnp.exp(cum)), (bt_scr, bvec * e_neg),
                     (kt_scr, k * e_neg), (bh_scr, bvec * e_last), (kh_scr, k * e_last), (v_scr, v),
                     (wc_scr, jnp.exp(tot))):
        ref[...] = val

    def chunk(c, carry):
        rows = pl.ds(pl.multiple_of(c * cs, cs), cs)
        at = at_scr[rows, :]
        rt = rt_scr[rows, :]
        bt = bt_scr[rows, :]
        kt = kt_scr[rows, :]
        bh = bh_scr[rows, :]
        kh = kh_scr[rows, :]
        vc = v_scr[rows, :]
        wc = wc_scr[pl.ds(pl.multiple_of(c * cs, cs), 1), :]
        heads = range(RWKV_HEADS)
        sls = [slice(h * hd, (h + 1) * hd) for h in heads]
        amat = [_mm(jnp.concatenate([at[:, sl], rt[:, sl]], axis=0),
                    jnp.concatenate([bt[:, sl], kt[:, sl]], axis=0), _NT) for sl in sls]
        a_ab = [jnp.where(tri_strict, m[:cs, :cs], 0.0) for m in amat]
        a_rb = [jnp.where(tri_incl, m[cs:, :cs], 0.0) for m in amat]
        a_xk = [jnp.where(tri_col, m[:, cs:], 0.0) for m in amat]
        avrv = [_mm(a_xk[h], vc[:, sls[h]]) for h in heads]
        inv = [eye + m for m in a_ab]
        pw = [_mm(m, m) for m in a_ab]
        for _ in range(int(math.log2(cs)) - 2):
            both = [_mm(jnp.concatenate([inv[h], pw[h]], axis=0), pw[h]) for h in heads]
            inv = [inv[h] + both[h][:cs] for h in heads]
            pw = [m[cs:] for m in both]
        inv = [inv[h] + _mm(inv[h], pw[h]) for h in heads]
        tu = [_mm(inv[h], jnp.concatenate([at[:, sls[h]], avrv[h][:cs]], axis=1)) for h in heads]
        qy = [_mm(a_rb[h], tu[h]) + jnp.concatenate([rt[:, sls[h]], avrv[h][cs:]], axis=1)
              for h in heads]
        zb =[_mm(tu[h], bh[:, sls[h]], _TN) for h in heads]
        vk = [_mm(vc[:, sls[h]], kh[:, sls[h]], _TN) for h in heads]
        for h in heads:
            st = st_ref[h]
            y_scr[rows, sls[h]] = _mm(qy[h][:, :hd], st, _NT) + qy[h][:, hd:]
            st_ref[h] = st * wc[:, sls[h]] + _mm(st, zb[h][:hd]) + zb[h][hd:] + vk[h]
        return carry

    lax.fori_loop(0, tt // cs, chunk, 0, unroll=2)

    y = y_scr[...]
    inv_n = 1.0 / hd
    mu_y = _mm_01(y, ones_bd) * inv_n
    dy = y - mu_y
    var_y = _mm_01(dy * dy, ones_bd) * inv_n
    yn = dy * lax.rsqrt(var_y + RWKV_GN_EPS) * lng_ref[...] + lnb_ref[...]
    bonus = _mm_01(r * k * rk_ref[...], ones_bd) * v
    o_ref[...] = ((yn + bonus) * gate).astype(o_ref.dtype)


def _rwkv(p, mu, w0, w2p, a0, a2p, g2, k_k, k_a, r_k, ln_g, ln_b, ones_bd, tt):
    bsz, t, _ = p.shape
    w = RWKV_WIDTH

    def col(name):
        off, wd = ODD_COLS[name]
        return pl.BlockSpec((None, tt, wd), lambda b, i, j=off // wd: (b, i, j))

    fix = lambda b, i: (0, 0)
    full = lambda a: pl.BlockSpec(a.shape, fix)
    mur, muk, muv = mu[:, :w], mu[:, w:2 * w], mu[:, 2 * w:3 * w]
    muwa, mug = mu[:, 3 * w:3 * w + LANES], mu[:, 3 * w + LANES:]
    sel = jnp.concatenate([_chunk_sel(tt, RWKV_CHUNK), _chunk_sel(tt, RWKV_CHUNK, "all")], axis=0)
    args = (mur, muk, muv, muwa, mug, w0, w2p, a0, a2p, g2, k_k, k_a, r_k, ln_g, ln_b, ones_bd, sel)
    return pl.pallas_call(
        _rwkv_kernel,
        grid=(bsz, t // tt),
        in_specs=[col("r"), col("k"), col("v"), col("wa"), col("xg")] + [full(a) for a in args],
        out_specs=pl.BlockSpec((None, tt, w), lambda b, i: (b, i, 0)),
        out_shape=jax.ShapeDtypeStruct((bsz, t, w), BF16),
        scratch_shapes=[pltpu.VMEM((SUBLANES, w), F32), pltpu.VMEM((SUBLANES, w), F32),
                        pltpu.VMEM((SUBLANES, w), F32), pltpu.VMEM((SUBLANES, LANES), F32),
                        pltpu.VMEM((SUBLANES, LANES), F32),
                        pltpu.VMEM((RWKV_HEADS, RWKV_HEADDIM, RWKV_HEADDIM), F32)]
        + [pltpu.VMEM((tt, w), F32)] * 9,
        compiler_params=_params("parallel", "arbitrary"),
        name="rwkv7",
    )(p, p, p, p, p, *args)


def _reorder_cols(w, pieces, order, width):
    parts, pos = [], 0
    for name, (off, slot) in order.items():
        assert off == pos, "slots must be listed in order and contiguous"
        start, size = pieces[name]
        parts.append(w[:, start:start + size])
        if size < slot:
            parts.append(jnp.zeros((w.shape[0], slot - size), w.dtype))
        pos += slot
    assert pos == width
    return jnp.concatenate(parts, axis=1)


def _block_diag(w):
    n, i, j = w.shape
    eye = jnp.eye(n, dtype=w.dtype)
    return (eye[:, None, :, None] * w[:, :, None, :]).reshape(n * i, n * j)


def _row(v, width=None):
    v = v.reshape(1, -1).astype(F32)
    if width is not None and v.shape[1] < width:
        v = jnp.pad(v, ((0, 0), (0, width - v.shape[1])))
    return v


def _even_layer(x, x_in, w_in, gk_w2, gk_b, gla_norm, conv_w, conv_b, wa, ba, wx, bx, lam,
                w_out, ln1_g, ln1_b, f_w1, f_w3, f_w2, ln2_g, ln2_b, bsz, t):
    qk, vw, lw = GLA_QK, GLA_VW, LRU_WIDTH
    pieces = dict(q=(0, qk), k=(qk, qk), v=(2 * qk, vw), g=(2 * qk + vw, vw), gk=(2 * qk + 2 * vw, GLA_GK_RANK),
                  xb=(2 * qk + 2 * vw + GLA_GK_RANK, lw), gate=(2 * qk + 2 * vw + GLA_GK_RANK + lw, lw))
    w_in_r = _reorder_cols(w_in, pieces, EVEN_COLS, EVEN_WIDTH).astype(BF16)
    p = _proj(x_in, w_in_r, tm=min(512, x.shape[0])).reshape(bsz, t, EVEN_WIDTH)
    tt = min(512, t)
    gkw = jnp.pad(gk_w2, ((0, LANES - GLA_GK_RANK), (0, 0)))
    y_a = _gla(p, gkw, _row(gk_b), _row(gla_norm), tt)
    y_b = _rglru(p, conv_w, _row(conv_b), _block_diag(wa).astype(BF16), _row(ba), _block_diag(wx).astype(BF16),
                 _row(bx), _row(lam), tt)
    n = bsz * t
    tm = min(512, n)
    x1, x1b = _outproj_ln(y_a.reshape(n, vw), y_b.reshape(n, lw), w_out[:vw].astype(BF16), w_out[vw:].astype(BF16),
                          x, _row(ln1_g), _row(ln1_b), tm)
    return _ffn_ln(x1b, f_w1.astype(BF16), f_w3.astype(BF16), f_w2.astype(BF16), x1,
                   _row(ln2_g), _row(ln2_b), tm=min(512, n))


def _odd_layer(x, xb, w_in, conv_w, conv_b, dt_bias, a_log, d_skip, ssd_norm, mu, w0, w2, a0, a2, g2, k_k, k_a,
               r_k, rln_g, rln_b, w_out, ln1_g, ln1_b, router, ew1, ew3, ew2, ln2_g, ln2_b, bsz, t):
    si, gs, rw = SSD_INNER, SSD_GROUPS * SSD_STATE, RWKV_WIDTH
    o = 2 * si + 2 * gs + SSD_HEADS
    pieces = dict(z=(0, si), x=(si, si), bm=(2 * si, gs), cm=(2 * si + gs, gs), dt=(2 * si + 2 * gs, SSD_HEADS),
                  r=(o, rw), k=(o + rw, rw), v=(o + 2 * rw, rw),
                  wa=(o + 3 * rw, RWKV_DECAY_LORA + RWKV_AAA_LORA), xg=(o + 3 * rw + LANES, RWKV_GATE_LORA))
    w_in_r = _reorder_cols(w_in, pieces, ODD_COLS, ODD_WIDTH).astype(BF16)
    n = bsz * t
    p = _proj(xb, w_in_r, tm=min(512, n)).reshape(bsz, t, ODD_WIDTH)
    expand = jnp.pad(jnp.repeat(jnp.eye(SSD_HEADS, dtype=BF16), SSD_HEADDIM, axis=1), ((0, LANES - SSD_HEADS), (0, 0)))
    y_c = _ssd(p, conv_w, _row(conv_b), _row(dt_bias, LANES), _row(a_log, LANES),
               _row(jnp.repeat(d_skip, SSD_HEADDIM)), _row(ssd_norm), expand, tt=min(256, t))
    zeros = jnp.zeros((RWKV_DECAY_LORA, rw), F32)
    ones_bd = _block_diag(jnp.ones((RWKV_HEADS, RWKV_HEADDIM, RWKV_HEADDIM), BF16))
    y_d = _rwkv(p, _row(mu), _row(w0), jnp.concatenate([w2, zeros]), _row(a0), jnp.concatenate([zeros, a2]), g2,
                _row(k_k), _row(k_a), _row(r_k), _row(rln_g), _row(rln_b), ones_bd, tt=min(256, t))
    tm = min(512, n)
    x1, x1b = _outproj_ln(y_c.reshape(n, si), y_d.reshape(n, rw), w_out[:si].astype(BF16), w_out[si:].astype(BF16),
                          x, _row(ln1_g), _row(ln1_b), tm)
    y = _moe_ln(x1, x1b, jnp.pad(router, ((0, 0), (0, LANES - N_EXPERTS))), ew1.astype(BF16), ew3.astype(BF16),
                ew2.astype(BF16), _row(ln2_g), _row(ln2_b))
    return y, None


def kernel(x, e_w_in, e_gk_w2, e_gk_b, e_gla_norm, e_conv_w, e_conv_b, e_lru_wa, e_lru_ba, e_lru_wx, e_lru_bx, e_lru_lambda, e_w_out, e_ln1_g, e_ln1_b, e_ffn_w1, e_ffn_w3, e_ffn_w2, e_ln2_g, e_ln2_b, o_w_in, o_conv_w, o_conv_b, o_dt_bias, o_a_log, o_d_skip, o_ssd_norm, o_rwkv_mu, o_rwkv_w0, o_rwkv_w2, o_rwkv_a0, o_rwkv_a2, o_rwkv_g2, o_rwkv_k_k, o_rwkv_k_a, o_rwkv_r_k, o_rwkv_ln_g, o_rwkv_ln_b, o_w_out, o_ln1_g, o_ln1_b, o_router, o_exp_w1, o_exp_w3, o_exp_w2, o_ln2_g, o_ln2_b):
    bsz, t, d = x.shape
    h = x.reshape(bsz * t, d)
    hb = h
    for i in range(DEPTH):
        j = i // 2
        if i % 2 == 0:
            h, hb = _even_layer(h, hb, e_w_in[j], e_gk_w2[j], e_gk_b[j], e_gla_norm[j], e_conv_w[j], e_conv_b[j],
                                e_lru_wa[j], e_lru_ba[j], e_lru_wx[j], e_lru_bx[j], e_lru_lambda[j], e_w_out[j],
                                e_ln1_g[j], e_ln1_b[j], e_ffn_w1[j], e_ffn_w3[j], e_ffn_w2[j], e_ln2_g[j],
                                e_ln2_b[j], bsz, t)
        else:
            h, hb = _odd_layer(h, hb, o_w_in[j], o_conv_w[j], o_conv_b[j], o_dt_bias[j], o_a_log[j], o_d_skip[j],
                               o_ssd_norm[j], o_rwkv_mu[j], o_rwkv_w0[j], o_rwkv_w2[j], o_rwkv_a0[j],
                               o_rwkv_a2[j], o_rwkv_g2[j], o_rwkv_k_k[j], o_rwkv_k_a[j], o_rwkv_r_k[j],
                               o_rwkv_ln_g[j], o_rwkv_ln_b[j], o_w_out[j], o_ln1_g[j], o_ln1_b[j], o_router[j],
                               o_exp_w1[j], o_exp_w3[j], o_exp_w2[j], o_ln2_g[j], o_ln2_b[j], bsz, t)
    return h.reshape(bsz, t, d)
```

```python
import functools
import math

import jax
import jax.numpy as jnp
from jax import lax
from jax.experimental import pallas as pl
from jax.experimental.pallas import tpu as pltpu

F32 = jnp.float32
BF16 = jnp.bfloat16

D_MODEL = 1024
DEPTH = 2
DEEPNORM_ALPHA = (2 * DEPTH) ** 0.25
LN_EPS = 1e-5
CONV_K = 4
CHUNK = 64

GLA_HEADS = 4
GLA_DK = 64
GLA_DV = 128
GLA_QK = GLA_HEADS * GLA_DK
GLA_VW = GLA_HEADS * GLA_DV
GLA_GK_RANK = 16
GLA_GATE_NORM = 16.0
GLA_NORM_EPS = 1e-5

LRU_WIDTH = 512
LRU_BLOCKS = 8
LRU_C = 8.0

SSD_HEADS = 16
SSD_HEADDIM = 64
SSD_INNER = SSD_HEADS * SSD_HEADDIM
SSD_GROUPS = 2
SSD_STATE = 128
SSD_GROUP_WIDTH = SSD_INNER // SSD_GROUPS
SSD_NORM_EPS = 1e-5

RWKV_HEADS = 8
RWKV_HEADDIM = 64
RWKV_WIDTH = RWKV_HEADS * RWKV_HEADDIM
RWKV_DECAY_LORA = 64
RWKV_AAA_LORA = 64
RWKV_GATE_LORA = 128
RWKV_GN_EPS = 64e-5
RWKV_CHUNK = 128

N_EXPERTS = 8
LANES = 128
SUBLANES = 8
assert N_EXPERTS == SUBLANES
MOE_TB = 512
MOE_TR = 256
MOE_TE = 512
assert MOE_TE % MOE_TR == 0
MOE_COMBINE_BUFS = 4
VMEM_LIMIT = 56 * 1024 * 1024

EVEN_COLS = dict(v=(0, 512), g=(512, 512), xb=(1024, 512), gate=(1536, 512), q=(2048, 256), k=(2304, 256),
                 gk=(2560, 128))
EVEN_WIDTH = 2688
ODD_COLS = dict(z=(0, 1024), x=(1024, 1024), r=(2048, 512), k=(2560, 512), v=(3072, 512), bm=(3584, 256),
                cm=(3840, 256), wa=(4096, 128), xg=(4224, 128), dt=(4352, 128))
ODD_WIDTH = 4480


def _mm(a, b, dims=((1,), (0,))):
    return lax.dot_general(a.astype(BF16), b.astype(BF16), (dims, ((), ())), preferred_element_type=F32)


def _mm_split(a, b):
    a_hi = a.astype(BF16)
    b_hi = b.astype(BF16)
    a_lo = (a - a_hi.astype(F32)).astype(BF16)
    b_lo = (b - b_hi.astype(F32)).astype(BF16)
    return (jnp.dot(a_hi, b_hi, preferred_element_type=F32) + jnp.dot(a_lo, b_hi, preferred_element_type=F32)
            + jnp.dot(a_hi, b_lo, preferred_element_type=F32))


_NT = ((1,), (1,))
_TN = ((0,), (0,))


def _mm_01(x, sel):
    hi = x.astype(BF16)
    lo = (x - hi.astype(F32)).astype(BF16)
    return (jnp.dot(hi, sel, preferred_element_type=F32) + jnp.dot(lo, sel, preferred_element_type=F32))


def _sigmoid(x):
    return 1.0 / (1.0 + jnp.exp(-x))


def _softplus(x):
    return jnp.maximum(x, 0.0) + jnp.log(1.0 + jnp.exp(-jnp.abs(x)))


def _silu(x):
    return x * _sigmoid(x)


def _layer_norm(h, g, b):
    mu = jnp.mean(h, axis=-1, keepdims=True)
    d = h - mu
    var = jnp.mean(d * d, axis=-1, keepdims=True)
    return d * lax.rsqrt(var + LN_EPS) * g + b


def _params(*sem):
    return pltpu.CompilerParams(dimension_semantics=sem, vmem_limit_bytes=VMEM_LIMIT)


def _tri(n, strict=False):
    r = lax.broadcasted_iota(jnp.int32, (n, n), 0)
    c = lax.broadcasted_iota(jnp.int32, (n, n), 1)
    return (r > c) if strict else (r >= c)


def _chunk_sel(n, chunk, kind="incl"):
    r = jnp.arange(n)[:, None]
    c = jnp.arange(n)[None, :]
    first = r - r % chunk
    upper = r if kind == "incl" else first + (chunk - 1)
    return ((c >= first) & (c <= upper)).astype(BF16)


def _mm_01_left(sel, x):
    hi = x.astype(BF16)
    lo = (x - hi.astype(F32)).astype(BF16)
    return jnp.dot(sel, hi, preferred_element_type=F32) + jnp.dot(sel, lo, preferred_element_type=F32)


def _shift_rows(p, tail_ref, s):
    n = p.shape[0]
    rolled = pltpu.roll(p, s, axis=0)
    head = pltpu.roll(tail_ref[...], s, axis=0)
    row = lax.broadcasted_iota(jnp.int32, (SUBLANES, p.shape[1]), 0)
    fixed = jnp.where(row < s, head, rolled[:SUBLANES])
    return jnp.concatenate([fixed, rolled[SUBLANES:]], axis=0) if n > SUBLANES else fixed


def _proj_kernel(x_ref, w_ref, o_ref):
    o_ref[...] = jnp.dot(x_ref[...].astype(BF16), w_ref[...], preferred_element_type=F32)


def _resident(shape):
    return pl.BlockSpec(shape, lambda *_: (0,) * len(shape), pipeline_mode=pl.Buffered(1))


def _proj(x, w, tm):
    m, k = x.shape
    n = w.shape[1]
    return pl.pallas_call(
        _proj_kernel,
        grid=(m // tm,),
        in_specs=[pl.BlockSpec((tm, k), lambda i: (i, 0)), _resident((k, n))],
        out_specs=pl.BlockSpec((tm, n), lambda i: (i, 0)),
        out_shape=jax.ShapeDtypeStruct((m, n), F32),
        compiler_params=_params("parallel"),
        name="in_proj",
    )(x, w)


def _outproj_ln_kernel(ya_ref, yb_ref, wa_ref, wb_ref, res_ref, g_ref, b_ref, o_ref, ob_ref):
    acc = jnp.dot(ya_ref[...].astype(BF16), wa_ref[...], preferred_element_type=F32)
    acc += jnp.dot(yb_ref[...].astype(BF16), wb_ref[...], preferred_element_type=F32)
    y = _layer_norm(DEEPNORM_ALPHA * res_ref[...] + acc, g_ref[...], b_ref[...])
    o_ref[...] = y
    ob_ref[...] = y.astype(BF16)


def _outproj_ln(ya, yb, wa, wb, res, g, b, tm):
    m = ya.shape[0]
    d = res.shape[1]
    row = lambda i: (i, 0)
    fix = lambda i: (0, 0)
    return pl.pallas_call(
        _outproj_ln_kernel,
        grid=(m // tm,),
        in_specs=[pl.BlockSpec((tm, ya.shape[1]), row), pl.BlockSpec((tm, yb.shape[1]), row),
                  pl.BlockSpec(wa.shape, fix), pl.BlockSpec(wb.shape, fix), pl.BlockSpec((tm, d), row),
                  pl.BlockSpec((1, d), fix), pl.BlockSpec((1, d), fix)],
        out_specs=[pl.BlockSpec((tm, d), row), pl.BlockSpec((tm, d), row)],
        out_shape=[jax.ShapeDtypeStruct((m, d), F32), jax.ShapeDtypeStruct((m, d), BF16)],
        compiler_params=_params("parallel"),
        name="out_proj_ln",
    )(ya, yb, wa, wb, res, g, b)


def _ffn_kernel(xb_ref, w1_ref, w3_ref, w2_ref, res_ref, g_ref, b_ref, o_ref, ob_ref):
    x = xb_ref[...]
    h1 = jnp.dot(x, w1_ref[...], preferred_element_type=F32)
    h3 = jnp.dot(x, w3_ref[...], preferred_element_type=F32)
    ff = jnp.dot((_silu(h1) * h3).astype(BF16), w2_ref[...], preferred_element_type=F32)
    y = _layer_norm(DEEPNORM_ALPHA * res_ref[...] + ff, g_ref[...], b_ref[...])
    o_ref[...] = y
    ob_ref[...] = y.astype(BF16)


def _ffn_ln(xb, w1, w3, w2, res, g, b, tm):
    m, d = xb.shape
    row = lambda i: (i, 0)
    return pl.pallas_call(
        _ffn_kernel,
        grid=(m // tm,),
        in_specs=[pl.BlockSpec((tm, d), row), _resident(w1.shape), _resident(w3.shape), _resident(w2.shape),
                  pl.BlockSpec((tm, d), row), _resident((1, d)), _resident((1, d))],
        out_specs=[pl.BlockSpec((tm, d), row), pl.BlockSpec((tm, d), row)],
        out_shape=[jax.ShapeDtypeStruct((m, d), F32), jax.ShapeDtypeStruct((m, d), BF16)],
        compiler_params=_params("parallel"),
        name="ffn_ln",
    )(xb, w1, w3, w2, res, g, b)


def _router_kernel(x_ref, w_ref, gates_ref, rank_ref, rank_t_ref, cnt_ref, carry_ref):
    @pl.when(pl.program_id(0) == 0)
    def _():
        carry_ref[...] = jnp.zeros_like(carry_ref)

    logits = _mm_split(x_ref[...], w_ref[...])
    lane = lax.broadcasted_iota(jnp.int32, logits.shape, 1)
    neg = jnp.float32(-jnp.inf)
    l1 = jnp.where(lane < N_EXPERTS, logits, neg)
    m1 = jnp.max(l1, axis=1, keepdims=True)
    i1 = jnp.min(jnp.where(l1 == m1, lane, LANES), axis=1, keepdims=True)
    l2 = jnp.where(lane == i1, neg, l1)
    m2 = jnp.max(l2, axis=1, keepdims=True)
    i2 = jnp.min(jnp.where(l2 == m2, lane, LANES), axis=1, keepdims=True)
    ex = jnp.exp(m2 - m1)
    w_top = 1.0 / (1.0 + ex)
    gates_ref[...] = jnp.where(lane == i1, w_top, 0.0) + jnp.where(lane == i2, ex * w_top, 0.0)

    sel = jnp.where(lane == i1, 1.0, 0.0) + jnp.where(lane == i2, 1.0, 0.0)
    tb = sel.shape[0]
    before = _mm(_tri(tb, strict=True).astype(F32), sel)
    carry = carry_ref[...]
    rank = jnp.where(sel > 0.0, carry + before, -1.0)
    rank_ref[...] = rank
    rank_t_ref[...] = rank.T[:SUBLANES, :]
    carry = carry + jnp.sum(sel, axis=0, keepdims=True)
    carry_ref[...] = carry
    cnt_ref[...] = jnp.broadcast_to(carry, cnt_ref.shape)


def _router(x, w):
    m, d = x.shape
    nblk = m // MOE_TB
    return pl.pallas_call(
        _router_kernel,
        grid=(nblk,),
        in_specs=[pl.BlockSpec((MOE_TB, d), lambda i: (i, 0)), pl.BlockSpec((d, LANES), lambda i: (0, 0))],
        out_specs=[pl.BlockSpec((MOE_TB, LANES), lambda i: (i, 0)), pl.BlockSpec((MOE_TB, LANES), lambda i: (i, 0)),
                   pl.BlockSpec((SUBLANES, MOE_TB), lambda i: (0, i)), pl.BlockSpec((SUBLANES, LANES), lambda i: (i, 0))],
        out_shape=[jax.ShapeDtypeStruct((m, LANES), F32), jax.ShapeDtypeStruct((m, LANES), F32),
                   jax.ShapeDtypeStruct((SUBLANES, m), F32), jax.ShapeDtypeStruct((nblk * SUBLANES, LANES), F32)],
        scratch_shapes=[pltpu.VMEM((1, LANES), F32)],
        compiler_params=_params("arbitrary"),
        name="router",
    )(x, w)


def _moe_plan(cnt, n_tok):
    i32 = jnp.int32
    nblk = n_tok // MOE_TB
    c_inc = cnt.reshape(nblk, SUBLANES, LANES)[:, 0, :N_EXPERTS].astype(i32)
    c_exc = jnp.concatenate([jnp.zeros((1, N_EXPERTS), i32), c_inc[:-1]], axis=0)
    gsz = (c_inc[-1] + MOE_TE - 1) // MOE_TE * MOE_TE
    gend = jnp.cumsum(gsz)
    start = gend - gsz
    n_et = _moe_rows(n_tok) // MOE_TE
    n_act = gend[-1] // MOE_TE
    et = jnp.minimum(jnp.arange(n_et, dtype=i32), jnp.maximum(n_act - 1, 0))
    et_expert = jnp.sum((gend[None, :] <= (et * MOE_TE)[:, None]).astype(i32), axis=1)
    et_active = (jnp.arange(n_et, dtype=i32) < n_act).astype(i32)

    lo = start[None, :] + c_exc
    hi = start[None, :] + c_inc
    first_tile = lo // MOE_TR
    n_items = jnp.where(hi > lo, (hi - 1) // MOE_TR - first_tile + 1, 0)
    wmax = _moe_rows(n_tok) // MOE_TR + N_EXPERTS * nblk
    exp_id = jnp.broadcast_to(jnp.arange(N_EXPERTS, dtype=i32)[None, :], (nblk, N_EXPERTS))

    n, ft, exp = (a.reshape(-1) for a in (n_items, first_tile, exp_id))
    inc = jnp.cumsum(n)
    wc = jnp.minimum(jnp.arange(wmax, dtype=i32), jnp.maximum(inc[-1] - 1, 0))
    idx = jnp.sum((inc[None, :] <= wc[:, None]).astype(i32), axis=1)
    per_block = jnp.sum(n_items, axis=1)
    by_block = (ft[idx] + wc - (inc - n)[idx], exp[idx], jnp.cumsum(per_block) - per_block, per_block)

    row0 = jnp.arange(_moe_rows(n_tok) // MOE_TR, dtype=i32) * MOE_TR
    group = jnp.minimum(jnp.sum((gend[None, :] <= row0[:, None]).astype(i32), axis=1), N_EXPERTS - 1)
    rank_lo = row0 - start[group]
    count = c_inc[-1][group]
    rank_hi = jnp.minimum(rank_lo + MOE_TR, count)
    holds = (rank_lo < count) & (row0 < gend[-1])
    blk_lo = jnp.sum((c_inc[:, group].T <= rank_lo[:, None]).astype(i32), axis=1)
    blk_hi = jnp.sum((c_exc[:, group].T < rank_hi[:, None]).astype(i32), axis=1)
    by_tile = (group, jnp.minimum(blk_lo, nblk - 1), jnp.where(holds, blk_hi - blk_lo, 0))

    return start, (et, et_expert, et_active), by_tile, by_block


def _moe_rows(n_tok):
    return 2 * n_tok + N_EXPERTS * MOE_TE


def _one_hot_rows(dest, tile):
    row = lax.broadcasted_iota(jnp.int32, (MOE_TR, MOE_TB), 0) + tile * MOE_TR
    return jnp.where(dest == row.astype(F32), 1.0, 0.0).astype(BF16)


def _dispatch_kernel(exp_ref, blk0_ref, nblk_ref, x_ref, dest_ref, xs_ref, acc_ref):
    j = pl.program_id(0)
    acc_ref[...] = jnp.zeros_like(acc_ref)

    def block(i, carry):
        b = blk0_ref[j] + i
        onehot = _one_hot_rows(dest_ref[b, pl.ds(exp_ref[j], 1), :], j)
        xblk = x_ref[pl.ds(pl.multiple_of(b * MOE_TB, MOE_TB), MOE_TB), :]
        acc_ref[...] += jnp.dot(onehot, xblk, preferred_element_type=F32)
        return carry

    lax.fori_loop(0, nblk_ref[j], block, 0)
    xs_ref[...] = acc_ref[...].astype(BF16)


def _dispatch(plan, xb, dest_t):
    exp, blk0, nblk = plan
    n, d = xb.shape
    rows = _moe_rows(n)
    dest3 = dest_t.reshape(N_EXPERTS, n // MOE_TB, MOE_TB).transpose(1, 0, 2)
    grid_spec = pltpu.PrefetchScalarGridSpec(
        num_scalar_prefetch=3,
        grid=(rows // MOE_TR,),
        in_specs=[_resident(xb.shape), _resident(dest3.shape)],
        out_specs=pl.BlockSpec((MOE_TR, d), lambda j, e, b, c: (j, 0)),
        scratch_shapes=[pltpu.VMEM((MOE_TR, d), F32)],
    )
    return pl.pallas_call(
        _dispatch_kernel,
        grid_spec=grid_spec,
        out_shape=jax.ShapeDtypeStruct((rows, d), BF16),
        compiler_params=_params("arbitrary"),
        name="moe_dispatch",
    )(exp, blk0, nblk, xb, dest3)


def _expert_kernel(et_ref, ee_ref, ea_ref, xs_ref, w1_ref, w3_ref, w2_ref, ys_ref, acc_ref):
    j = pl.program_id(0)
    f = pl.program_id(1)

    @pl.when((ea_ref[j] == 0) & (f == 0))
    def _():
        ys_ref[...] = jnp.zeros_like(ys_ref)

    @pl.when(ea_ref[j] == 1)
    def _():
        @pl.when(f == 0)
        def _():
            acc_ref[...] = jnp.zeros_like(acc_ref)

        x = xs_ref[...]
        h1 = jnp.dot(x, w1_ref[...], preferred_element_type=F32)
        h3 = jnp.dot(x, w3_ref[...], preferred_element_type=F32)
        acc_ref[...] += jnp.dot((_silu(h1) * h3).astype(BF16), w2_ref[...], preferred_element_type=F32)

        @pl.when(f == pl.num_programs(1) - 1)
        def _():
            ys_ref[...] = acc_ref[...].astype(BF16)


def _experts(plan, xs, w1, w3, w2, tf):
    et, ee, ea = plan
    rows, d = xs.shape
    nf = w1.shape[2] // tf
    fidx = lambda f, a, j: f * a[j] + (nf - 1) * (1 - a[j])
    grid_spec = pltpu.PrefetchScalarGridSpec(
        num_scalar_prefetch=3,
        grid=(et.shape[0], nf),
        in_specs=[pl.BlockSpec((MOE_TE, d), lambda j, f, t, e, a: (t[j], 0)),
                  pl.BlockSpec((None, d, tf), lambda j, f, t, e, a: (e[j], 0, fidx(f, a, j))),
                  pl.BlockSpec((None, d, tf), lambda j, f, t, e, a: (e[j], 0, fidx(f, a, j))),
                  pl.BlockSpec((None, tf, d), lambda j, f, t, e, a: (e[j], fidx(f, a, j), 0))],
        out_specs=pl.BlockSpec((MOE_TE, d), lambda j, f, t, e, a: (j, 0)),
        scratch_shapes=[pltpu.VMEM((MOE_TE, d), F32)],
    )
    return pl.pallas_call(
        _expert_kernel,
        grid_spec=grid_spec,
        out_shape=jax.ShapeDtypeStruct((rows, d), BF16),
        compiler_params=_params("arbitrary", "arbitrary"),
        name="moe_experts",
    )(et, ee, ea, xs, w1, w3, w2)


def _combine_kernel(tile_ref, exp_ref, off_ref, cnt_ref, ys_hbm, dest_ref, gates_ref, res_ref, g_ref, b_ref, o_ref,
                    buf_ref, sem_ref, acc_ref):
    j = pl.program_id(0)
    base = off_ref[j]
    n = cnt_ref[j]

    def fetch(i, slot):
        row0 = pl.multiple_of(tile_ref[base + i] * MOE_TR, MOE_TR)
        return pltpu.make_async_copy(ys_hbm.at[pl.ds(row0, MOE_TR), :], buf_ref.at[slot], sem_ref.at[slot])

    for i0 in range(MOE_COMBINE_BUFS - 1):
        @pl.when(i0 < n)
        def _(i0=i0):
            fetch(i0, i0).start()

    acc_ref[...] = jnp.zeros_like(acc_ref)
    dest = dest_ref[...]
    gates = gates_ref[...]
    lane = lax.broadcasted_iota(jnp.int32, dest.shape, 1)

    def item(i, carry):
        slot = lax.rem(i, MOE_COMBINE_BUFS)
        ahead = i + (MOE_COMBINE_BUFS - 1)

        @pl.when(ahead < n)
        def _():
            fetch(ahead, lax.rem(ahead, MOE_COMBINE_BUFS)).start()

        mine = lane == exp_ref[base + i]
        dcol = jnp.sum(jnp.where(mine, dest, 0.0), axis=1, keepdims=True)
        gcol = jnp.sum(jnp.where(mine, gates, 0.0), axis=1, keepdims=True)
        col = lax.broadcasted_iota(jnp.int32, (MOE_TB, MOE_TR), 1) + tile_ref[base + i] * MOE_TR
        onehot = jnp.where(dcol == col.astype(F32), 1.0, 0.0).astype(BF16)
        fetch(i, slot).wait()
        acc_ref[...] += jnp.dot(onehot, buf_ref[slot], preferred_element_type=F32) * gcol
        return carry

    lax.fori_loop(0, n, item, 0)
    o_ref[...] = _layer_norm(DEEPNORM_ALPHA * res_ref[...] + acc_ref[...], g_ref[...], b_ref[...])


def _combine_ln(plan, ys, dest, gates, res, g, b):
    tile, exp, off, cnt = plan
    n, d = res.shape
    tok = lambda j, *_: (j, 0)
    grid_spec = pltpu.PrefetchScalarGridSpec(
        num_scalar_prefetch=4,
        grid=(n // MOE_TB,),
        in_specs=[pl.BlockSpec(memory_space=pl.ANY),
                  pl.BlockSpec((MOE_TB, LANES), tok), pl.BlockSpec((MOE_TB, LANES), tok), pl.BlockSpec((MOE_TB, d), tok),
                  _resident((1, d)), _resident((1, d))],
        out_specs=pl.BlockSpec((MOE_TB, d), tok),
        scratch_shapes=[pltpu.VMEM((MOE_COMBINE_BUFS, MOE_TR, d), BF16), pltpu.SemaphoreType.DMA((MOE_COMBINE_BUFS,)),
                        pltpu.VMEM((MOE_TB, d), F32)],
    )
    return pl.pallas_call(
        _combine_kernel,
        grid_spec=grid_spec,
        out_shape=jax.ShapeDtypeStruct((n, d), F32),
        compiler_params=_params("arbitrary"),
        name="moe_combine_ln",
    )(tile, exp, off, cnt, ys, dest, gates, res, g, b)


def _moe_ln(x, xb, router_w, w1, w3, w2, g, b):
    n = x.shape[0]
    gates, rank, rank_t, cnt = _router(x, router_w)
    start, expert_tiles, by_tile, by_block = _moe_plan(cnt, n)
    startf = start.astype(F32)
    dest = jnp.where(rank >= 0.0, rank + jnp.pad(startf, (0, LANES - N_EXPERTS))[None, :], -1.0)
    dest_t = jnp.where(rank_t >= 0.0, rank_t + startf[:, None], -1.0)
    xs = _dispatch(by_tile, xb, dest_t)
    ys = _experts(expert_tiles, xs, w1, w3, w2, tf=1792)
    return _combine_ln(by_block, ys, dest, gates, x, g, b)


def _gla_kernel(q_ref, k_ref, v_ref, g_ref, gk_ref, gkw_ref, gkb_ref, nw_ref, sel_ref, o_ref, st_ref, bc_scr,
                qd_scr, kd_scr):
    @pl.when(pl.program_id(1) == 0)
    def _():
        st_ref[...] = jnp.zeros_like(st_ref)

    tt = q_ref.shape[0]
    tri = _tri(CHUNK)
    nw = nw_ref[...]

    pre = _mm_split(gk_ref[...], gkw_ref[...]) + gkb_ref[...]
    gk = (jnp.minimum(pre, 0.0) - jnp.log(1.0 + jnp.exp(-jnp.abs(pre)))) * (1.0 / GLA_GATE_NORM)
    bc_all = _mm_01_left(sel_ref[...], gk)
    bc_scr[...] = bc_all
    qd_scr[...] = q_ref[...] * (GLA_DK ** -0.5) * jnp.exp(bc_all)
    kd_scr[...] = k_ref[...] * jnp.exp(-bc_all)

    def chunk(c, carry):
        rows = pl.ds(pl.multiple_of(c * CHUNK, CHUNK), CHUNK)
        bc = bc_scr[rows, :]
        bl = bc[CHUNK - 1:CHUNK, :]
        qd = qd_scr[rows, :]
        kd = kd_scr[rows, :]
        kl = k_ref[rows, :] * jnp.exp(bl - bc)
        dec = jnp.exp(bl)
        heads = range(GLA_HEADS)
        sks = [slice(h * GLA_DK, (h + 1) * GLA_DK) for h in heads]
        svs = [slice(h * GLA_DV, (h + 1) * GLA_DV) for h in heads]
        att = [jnp.where(tri, _mm(qd[:, sk], kd[:, sk], _NT), 0.0) for sk in sks]
        vs = [v_ref[rows, sv] for sv in svs]
        sts = [st_ref[h] for h in heads]
        os_ = [_mm(att[h], vs[h]) + _mm(qd[:, sks[h]], sts[h], _NT) for h in heads]
        for h in heads:
            st_ref[h] = sts[h] * dec[:, sks[h]] + _mm(vs[h], kl[:, sks[h]], _TN)
        for h in heads:
            o = os_[h]
            ms = jnp.mean(o * o, axis=-1, keepdims=True)
            y = o * lax.rsqrt(ms + GLA_NORM_EPS) * nw * _silu(g_ref[rows, svs[h]])
            o_ref[rows, svs[h]] = y.astype(o_ref.dtype)
        return carry

    lax.fori_loop(0, tt // CHUNK, chunk, 0, unroll=2)


def _gla(p, gkw, gkb, nw, tt):
    bsz, t, _ = p.shape

    def col(name):
        off, w = EVEN_COLS[name]
        return pl.BlockSpec((None, tt, w), lambda b, i, j=off // w: (b, i, j))

    fix = lambda b, i: (0, 0)
    return pl.pallas_call(
        _gla_kernel,
        grid=(bsz, t // tt),
        in_specs=[col("q"), col("k"), col("v"), col("g"), col("gk"),
                  pl.BlockSpec(gkw.shape, fix), pl.BlockSpec(gkb.shape, fix), pl.BlockSpec(nw.shape, fix),
                  pl.BlockSpec((tt, tt), fix)],
        out_specs=pl.BlockSpec((None, tt, GLA_VW), lambda b, i: (b, i, 0)),
        out_shape=jax.ShapeDtypeStruct((bsz, t, GLA_VW), BF16),
        scratch_shapes=[pltpu.VMEM((GLA_HEADS, GLA_DV, GLA_DK), F32)] + [pltpu.VMEM((tt, GLA_QK), F32)] * 3,
        compiler_params=_params("parallel", "arbitrary"),
        name="gla",
    )(p, p, p, p, p, gkw, gkb, nw, _chunk_sel(tt, CHUNK))


def _rglru_kernel(x_ref, gate_ref, cw_ref, cb_ref, wa_ref, ba_ref, wx_ref, bx_ref, lam_ref, o_ref,
                  tail_ref, h_ref, a_scr, u_scr):
    @pl.when(pl.program_id(1) == 0)
    def _():
        tail_ref[...] = jnp.zeros_like(tail_ref)
        h_ref[...] = jnp.zeros_like(h_ref)

    tt = x_ref.shape[0]
    x = x_ref[...]
    cw = cw_ref[...]
    xc = x * cw[CONV_K - 1:CONV_K, :] + cb_ref[...]
    for s in range(1, CONV_K):
        xc += _shift_rows(x, tail_ref, s) * cw[CONV_K - 1 - s:CONV_K - s, :]
    tail_ref[...] = x[tt - SUBLANES:, :]

    r = _sigmoid(_mm(xc, wa_ref[...]) + ba_ref[...])
    i = _sigmoid(_mm(xc, wx_ref[...]) + bx_ref[...])
    log_a = -LRU_C * r * _softplus(-lam_ref[...])
    a_scr[...] = jnp.exp(log_a)
    u_scr[...] = jnp.sqrt(1.0 - jnp.exp(2.0 * log_a)) * (i * xc)

    row = lax.broadcasted_iota(jnp.int32, (SUBLANES, x.shape[1]), 0)

    def step(i, h):
        rows = pl.ds(pl.multiple_of(i * SUBLANES, SUBLANES), SUBLANES)
        a = a_scr[rows, :]
        u = u_scr[rows, :]
        for s in (1, 2, 4):
            u = u + a * jnp.where(row >= s, pltpu.roll(u, s, axis=0), 0.0)
            a = a * jnp.where(row >= s, pltpu.roll(a, s, axis=0), 1.0)
        hb = u + a * h
        u_scr[rows, :] = hb
        return hb[SUBLANES - 1:, :]

    h_ref[...] = lax.fori_loop(0, tt // SUBLANES, step, h_ref[...], unroll=4)
    gate = gate_ref[...]
    gelu = 0.5 * gate * (1.0 + jnp.tanh(math.sqrt(2.0 / math.pi) * (gate + 0.044715 * gate * gate * gate)))
    o_ref[...] = (u_scr[...] * gelu).astype(o_ref.dtype)


def _rglru(p, cw, cb, wa, ba, wx, bx, lam, tt):
    bsz, t, _ = p.shape
    w = LRU_WIDTH

    def col(name):
        off, _ = EVEN_COLS[name]
        return pl.BlockSpec((None, tt, w), lambda b, i, j=off // w: (b, i, j))

    fix = lambda b, i: (0, 0)
    vec = pl.BlockSpec((1, w), fix)
    return pl.pallas_call(
        _rglru_kernel,
        grid=(bsz, t // tt),
        in_specs=[col("xb"), col("gate"), pl.BlockSpec((CONV_K, w), fix), vec,
                  pl.BlockSpec((w, w), fix), vec, pl.BlockSpec((w, w), fix), vec, vec],
        out_specs=pl.BlockSpec((None, tt, w), lambda b, i: (b, i, 0)),
        out_shape=jax.ShapeDtypeStruct((bsz, t, w), BF16),
        scratch_shapes=[pltpu.VMEM((SUBLANES, w), F32), pltpu.VMEM((1, w), F32),
                        pltpu.VMEM((tt, w), F32), pltpu.VMEM((tt, w), F32)],
        compiler_params=_params("parallel", "arbitrary"),
        name="rglru",
    )(p, p, cw, cb, wa, ba, wx, bx, lam)


def _ssd_kernel(z_ref, x_ref, bm_ref, cm_ref, dt_ref, cwx_ref, cbx_ref, cwb_ref, cbb_ref, cwc_ref, cbc_ref,
                dtb_ref, alog_ref, dskip_ref, nw_ref, expand_ref, sel_ref, o_ref,
                tx_ref, tb_ref, tc_ref, st_ref, xs_scr, bs_scr, cs_scr, acs_scr, xdt_scr, xdec_scr, ea_scr):
    @pl.when(pl.program_id(1) == 0)
    def _():
        tx_ref[...] = jnp.zeros_like(tx_ref)
        tb_ref[...] = jnp.zeros_like(tb_ref)
        tc_ref[...] = jnp.zeros_like(tc_ref)
        st_ref[...] = jnp.zeros_like(st_ref)

    tt = x_ref.shape[0]

    def conv_silu(src_ref, tail_ref, cw_ref, cb_ref, dst_ref):
        x = src_ref[...]
        cw = cw_ref[...]
        y = x * cw[CONV_K - 1:CONV_K, :] + cb_ref[...]
        for s in range(1, CONV_K):
            y += _shift_rows(x, tail_ref, s) * cw[CONV_K - 1 - s:CONV_K - s, :]
        tail_ref[...] = x[tt - SUBLANES:, :]
        dst_ref[...] = _silu(y)

    conv_silu(x_ref, tx_ref, cwx_ref, cbx_ref, xs_scr)
    conv_silu(bm_ref, tb_ref, cwb_ref, cbb_ref, bs_scr)
    conv_silu(cm_ref, tc_ref, cwc_ref, cbc_ref, cs_scr)

    tri = _tri(CHUNK)
    expand = expand_ref[...]
    dskip = dskip_ref[...]
    nw = nw_ref[...]
    hpg = SSD_HEADS // SSD_GROUPS

    dtc = _softplus(dt_ref[...] + dtb_ref[...])
    da = dtc * -jnp.exp(alog_ref[...])
    sums = _mm_01_left(sel_ref[...], da)
    acs_all = sums[:tt]
    tot = sums[tt:]
    acs_scr[...] = acs_all
    xdt_all = xs_scr[...] * _mm_01(dtc, expand)
    xdt_scr[...] = xdt_all
    xdec_scr[...] = xdt_all * _mm_01(jnp.exp(tot - acs_all), expand)
    ea_scr[...] = _mm_01(jnp.exp(acs_all), expand)

    def chunk(c, carry):
        rows = pl.ds(pl.multiple_of(c * CHUNK, CHUNK), CHUNK)
        acs = acs_scr[rows, :]
        acs_t = acs.T
        ea_x = ea_scr[rows, :]
        x = xs_scr[rows, :]
        xdt = xdt_scr[rows, :]
        xdec = xdec_scr[rows, :]
        cd_x = ea_x[CHUNK - 1:CHUNK, :]
        ys = []
        for g in range(SSD_GROUPS):
            sg = slice(g * SSD_GROUP_WIDTH, (g + 1) * SSD_GROUP_WIDTH)
            ss = slice(g * SSD_STATE, (g + 1) * SSD_STATE)
            bg = bs_scr[rows, ss]
            cg = cs_scr[rows, ss]
            cb = _mm(cg, bg, _NT)
            st = st_ref[g]
            yg = _mm(cg, st) * ea_x[:, sg]
            st_ref[g] = st * cd_x[:, sg] + _mm(bg, xdec[:, sg], _TN)
            yh = []
            for j in range(hpg):
                h = g * hpg + j
                seg = acs[:, h:h + 1] - acs_t[h:h + 1, :]
                lmat = jnp.exp(jnp.where(tri, seg, -jnp.inf))
                sh = slice(h * SSD_HEADDIM, (h + 1) * SSD_HEADDIM)
                yh.append(_mm(cb * lmat, xdt[:, sh]))
            ys.append(yg + jnp.concatenate(yh, axis=1))
        y = jnp.concatenate(ys, axis=1) + x * dskip
        y = y * _silu(z_ref[rows, :])
        outs = []
        for g in range(SSD_GROUPS):
            sg = slice(g * SSD_GROUP_WIDTH, (g + 1) * SSD_GROUP_WIDTH)
            yg = y[:, sg]
            ms = jnp.mean(yg * yg, axis=-1, keepdims=True)
            outs.append(yg * lax.rsqrt(ms + SSD_NORM_EPS))
        o_ref[rows, :] = (jnp.concatenate(outs, axis=1) * nw).astype(o_ref.dtype)
        return carry

    lax.fori_loop(0, tt // CHUNK, chunk, 0, unroll=2)


def _ssd(p, cw, cb, dtb, alog, dskip_x, nw, expand, tt):
    bsz, t, _ = p.shape

    def col(name):
        off, w = ODD_COLS[name]
        return pl.BlockSpec((None, tt, w), lambda b, i, j=off // w: (b, i, j))

    fix = lambda b, i: (0, 0)
    full = lambda a: pl.BlockSpec(a.shape, fix)
    gs = SSD_GROUPS * SSD_STATE
    cwx, cwb, cwc = cw[:, :SSD_INNER], cw[:, SSD_INNER:SSD_INNER + gs], cw[:, SSD_INNER + gs:]
    cbx, cbb, cbc = cb[:, :SSD_INNER], cb[:, SSD_INNER:SSD_INNER + gs], cb[:, SSD_INNER + gs:]
    sel = jnp.concatenate([_chunk_sel(tt, CHUNK), _chunk_sel(tt, CHUNK, "all")], axis=0)
    args = (cwx, cbx, cwb, cbb, cwc, cbc, dtb, alog, dskip_x, nw, expand, sel)
    return pl.pallas_call(
        _ssd_kernel,
        grid=(bsz, t // tt),
        in_specs=[col("z"), col("x"), col("bm"), col("cm"), col("dt")] + [full(a) for a in args],
        out_specs=pl.BlockSpec((None, tt, SSD_INNER), lambda b, i: (b, i, 0)),
        out_shape=jax.ShapeDtypeStruct((bsz, t, SSD_INNER), BF16),
        scratch_shapes=[pltpu.VMEM((SUBLANES, SSD_INNER), F32), pltpu.VMEM((SUBLANES, gs), F32),
                        pltpu.VMEM((SUBLANES, gs), F32),
                        pltpu.VMEM((SSD_GROUPS, SSD_STATE, SSD_GROUP_WIDTH), F32),
                        pltpu.VMEM((tt, SSD_INNER), F32), pltpu.VMEM((tt, gs), F32), pltpu.VMEM((tt, gs), F32),
                        pltpu.VMEM((tt, LANES), F32)] + [pltpu.VMEM((tt, SSD_INNER), F32)] * 3,
        compiler_params=_params("parallel", "arbitrary"),
        name="ssd",
    )(p, p, p, p, p, *args)


def _rwkv_kernel(r_ref, k_ref, v_ref, wa_ref, xg_ref, mur_ref, muk_ref, muv_ref, muwa_ref, mug_ref,
                 w0_ref, w2_ref, a0_ref, a2_ref, g2_ref, kk_ref, ka_ref, rk_ref, lng_ref, lnb_ref, ones_ref, sel_ref,
                 o_ref, tr_ref, tk_ref, tv_ref, twa_ref, tg_ref, st_ref, y_scr,
                 at_scr, rt_scr, bt_scr, kt_scr, bh_scr, kh_scr, v_scr, wc_scr):
    @pl.when(pl.program_id(1) == 0)
    def _():
        for ref in (tr_ref, tk_ref, tv_ref, twa_ref, tg_ref, st_ref):
            ref[...] = jnp.zeros_like(ref)

    tt = r_ref.shape[0]
    hd = RWKV_HEADDIM

    def mix(p_ref, tail_ref, mu_ref):
        p = p_ref[...]
        prev = _shift_rows(p, tail_ref, 1)
        tail_ref[...] = p[tt - SUBLANES:, :]
        return p + (prev - p) * mu_ref[...]

    r = mix(r_ref, tr_ref, mur_ref)
    k = mix(k_ref, tk_ref, muk_ref)
    v = mix(v_ref, tv_ref, muv_ref)
    xwa = mix(wa_ref, twa_ref, muwa_ref)
    xg = mix(xg_ref, tg_ref, mug_ref)

    lane = lax.broadcasted_iota(jnp.int32, xwa.shape, 1)
    lora_in = jnp.where(lane < RWKV_DECAY_LORA, jnp.tanh(xwa), xwa)
    w_log = -_softplus(-(w0_ref[...] + _mm_split(lora_in, w2_ref[...]))) - 0.5
    lw = -jnp.exp(w_log)
    a_sig = _sigmoid(a0_ref[...] + _mm_split(lora_in, a2_ref[...]))
    gate = _mm_split(_sigmoid(xg), g2_ref[...])
    ones_bd = ones_ref[...]
    kk = k * kk_ref[...]
    kk = kk / jnp.maximum(jnp.sqrt(_mm_01(kk * kk, ones_bd)), 1e-12)
    k = k * (1.0 + (a_sig - 1.0) * ka_ref[...])

    cs = RWKV_CHUNK
    tri_incl = _tri(cs)
    tri_strict = _tri(cs, strict=True)
    tri_col = jnp.concatenate([tri_strict, tri_incl], axis=0)
    eye = (tri_incl & ~tri_strict).astype(F32)

    sums = _mm_01_left(sel_ref[...], lw)
    cum = sums[:tt]
    tot = sums[tt:]
    bvec = kk * a_sig
    e_neg = jnp.exp(-cum)
    e_last = jnp.exp(tot - cum)
    for ref, val in ((at_scr, -kk * jnp.exp(cum - lw)), (rt_scr, r * jnp.exp(cum)), (bt_scr, bvec * e_neg),
                     (kt_scr, k * e_neg), (bh_scr, bvec * e_last), (kh_scr, k * e_last), (v_scr, v),
                     (wc_scr, jnp.exp(tot))):
        ref[...] = val

    def chunk(c, carry):
        rows = pl.ds(pl.multiple_of(c * cs, cs), cs)
        at = at_scr[rows, :]
        rt = rt_scr[rows, :]
        bt = bt_scr[rows, :]
        kt = kt_scr[rows, :]
        bh = bh_scr[rows, :]
        kh = kh_scr[rows, :]
        vc = v_scr[rows, :]
        wc = wc_scr[pl.ds(pl.multiple_of(c * cs, cs), 1), :]
        heads = range(RWKV_HEADS)
        sls = [slice(h * hd, (h + 1) * hd) for h in heads]
        amat = [_mm(jnp.concatenate([at[:, sl], rt[:, sl]], axis=0),
                    jnp.concatenate([bt[:, sl], kt[:, sl]], axis=0), _NT) for sl in sls]
        a_ab = [jnp.where(tri_strict, m[:cs, :cs], 0.0) for m in amat]
        a_rb = [jnp.where(tri_incl, m[cs:, :cs], 0.0) for m in amat]
        a_xk = [jnp.where(tri_col, m[:, cs:], 0.0) for m in amat]
        avrv = [_mm(a_xk[h], vc[:, sls[h]]) for h in heads]
        inv = [eye + m for m in a_ab]
        pw = [_mm(m, m) for m in a_ab]
        for _ in range(int(math.log2(cs)) - 2):
            both = [_mm(jnp.concatenate([inv[h], pw[h]], axis=0), pw[h]) for h in heads]
            inv = [inv[h] + both[h][:cs] for h in heads]
            pw = [m[cs:] for m in both]
        inv = [inv[h] + _mm(inv[h], pw[h]) for h in heads]
        tu = [_mm(inv[h], jnp.concatenate([at[:, sls[h]], avrv[h][:cs]], axis=1)) for h in heads]
        qy = [_mm(a_rb[h], tu[h]) + jnp.concatenate([rt[:, sls[h]], avrv[h][cs:]], axis=1)
              for h in heads]
        zb =[_mm(tu[h], bh[:, sls[h]], _TN) for h in heads]
        vk = [_mm(vc[:, sls[h]], kh[:, sls[h]], _TN) for h in heads]
        for h in heads:
            st = st_ref[h]
            y_scr[rows, sls[h]] = _mm(qy[h][:, :hd], st, _NT) + qy[h][:, hd:]
            st_ref[h] = st * wc[:, sls[h]] + _mm(st, zb[h][:hd]) + zb[h][hd:] + vk[h]
        return carry

    lax.fori_loop(0, tt // cs, chunk, 0, unroll=2)

    y = y_scr[...]
    inv_n = 1.0 / hd
    mu_y = _mm_01(y, ones_bd) * inv_n
    dy = y - mu_y
    var_y = _mm_01(dy * dy, ones_bd) * inv_n
    yn = dy * lax.rsqrt(var_y + RWKV_GN_EPS) * lng_ref[...] + lnb_ref[...]
    bonus = _mm_01(r * k * rk_ref[...], ones_bd) * v
    o_ref[...] = ((yn + bonus) * gate).astype(o_ref.dtype)


def _rwkv(p, mu, w0, w2p, a0, a2p, g2, k_k, k_a, r_k, ln_g, ln_b, ones_bd, tt):
    bsz, t, _ = p.shape
    w = RWKV_WIDTH

    def col(name):
        off, wd = ODD_COLS[name]
        return pl.BlockSpec((None, tt, wd), lambda b, i, j=off // wd: (b, i, j))

    fix = lambda b, i: (0, 0)
    full = lambda a: pl.BlockSpec(a.shape, fix)
    mur, muk, muv = mu[:, :w], mu[:, w:2 * w], mu[:, 2 * w:3 * w]
    muwa, mug = mu[:, 3 * w:3 * w + LANES], mu[:, 3 * w + LANES:]
    sel = jnp.concatenate([_chunk_sel(tt, RWKV_CHUNK), _chunk_sel(tt, RWKV_CHUNK, "all")], axis=0)
    args = (mur, muk, muv, muwa, mug, w0, w2p, a0, a2p, g2, k_k, k_a, r_k, ln_g, ln_b, ones_bd, sel)
    return pl.pallas_call(
        _rwkv_kernel,
        grid=(bsz, t // tt),
        in_specs=[col("r"), col("k"), col("v"), col("wa"), col("xg")] + [full(a) for a in args],
        out_specs=pl.BlockSpec((None, tt, w), lambda b, i: (b, i, 0)),
        out_shape=jax.ShapeDtypeStruct((bsz, t, w), BF16),
        scratch_shapes=[pltpu.VMEM((SUBLANES, w), F32), pltpu.VMEM((SUBLANES, w), F32),
                        pltpu.VMEM((SUBLANES, w), F32), pltpu.VMEM((SUBLANES, LANES), F32),
                        pltpu.VMEM((SUBLANES, LANES), F32),
                        pltpu.VMEM((RWKV_HEADS, RWKV_HEADDIM, RWKV_HEADDIM), F32)]
        + [pltpu.VMEM((tt, w), F32)] * 9,
        compiler_params=_params("parallel", "arbitrary"),
        name="rwkv7",
    )(p, p, p, p, p, *args)


def _reorder_cols(w, pieces, order, width):
    parts, pos = [], 0
    for name, (off, slot) in order.items():
        assert off == pos, "slots must be listed in order and contiguous"
        start, size = pieces[name]
        parts.append(w[:, start:start + size])
        if size < slot:
            parts.append(jnp.zeros((w.shape[0], slot - size), w.dtype))
        pos += slot
    assert pos == width
    return jnp.concatenate(parts, axis=1)


def _block_diag(w):
    n, i, j = w.shape
    eye = jnp.eye(n, dtype=w.dtype)
    return (eye[:, None, :, None] * w[:, :, None, :]).reshape(n * i, n * j)


def _row(v, width=None):
    v = v.reshape(1, -1).astype(F32)
    if width is not None and v.shape[1] < width:
        v = jnp.pad(v, ((0, 0), (0, width - v.shape[1])))
    return v


def _even_layer(x, x_in, w_in, gk_w2, gk_b, gla_norm, conv_w, conv_b, wa, ba, wx, bx, lam,
                w_out, ln1_g, ln1_b, f_w1, f_w3, f_w2, ln2_g, ln2_b, bsz, t):
    qk, vw, lw = GLA_QK, GLA_VW, LRU_WIDTH
    pieces = dict(q=(0, qk), k=(qk, qk), v=(2 * qk, vw), g=(2 * qk + vw, vw), gk=(2 * qk + 2 * vw, GLA_GK_RANK),
                  xb=(2 * qk + 2 * vw + GLA_GK_RANK, lw), gate=(2 * qk + 2 * vw + GLA_GK_RANK + lw, lw))
    w_in_r = _reorder_cols(w_in, pieces, EVEN_COLS, EVEN_WIDTH).astype(BF16)
    p = _proj(x_in, w_in_r, tm=min(512, x.shape[0])).reshape(bsz, t, EVEN_WIDTH)
    tt = min(512, t)
    gkw = jnp.pad(gk_w2, ((0, LANES - GLA_GK_RANK), (0, 0)))
    y_a = _gla(p, gkw, _row(gk_b), _row(gla_norm), tt)
    y_b = _rglru(p, conv_w, _row(conv_b), _block_diag(wa).astype(BF16), _row(ba), _block_diag(wx).astype(BF16),
                 _row(bx), _row(lam), tt)
    n = bsz * t
    tm = min(512, n)
    x1, x1b = _outproj_ln(y_a.reshape(n, vw), y_b.reshape(n, lw), w_out[:vw].astype(BF16), w_out[vw:].astype(BF16),
                          x, _row(ln1_g), _row(ln1_b), tm)
    return _ffn_ln(x1b, f_w1.astype(BF16), f_w3.astype(BF16), f_w2.astype(BF16), x1,
                   _row(ln2_g), _row(ln2_b), tm=min(512, n))


def _odd_layer(x, xb, w_in, conv_w, conv_b, dt_bias, a_log, d_skip, ssd_norm, mu, w0, w2, a0, a2, g2, k_k, k_a,
               r_k, rln_g, rln_b, w_out, ln1_g, ln1_b, router, ew1, ew3, ew2, ln2_g, ln2_b, bsz, t):
    si, gs, rw = SSD_INNER, SSD_GROUPS * SSD_STATE, RWKV_WIDTH
    o = 2 * si + 2 * gs + SSD_HEADS
    pieces = dict(z=(0, si), x=(si, si), bm=(2 * si, gs), cm=(2 * si + gs, gs), dt=(2 * si + 2 * gs, SSD_HEADS),
                  r=(o, rw), k=(o + rw, rw), v=(o + 2 * rw, rw),
                  wa=(o + 3 * rw, RWKV_DECAY_LORA + RWKV_AAA_LORA), xg=(o + 3 * rw + LANES, RWKV_GATE_LORA))
    w_in_r = _reorder_cols(w_in, pieces, ODD_COLS, ODD_WIDTH).astype(BF16)
    n = bsz * t
    p = _proj(xb, w_in_r, tm=min(512, n)).reshape(bsz, t, ODD_WIDTH)
    expand = jnp.pad(jnp.repeat(jnp.eye(SSD_HEADS, dtype=BF16), SSD_HEADDIM, axis=1), ((0, LANES - SSD_HEADS), (0, 0)))
    y_c = _ssd(p, conv_w, _row(conv_b), _row(dt_bias, LANES), _row(a_log, LANES),
               _row(jnp.repeat(d_skip, SSD_HEADDIM)), _row(ssd_norm), expand, tt=min(256, t))
    zeros = jnp.zeros((RWKV_DECAY_LORA, rw), F32)
    ones_bd = _block_diag(jnp.ones((RWKV_HEADS, RWKV_HEADDIM, RWKV_HEADDIM), BF16))
    y_d = _rwkv(p, _row(mu), _row(w0), jnp.concatenate([w2, zeros]), _row(a0), jnp.concatenate([zeros, a2]), g2,
                _row(k_k), _row(k_a), _row(r_k), _row(rln_g), _row(rln_b), ones_bd, tt=min(256, t))
    tm = min(512, n)
    x1, x1b = _outproj_ln(y_c.reshape(n, si), y_d.reshape(n, rw), w_out[:si].astype(BF16), w_out[si:].astype(BF16),
                          x, _row(ln1_g), _row(ln1_b), tm)
    y = _moe_ln(x1, x1b, jnp.pad(router, ((0, 0), (0, LANES - N_EXPERTS))), ew1.astype(BF16), ew3.astype(BF16),
                ew2.astype(BF16), _row(ln2_g), _row(ln2_b))
    return y, None


def kernel(x, e_w_in, e_gk_w2, e_gk_b, e_gla_norm, e_conv_w, e_conv_b, e_lru_wa, e_lru_ba, e_lru_wx, e_lru_bx, e_lru_lambda, e_w_out, e_ln1_g, e_ln1_b, e_ffn_w1, e_ffn_w3, e_ffn_w2, e_ln2_g, e_ln2_b, o_w_in, o_conv_w, o_conv_b, o_dt_bias, o_a_log, o_d_skip, o_ssd_norm, o_rwkv_mu, o_rwkv_w0, o_rwkv_w2, o_rwkv_a0, o_rwkv_a2, o_rwkv_g2, o_rwkv_k_k, o_rwkv_k_a, o_rwkv_r_k, o_rwkv_ln_g, o_rwkv_ln_b, o_w_out, o_ln1_g, o_ln1_b, o_router, o_exp_w1, o_exp_w3, o_exp_w2, o_ln2_g, o_ln2_b):
    bsz, t, d = x.shape
    h = x.reshape(bsz * t, d)
    hb = h
    for i in range(DEPTH):
        j = i // 2
        if i % 2 == 0:
            h, hb = _even_layer(h, hb, e_w_in[j], e_gk_w2[j], e_gk_b[j], e_gla_norm[j], e_conv_w[j], e_conv_b[j],
                                e_lru_wa[j], e_lru_ba[j], e_lru_wx[j], e_lru_bx[j], e_lru_lambda[j], e_w_out[j],
                                e_ln1_g[j], e_ln1_b[j], e_ffn_w1[j], e_ffn_w3[j], e_ffn_w2[j], e_ln2_g[j],
                                e_ln2_b[j], bsz, t)
        else:
            h, hb = _odd_layer(h, hb, o_w_in[j], o_conv_w[j], o_conv_b[j], o_dt_bias[j], o_a_log[j], o_d_skip[j],
                               o_ssd_norm[j], o_rwkv_mu[j], o_rwkv_w0[j], o_rwkv_w2[j], o_rwkv_a0[j],
                               o_rwkv_a2[j], o_rwkv_g2[j], o_rwkv_k_k[j], o_rwkv_k_a[j], o_rwkv_r_k[j],
                               o_rwkv_ln_g[j], o_rwkv_ln_b[j], o_w_out[j], o_ln1_g[j], o_ln1_b[j], o_router[j],
                               o_exp_w1[j], o_exp_w3[j], o_exp_w2[j], o_ln2_g[j], o_ln2_b[j], bsz, t)
    return h.reshape(bsz, t, d)
```

```python
import functools
import math

import jax
import jax.numpy as jnp
from jax import lax
from jax.experimental import pallas as pl
from jax.experimental.pallas import tpu as pltpu

F32 = jnp.float32
BF16 = jnp.bfloat16

D_MODEL = 1024
DEPTH = 2
DEEPNORM_ALPHA = (2 * DEPTH) ** 0.25
LN_EPS = 1e-5
CONV_K = 4
CHUNK = 64

GLA_HEADS = 4
GLA_DK = 64
GLA_DV = 128
GLA_QK = GLA_HEADS * GLA_DK
GLA_VW = GLA_HEADS * GLA_DV
GLA_GK_RANK = 16
GLA_GATE_NORM = 16.0
GLA_NORM_EPS = 1e-5

LRU_WIDTH = 512
LRU_BLOCKS = 8
LRU_C = 8.0

SSD_HEADS = 16
SSD_HEADDIM = 64
SSD_INNER = SSD_HEADS * SSD_HEADDIM
SSD_GROUPS = 2
SSD_STATE = 128
SSD_GROUP_WIDTH = SSD_INNER // SSD_GROUPS
SSD_NORM_EPS = 1e-5

RWKV_HEADS = 8
RWKV_HEADDIM = 64
RWKV_WIDTH = RWKV_HEADS * RWKV_HEADDIM
RWKV_DECAY_LORA = 64
RWKV_AAA_LORA = 64
RWKV_GATE_LORA = 128
RWKV_GN_EPS = 64e-5
RWKV_CHUNK = 128

N_EXPERTS = 8
LANES = 128
SUBLANES = 8
assert N_EXPERTS == SUBLANES
MOE_TB = 512
MOE_TR = 256
MOE_TE = 512
assert MOE_TE % MOE_TR == 0
MOE_COMBINE_BUFS = 4
VMEM_LIMIT = 56 * 1024 * 1024

EVEN_COLS = dict(v=(0, 512), g=(512, 512), xb=(1024, 512), gate=(1536, 512), q=(2048, 256), k=(2304, 256),
                 gk=(2560, 128))
EVEN_WIDTH = 2688
ODD_COLS = dict(z=(0, 1024), x=(1024, 1024), r=(2048, 512), k=(2560, 512), v=(3072, 512), bm=(3584, 256),
                cm=(3840, 256), wa=(4096, 128), xg=(4224, 128), dt=(4352, 128))
ODD_WIDTH = 4480


def _mm(a, b, dims=((1,), (0,))):
    return lax.dot_general(a.astype(BF16), b.astype(BF16), (dims, ((), ())), preferred_element_type=F32)


def _mm_split(a, b):
    a_hi = a.astype(BF16)
    b_hi = b.astype(BF16)
    a_lo = (a - a_hi.astype(F32)).astype(BF16)
    b_lo = (b - b_hi.astype(F32)).astype(BF16)
    return (jnp.dot(a_hi, b_hi, preferred_element_type=F32) + jnp.dot(a_lo, b_hi, preferred_element_type=F32)
            + jnp.dot(a_hi, b_lo, preferred_element_type=F32))


_NT = ((1,), (1,))
_TN = ((0,), (0,))


def _mm_01(x, sel):
    hi = x.astype(BF16)
    lo = (x - hi.astype(F32)).astype(BF16)
    return (jnp.dot(hi, sel, preferred_element_type=F32) + jnp.dot(lo, sel, preferred_element_type=F32))


def _sigmoid(x):
    return 1.0 / (1.0 + jnp.exp(-x))


def _softplus(x):
    return jnp.maximum(x, 0.0) + jnp.log(1.0 + jnp.exp(-jnp.abs(x)))


def _silu(x):
    return x * _sigmoid(x)


def _layer_norm(h, g, b):
    mu = jnp.mean(h, axis=-1, keepdims=True)
    d = h - mu
    var = jnp.mean(d * d, axis=-1, keepdims=True)
    return d * lax.rsqrt(var + LN_EPS) * g + b


def _params(*sem):
    return pltpu.CompilerParams(dimension_semantics=sem, vmem_limit_bytes=VMEM_LIMIT)


def _tri(n, strict=False):
    r = lax.broadcasted_iota(jnp.int32, (n, n), 0)
    c = lax.broadcasted_iota(jnp.int32, (n, n), 1)
    return (r > c) if strict else (r >= c)


def _chunk_sel(n, chunk, kind="incl"):
    r = jnp.arange(n)[:, None]
    c = jnp.arange(n)[None, :]
    first = r - r % chunk
    upper = r if kind == "incl" else first + (chunk - 1)
    return ((c >= first) & (c <= upper)).astype(BF16)


def _mm_01_left(sel, x):
    hi = x.astype(BF16)
    lo = (x - hi.astype(F32)).astype(BF16)
    return jnp.dot(sel, hi, preferred_element_type=F32) + jnp.dot(sel, lo, preferred_element_type=F32)


def _shift_rows(p, tail_ref, s):
    n = p.shape[0]
    rolled = pltpu.roll(p, s, axis=0)
    head = pltpu.roll(tail_ref[...], s, axis=0)
    row = lax.broadcasted_iota(jnp.int32, (SUBLANES, p.shape[1]), 0)
    fixed = jnp.where(row < s, head, rolled[:SUBLANES])
    return jnp.concatenate([fixed, rolled[SUBLANES:]], axis=0) if n > SUBLANES else fixed


def _proj_kernel(x_ref, w_ref, o_ref):
    o_ref[...] = jnp.dot(x_ref[...].astype(BF16), w_ref[...], preferred_element_type=F32)


def _resident(shape):
    return pl.BlockSpec(shape, lambda *_: (0,) * len(shape), pipeline_mode=pl.Buffered(1))


def _proj(x, w, tm):
    m, k = x.shape
    n = w.shape[1]
    return pl.pallas_call(
        _proj_kernel,
        grid=(m // tm,),
        in_specs=[pl.BlockSpec((tm, k), lambda i: (i, 0)), _resident((k, n))],
        out_specs=pl.BlockSpec((tm, n), lambda i: (i, 0)),
        out_shape=jax.ShapeDtypeStruct((m, n), F32),
        compiler_params=_params("parallel"),
        name="in_proj",
    )(x, w)


def _outproj_ln_value(ya_ref, yb_ref, wa_ref, wb_ref, res_ref, g_ref, b_ref):
    acc = jnp.dot(ya_ref[...], wa_ref[...], preferred_element_type=F32)
    acc += jnp.dot(yb_ref[...], wb_ref[...], preferred_element_type=F32)
    return _layer_norm(DEEPNORM_ALPHA * res_ref[...] + acc, g_ref[...], b_ref[...])


def _mix_ffn_kernel(ya_ref, yb_ref, wa_ref, wb_ref, res_ref, g1_ref, b1_ref, w1_ref, w3_ref, w2_ref, g2_ref, b2_ref,
                    o_ref, ob_ref):
    x1 = _outproj_ln_value(ya_ref, yb_ref, wa_ref, wb_ref, res_ref, g1_ref, b1_ref)
    xb = x1.astype(BF16)
    h1 = jnp.dot(xb, w1_ref[...], preferred_element_type=F32)
    h3 = jnp.dot(xb, w3_ref[...], preferred_element_type=F32)
    ff = jnp.dot((_silu(h1) * h3).astype(BF16), w2_ref[...], preferred_element_type=F32)
    y = _layer_norm(DEEPNORM_ALPHA * x1 + ff, g2_ref[...], b2_ref[...])
    o_ref[...] = y
    ob_ref[...] = y.astype(BF16)


def _mix_ffn(ya, yb, wa, wb, res, g1, b1, w1, w3, w2, g2, b2, tm):
    m, d = res.shape
    row = lambda i: (i, 0)
    consts = (wa, wb, None, g1, b1, w1, w3, w2, g2, b2)
    return pl.pallas_call(
        _mix_ffn_kernel,
        grid=(m // tm,),
        in_specs=[pl.BlockSpec((tm, ya.shape[1]), row), pl.BlockSpec((tm, yb.shape[1]), row)]
        + [pl.BlockSpec((tm, d), row) if c is None else _resident(c.shape) for c in consts],
        out_specs=[pl.BlockSpec((tm, d), row), pl.BlockSpec((tm, d), row)],
        out_shape=[jax.ShapeDtypeStruct((m, d), F32), jax.ShapeDtypeStruct((m, d), BF16)],
        compiler_params=_params("parallel"),
        name="out_proj_ffn_ln",
    )(ya, yb, wa, wb, res, g1, b1, w1, w3, w2, g2, b2)


def _mix_router_kernel(ya_ref, yb_ref, wa_ref, wb_ref, res_ref, g_ref, b_ref, w_ref, o_ref, ob_ref, gates_ref,
                       rank_ref, rank_t_ref, cnt_ref, carry_ref):
    @pl.when(pl.program_id(0) == 0)
    def _():
        carry_ref[...] = jnp.zeros_like(carry_ref)

    x1 = _outproj_ln_value(ya_ref, yb_ref, wa_ref, wb_ref, res_ref, g_ref, b_ref)
    o_ref[...] = x1
    ob_ref[...] = x1.astype(BF16)
    logits = _mm_split(x1, w_ref[...])
    lane = lax.broadcasted_iota(jnp.int32, logits.shape, 1)
    neg = jnp.float32(-jnp.inf)
    l1 = jnp.where(lane < N_EXPERTS, logits, neg)
    m1 = jnp.max(l1, axis=1, keepdims=True)
    i1 = jnp.min(jnp.where(l1 == m1, lane, LANES), axis=1, keepdims=True)
    l2 = jnp.where(lane == i1, neg, l1)
    m2 = jnp.max(l2, axis=1, keepdims=True)
    i2 = jnp.min(jnp.where(l2 == m2, lane, LANES), axis=1, keepdims=True)
    ex = jnp.exp(m2 - m1)
    w_top = 1.0 / (1.0 + ex)
    gates_ref[...] = jnp.where(lane == i1, w_top, 0.0) + jnp.where(lane == i2, ex * w_top, 0.0)

    sel = jnp.where(lane == i1, 1.0, 0.0) + jnp.where(lane == i2, 1.0, 0.0)
    tb = sel.shape[0]
    before = _mm(_tri(tb, strict=True).astype(F32), sel)
    carry = carry_ref[...]
    rank = jnp.where(sel > 0.0, carry + before, -1.0)
    rank_ref[...] = rank
    rank_t_ref[...] = rank.T[:SUBLANES, :]
    carry = carry + jnp.sum(sel, axis=0, keepdims=True)
    carry_ref[...] = carry
    cnt_ref[...] = jnp.broadcast_to(carry, cnt_ref.shape)


def _mix_router(ya, yb, wa, wb, res, g, b, w):
    m, d = res.shape
    nblk = m // MOE_TB
    row = lambda i: (i, 0)
    return pl.pallas_call(
        _mix_router_kernel,
        grid=(nblk,),
        in_specs=[pl.BlockSpec((MOE_TB, ya.shape[1]), row), pl.BlockSpec((MOE_TB, yb.shape[1]), row),
                  _resident(wa.shape), _resident(wb.shape), pl.BlockSpec((MOE_TB, d), row),
                  _resident((1, d)), _resident((1, d)), _resident(w.shape)],
        out_specs=[pl.BlockSpec((MOE_TB, d), row), pl.BlockSpec((MOE_TB, d), row),
                   pl.BlockSpec((MOE_TB, LANES), row), pl.BlockSpec((MOE_TB, LANES), row),
                   pl.BlockSpec((SUBLANES, MOE_TB), lambda i: (0, i)), pl.BlockSpec((SUBLANES, LANES), row)],
        out_shape=[jax.ShapeDtypeStruct((m, d), F32), jax.ShapeDtypeStruct((m, d), BF16),
                   jax.ShapeDtypeStruct((m, LANES), F32), jax.ShapeDtypeStruct((m, LANES), F32),
                   jax.ShapeDtypeStruct((SUBLANES, m), F32), jax.ShapeDtypeStruct((nblk * SUBLANES, LANES), F32)],
        scratch_shapes=[pltpu.VMEM((1, LANES), F32)],
        compiler_params=_params("arbitrary"),
        name="out_proj_ln_router",
    )(ya, yb, wa, wb, res, g, b, w)


def _moe_plan(cnt, n_tok):
    i32 = jnp.int32
    nblk = n_tok // MOE_TB
    c_inc = cnt.reshape(nblk, SUBLANES, LANES)[:, 0, :N_EXPERTS].astype(i32)
    c_exc = jnp.concatenate([jnp.zeros((1, N_EXPERTS), i32), c_inc[:-1]], axis=0)
    gsz = (c_inc[-1] + MOE_TE - 1) // MOE_TE * MOE_TE
    gend = jnp.cumsum(gsz)
    start = gend - gsz
    n_et = _moe_rows(n_tok) // MOE_TE
    n_act = gend[-1] // MOE_TE
    et = jnp.minimum(jnp.arange(n_et, dtype=i32), jnp.maximum(n_act - 1, 0))
    et_expert = jnp.sum((gend[None, :] <= (et * MOE_TE)[:, None]).astype(i32), axis=1)
    et_active = (jnp.arange(n_et, dtype=i32) < n_act).astype(i32)

    lo = start[None, :] + c_exc
    hi = start[None, :] + c_inc
    first_tile = lo // MOE_TR
    n_items = jnp.where(hi > lo, (hi - 1) // MOE_TR - first_tile + 1, 0)
    wmax = _moe_rows(n_tok) // MOE_TR + N_EXPERTS * nblk
    exp_id = jnp.broadcast_to(jnp.arange(N_EXPERTS, dtype=i32)[None, :], (nblk, N_EXPERTS))

    n, ft, exp = (a.reshape(-1) for a in (n_items, first_tile, exp_id))
    inc = jnp.cumsum(n)
    wc = jnp.minimum(jnp.arange(wmax, dtype=i32), jnp.maximum(inc[-1] - 1, 0))
    idx = jnp.sum((inc[None, :] <= wc[:, None]).astype(i32), axis=1)
    per_block = jnp.sum(n_items, axis=1)
    by_block = (ft[idx] + wc - (inc - n)[idx], exp[idx], jnp.cumsum(per_block) - per_block, per_block)

    row0 = jnp.arange(_moe_rows(n_tok) // MOE_TR, dtype=i32) * MOE_TR
    group = jnp.minimum(jnp.sum((gend[None, :] <= row0[:, None]).astype(i32), axis=1), N_EXPERTS - 1)
    rank_lo = row0 - start[group]
    count = c_inc[-1][group]
    rank_hi = jnp.minimum(rank_lo + MOE_TR, count)
    holds = (rank_lo < count) & (row0 < gend[-1])
    blk_lo = jnp.sum((c_inc[:, group].T <= rank_lo[:, None]).astype(i32), axis=1)
    blk_hi = jnp.sum((c_exc[:, group].T < rank_hi[:, None]).astype(i32), axis=1)
    by_tile = (group, jnp.minimum(blk_lo, nblk - 1), jnp.where(holds, blk_hi - blk_lo, 0))

    return start, (et, et_expert, et_active), by_tile, by_block


def _moe_rows(n_tok):
    return 2 * n_tok + N_EXPERTS * MOE_TE


def _one_hot_rows(dest, tile):
    row = lax.broadcasted_iota(jnp.int32, (MOE_TR, MOE_TB), 0) + tile * MOE_TR
    return jnp.where(dest == row.astype(F32), 1.0, 0.0).astype(BF16)


def _dispatch_kernel(exp_ref, blk0_ref, nblk_ref, x_ref, dest_ref, xs_ref, acc_ref):
    j = pl.program_id(0)
    acc_ref[...] = jnp.zeros_like(acc_ref)

    def block(i, carry):
        b = blk0_ref[j] + i
        onehot = _one_hot_rows(dest_ref[b, pl.ds(exp_ref[j], 1), :], j)
        xblk = x_ref[pl.ds(pl.multiple_of(b * MOE_TB, MOE_TB), MOE_TB), :]
        acc_ref[...] += jnp.dot(onehot, xblk, preferred_element_type=F32)
        return carry

    lax.fori_loop(0, nblk_ref[j], block, 0)
    xs_ref[...] = acc_ref[...].astype(BF16)


def _dispatch(plan, xb, dest_t):
    exp, blk0, nblk = plan
    n, d = xb.shape
    rows = _moe_rows(n)
    dest3 = dest_t.reshape(N_EXPERTS, n // MOE_TB, MOE_TB).transpose(1, 0, 2)
    grid_spec = pltpu.PrefetchScalarGridSpec(
        num_scalar_prefetch=3,
        grid=(rows // MOE_TR,),
        in_specs=[_resident(xb.shape), _resident(dest3.shape)],
        out_specs=pl.BlockSpec((MOE_TR, d), lambda j, e, b, c: (j, 0)),
        scratch_shapes=[pltpu.VMEM((MOE_TR, d), F32)],
    )
    return pl.pallas_call(
        _dispatch_kernel,
        grid_spec=grid_spec,
        out_shape=jax.ShapeDtypeStruct((rows, d), BF16),
        compiler_params=_params("arbitrary"),
        name="moe_dispatch",
    )(exp, blk0, nblk, xb, dest3)


def _expert_kernel(et_ref, ee_ref, ea_ref, xs_ref, w1_ref, w3_ref, w2_ref, ys_ref, acc_ref):
    j = pl.program_id(0)
    f = pl.program_id(1)

    @pl.when((ea_ref[j] == 0) & (f == 0))
    def _():
        ys_ref[...] = jnp.zeros_like(ys_ref)

    @pl.when(ea_ref[j] == 1)
    def _():
        @pl.when(f == 0)
        def _():
            acc_ref[...] = jnp.zeros_like(acc_ref)

        x = xs_ref[...]
        h1 = jnp.dot(x, w1_ref[...], preferred_element_type=F32)
        h3 = jnp.dot(x, w3_ref[...], preferred_element_type=F32)
        acc_ref[...] += jnp.dot((_silu(h1) * h3).astype(BF16), w2_ref[...], preferred_element_type=F32)

        @pl.when(f == pl.num_programs(1) - 1)
        def _():
            ys_ref[...] = acc_ref[...].astype(BF16)


def _experts(plan, xs, w1, w3, w2, tf):
    et, ee, ea = plan
    rows, d = xs.shape
    nf = w1.shape[2] // tf
    fidx = lambda f, a, j: f * a[j] + (nf - 1) * (1 - a[j])
    grid_spec = pltpu.PrefetchScalarGridSpec(
        num_scalar_prefetch=3,
        grid=(et.shape[0], nf),
        in_specs=[pl.BlockSpec((MOE_TE, d), lambda j, f, t, e, a: (t[j], 0)),
                  pl.BlockSpec((None, d, tf), lambda j, f, t, e, a: (e[j], 0, fidx(f, a, j))),
                  pl.BlockSpec((None, d, tf), lambda j, f, t, e, a: (e[j], 0, fidx(f, a, j))),
                  pl.BlockSpec((None, tf, d), lambda j, f, t, e, a: (e[j], fidx(f, a, j), 0))],
        out_specs=pl.BlockSpec((MOE_TE, d), lambda j, f, t, e, a: (j, 0)),
        scratch_shapes=[pltpu.VMEM((MOE_TE, d), F32)],
    )
    return pl.pallas_call(
        _expert_kernel,
        grid_spec=grid_spec,
        out_shape=jax.ShapeDtypeStruct((rows, d), BF16),
        compiler_params=_params("arbitrary", "arbitrary"),
        name="moe_experts",
    )(et, ee, ea, xs, w1, w3, w2)


def _combine_kernel(tile_ref, exp_ref, off_ref, cnt_ref, ys_hbm, dest_ref, gates_ref, res_ref, g_ref, b_ref, o_ref,
                    buf_ref, sem_ref, acc_ref):
    j = pl.program_id(0)
    base = off_ref[j]
    n = cnt_ref[j]

    def fetch(i, slot, first=base):
        row0 = pl.multiple_of(tile_ref[first + i] * MOE_TR, MOE_TR)
        return pltpu.make_async_copy(ys_hbm.at[pl.ds(row0, MOE_TR), :], buf_ref.at[slot], sem_ref.at[slot])

    def prime(blk):
        for i0 in range(MOE_COMBINE_BUFS - 1):
            @pl.when(i0 < cnt_ref[blk])
            def _(i0=i0):
                fetch(i0, i0, off_ref[blk]).start()

    @pl.when(j == 0)
    def _():
        prime(0)

    acc_ref[...] = jnp.zeros_like(acc_ref)
    dest = dest_ref[...]
    gates = gates_ref[...]
    lane = lax.broadcasted_iota(jnp.int32, dest.shape, 1)

    def item(i, carry):
        slot = lax.rem(i, MOE_COMBINE_BUFS)
        ahead = i + (MOE_COMBINE_BUFS - 1)

        @pl.when(ahead < n)
        def _():
            fetch(ahead, lax.rem(ahead, MOE_COMBINE_BUFS)).start()

        mine = lane == exp_ref[base + i]
        dcol = jnp.sum(jnp.where(mine, dest, 0.0), axis=1, keepdims=True)
        gcol = jnp.sum(jnp.where(mine, gates, 0.0), axis=1, keepdims=True)
        col = lax.broadcasted_iota(jnp.int32, (MOE_TB, MOE_TR), 1) + tile_ref[base + i] * MOE_TR
        onehot = jnp.where(dcol == col.astype(F32), 1.0, 0.0).astype(BF16)
        fetch(i, slot).wait()
        acc_ref[...] += jnp.dot(onehot, buf_ref[slot], preferred_element_type=F32) * gcol
        return carry

    lax.fori_loop(0, n, item, 0)

    @pl.when(j + 1 < pl.num_programs(0))
    def _():
        prime(j + 1)

    o_ref[...] = _layer_norm(DEEPNORM_ALPHA * res_ref[...] + acc_ref[...], g_ref[...], b_ref[...])


def _combine_ln(plan, ys, dest, gates, res, g, b):
    tile, exp, off, cnt = plan
    n, d = res.shape
    tok = lambda j, *_: (j, 0)
    grid_spec = pltpu.PrefetchScalarGridSpec(
        num_scalar_prefetch=4,
        grid=(n // MOE_TB,),
        in_specs=[pl.BlockSpec(memory_space=pl.ANY),
                  pl.BlockSpec((MOE_TB, LANES), tok), pl.BlockSpec((MOE_TB, LANES), tok), pl.BlockSpec((MOE_TB, d), tok),
                  _resident((1, d)), _resident((1, d))],
        out_specs=pl.BlockSpec((MOE_TB, d), tok),
        scratch_shapes=[pltpu.VMEM((MOE_COMBINE_BUFS, MOE_TR, d), BF16), pltpu.SemaphoreType.DMA((MOE_COMBINE_BUFS,)),
                        pltpu.VMEM((MOE_TB, d), F32)],
    )
    return pl.pallas_call(
        _combine_kernel,
        grid_spec=grid_spec,
        out_shape=jax.ShapeDtypeStruct((n, d), F32),
        compiler_params=_params("arbitrary"),
        name="moe_combine_ln",
    )(tile, exp, off, cnt, ys, dest, gates, res, g, b)


def _moe_ln(x, xb, routing, w1, w3, w2, g, b):
    n = x.shape[0]
    gates, rank, rank_t, cnt = routing
    start, expert_tiles, by_tile, by_block = _moe_plan(cnt, n)
    startf = start.astype(F32)
    dest = jnp.where(rank >= 0.0, rank + jnp.pad(startf, (0, LANES - N_EXPERTS))[None, :], -1.0)
    dest_t = jnp.where(rank_t >= 0.0, rank_t + startf[:, None], -1.0)
    xs = _dispatch(by_tile, xb, dest_t)
    ys = _experts(expert_tiles, xs, w1, w3, w2, tf=1792)
    return _combine_ln(by_block, ys, dest, gates, x, g, b)


def _gla_kernel(q_ref, k_ref, v_ref, g_ref, gk_ref, gkw_ref, gkb_ref, nw_ref, sel_ref, o_ref, st_ref, bc_scr,
                qd_scr, kd_scr):
    @pl.when(pl.program_id(1) == 0)
    def _():
        st_ref[...] = jnp.zeros_like(st_ref)

    tt = q_ref.shape[0]
    tri = _tri(CHUNK)
    nw = nw_ref[...]

    pre = _mm_split(gk_ref[...], gkw_ref[...]) + gkb_ref[...]
    gk = (jnp.minimum(pre, 0.0) - jnp.log(1.0 + jnp.exp(-jnp.abs(pre)))) * (1.0 / GLA_GATE_NORM)
    bc_all = _mm_01_left(sel_ref[...], gk)
    bc_scr[...] = bc_all
    qd_scr[...] = q_ref[...] * (GLA_DK ** -0.5) * jnp.exp(bc_all)
    kd_scr[...] = k_ref[...] * jnp.exp(-bc_all)

    def chunk(c, carry):
        rows = pl.ds(pl.multiple_of(c * CHUNK, CHUNK), CHUNK)
        bc = bc_scr[rows, :]
        bl = bc[CHUNK - 1:CHUNK, :]
        qd = qd_scr[rows, :]
        kd = kd_scr[rows, :]
        kl = k_ref[rows, :] * jnp.exp(bl - bc)
        dec = jnp.exp(bl)
        heads = range(GLA_HEADS)
        sks = [slice(h * GLA_DK, (h + 1) * GLA_DK) for h in heads]
        svs = [slice(h * GLA_DV, (h + 1) * GLA_DV) for h in heads]
        att = [jnp.where(tri, _mm(qd[:, sk], kd[:, sk], _NT), 0.0) for sk in sks]
        vs = [v_ref[rows, sv] for sv in svs]
        sts = [st_ref[h] for h in heads]
        os_ = [_mm(att[h], vs[h]) + _mm(qd[:, sks[h]], sts[h], _NT) for h in heads]
        for h in heads:
            st_ref[h] = sts[h] * dec[:, sks[h]] + _mm(vs[h], kl[:, sks[h]], _TN)
        for h in heads:
            o = os_[h]
            ms = jnp.mean(o * o, axis=-1, keepdims=True)
            y = o * lax.rsqrt(ms + GLA_NORM_EPS) * nw * _silu(g_ref[rows, svs[h]])
            o_ref[rows, svs[h]] = y.astype(o_ref.dtype)
        return carry

    lax.fori_loop(0, tt // CHUNK, chunk, 0, unroll=2)


def _gla(p, gkw, gkb, nw, tt):
    bsz, t, _ = p.shape

    def col(name):
        off, w = EVEN_COLS[name]
        return pl.BlockSpec((None, tt, w), lambda b, i, j=off // w: (b, i, j))

    fix = lambda b, i: (0, 0)
    return pl.pallas_call(
        _gla_kernel,
        grid=(bsz, t // tt),
        in_specs=[col("q"), col("k"), col("v"), col("g"), col("gk"),
                  pl.BlockSpec(gkw.shape, fix), pl.BlockSpec(gkb.shape, fix), pl.BlockSpec(nw.shape, fix),
                  pl.BlockSpec((tt, tt), fix)],
        out_specs=pl.BlockSpec((None, tt, GLA_VW), lambda b, i: (b, i, 0)),
        out_shape=jax.ShapeDtypeStruct((bsz, t, GLA_VW), BF16),
        scratch_shapes=[pltpu.VMEM((GLA_HEADS, GLA_DV, GLA_DK), F32)] + [pltpu.VMEM((tt, GLA_QK), F32)] * 3,
        compiler_params=_params("parallel", "arbitrary"),
        name="gla",
    )(p, p, p, p, p, gkw, gkb, nw, _chunk_sel(tt, CHUNK))


def _rglru_kernel(x_ref, gate_ref, cw_ref, cb_ref, wa_ref, ba_ref, wx_ref, bx_ref, lam_ref, o_ref,
                  tail_ref, h_ref, a_scr, u_scr):
    @pl.when(pl.program_id(1) == 0)
    def _():
        tail_ref[...] = jnp.zeros_like(tail_ref)
        h_ref[...] = jnp.zeros_like(h_ref)

    tt = x_ref.shape[0]
    x = x_ref[...]
    cw = cw_ref[...]
    xc = x * cw[CONV_K - 1:CONV_K, :] + cb_ref[...]
    for s in range(1, CONV_K):
        xc += _shift_rows(x, tail_ref, s) * cw[CONV_K - 1 - s:CONV_K - s, :]
    tail_ref[...] = x[tt - SUBLANES:, :]

    r = _sigmoid(_mm(xc, wa_ref[...]) + ba_ref[...])
    i = _sigmoid(_mm(xc, wx_ref[...]) + bx_ref[...])
    log_a = -LRU_C * r * _softplus(-lam_ref[...])
    a_scr[...] = jnp.exp(log_a)
    u_scr[...] = jnp.sqrt(1.0 - jnp.exp(2.0 * log_a)) * (i * xc)

    row = lax.broadcasted_iota(jnp.int32, (SUBLANES, x.shape[1]), 0)

    def step(i, h):
        rows = pl.ds(pl.multiple_of(i * SUBLANES, SUBLANES), SUBLANES)
        a = a_scr[rows, :]
        u = u_scr[rows, :]
        for s in (1, 2, 4):
            u = u + a * jnp.where(row >= s, pltpu.roll(u, s, axis=0), 0.0)
            a = a * jnp.where(row >= s, pltpu.roll(a, s, axis=0), 1.0)
        hb = u + a * h
        u_scr[rows, :] = hb
        return hb[SUBLANES - 1:, :]

    h_ref[...] = lax.fori_loop(0, tt // SUBLANES, step, h_ref[...], unroll=4)
    gate = gate_ref[...]
    gelu = 0.5 * gate * (1.0 + jnp.tanh(math.sqrt(2.0 / math.pi) * (gate + 0.044715 * gate * gate * gate)))
    o_ref[...] = (u_scr[...] * gelu).astype(o_ref.dtype)


def _rglru(p, cw, cb, wa, ba, wx, bx, lam, tt):
    bsz, t, _ = p.shape
    w = LRU_WIDTH

    def col(name):
        off, _ = EVEN_COLS[name]
        return pl.BlockSpec((None, tt, w), lambda b, i, j=off // w: (b, i, j))

    fix = lambda b, i: (0, 0)
    vec = pl.BlockSpec((1, w), fix)
    return pl.pallas_call(
        _rglru_kernel,
        grid=(bsz, t // tt),
        in_specs=[col("xb"), col("gate"), pl.BlockSpec((CONV_K, w), fix), vec,
                  pl.BlockSpec((w, w), fix), vec, pl.BlockSpec((w, w), fix), vec, vec],
        out_specs=pl.BlockSpec((None, tt, w), lambda b, i: (b, i, 0)),
        out_shape=jax.ShapeDtypeStruct((bsz, t, w), BF16),
        scratch_shapes=[pltpu.VMEM((SUBLANES, w), F32), pltpu.VMEM((1, w), F32),
                        pltpu.VMEM((tt, w), F32), pltpu.VMEM((tt, w), F32)],
        compiler_params=_params("parallel", "arbitrary"),
        name="rglru",
    )(p, p, cw, cb, wa, ba, wx, bx, lam)


def _ssd_kernel(z_ref, x_ref, bm_ref, cm_ref, dt_ref, cwx_ref, cbx_ref, cwb_ref, cbb_ref, cwc_ref, cbc_ref,
                dtb_ref, alog_ref, dskip_ref, nw_ref, expand_ref, sel_ref, o_ref,
                tx_ref, tb_ref, tc_ref, st_ref, xs_scr, bs_scr, cs_scr, acs_scr, xdt_scr, xdec_scr, ea_scr):
    @pl.when(pl.program_id(1) == 0)
    def _():
        tx_ref[...] = jnp.zeros_like(tx_ref)
        tb_ref[...] = jnp.zeros_like(tb_ref)
        tc_ref[...] = jnp.zeros_like(tc_ref)
        st_ref[...] = jnp.zeros_like(st_ref)

    tt = x_ref.shape[0]

    def conv_silu(src_ref, tail_ref, cw_ref, cb_ref, dst_ref):
        x = src_ref[...]
        cw = cw_ref[...]
        y = x * cw[CONV_K - 1:CONV_K, :] + cb_ref[...]
        for s in range(1, CONV_K):
            y += _shift_rows(x, tail_ref, s) * cw[CONV_K - 1 - s:CONV_K - s, :]
        tail_ref[...] = x[tt - SUBLANES:, :]
        dst_ref[...] = _silu(y)

    conv_silu(x_ref, tx_ref, cwx_ref, cbx_ref, xs_scr)
    conv_silu(bm_ref, tb_ref, cwb_ref, cbb_ref, bs_scr)
    conv_silu(cm_ref, tc_ref, cwc_ref, cbc_ref, cs_scr)

    tri = _tri(CHUNK)
    expand = expand_ref[...]
    dskip = dskip_ref[...]
    nw = nw_ref[...]
    hpg = SSD_HEADS // SSD_GROUPS

    dtc = _softplus(dt_ref[...] + dtb_ref[...])
    da = dtc * -jnp.exp(alog_ref[...])
    sums = _mm_01_left(sel_ref[...], da)
    acs_all = sums[:tt]
    tot = sums[tt:]
    acs_scr[...] = acs_all
    xdt_all = xs_scr[...] * _mm_01(dtc, expand)
    xdt_scr[...] = xdt_all
    xdec_scr[...] = xdt_all * _mm_01(jnp.exp(tot - acs_all), expand)
    ea_scr[...] = _mm_01(jnp.exp(acs_all), expand)

    def chunk(c, carry):
        rows = pl.ds(pl.multiple_of(c * CHUNK, CHUNK), CHUNK)
        acs = acs_scr[rows, :]
        acs_t = acs.T
        ea_x = ea_scr[rows, :]
        x = xs_scr[rows, :]
        xdt = xdt_scr[rows, :]
        xdec = xdec_scr[rows, :]
        cd_x = ea_x[CHUNK - 1:CHUNK, :]
        ys = []
        for g in range(SSD_GROUPS):
            sg = slice(g * SSD_GROUP_WIDTH, (g + 1) * SSD_GROUP_WIDTH)
            ss = slice(g * SSD_STATE, (g + 1) * SSD_STATE)
            bg = bs_scr[rows, ss]
            cg = cs_scr[rows, ss]
            cb = _mm(cg, bg, _NT)
            st = st_ref[g]
            yg = _mm(cg, st) * ea_x[:, sg]
            st_ref[g] = st * cd_x[:, sg] + _mm(bg, xdec[:, sg], _TN)
            yh = []
            for j in range(hpg):
                h = g * hpg + j
                seg = acs[:, h:h + 1] - acs_t[h:h + 1, :]
                lmat = jnp.exp(jnp.where(tri, seg, -jnp.inf))
                sh = slice(h * SSD_HEADDIM, (h + 1) * SSD_HEADDIM)
                yh.append(_mm(cb * lmat, xdt[:, sh]))
            ys.append(yg + jnp.concatenate(yh, axis=1))
        y = jnp.concatenate(ys, axis=1) + x * dskip
        y = y * _silu(z_ref[rows, :])
        outs = []
        for g in range(SSD_GROUPS):
            sg = slice(g * SSD_GROUP_WIDTH, (g + 1) * SSD_GROUP_WIDTH)
            yg = y[:, sg]
            ms = jnp.mean(yg * yg, axis=-1, keepdims=True)
            outs.append(yg * lax.rsqrt(ms + SSD_NORM_EPS))
        o_ref[rows, :] = (jnp.concatenate(outs, axis=1) * nw).astype(o_ref.dtype)
        return carry

    lax.fori_loop(0, tt // CHUNK, chunk, 0, unroll=2)


def _ssd(p, cw, cb, dtb, alog, dskip_x, nw, expand, tt):
    bsz, t, _ = p.shape

    def col(name):
        off, w = ODD_COLS[name]
        return pl.BlockSpec((None, tt, w), lambda b, i, j=off // w: (b, i, j))

    fix = lambda b, i: (0, 0)
    full = lambda a: pl.BlockSpec(a.shape, fix)
    gs = SSD_GROUPS * SSD_STATE
    cwx, cwb, cwc = cw[:, :SSD_INNER], cw[:, SSD_INNER:SSD_INNER + gs], cw[:, SSD_INNER + gs:]
    cbx, cbb, cbc = cb[:, :SSD_INNER], cb[:, SSD_INNER:SSD_INNER + gs], cb[:, SSD_INNER + gs:]
    sel = jnp.concatenate([_chunk_sel(tt, CHUNK), _chunk_sel(tt, CHUNK, "all")], axis=0)
    args = (cwx, cbx, cwb, cbb, cwc, cbc, dtb, alog, dskip_x, nw, expand, sel)
    return pl.pallas_call(
        _ssd_kernel,
        grid=(bsz, t // tt),
        in_specs=[col("z"), col("x"), col("bm"), col("cm"), col("dt")] + [full(a) for a in args],
        out_specs=pl.BlockSpec((None, tt, SSD_INNER), lambda b, i: (b, i, 0)),
        out_shape=jax.ShapeDtypeStruct((bsz, t, SSD_INNER), BF16),
        scratch_shapes=[pltpu.VMEM((SUBLANES, SSD_INNER), F32), pltpu.VMEM((SUBLANES, gs), F32),
                        pltpu.VMEM((SUBLANES, gs), F32),
                        pltpu.VMEM((SSD_GROUPS, SSD_STATE, SSD_GROUP_WIDTH), F32),
                        pltpu.VMEM((tt, SSD_INNER), F32), pltpu.VMEM((tt, gs), F32), pltpu.VMEM((tt, gs), F32),
                        pltpu.VMEM((tt, LANES), F32)] + [pltpu.VMEM((tt, SSD_INNER), F32)] * 3,
        compiler_params=_params("parallel", "arbitrary"),
        name="ssd",
    )(p, p, p, p, p, *args)


def _rwkv_kernel(r_ref, k_ref, v_ref, wa_ref, xg_ref, mur_ref, muk_ref, muv_ref, muwa_ref, mug_ref,
                 w0_ref, w2_ref, a0_ref, a2_ref, g2_ref, kk_ref, ka_ref, rk_ref, lng_ref, lnb_ref, ones_ref, sel_ref,
                 o_ref, tr_ref, tk_ref, tv_ref, twa_ref, tg_ref, st_ref, y_scr,
                 at_scr, rt_scr, bt_scr, kt_scr, bh_scr, kh_scr, v_scr, wc_scr):
    @pl.when(pl.program_id(1) == 0)
    def _():
        for ref in (tr_ref, tk_ref, tv_ref, twa_ref, tg_ref, st_ref):
            ref[...] = jnp.zeros_like(ref)

    tt = r_ref.shape[0]
    hd = RWKV_HEADDIM

    def mix(p_ref, tail_ref, mu_ref):
        p = p_ref[...]
        prev = _shift_rows(p, tail_ref, 1)
        tail_ref[...] = p[tt - SUBLANES:, :]
        return p + (prev - p) * mu_ref[...]

    r = mix(r_ref, tr_ref, mur_ref)
    k = mix(k_ref, tk_ref, muk_ref)
    v = mix(v_ref, tv_ref, muv_ref)
    xwa = mix(wa_ref, twa_ref, muwa_ref)
    xg = mix(xg_ref, tg_ref, mug_ref)

    lane = lax.broadcasted_iota(jnp.int32, xwa.shape, 1)
    lora_in = jnp.where(lane < RWKV_DECAY_LORA, jnp.tanh(xwa), xwa)
    w_log = -_softplus(-(w0_ref[...] + _mm_split(lora_in, w2_ref[...]))) - 0.5
    lw = -jnp.exp(w_log)
    a_sig = _sigmoid(a0_ref[...] + _mm_split(lora_in, a2_ref[...]))
    gate = _mm_split(_sigmoid(xg), g2_ref[...])
    ones_bd = ones_ref[...]
    kk = k * kk_ref[...]
    kk = kk / jnp.maximum(jnp.sqrt(_mm_01(kk * kk, ones_bd)), 1e-12)
    k = k * (1.0 + (a_sig - 1.0) * ka_ref[...])

    cs = RWKV_CHUNK
    tri_incl = _tri(cs)
    tri_strict = _tri(cs, strict=True)
    tri_col = jnp.concatenate([tri_strict, tri_incl], axis=0)
    eye = (tri_incl & ~tri_strict).astype(F32)

    sums = _mm_01_left(sel_ref[...], lw)
    cum = sums[:tt]
    tot = sums[tt:]
    bvec = kk * a_sig
    e_neg = jnp.exp(-cum)
    e_last = jnp.exp(tot - cum)
    for ref, val in ((at_scr, -kk * jnp.exp(cum - lw)), (rt_scr, r * jnp.exp(cum)), (bt_scr, bvec * e_neg),
                     (kt_scr, k * e_neg), (bh_scr, bvec * e_last), (kh_scr, k * e_last), (v_scr, v),
                     (wc_scr, jnp.exp(tot))):
        ref[...] = val

    def chunk(c, carry):
        rows = pl.ds(pl.multiple_of(c * cs, cs), cs)
        at = at_scr[rows, :]
        rt = rt_scr[rows, :]
        bt = bt_scr[rows, :]
        kt = kt_scr[rows, :]
        bh = bh_scr[rows, :]
        kh = kh_scr[rows, :]
        vc = v_scr[rows, :]
        wc = wc_scr[pl.ds(pl.multiple_of(c * cs, cs), 1), :]
        heads = range(RWKV_HEADS)
        sls = [slice(h * hd, (h + 1) * hd) for h in heads]
        amat = [_mm(jnp.concatenate([at[:, sl], rt[:, sl]], axis=0),
                    jnp.concatenate([bt[:, sl], kt[:, sl]], axis=0), _NT) for sl in sls]
        a_ab = [jnp.where(tri_strict, m[:cs, :cs], 0.0) for m in amat]
        a_rb = [jnp.where(tri_incl, m[cs:, :cs], 0.0) for m in amat]
        a_xk = [jnp.where(tri_col, m[:, cs:], 0.0) for m in amat]
        avrv = [_mm(a_xk[h], vc[:, sls[h]]) for h in heads]
        inv = [eye + m for m in a_ab]
        pw = [_mm(m, m) for m in a_ab]
        for _ in range(int(math.log2(cs)) - 2):
            both = [_mm(jnp.concatenate([inv[h], pw[h]], axis=0), pw[h]) for h in heads]
            inv = [inv[h] + both[h][:cs] for h in heads]
            pw = [m[cs:] for m in both]
        inv = [inv[h] + _mm(inv[h], pw[h]) for h in heads]
        tu = [_mm(inv[h], jnp.concatenate([at[:, sls[h]], avrv[h][:cs]], axis=1)) for h in heads]
        qy = [_mm(a_rb[h], tu[h]) + jnp.concatenate([rt[:, sls[h]], avrv[h][cs:]], axis=1)
              for h in heads]
        zb =[_mm(tu[h], bh[:, sls[h]], _TN) for h in heads]
        vk = [_mm(vc[:, sls[h]], kh[:, sls[h]], _TN) for h in heads]
        for h in heads:
            st = st_ref[h]
            y_scr[rows, sls[h]] = _mm(qy[h][:, :hd], st, _NT) + qy[h][:, hd:]
            st_ref[h] = st * wc[:, sls[h]] + _mm(st, zb[h][:hd]) + zb[h][hd:] + vk[h]
        return carry

    lax.fori_loop(0, tt // cs, chunk, 0, unroll=2)

    y = y_scr[...]
    inv_n = 1.0 / hd
    mu_y = _mm_01(y, ones_bd) * inv_n
    dy = y - mu_y
    var_y = _mm_01(dy * dy, ones_bd) * inv_n
    yn = dy * lax.rsqrt(var_y + RWKV_GN_EPS) * lng_ref[...] + lnb_ref[...]
    bonus = _mm_01(r * k * rk_ref[...], ones_bd) * v
    o_ref[...] = ((yn + bonus) * gate).astype(o_ref.dtype)


def _rwkv(p, mu, w0, w2p, a0, a2p, g2, k_k, k_a, r_k, ln_g, ln_b, ones_bd, tt):
    bsz, t, _ = p.shape
    w = RWKV_WIDTH

    def col(name):
        off, wd = ODD_COLS[name]
        return pl.BlockSpec((None, tt, wd), lambda b, i, j=off // wd: (b, i, j))

    fix = lambda b, i: (0, 0)
    full = lambda a: pl.BlockSpec(a.shape, fix)
    mur, muk, muv = mu[:, :w], mu[:, w:2 * w], mu[:, 2 * w:3 * w]
    muwa, mug = mu[:, 3 * w:3 * w + LANES], mu[:, 3 * w + LANES:]
    sel = jnp.concatenate([_chunk_sel(tt, RWKV_CHUNK), _chunk_sel(tt, RWKV_CHUNK, "all")], axis=0)
    args = (mur, muk, muv, muwa, mug, w0, w2p, a0, a2p, g2, k_k, k_a, r_k, ln_g, ln_b, ones_bd, sel)
    return pl.pallas_call(
        _rwkv_kernel,
        grid=(bsz, t // tt),
        in_specs=[col("r"), col("k"), col("v"), col("wa"), col("xg")] + [full(a) for a in args],
        out_specs=pl.BlockSpec((None, tt, w), lambda b, i: (b, i, 0)),
        out_shape=jax.ShapeDtypeStruct((bsz, t, w), BF16),
        scratch_shapes=[pltpu.VMEM((SUBLANES, w), F32), pltpu.VMEM((SUBLANES, w), F32),
                        pltpu.VMEM((SUBLANES, w), F32), pltpu.VMEM((SUBLANES, LANES), F32),
                        pltpu.VMEM((SUBLANES, LANES), F32),
                        pltpu.VMEM((RWKV_HEADS, RWKV_HEADDIM, RWKV_HEADDIM), F32)]
        + [pltpu.VMEM((tt, w), F32)] * 9,
        compiler_params=_params("parallel", "arbitrary"),
        name="rwkv7",
    )(p, p, p, p, p, *args)


def _reorder_cols(w, pieces, order, width):
    parts, pos = [], 0
    for name, (off, slot) in order.items():
        assert off == pos, "slots must be listed in order and contiguous"
        start, size = pieces[name]
        parts.append(w[:, start:start + size])
        if size < slot:
            parts.append(jnp.zeros((w.shape[0], slot - size), w.dtype))
        pos += slot
    assert pos == width
    return jnp.concatenate(parts, axis=1)


def _block_diag(w):
    n, i, j = w.shape
    eye = jnp.eye(n, dtype=w.dtype)
    return (eye[:, None, :, None] * w[:, :, None, :]).reshape(n * i, n * j)


def _row(v, width=None):
    v = v.reshape(1, -1).astype(F32)
    if width is not None and v.shape[1] < width:
        v = jnp.pad(v, ((0, 0), (0, width - v.shape[1])))
    return v


def _even_layer(x, x_in, w_in, gk_w2, gk_b, gla_norm, conv_w, conv_b, wa, ba, wx, bx, lam,
                w_out, ln1_g, ln1_b, f_w1, f_w3, f_w2, ln2_g, ln2_b, bsz, t):
    qk, vw, lw = GLA_QK, GLA_VW, LRU_WIDTH
    pieces = dict(q=(0, qk), k=(qk, qk), v=(2 * qk, vw), g=(2 * qk + vw, vw), gk=(2 * qk + 2 * vw, GLA_GK_RANK),
                  xb=(2 * qk + 2 * vw + GLA_GK_RANK, lw), gate=(2 * qk + 2 * vw + GLA_GK_RANK + lw, lw))
    w_in_r = _reorder_cols(w_in, pieces, EVEN_COLS, EVEN_WIDTH).astype(BF16)
    p = _proj(x_in, w_in_r, tm=min(512, x.shape[0])).reshape(bsz, t, EVEN_WIDTH)
    tt = min(512, t)
    gkw = jnp.pad(gk_w2, ((0, LANES - GLA_GK_RANK), (0, 0)))
    y_a = _gla(p, gkw, _row(gk_b), _row(gla_norm), tt)
    y_b = _rglru(p, conv_w, _row(conv_b), _block_diag(wa).astype(BF16), _row(ba), _block_diag(wx).astype(BF16),
                 _row(bx), _row(lam), tt)
    n = bsz * t
    tm = min(512, n)
    return _mix_ffn(y_a.reshape(n, vw), y_b.reshape(n, lw), w_out[:vw].astype(BF16), w_out[vw:].astype(BF16), x,
                    _row(ln1_g), _row(ln1_b), f_w1.astype(BF16), f_w3.astype(BF16), f_w2.astype(BF16),
                    _row(ln2_g), _row(ln2_b), tm)


def _odd_layer(x, xb, w_in, conv_w, conv_b, dt_bias, a_log, d_skip, ssd_norm, mu, w0, w2, a0, a2, g2, k_k, k_a,
               r_k, rln_g, rln_b, w_out, ln1_g, ln1_b, router, ew1, ew3, ew2, ln2_g, ln2_b, bsz, t):
    si, gs, rw = SSD_INNER, SSD_GROUPS * SSD_STATE, RWKV_WIDTH
    o = 2 * si + 2 * gs + SSD_HEADS
    pieces = dict(z=(0, si), x=(si, si), bm=(2 * si, gs), cm=(2 * si + gs, gs), dt=(2 * si + 2 * gs, SSD_HEADS),
                  r=(o, rw), k=(o + rw, rw), v=(o + 2 * rw, rw),
                  wa=(o + 3 * rw, RWKV_DECAY_LORA + RWKV_AAA_LORA), xg=(o + 3 * rw + LANES, RWKV_GATE_LORA))
    w_in_r = _reorder_cols(w_in, pieces, ODD_COLS, ODD_WIDTH).astype(BF16)
    n = bsz * t
    p = _proj(xb, w_in_r, tm=min(512, n)).reshape(bsz, t, ODD_WIDTH)
    expand = jnp.pad(jnp.repeat(jnp.eye(SSD_HEADS, dtype=BF16), SSD_HEADDIM, axis=1), ((0, LANES - SSD_HEADS), (0, 0)))
    y_c = _ssd(p, conv_w, _row(conv_b), _row(dt_bias, LANES), _row(a_log, LANES),
               _row(jnp.repeat(d_skip, SSD_HEADDIM)), _row(ssd_norm), expand, tt=min(256, t))
    zeros = jnp.zeros((RWKV_DECAY_LORA, rw), F32)
    ones_bd = _block_diag(jnp.ones((RWKV_HEADS, RWKV_HEADDIM, RWKV_HEADDIM), BF16))
    y_d = _rwkv(p, _row(mu), _row(w0), jnp.concatenate([w2, zeros]), _row(a0), jnp.concatenate([zeros, a2]), g2,
                _row(k_k), _row(k_a), _row(r_k), _row(rln_g), _row(rln_b), ones_bd, tt=min(256, t))
    x1, x1b, *routing = _mix_router(y_c.reshape(n, si), y_d.reshape(n, rw), w_out[:si].astype(BF16),
                                    w_out[si:].astype(BF16), x, _row(ln1_g), _row(ln1_b),
                                    jnp.pad(router, ((0, 0), (0, LANES - N_EXPERTS))))
    y = _moe_ln(x1, x1b, routing, ew1.astype(BF16), ew3.astype(BF16), ew2.astype(BF16), _row(ln2_g), _row(ln2_b))
    return y, None


def kernel(x, e_w_in, e_gk_w2, e_gk_b, e_gla_norm, e_conv_w, e_conv_b, e_lru_wa, e_lru_ba, e_lru_wx, e_lru_bx, e_lru_lambda, e_w_out, e_ln1_g, e_ln1_b, e_ffn_w1, e_ffn_w3, e_ffn_w2, e_ln2_g, e_ln2_b, o_w_in, o_conv_w, o_conv_b, o_dt_bias, o_a_log, o_d_skip, o_ssd_norm, o_rwkv_mu, o_rwkv_w0, o_rwkv_w2, o_rwkv_a0, o_rwkv_a2, o_rwkv_g2, o_rwkv_k_k, o_rwkv_k_a, o_rwkv_r_k, o_rwkv_ln_g, o_rwkv_ln_b, o_w_out, o_ln1_g, o_ln1_b, o_router, o_exp_w1, o_exp_w3, o_exp_w2, o_ln2_g, o_ln2_b):
    bsz, t, d = x.shape
    h = x.reshape(bsz * t, d)
    hb = h
    for i in range(DEPTH):
        j = i // 2
        if i % 2 == 0:
            h, hb = _even_layer(h, hb, e_w_in[j], e_gk_w2[j], e_gk_b[j], e_gla_norm[j], e_conv_w[j], e_conv_b[j],
                                e_lru_wa[j], e_lru_ba[j], e_lru_wx[j], e_lru_bx[j], e_lru_lambda[j], e_w_out[j],
                                e_ln1_g[j], e_ln1_b[j], e_ffn_w1[j], e_ffn_w3[j], e_ffn_w2[j], e_ln2_g[j],
                                e_ln2_b[j], bsz, t)
        else:
            h, hb = _odd_layer(h, hb, o_w_in[j], o_conv_w[j], o_conv_b[j], o_dt_bias[j], o_a_log[j], o_d_skip[j],
                               o_ssd_norm[j], o_rwkv_mu[j], o_rwkv_w0[j], o_rwkv_w2[j], o_rwkv_a0[j],
                               o_rwkv_a2[j], o_rwkv_g2[j], o_rwkv_k_k[j], o_rwkv_k_a[j], o_rwkv_r_k[j],
                               o_rwkv_ln_g[j], o_rwkv_ln_b[j], o_w_out[j], o_ln1_g[j], o_ln1_b[j], o_router[j],
                               o_exp_w1[j], o_exp_w3[j], o_exp_w2[j], o_ln2_g[j], o_ln2_b[j], bsz, t)
    return h.reshape(bsz, t, d)
```

```python
import math

import jax
import jax.numpy as jnp
from jax import lax
from jax.experimental import pallas as pl
from jax.experimental.pallas import tpu as pltpu

F32 = jnp.float32
BF16 = jnp.bfloat16

D_MODEL = 1024
DEPTH = 2
DEEPNORM_ALPHA = (2 * DEPTH) ** 0.25
LN_EPS = 1e-5
CONV_K = 4
CHUNK = 64

GLA_HEADS = 4
GLA_DK = 64
GLA_DV = 128
GLA_QK = GLA_HEADS * GLA_DK
GLA_VW = GLA_HEADS * GLA_DV
GLA_GK_RANK = 16
GLA_GATE_NORM = 16.0
GLA_NORM_EPS = 1e-5

LRU_WIDTH = 512
LRU_BLOCKS = 8
LRU_C = 8.0

SSD_HEADS = 16
SSD_HEADDIM = 64
SSD_INNER = SSD_HEADS * SSD_HEADDIM
SSD_GROUPS = 2
SSD_STATE = 128
SSD_GROUP_WIDTH = SSD_INNER // SSD_GROUPS
SSD_NORM_EPS = 1e-5

RWKV_HEADS = 8
RWKV_HEADDIM = 64
RWKV_WIDTH = RWKV_HEADS * RWKV_HEADDIM
RWKV_DECAY_LORA = 64
RWKV_AAA_LORA = 64
RWKV_GATE_LORA = 128
RWKV_GN_EPS = 64e-5
RWKV_CHUNK = 128

N_EXPERTS = 8
LANES = 128
SUBLANES = 8
assert N_EXPERTS == SUBLANES
ROUTER_ROWS = 16
MOE_TB = 512
MOE_TR = 256
MOE_TE = 512
assert MOE_TE % MOE_TR == 0
MOE_COMBINE_BUFS = 4
VMEM_LIMIT = 56 * 1024 * 1024

ROW_TILE = 512
GLA_RGLRU_TIME_TILE = 512
SSD_RWKV_TIME_TILE = 256
EXPERT_F_TILE = 1792

EVEN_COLS = dict(v=(0, 512), g=(512, 512), xb=(1024, 512), gate=(1536, 512), q=(2048, 256), k=(2304, 256),
                 gk=(2560, 128))
EVEN_WIDTH = 2688
ODD_COLS = dict(z=(0, 1024), x=(1024, 1024), r=(2048, 512), k=(2560, 512), v=(3072, 512), bm=(3584, 256),
                cm=(3840, 256), wa=(4096, 128), xg=(4224, 128), dt=(4352, 128))
ODD_WIDTH = 4480


def _mm(a, b, dims=((1,), (0,))):
    return lax.dot_general(a.astype(BF16), b.astype(BF16), (dims, ((), ())), preferred_element_type=F32)


def _mm_split(a, b, dims=((1,), (0,))):
    a_hi = a.astype(BF16)
    b_hi = b.astype(BF16)
    a_lo = (a - a_hi.astype(F32)).astype(BF16)
    b_lo = (b - b_hi.astype(F32)).astype(BF16)
    return _mm(a_hi, b_hi, dims) + _mm(a_lo, b_hi, dims) + _mm(a_hi, b_lo, dims)


_NT = ((1,), (1,))
_TN = ((0,), (0,))


def _mm_01(x, sel):
    hi = x.astype(BF16)
    lo = (x - hi.astype(F32)).astype(BF16)
    return (jnp.dot(hi, sel, preferred_element_type=F32) + jnp.dot(lo, sel, preferred_element_type=F32))


def _sigmoid(x):
    return 1.0 / (1.0 + jnp.exp(-x))


def _softplus(x):
    return jnp.maximum(x, 0.0) + jnp.log(1.0 + jnp.exp(-jnp.abs(x)))


def _silu(x):
    return x * _sigmoid(x)


def _layer_norm(h, g, b):
    mu = jnp.mean(h, axis=-1, keepdims=True)
    d = h - mu
    var = jnp.mean(d * d, axis=-1, keepdims=True)
    return d * lax.rsqrt(var + LN_EPS) * g + b


def _params(*sem):
    return pltpu.CompilerParams(dimension_semantics=sem, vmem_limit_bytes=VMEM_LIMIT)


def _tri(n, strict=False):
    r = lax.broadcasted_iota(jnp.int32, (n, n), 0)
    c = lax.broadcasted_iota(jnp.int32, (n, n), 1)
    return (r > c) if strict else (r >= c)


def _chunk_sel(n, chunk, kind="incl"):
    r = jnp.arange(n)[:, None]
    c = jnp.arange(n)[None, :]
    first = r - r % chunk
    upper = r if kind == "incl" else first + (chunk - 1)
    return ((c >= first) & (c <= upper)).astype(BF16)


def _mm_01_left(sel, x):
    hi = x.astype(BF16)
    lo = (x - hi.astype(F32)).astype(BF16)
    return jnp.dot(sel, hi, preferred_element_type=F32) + jnp.dot(sel, lo, preferred_element_type=F32)


def _shift_rows(p, tail_ref, s):
    n = p.shape[0]
    rolled = pltpu.roll(p, s, axis=0)
    head = pltpu.roll(tail_ref[...], s, axis=0)
    row = lax.broadcasted_iota(jnp.int32, (SUBLANES, p.shape[1]), 0)
    fixed = jnp.where(row < s, head, rolled[:SUBLANES])
    return jnp.concatenate([fixed, rolled[SUBLANES:]], axis=0) if n > SUBLANES else fixed


def _proj_kernel(x_ref, w_ref, o_ref):
    o_ref[...] = jnp.dot(x_ref[...].astype(BF16), w_ref[...], preferred_element_type=F32)


def _resident(shape):
    return pl.BlockSpec(shape, lambda *_: (0,) * len(shape), pipeline_mode=pl.Buffered(1))


def _proj(x, w, tm):
    m, k = x.shape
    n = w.shape[1]
    return pl.pallas_call(
        _proj_kernel,
        grid=(m // tm,),
        in_specs=[pl.BlockSpec((tm, k), lambda i: (i, 0)), _resident((k, n))],
        out_specs=pl.BlockSpec((tm, n), lambda i: (i, 0)),
        out_shape=jax.ShapeDtypeStruct((m, n), F32),
        compiler_params=_params("parallel"),
        name="in_proj",
    )(x, w)


def _outproj_ln_value(ya_ref, yb_ref, wa_ref, wb_ref, res_ref, g_ref, b_ref):
    acc = jnp.dot(ya_ref[...], wa_ref[...], preferred_element_type=F32)
    acc += jnp.dot(yb_ref[...], wb_ref[...], preferred_element_type=F32)
    return _layer_norm(DEEPNORM_ALPHA * res_ref[...] + acc, g_ref[...], b_ref[...])


def _mix_ffn_kernel(ya_ref, yb_ref, wa_ref, wb_ref, res_ref, g1_ref, b1_ref, w1_ref, w3_ref, w2_ref, g2_ref, b2_ref,
                    o_ref, ob_ref):
    x1 = _outproj_ln_value(ya_ref, yb_ref, wa_ref, wb_ref, res_ref, g1_ref, b1_ref)
    xb = x1.astype(BF16)
    h1 = jnp.dot(xb, w1_ref[...], preferred_element_type=F32)
    h3 = jnp.dot(xb, w3_ref[...], preferred_element_type=F32)
    ff = jnp.dot((_silu(h1) * h3).astype(BF16), w2_ref[...], preferred_element_type=F32)
    y = _layer_norm(DEEPNORM_ALPHA * x1 + ff, g2_ref[...], b2_ref[...])
    o_ref[...] = y
    ob_ref[...] = y.astype(BF16)


def _mix_ffn(ya, yb, wa, wb, res, g1, b1, w1, w3, w2, g2, b2, tm):
    m, d = res.shape
    row = lambda i: (i, 0)
    consts = (wa, wb, None, g1, b1, w1, w3, w2, g2, b2)
    return pl.pallas_call(
        _mix_ffn_kernel,
        grid=(m // tm,),
        in_specs=[pl.BlockSpec((tm, ya.shape[1]), row), pl.BlockSpec((tm, yb.shape[1]), row)]
        + [pl.BlockSpec((tm, d), row) if c is None else _resident(c.shape) for c in consts],
        out_specs=[pl.BlockSpec((tm, d), row), pl.BlockSpec((tm, d), row)],
        out_shape=[jax.ShapeDtypeStruct((m, d), F32), jax.ShapeDtypeStruct((m, d), BF16)],
        compiler_params=_params("parallel"),
        name="out_proj_ffn_ln",
    )(ya, yb, wa, wb, res, g1, b1, w1, w3, w2, g2, b2)


def _mix_router_kernel(ya_ref, yb_ref, wa_ref, wb_ref, res_ref, g_ref, b_ref, w_ref, before_ref, o_ref, ob_ref,
                       gates_ref, rank_ref, rank_t_ref, cnt_ref, carry_ref):
    @pl.when(pl.program_id(0) == 0)
    def _():
        carry_ref[...] = jnp.zeros_like(carry_ref)

    x1 = _outproj_ln_value(ya_ref, yb_ref, wa_ref, wb_ref, res_ref, g_ref, b_ref)
    o_ref[...] = x1
    ob_ref[...] = x1.astype(BF16)
    logits = _mm_split(w_ref[...], x1, _NT)
    sub = lax.broadcasted_iota(jnp.int32, logits.shape, 0)
    neg = jnp.float32(-jnp.inf)
    l1 = jnp.where(sub < N_EXPERTS, logits, neg)
    m1 = jnp.max(l1, axis=0, keepdims=True)
    i1 = jnp.min(jnp.where(l1 == m1, sub, ROUTER_ROWS), axis=0, keepdims=True)
    l2 = jnp.where(sub == i1, neg, l1)
    m2 = jnp.max(l2, axis=0, keepdims=True)
    i2 = jnp.min(jnp.where(l2 == m2, sub, ROUTER_ROWS), axis=0, keepdims=True)
    ex = jnp.exp(m2 - m1)
    w_top = 1.0 / (1.0 + ex)
    gates_t = jnp.where(sub == i1, w_top, 0.0) + jnp.where(sub == i2, ex * w_top, 0.0)

    sel = jnp.where(sub == i1, 1.0, 0.0) + jnp.where(sub == i2, 1.0, 0.0)
    before = _mm(sel, before_ref[...])
    carry = carry_ref[...]
    rank_t = jnp.where(sel > 0.0, carry[:, :1] + before, -1.0)
    rank_t_ref[...] = rank_t[:SUBLANES, :]
    carry = carry + jnp.sum(sel, axis=1, keepdims=True)
    carry_ref[...] = carry
    cnt_ref[...] = carry[:SUBLANES, :]
    pad = jnp.zeros((LANES - ROUTER_ROWS, sel.shape[1]), F32)
    gates_ref[...] = jnp.concatenate([gates_t, pad], axis=0).T
    rank_ref[...] = jnp.concatenate([rank_t, pad], axis=0).T


def _mix_router(ya, yb, wa, wb, res, g, b, router):
    m, d = res.shape
    nblk = m // MOE_TB
    row = lambda i: (i, 0)
    w = jnp.pad(router.T, ((0, ROUTER_ROWS - N_EXPERTS), (0, 0)))
    earlier = (jnp.arange(MOE_TB)[:, None] < jnp.arange(MOE_TB)[None, :]).astype(BF16)
    return pl.pallas_call(
        _mix_router_kernel,
        grid=(nblk,),
        in_specs=[pl.BlockSpec((MOE_TB, ya.shape[1]), row), pl.BlockSpec((MOE_TB, yb.shape[1]), row),
                  _resident(wa.shape), _resident(wb.shape), pl.BlockSpec((MOE_TB, d), row),
                  _resident((1, d)), _resident((1, d)), _resident(w.shape), _resident(earlier.shape)],
        out_specs=[pl.BlockSpec((MOE_TB, d), row), pl.BlockSpec((MOE_TB, d), row),
                   pl.BlockSpec((MOE_TB, LANES), row), pl.BlockSpec((MOE_TB, LANES), row),
                   pl.BlockSpec((SUBLANES, MOE_TB), lambda i: (0, i)), pl.BlockSpec((SUBLANES, LANES), row)],
        out_shape=[jax.ShapeDtypeStruct((m, d), F32), jax.ShapeDtypeStruct((m, d), BF16),
                   jax.ShapeDtypeStruct((m, LANES), F32), jax.ShapeDtypeStruct((m, LANES), F32),
                   jax.ShapeDtypeStruct((SUBLANES, m), F32), jax.ShapeDtypeStruct((nblk * SUBLANES, LANES), F32)],
        scratch_shapes=[pltpu.VMEM((ROUTER_ROWS, LANES), F32)],
        compiler_params=_params("arbitrary"),
        name="out_proj_ln_router",
    )(ya, yb, wa, wb, res, g, b, w, earlier)


def _moe_plan(cnt, n_tok):
    i32 = jnp.int32
    nblk = n_tok // MOE_TB
    c_inc = cnt.reshape(nblk, SUBLANES, LANES)[:, :N_EXPERTS, 0].astype(i32)
    c_exc = jnp.concatenate([jnp.zeros((1, N_EXPERTS), i32), c_inc[:-1]], axis=0)
    gsz = (c_inc[-1] + MOE_TE - 1) // MOE_TE * MOE_TE
    gend = jnp.cumsum(gsz)
    start = gend - gsz
    n_et = _moe_rows(n_tok) // MOE_TE
    n_act = gend[-1] // MOE_TE
    et = jnp.minimum(jnp.arange(n_et, dtype=i32), jnp.maximum(n_act - 1, 0))
    et_expert = jnp.sum((gend[None, :] <= (et * MOE_TE)[:, None]).astype(i32), axis=1)
    et_active = (jnp.arange(n_et, dtype=i32) < n_act).astype(i32)

    lo = start[None, :] + c_exc
    hi = start[None, :] + c_inc
    first_tile = lo // MOE_TR
    n_items = jnp.where(hi > lo, (hi - 1) // MOE_TR - first_tile + 1, 0)
    wmax = _moe_rows(n_tok) // MOE_TR + N_EXPERTS * nblk
    exp_id = jnp.broadcast_to(jnp.arange(N_EXPERTS, dtype=i32)[None, :], (nblk, N_EXPERTS))

    n, ft, exp = (a.reshape(-1) for a in (n_items, first_tile, exp_id))
    inc = jnp.cumsum(n)
    wc = jnp.minimum(jnp.arange(wmax, dtype=i32), jnp.maximum(inc[-1] - 1, 0))
    idx = jnp.sum((inc[None, :] <= wc[:, None]).astype(i32), axis=1)
    per_block = jnp.sum(n_items, axis=1)
    by_block = (ft[idx] + wc - (inc - n)[idx], exp[idx], jnp.cumsum(per_block) - per_block, per_block)

    row0 = jnp.arange(_moe_rows(n_tok) // MOE_TR, dtype=i32) * MOE_TR
    group = jnp.minimum(jnp.sum((gend[None, :] <= row0[:, None]).astype(i32), axis=1), N_EXPERTS - 1)
    rank_lo = row0 - start[group]
    count = c_inc[-1][group]
    rank_hi = jnp.minimum(rank_lo + MOE_TR, count)
    holds = (rank_lo < count) & (row0 < gend[-1])
    blk_lo = jnp.sum((c_inc[:, group].T <= rank_lo[:, None]).astype(i32), axis=1)
    blk_hi = jnp.sum((c_exc[:, group].T < rank_hi[:, None]).astype(i32), axis=1)
    by_tile = (group, jnp.minimum(blk_lo, nblk - 1), jnp.where(holds, blk_hi - blk_lo, 0))

    return start, (et, et_expert, et_active), by_tile, by_block


def _moe_rows(n_tok):
    return 2 * n_tok + N_EXPERTS * MOE_TE


def _one_hot_rows(dest, tile):
    row = lax.broadcasted_iota(jnp.int32, (MOE_TR, MOE_TB), 0) + tile * MOE_TR
    return jnp.where(dest == row.astype(F32), 1.0, 0.0).astype(BF16)


def _dispatch_kernel(exp_ref, blk0_ref, nblk_ref, x_ref, dest_ref, xs_ref, acc_ref):
    j = pl.program_id(0)
    acc_ref[...] = jnp.zeros_like(acc_ref)

    def block(i, carry):
        b = blk0_ref[j] + i
        onehot = _one_hot_rows(dest_ref[b, pl.ds(exp_ref[j], 1), :], j)
        xblk = x_ref[pl.ds(pl.multiple_of(b * MOE_TB, MOE_TB), MOE_TB), :]
        acc_ref[...] += jnp.dot(onehot, xblk, preferred_element_type=F32)
        return carry

    lax.fori_loop(0, nblk_ref[j], block, 0)
    xs_ref[...] = acc_ref[...].astype(BF16)


def _dispatch(plan, xb, dest_t):
    exp, blk0, nblk = plan
    n, d = xb.shape
    rows = _moe_rows(n)
    dest3 = dest_t.reshape(N_EXPERTS, n // MOE_TB, MOE_TB).transpose(1, 0, 2)
    grid_spec = pltpu.PrefetchScalarGridSpec(
        num_scalar_prefetch=3,
        grid=(rows // MOE_TR,),
        in_specs=[_resident(xb.shape), _resident(dest3.shape)],
        out_specs=pl.BlockSpec((MOE_TR, d), lambda j, e, b, c: (j, 0)),
        scratch_shapes=[pltpu.VMEM((MOE_TR, d), F32)],
    )
    return pl.pallas_call(
        _dispatch_kernel,
        grid_spec=grid_spec,
        out_shape=jax.ShapeDtypeStruct((rows, d), BF16),
        compiler_params=_params("arbitrary"),
        name="moe_dispatch",
    )(exp, blk0, nblk, xb, dest3)


def _expert_kernel(et_ref, ee_ref, ea_ref, xs_ref, w1_ref, w3_ref, w2_ref, ys_ref, acc_ref):
    j = pl.program_id(0)
    f = pl.program_id(1)

    @pl.when((ea_ref[j] == 0) & (f == 0))
    def _():
        ys_ref[...] = jnp.zeros_like(ys_ref)

    @pl.when(ea_ref[j] == 1)
    def _():
        @pl.when(f == 0)
        def _():
            acc_ref[...] = jnp.zeros_like(acc_ref)

        x = xs_ref[...]
        h1 = jnp.dot(x, w1_ref[...], preferred_element_type=F32)
        h3 = jnp.dot(x, w3_ref[...], preferred_element_type=F32)
        acc_ref[...] += jnp.dot((_silu(h1) * h3).astype(BF16), w2_ref[...], preferred_element_type=F32)

        @pl.when(f == pl.num_programs(1) - 1)
        def _():
            ys_ref[...] = acc_ref[...].astype(BF16)


def _experts(plan, xs, w1, w3, w2, tf):
    et, ee, ea = plan
    rows, d = xs.shape
    nf = w1.shape[2] // tf
    fidx = lambda f, a, j: f * a[j] + (nf - 1) * (1 - a[j])
    grid_spec = pltpu.PrefetchScalarGridSpec(
        num_scalar_prefetch=3,
        grid=(et.shape[0], nf),
        in_specs=[pl.BlockSpec((MOE_TE, d), lambda j, f, t, e, a: (t[j], 0)),
                  pl.BlockSpec((None, d, tf), lambda j, f, t, e, a: (e[j], 0, fidx(f, a, j))),
                  pl.BlockSpec((None, d, tf), lambda j, f, t, e, a: (e[j], 0, fidx(f, a, j))),
                  pl.BlockSpec((None, tf, d), lambda j, f, t, e, a: (e[j], fidx(f, a, j), 0))],
        out_specs=pl.BlockSpec((MOE_TE, d), lambda j, f, t, e, a: (j, 0)),
        scratch_shapes=[pltpu.VMEM((MOE_TE, d), F32)],
    )
    return pl.pallas_call(
        _expert_kernel,
        grid_spec=grid_spec,
        out_shape=jax.ShapeDtypeStruct((rows, d), BF16),
        compiler_params=_params("arbitrary", "arbitrary"),
        name="moe_experts",
    )(et, ee, ea, xs, w1, w3, w2)


def _combine_kernel(tile_ref, exp_ref, off_ref, cnt_ref, ys_hbm, dest_ref, gates_ref, res_ref, g_ref, b_ref, o_ref,
                    buf_ref, sem_ref, acc_ref):
    j = pl.program_id(0)
    base = off_ref[j]
    n = cnt_ref[j]

    def fetch(i, slot, first=base):
        row0 = pl.multiple_of(tile_ref[first + i] * MOE_TR, MOE_TR)
        return pltpu.make_async_copy(ys_hbm.at[pl.ds(row0, MOE_TR), :], buf_ref.at[slot], sem_ref.at[slot])

    def prime(blk):
        for i0 in range(MOE_COMBINE_BUFS - 1):
            @pl.when(i0 < cnt_ref[blk])
            def _(i0=i0):
                fetch(i0, i0, off_ref[blk]).start()

    @pl.when(j == 0)
    def _():
        prime(0)

    acc_ref[...] = jnp.zeros_like(acc_ref)
    dest = dest_ref[...]
    gates = gates_ref[...]
    lane = lax.broadcasted_iota(jnp.int32, dest.shape, 1)

    def item(i, carry):
        slot = lax.rem(i, MOE_COMBINE_BUFS)
        ahead = i + (MOE_COMBINE_BUFS - 1)

        @pl.when(ahead < n)
        def _():
            fetch(ahead, lax.rem(ahead, MOE_COMBINE_BUFS)).start()

        mine = lane == exp_ref[base + i]
        dcol = jnp.sum(jnp.where(mine, dest, 0.0), axis=1, keepdims=True)
        gcol = jnp.sum(jnp.where(mine, gates, 0.0), axis=1, keepdims=True)
        col = lax.broadcasted_iota(jnp.int32, (MOE_TB, MOE_TR), 1) + tile_ref[base + i] * MOE_TR
        onehot = jnp.where(dcol == col.astype(F32), 1.0, 0.0).astype(BF16)
        fetch(i, slot).wait()
        acc_ref[...] += jnp.dot(onehot, buf_ref[slot], preferred_element_type=F32) * gcol
        return carry

    lax.fori_loop(0, n, item, 0)

    @pl.when(j + 1 < pl.num_programs(0))
    def _():
        prime(j + 1)

    o_ref[...] = _layer_norm(DEEPNORM_ALPHA * res_ref[...] + acc_ref[...], g_ref[...], b_ref[...])


def _combine_ln(plan, ys, dest, gates, res, g, b):
    tile, exp, off, cnt = plan
    n, d = res.shape
    tok = lambda j, *_: (j, 0)
    grid_spec = pltpu.PrefetchScalarGridSpec(
        num_scalar_prefetch=4,
        grid=(n // MOE_TB,),
        in_specs=[pl.BlockSpec(memory_space=pl.ANY),
                  pl.BlockSpec((MOE_TB, LANES), tok), pl.BlockSpec((MOE_TB, LANES), tok), pl.BlockSpec((MOE_TB, d), tok),
                  _resident((1, d)), _resident((1, d))],
        out_specs=pl.BlockSpec((MOE_TB, d), tok),
        scratch_shapes=[pltpu.VMEM((MOE_COMBINE_BUFS, MOE_TR, d), BF16), pltpu.SemaphoreType.DMA((MOE_COMBINE_BUFS,)),
                        pltpu.VMEM((MOE_TB, d), F32)],
    )
    return pl.pallas_call(
        _combine_kernel,
        grid_spec=grid_spec,
        out_shape=jax.ShapeDtypeStruct((n, d), F32),
        compiler_params=_params("arbitrary"),
        name="moe_combine_ln",
    )(tile, exp, off, cnt, ys, dest, gates, res, g, b)


def _moe_ln(x, xb, routing, w1, w3, w2, g, b):
    n = x.shape[0]
    gates, rank, rank_t, cnt = routing
    start, expert_tiles, by_tile, by_block = _moe_plan(cnt, n)
    startf = start.astype(F32)
    dest = jnp.where(rank >= 0.0, rank + jnp.pad(startf, (0, LANES - N_EXPERTS))[None, :], -1.0)
    dest_t = jnp.where(rank_t >= 0.0, rank_t + startf[:, None], -1.0)
    xs = _dispatch(by_tile, xb, dest_t)
    ys = _experts(expert_tiles, xs, w1, w3, w2, tf=EXPERT_F_TILE)
    return _combine_ln(by_block, ys, dest, gates, x, g, b)


def _gla_kernel(q_ref, k_ref, v_ref, g_ref, gk_ref, gkw_ref, gkb_ref, nw_ref, sel_ref, o_ref, st_ref, bc_scr,
                qd_scr, kd_scr):
    @pl.when(pl.program_id(1) == 0)
    def _():
        st_ref[...] = jnp.zeros_like(st_ref)

    tt = q_ref.shape[0]
    tri = _tri(CHUNK)
    nw = nw_ref[...]

    pre = _mm_split(gk_ref[...], gkw_ref[...]) + gkb_ref[...]
    gk = (jnp.minimum(pre, 0.0) - jnp.log(1.0 + jnp.exp(-jnp.abs(pre)))) * (1.0 / GLA_GATE_NORM)
    bc_all = _mm_01_left(sel_ref[...], gk)
    bc_scr[...] = bc_all
    qd_scr[...] = q_ref[...] * (GLA_DK ** -0.5) * jnp.exp(bc_all)
    kd_scr[...] = k_ref[...] * jnp.exp(-bc_all)

    def chunk(c, carry):
        rows = pl.ds(pl.multiple_of(c * CHUNK, CHUNK), CHUNK)
        bc = bc_scr[rows, :]
        bl = bc[CHUNK - 1:CHUNK, :]
        qd = qd_scr[rows, :]
        kd = kd_scr[rows, :]
        kl = k_ref[rows, :] * jnp.exp(bl - bc)
        dec = jnp.exp(bl)
        heads = range(GLA_HEADS)
        sks = [slice(h * GLA_DK, (h + 1) * GLA_DK) for h in heads]
        svs = [slice(h * GLA_DV, (h + 1) * GLA_DV) for h in heads]
        att = [jnp.where(tri, _mm(qd[:, sk], kd[:, sk], _NT), 0.0) for sk in sks]
        vs = [v_ref[rows, sv] for sv in svs]
        sts = [st_ref[h] for h in heads]
        os_ = [_mm(att[h], vs[h]) + _mm(qd[:, sks[h]], sts[h], _NT) for h in heads]
        for h in heads:
            st_ref[h] = sts[h] * dec[:, sks[h]] + _mm(vs[h], kl[:, sks[h]], _TN)
        for h in heads:
            o = os_[h]
            ms = jnp.mean(o * o, axis=-1, keepdims=True)
            y = o * lax.rsqrt(ms + GLA_NORM_EPS) * nw * _silu(g_ref[rows, svs[h]])
            o_ref[rows, svs[h]] = y.astype(o_ref.dtype)
        return carry

    lax.fori_loop(0, tt // CHUNK, chunk, 0, unroll=2)


def _gla(p, gkw, gkb, nw, tt):
    bsz, t, _ = p.shape

    def col(name):
        off, w = EVEN_COLS[name]
        return pl.BlockSpec((None, tt, w), lambda b, i, j=off // w: (b, i, j))

    fix = lambda b, i: (0, 0)
    return pl.pallas_call(
        _gla_kernel,
        grid=(bsz, t // tt),
        in_specs=[col("q"), col("k"), col("v"), col("g"), col("gk"),
                  pl.BlockSpec(gkw.shape, fix), pl.BlockSpec(gkb.shape, fix), pl.BlockSpec(nw.shape, fix),
                  pl.BlockSpec((tt, tt), fix)],
        out_specs=pl.BlockSpec((None, tt, GLA_VW), lambda b, i: (b, i, 0)),
        out_shape=jax.ShapeDtypeStruct((bsz, t, GLA_VW), BF16),
        scratch_shapes=[pltpu.VMEM((GLA_HEADS, GLA_DV, GLA_DK), F32)] + [pltpu.VMEM((tt, GLA_QK), F32)] * 3,
        compiler_params=_params("parallel", "arbitrary"),
        name="gla",
    )(p, p, p, p, p, gkw, gkb, nw, _chunk_sel(tt, CHUNK))


def _rglru_kernel(x_ref, gate_ref, cw_ref, cb_ref, wa_ref, ba_ref, wx_ref, bx_ref, lam_ref, o_ref,
                  tail_ref, h_ref, a_scr, u_scr):
    @pl.when(pl.program_id(1) == 0)
    def _():
        tail_ref[...] = jnp.zeros_like(tail_ref)
        h_ref[...] = jnp.zeros_like(h_ref)

    tt = x_ref.shape[0]
    x = x_ref[...]
    cw = cw_ref[...]
    xc = x * cw[CONV_K - 1:CONV_K, :] + cb_ref[...]
    for s in range(1, CONV_K):
        xc += _shift_rows(x, tail_ref, s) * cw[CONV_K - 1 - s:CONV_K - s, :]
    tail_ref[...] = x[tt - SUBLANES:, :]

    r = _sigmoid(_mm(xc, wa_ref[...]) + ba_ref[...])
    i = _sigmoid(_mm(xc, wx_ref[...]) + bx_ref[...])
    log_a = -LRU_C * r * _softplus(-lam_ref[...])
    a_scr[...] = jnp.exp(log_a)
    u_scr[...] = jnp.sqrt(1.0 - jnp.exp(2.0 * log_a)) * (i * xc)

    row = lax.broadcasted_iota(jnp.int32, (SUBLANES, x.shape[1]), 0)

    def step(i, h):
        rows = pl.ds(pl.multiple_of(i * SUBLANES, SUBLANES), SUBLANES)
        a = a_scr[rows, :]
        u = u_scr[rows, :]
        for s in (1, 2, 4):
            u = u + a * jnp.where(row >= s, pltpu.roll(u, s, axis=0), 0.0)
            a = a * jnp.where(row >= s, pltpu.roll(a, s, axis=0), 1.0)
        hb = u + a * h
        u_scr[rows, :] = hb
        return hb[SUBLANES - 1:, :]

    h_ref[...] = lax.fori_loop(0, tt // SUBLANES, step, h_ref[...], unroll=4)
    gate = gate_ref[...]
    gelu = 0.5 * gate * (1.0 + jnp.tanh(math.sqrt(2.0 / math.pi) * (gate + 0.044715 * gate * gate * gate)))
    o_ref[...] = (u_scr[...] * gelu).astype(o_ref.dtype)


def _rglru(p, cw, cb, wa, ba, wx, bx, lam, tt):
    bsz, t, _ = p.shape
    w = LRU_WIDTH

    def col(name):
        off, _ = EVEN_COLS[name]
        return pl.BlockSpec((None, tt, w), lambda b, i, j=off // w: (b, i, j))

    fix = lambda b, i: (0, 0)
    vec = pl.BlockSpec((1, w), fix)
    return pl.pallas_call(
        _rglru_kernel,
        grid=(bsz, t // tt),
        in_specs=[col("xb"), col("gate"), pl.BlockSpec((CONV_K, w), fix), vec,
                  pl.BlockSpec((w, w), fix), vec, pl.BlockSpec((w, w), fix), vec, vec],
        out_specs=pl.BlockSpec((None, tt, w), lambda b, i: (b, i, 0)),
        out_shape=jax.ShapeDtypeStruct((bsz, t, w), BF16),
        scratch_shapes=[pltpu.VMEM((SUBLANES, w), F32), pltpu.VMEM((1, w), F32),
                        pltpu.VMEM((tt, w), F32), pltpu.VMEM((tt, w), F32)],
        compiler_params=_params("parallel", "arbitrary"),
        name="rglru",
    )(p, p, cw, cb, wa, ba, wx, bx, lam)


def _ssd_kernel(z_ref, x_ref, bm_ref, cm_ref, dt_ref, cwx_ref, cbx_ref, cwb_ref, cbb_ref, cwc_ref, cbc_ref,
                dtb_ref, alog_ref, dskip_ref, nw_ref, expand_ref, sel_ref, o_ref,
                tx_ref, tb_ref, tc_ref, st_ref, xs_scr, bs_scr, cs_scr, acs_scr, xdt_scr, xdec_scr, ea_scr):
    @pl.when(pl.program_id(1) == 0)
    def _():
        tx_ref[...] = jnp.zeros_like(tx_ref)
        tb_ref[...] = jnp.zeros_like(tb_ref)
        tc_ref[...] = jnp.zeros_like(tc_ref)
        st_ref[...] = jnp.zeros_like(st_ref)

    tt = x_ref.shape[0]

    def conv_silu(src_ref, tail_ref, cw_ref, cb_ref, dst_ref):
        x = src_ref[...]
        cw = cw_ref[...]
        y = x * cw[CONV_K - 1:CONV_K, :] + cb_ref[...]
        for s in range(1, CONV_K):
            y += _shift_rows(x, tail_ref, s) * cw[CONV_K - 1 - s:CONV_K - s, :]
        tail_ref[...] = x[tt - SUBLANES:, :]
        dst_ref[...] = _silu(y)

    conv_silu(x_ref, tx_ref, cwx_ref, cbx_ref, xs_scr)
    conv_silu(bm_ref, tb_ref, cwb_ref, cbb_ref, bs_scr)
    conv_silu(cm_ref, tc_ref, cwc_ref, cbc_ref, cs_scr)

    tri = _tri(CHUNK)
    expand = expand_ref[...]
    dskip = dskip_ref[...]
    nw = nw_ref[...]
    hpg = SSD_HEADS // SSD_GROUPS

    dtc = _softplus(dt_ref[...] + dtb_ref[...])
    da = dtc * -jnp.exp(alog_ref[...])
    sums = _mm_01_left(sel_ref[...], da)
    acs_all = sums[:tt]
    tot = sums[tt:]
    acs_scr[...] = acs_all
    xdt_all = xs_scr[...] * _mm_01(dtc, expand)
    xdt_scr[...] = xdt_all
    xdec_scr[...] = xdt_all * _mm_01(jnp.exp(tot - acs_all), expand)
    ea_scr[...] = _mm_01(jnp.exp(acs_all), expand)

    def chunk(c, carry):
        rows = pl.ds(pl.multiple_of(c * CHUNK, CHUNK), CHUNK)
        acs = acs_scr[rows, :]
        acs_t = acs.T
        ea_x = ea_scr[rows, :]
        x = xs_scr[rows, :]
        xdt = xdt_scr[rows, :]
        xdec = xdec_scr[rows, :]
        cd_x = ea_x[CHUNK - 1:CHUNK, :]
        ys = []
        for g in range(SSD_GROUPS):
            sg = slice(g * SSD_GROUP_WIDTH, (g + 1) * SSD_GROUP_WIDTH)
            ss = slice(g * SSD_STATE, (g + 1) * SSD_STATE)
            bg = bs_scr[rows, ss]
            cg = cs_scr[rows, ss]
            cb = _mm(cg, bg, _NT)
            st = st_ref[g]
            yg = _mm(cg, st) * ea_x[:, sg]
            st_ref[g] = st * cd_x[:, sg] + _mm(bg, xdec[:, sg], _TN)
            yh = []
            for j in range(hpg):
                h = g * hpg + j
                seg = acs[:, h:h + 1] - acs_t[h:h + 1, :]
                lmat = jnp.exp(jnp.where(tri, seg, -jnp.inf))
                sh = slice(h * SSD_HEADDIM, (h + 1) * SSD_HEADDIM)
                yh.append(_mm(cb * lmat, xdt[:, sh]))
            ys.append(yg + jnp.concatenate(yh, axis=1))
        y = jnp.concatenate(ys, axis=1) + x * dskip
        y = y * _silu(z_ref[rows, :])
        outs = []
        for g in range(SSD_GROUPS):
            sg = slice(g * SSD_GROUP_WIDTH, (g + 1) * SSD_GROUP_WIDTH)
            yg = y[:, sg]
            ms = jnp.mean(yg * yg, axis=-1, keepdims=True)
            outs.append(yg * lax.rsqrt(ms + SSD_NORM_EPS))
        o_ref[rows, :] = (jnp.concatenate(outs, axis=1) * nw).astype(o_ref.dtype)
        return carry

    lax.fori_loop(0, tt // CHUNK, chunk, 0, unroll=2)


def _ssd(p, cw, cb, dtb, alog, dskip_x, nw, expand, tt):
    bsz, t, _ = p.shape

    def col(name):
        off, w = ODD_COLS[name]
        return pl.BlockSpec((None, tt, w), lambda b, i, j=off // w: (b, i, j))

    fix = lambda b, i: (0, 0)
    full = lambda a: pl.BlockSpec(a.shape, fix)
    gs = SSD_GROUPS * SSD_STATE
    cwx, cwb, cwc = cw[:, :SSD_INNER], cw[:, SSD_INNER:SSD_INNER + gs], cw[:, SSD_INNER + gs:]
    cbx, cbb, cbc = cb[:, :SSD_INNER], cb[:, SSD_INNER:SSD_INNER + gs], cb[:, SSD_INNER + gs:]
    sel = jnp.concatenate([_chunk_sel(tt, CHUNK), _chunk_sel(tt, CHUNK, "all")], axis=0)
    args = (cwx, cbx, cwb, cbb, cwc, cbc, dtb, alog, dskip_x, nw, expand, sel)
    return pl.pallas_call(
        _ssd_kernel,
        grid=(bsz, t // tt),
        in_specs=[col("z"), col("x"), col("bm"), col("cm"), col("dt")] + [full(a) for a in args],
        out_specs=pl.BlockSpec((None, tt, SSD_INNER), lambda b, i: (b, i, 0)),
        out_shape=jax.ShapeDtypeStruct((bsz, t, SSD_INNER), BF16),
        scratch_shapes=[pltpu.VMEM((SUBLANES, SSD_INNER), F32), pltpu.VMEM((SUBLANES, gs), F32),
                        pltpu.VMEM((SUBLANES, gs), F32),
                        pltpu.VMEM((SSD_GROUPS, SSD_STATE, SSD_GROUP_WIDTH), F32),
                        pltpu.VMEM((tt, SSD_INNER), F32), pltpu.VMEM((tt, gs), F32), pltpu.VMEM((tt, gs), F32),
                        pltpu.VMEM((tt, LANES), F32)] + [pltpu.VMEM((tt, SSD_INNER), F32)] * 3,
        compiler_params=_params("parallel", "arbitrary"),
        name="ssd",
    )(p, p, p, p, p, *args)


def _rwkv_kernel(r_ref, k_ref, v_ref, wa_ref, xg_ref, mur_ref, muk_ref, muv_ref, muwa_ref, mug_ref,
                 w0_ref, w2_ref, a0_ref, a2_ref, g2_ref, kk_ref, ka_ref, rk_ref, lng_ref, lnb_ref, ones_ref, sel_ref,
                 o_ref, tr_ref, tk_ref, tv_ref, twa_ref, tg_ref, st_ref, y_scr,
                 at_scr, rt_scr, bt_scr, kt_scr, bh_scr, kh_scr, v_scr, wc_scr):
    @pl.when(pl.program_id(1) == 0)
    def _():
        for ref in (tr_ref, tk_ref, tv_ref, twa_ref, tg_ref, st_ref):
            ref[...] = jnp.zeros_like(ref)

    tt = r_ref.shape[0]
    hd = RWKV_HEADDIM

    def mix(p_ref, tail_ref, mu_ref):
        p = p_ref[...]
        prev = _shift_rows(p, tail_ref, 1)
        tail_ref[...] = p[tt - SUBLANES:, :]
        return p + (prev - p) * mu_ref[...]

    r = mix(r_ref, tr_ref, mur_ref)
    k = mix(k_ref, tk_ref, muk_ref)
    v = mix(v_ref, tv_ref, muv_ref)
    xwa = mix(wa_ref, twa_ref, muwa_ref)
    xg = mix(xg_ref, tg_ref, mug_ref)

    lane = lax.broadcasted_iota(jnp.int32, xwa.shape, 1)
    lora_in = jnp.where(lane < RWKV_DECAY_LORA, jnp.tanh(xwa), xwa)
    w_log = -_softplus(-(w0_ref[...] + _mm_split(lora_in, w2_ref[...]))) - 0.5
    lw = -jnp.exp(w_log)
    a_sig = _sigmoid(a0_ref[...] + _mm_split(lora_in, a2_ref[...]))
    gate = _mm_split(_sigmoid(xg), g2_ref[...])
    ones_bd = ones_ref[...]
    kk = k * kk_ref[...]
    kk = kk / jnp.maximum(jnp.sqrt(_mm_01(kk * kk, ones_bd)), 1e-12)
    k = k * (1.0 + (a_sig - 1.0) * ka_ref[...])

    cs = RWKV_CHUNK
    tri_incl = _tri(cs)
    tri_strict = _tri(cs, strict=True)
    tri_col = jnp.concatenate([tri_strict, tri_incl], axis=0)
    eye = (tri_incl & ~tri_strict).astype(F32)

    sums = _mm_01_left(sel_ref[...], lw)
    cum = sums[:tt]
    tot = sums[tt:]
    bvec = kk * a_sig
    e_neg = jnp.exp(-cum)
    e_last = jnp.exp(tot - cum)
    for ref, val in ((at_scr, -kk * jnp.exp(cum - lw)), (rt_scr, r * jnp.exp(cum)), (bt_scr, bvec * e_neg),
                     (kt_scr, k * e_neg), (bh_scr, bvec * e_last), (kh_scr, k * e_last), (v_scr, v),
                     (wc_scr, jnp.exp(tot))):
        ref[...] = val

    def chunk(c, carry):
        rows = pl.ds(pl.multiple_of(c * cs, cs), cs)
        at = at_scr[rows, :]
        rt = rt_scr[rows, :]
        bt = bt_scr[rows, :]
        kt = kt_scr[rows, :]
        bh = bh_scr[rows, :]
        kh = kh_scr[rows, :]
        vc = v_scr[rows, :]
        wc = wc_scr[pl.ds(pl.multiple_of(c * cs, cs), 1), :]
        heads = range(RWKV_HEADS)
        sls = [slice(h * hd, (h + 1) * hd) for h in heads]
        amat = [_mm(jnp.concatenate([at[:, sl], rt[:, sl]], axis=0),
                    jnp.concatenate([bt[:, sl], kt[:, sl]], axis=0), _NT) for sl in sls]
        a_ab = [jnp.where(tri_strict, m[:cs, :cs], 0.0) for m in amat]
        a_rb = [jnp.where(tri_incl, m[cs:, :cs], 0.0) for m in amat]
        a_xk = [jnp.where(tri_col, m[:, cs:], 0.0) for m in amat]
        avrv = [_mm(a_xk[h], vc[:, sls[h]]) for h in heads]
        inv = [eye + m for m in a_ab]
        pw = [_mm(m, m) for m in a_ab]
        for _ in range(int(math.log2(cs)) - 2):
            both = [_mm(jnp.concatenate([inv[h], pw[h]], axis=0), pw[h]) for h in heads]
            inv = [inv[h] + both[h][:cs] for h in heads]
            pw = [m[cs:] for m in both]
        inv = [inv[h] + _mm(inv[h], pw[h]) for h in heads]
        tu = [_mm(inv[h], jnp.concatenate([at[:, sls[h]], avrv[h][:cs]], axis=1)) for h in heads]
        qy = [_mm(a_rb[h], tu[h]) + jnp.concatenate([rt[:, sls[h]], avrv[h][cs:]], axis=1)
              for h in heads]
        zb =[_mm(tu[h], bh[:, sls[h]], _TN) for h in heads]
        vk = [_mm(vc[:, sls[h]], kh[:, sls[h]], _TN) for h in heads]
        for h in heads:
            st = st_ref[h]
            y_scr[rows, sls[h]] = _mm(qy[h][:, :hd], st, _NT) + qy[h][:, hd:]
            st_ref[h] = st * wc[:, sls[h]] + _mm(st, zb[h][:hd]) + zb[h][hd:] + vk[h]
        return carry

    lax.fori_loop(0, tt // cs, chunk, 0, unroll=2)

    y = y_scr[...]
    inv_n = 1.0 / hd
    mu_y = _mm_01(y, ones_bd) * inv_n
    dy = y - mu_y
    var_y = _mm_01(dy * dy, ones_bd) * inv_n
    yn = dy * lax.rsqrt(var_y + RWKV_GN_EPS) * lng_ref[...] + lnb_ref[...]
    bonus = _mm_01(r * k * rk_ref[...], ones_bd) * v
    o_ref[...] = ((yn + bonus) * gate).astype(o_ref.dtype)


def _rwkv(p, mu, w0, w2p, a0, a2p, g2, k_k, k_a, r_k, ln_g, ln_b, ones_bd, tt):
    bsz, t, _ = p.shape
    w = RWKV_WIDTH

    def col(name):
        off, wd = ODD_COLS[name]
        return pl.BlockSpec((None, tt, wd), lambda b, i, j=off // wd: (b, i, j))

    fix = lambda b, i: (0, 0)
    full = lambda a: pl.BlockSpec(a.shape, fix)
    mur, muk, muv = mu[:, :w], mu[:, w:2 * w], mu[:, 2 * w:3 * w]
    muwa, mug = mu[:, 3 * w:3 * w + LANES], mu[:, 3 * w + LANES:]
    sel = jnp.concatenate([_chunk_sel(tt, RWKV_CHUNK), _chunk_sel(tt, RWKV_CHUNK, "all")], axis=0)
    args = (mur, muk, muv, muwa, mug, w0, w2p, a0, a2p, g2, k_k, k_a, r_k, ln_g, ln_b, ones_bd, sel)
    return pl.pallas_call(
        _rwkv_kernel,
        grid=(bsz, t // tt),
        in_specs=[col("r"), col("k"), col("v"), col("wa"), col("xg")] + [full(a) for a in args],
        out_specs=pl.BlockSpec((None, tt, w), lambda b, i: (b, i, 0)),
        out_shape=jax.ShapeDtypeStruct((bsz, t, w), BF16),
        scratch_shapes=[pltpu.VMEM((SUBLANES, w), F32), pltpu.VMEM((SUBLANES, w), F32),
                        pltpu.VMEM((SUBLANES, w), F32), pltpu.VMEM((SUBLANES, LANES), F32),
                        pltpu.VMEM((SUBLANES, LANES), F32),
                        pltpu.VMEM((RWKV_HEADS, RWKV_HEADDIM, RWKV_HEADDIM), F32)]
        + [pltpu.VMEM((tt, w), F32)] * 9,
        compiler_params=_params("parallel", "arbitrary"),
        name="rwkv7",
    )(p, p, p, p, p, *args)


def _reorder_cols(w, pieces, order, width):
    parts, pos = [], 0
    for name, (off, slot) in order.items():
        assert off == pos, "slots must be listed in order and contiguous"
        start, size = pieces[name]
        parts.append(w[:, start:start + size])
        if size < slot:
            parts.append(jnp.zeros((w.shape[0], slot - size), w.dtype))
        pos += slot
    assert pos == width
    return jnp.concatenate(parts, axis=1)


def _block_diag(w):
    n, i, j = w.shape
    eye = jnp.eye(n, dtype=w.dtype)
    return (eye[:, None, :, None] * w[:, :, None, :]).reshape(n * i, n * j)


def _row(v, width=None):
    v = v.reshape(1, -1).astype(F32)
    if width is not None and v.shape[1] < width:
        v = jnp.pad(v, ((0, 0), (0, width - v.shape[1])))
    return v


def _even_layer(x, x_in, w_in, gk_w2, gk_b, gla_norm, conv_w, conv_b, wa, ba, wx, bx, lam,
                w_out, ln1_g, ln1_b, f_w1, f_w3, f_w2, ln2_g, ln2_b, bsz, t):
    qk, vw, lw = GLA_QK, GLA_VW, LRU_WIDTH
    pieces = dict(q=(0, qk), k=(qk, qk), v=(2 * qk, vw), g=(2 * qk + vw, vw), gk=(2 * qk + 2 * vw, GLA_GK_RANK),
                  xb=(2 * qk + 2 * vw + GLA_GK_RANK, lw), gate=(2 * qk + 2 * vw + GLA_GK_RANK + lw, lw))
    w_in_r = _reorder_cols(w_in, pieces, EVEN_COLS, EVEN_WIDTH).astype(BF16)
    p = _proj(x_in, w_in_r, tm=min(ROW_TILE, x.shape[0])).reshape(bsz, t, EVEN_WIDTH)
    tt = min(GLA_RGLRU_TIME_TILE, t)
    gkw = jnp.pad(gk_w2, ((0, LANES - GLA_GK_RANK), (0, 0)))
    y_a = _gla(p, gkw, _row(gk_b), _row(gla_norm), tt)
    y_b = _rglru(p, conv_w, _row(conv_b), _block_diag(wa).astype(BF16), _row(ba), _block_diag(wx).astype(BF16),
                 _row(bx), _row(lam), tt)
    n = bsz * t
    tm = min(ROW_TILE, n)
    return _mix_ffn(y_a.reshape(n, vw), y_b.reshape(n, lw), w_out[:vw].astype(BF16), w_out[vw:].astype(BF16), x,
                    _row(ln1_g), _row(ln1_b), f_w1.astype(BF16), f_w3.astype(BF16), f_w2.astype(BF16),
                    _row(ln2_g), _row(ln2_b), tm)


def _odd_layer(x, xb, w_in, conv_w, conv_b, dt_bias, a_log, d_skip, ssd_norm, mu, w0, w2, a0, a2, g2, k_k, k_a,
               r_k, rln_g, rln_b, w_out, ln1_g, ln1_b, router, ew1, ew3, ew2, ln2_g, ln2_b, bsz, t):
    si, gs, rw = SSD_INNER, SSD_GROUPS * SSD_STATE, RWKV_WIDTH
    o = 2 * si + 2 * gs + SSD_HEADS
    pieces = dict(z=(0, si), x=(si, si), bm=(2 * si, gs), cm=(2 * si + gs, gs), dt=(2 * si + 2 * gs, SSD_HEADS),
                  r=(o, rw), k=(o + rw, rw), v=(o + 2 * rw, rw),
                  wa=(o + 3 * rw, RWKV_DECAY_LORA + RWKV_AAA_LORA), xg=(o + 3 * rw + LANES, RWKV_GATE_LORA))
    w_in_r = _reorder_cols(w_in, pieces, ODD_COLS, ODD_WIDTH).astype(BF16)
    n = bsz * t
    p = _proj(xb, w_in_r, tm=min(ROW_TILE, n)).reshape(bsz, t, ODD_WIDTH)
    expand = jnp.pad(jnp.repeat(jnp.eye(SSD_HEADS, dtype=BF16), SSD_HEADDIM, axis=1), ((0, LANES - SSD_HEADS), (0, 0)))
    y_c = _ssd(p, conv_w, _row(conv_b), _row(dt_bias, LANES), _row(a_log, LANES),
               _row(jnp.repeat(d_skip, SSD_HEADDIM)), _row(ssd_norm), expand, tt=min(SSD_RWKV_TIME_TILE, t))
    zeros = jnp.zeros((RWKV_DECAY_LORA, rw), F32)
    ones_bd = _block_diag(jnp.ones((RWKV_HEADS, RWKV_HEADDIM, RWKV_HEADDIM), BF16))
    y_d = _rwkv(p, _row(mu), _row(w0), jnp.concatenate([w2, zeros]), _row(a0), jnp.concatenate([zeros, a2]), g2,
                _row(k_k), _row(k_a), _row(r_k), _row(rln_g), _row(rln_b), ones_bd, tt=min(SSD_RWKV_TIME_TILE, t))
    x1, x1b, *routing = _mix_router(y_c.reshape(n, si), y_d.reshape(n, rw), w_out[:si].astype(BF16),
                                    w_out[si:].astype(BF16), x, _row(ln1_g), _row(ln1_b), router)
    y = _moe_ln(x1, x1b, routing, ew1.astype(BF16), ew3.astype(BF16), ew2.astype(BF16), _row(ln2_g), _row(ln2_b))
    return y, None


def kernel(x, e_w_in, e_gk_w2, e_gk_b, e_gla_norm, e_conv_w, e_conv_b, e_lru_wa, e_lru_ba, e_lru_wx, e_lru_bx, e_lru_lambda, e_w_out, e_ln1_g, e_ln1_b, e_ffn_w1, e_ffn_w3, e_ffn_w2, e_ln2_g, e_ln2_b, o_w_in, o_conv_w, o_conv_b, o_dt_bias, o_a_log, o_d_skip, o_ssd_norm, o_rwkv_mu, o_rwkv_w0, o_rwkv_w2, o_rwkv_a0, o_rwkv_a2, o_rwkv_g2, o_rwkv_k_k, o_rwkv_k_a, o_rwkv_r_k, o_rwkv_ln_g, o_rwkv_ln_b, o_w_out, o_ln1_g, o_ln1_b, o_router, o_exp_w1, o_exp_w3, o_exp_w2, o_ln2_g, o_ln2_b):
    bsz, t, d = x.shape
    h = x.reshape(bsz * t, d)
    hb = h
    for i in range(DEPTH):
        j = i // 2
        if i % 2 == 0:
            h, hb = _even_layer(h, hb, e_w_in[j], e_gk_w2[j], e_gk_b[j], e_gla_norm[j], e_conv_w[j], e_conv_b[j],
                                e_lru_wa[j], e_lru_ba[j], e_lru_wx[j], e_lru_bx[j], e_lru_lambda[j], e_w_out[j],
                                e_ln1_g[j], e_ln1_b[j], e_ffn_w1[j], e_ffn_w3[j], e_ffn_w2[j], e_ln2_g[j],
                                e_ln2_b[j], bsz, t)
        else:
            h, hb = _odd_layer(h, hb, o_w_in[j], o_conv_w[j], o_conv_b[j], o_dt_bias[j], o_a_log[j], o_d_skip[j],
                               o_ssd_norm[j], o_rwkv_mu[j], o_rwkv_w0[j], o_rwkv_w2[j], o_rwkv_a0[j],
                               o_rwkv_a2[j], o_rwkv_g2[j], o_rwkv_k_k[j], o_rwkv_k_a[j], o_rwkv_r_k[j],
                               o_rwkv_ln_g[j], o_rwkv_ln_b[j], o_w_out[j], o_ln1_g[j], o_ln1_b[j], o_router[j],
                               o_exp_w1[j], o_exp_w3[j], o_exp_w2[j], o_ln2_g[j], o_ln2_b[j], bsz, t)
    return h.reshape(bsz, t, d)
```

```python
import math

import jax
import jax.numpy as jnp
from jax import lax
from jax.experimental import pallas as pl
from jax.experimental.pallas import tpu as pltpu

F32 = jnp.float32
BF16 = jnp.bfloat16

D_MODEL = 1024
DEPTH = 2
DEEPNORM_ALPHA = (2 * DEPTH) ** 0.25
LN_EPS = 1e-5
CONV_K = 4
CHUNK = 64

GLA_HEADS = 4
GLA_DK = 64
GLA_DV = 128
GLA_QK = GLA_HEADS * GLA_DK
GLA_VW = GLA_HEADS * GLA_DV
GLA_GK_RANK = 16
GLA_GATE_NORM = 16.0
GLA_NORM_EPS = 1e-5

LRU_WIDTH = 512
LRU_BLOCKS = 8
LRU_C = 8.0

SSD_HEADS = 16
SSD_HEADDIM = 64
SSD_INNER = SSD_HEADS * SSD_HEADDIM
SSD_GROUPS = 2
SSD_STATE = 128
SSD_GROUP_WIDTH = SSD_INNER // SSD_GROUPS
SSD_NORM_EPS = 1e-5

RWKV_HEADS = 8
RWKV_HEADDIM = 64
RWKV_WIDTH = RWKV_HEADS * RWKV_HEADDIM
RWKV_DECAY_LORA = 64
RWKV_AAA_LORA = 64
RWKV_GATE_LORA = 128
RWKV_GN_EPS = 64e-5
RWKV_CHUNK = 128

N_EXPERTS = 8
LANES = 128
SUBLANES = 8
assert N_EXPERTS == SUBLANES
ROUTER_ROWS = 16
MOE_TB = 512
MOE_TR = 256
MOE_TE = 512
assert MOE_TE % MOE_TR == 0
MOE_COMBINE_BUFS = 4
VMEM_LIMIT = 56 * 1024 * 1024

ROW_TILE = 512
GLA_RGLRU_TIME_TILE = 512
SSD_RWKV_TIME_TILE = 256
EXPERT_F_TILE = 1792

EVEN_COLS = dict(v=(0, 512), g=(512, 512), xb=(1024, 512), gate=(1536, 512), q=(2048, 256), k=(2304, 256),
                 gk=(2560, 128))
EVEN_WIDTH = 2688
ODD_COLS = dict(z=(0, 1024), x=(1024, 1024), r=(2048, 512), k=(2560, 512), v=(3072, 512), bm=(3584, 256),
                cm=(3840, 256), wa=(4096, 128), xg=(4224, 128), dt=(4352, 128))
ODD_WIDTH = 4480


def _mm(a, b, dims=((1,), (0,))):
    return lax.dot_general(a.astype(BF16), b.astype(BF16), (dims, ((), ())), preferred_element_type=F32)


def _mm_split(a, b, dims=((1,), (0,))):
    a_hi = a.astype(BF16)
    b_hi = b.astype(BF16)
    a_lo = (a - a_hi.astype(F32)).astype(BF16)
    b_lo = (b - b_hi.astype(F32)).astype(BF16)
    return _mm(a_hi, b_hi, dims) + _mm(a_lo, b_hi, dims) + _mm(a_hi, b_lo, dims)


_NT = ((1,), (1,))
_TN = ((0,), (0,))


def _mm_01(x, sel):
    hi = x.astype(BF16)
    lo = (x - hi.astype(F32)).astype(BF16)
    return (jnp.dot(hi, sel, preferred_element_type=F32) + jnp.dot(lo, sel, preferred_element_type=F32))


def _sigmoid(x):
    return 1.0 / (1.0 + jnp.exp(-x))


def _softplus(x):
    return jnp.maximum(x, 0.0) + jnp.log(1.0 + jnp.exp(-jnp.abs(x)))


def _silu(x):
    return x * _sigmoid(x)


def _layer_norm(h, g, b):
    mu = jnp.mean(h, axis=-1, keepdims=True)
    d = h - mu
    var = jnp.mean(d * d, axis=-1, keepdims=True)
    return d * lax.rsqrt(var + LN_EPS) * g + b


def _params(*sem):
    return pltpu.CompilerParams(dimension_semantics=sem, vmem_limit_bytes=VMEM_LIMIT)


def _tri(n, strict=False):
    r = lax.broadcasted_iota(jnp.int32, (n, n), 0)
    c = lax.broadcasted_iota(jnp.int32, (n, n), 1)
    return (r > c) if strict else (r >= c)


def _chunk_sel(n, chunk, kind="incl"):
    r = jnp.arange(n)[:, None]
    c = jnp.arange(n)[None, :]
    first = r - r % chunk
    upper = r if kind == "incl" else first + (chunk - 1)
    return ((c >= first) & (c <= upper)).astype(BF16)


def _mm_01_left(sel, x):
    hi = x.astype(BF16)
    lo = (x - hi.astype(F32)).astype(BF16)
    return jnp.dot(sel, hi, preferred_element_type=F32) + jnp.dot(sel, lo, preferred_element_type=F32)


def _shift_rows(p, tail_ref, s):
    n = p.shape[0]
    rolled = pltpu.roll(p, s, axis=0)
    head = pltpu.roll(tail_ref[...], s, axis=0)
    row = lax.broadcasted_iota(jnp.int32, (SUBLANES, p.shape[1]), 0)
    fixed = jnp.where(row < s, head, rolled[:SUBLANES])
    return jnp.concatenate([fixed, rolled[SUBLANES:]], axis=0) if n > SUBLANES else fixed


def _proj_kernel(x_ref, w_ref, o_ref):
    o_ref[...] = jnp.dot(x_ref[...].astype(BF16), w_ref[...], preferred_element_type=F32)


def _resident(shape):
    return pl.BlockSpec(shape, lambda *_: (0,) * len(shape), pipeline_mode=pl.Buffered(1))


def _proj(x, w, tm):
    m, k = x.shape
    n = w.shape[1]
    return pl.pallas_call(
        _proj_kernel,
        grid=(m // tm,),
        in_specs=[pl.BlockSpec((tm, k), lambda i: (i, 0)), _resident((k, n))],
        out_specs=pl.BlockSpec((tm, n), lambda i: (i, 0)),
        out_shape=jax.ShapeDtypeStruct((m, n), F32),
        compiler_params=_params("parallel"),
        name="in_proj",
    )(x, w)


def _outproj_ln_value(ya_ref, yb_ref, wa_ref, wb_ref, res_ref, g_ref, b_ref):
    acc = jnp.dot(ya_ref[...], wa_ref[...], preferred_element_type=F32)
    acc += jnp.dot(yb_ref[...], wb_ref[...], preferred_element_type=F32)
    return _layer_norm(DEEPNORM_ALPHA * res_ref[...] + acc, g_ref[...], b_ref[...])


def _mix_ffn_kernel(ya_ref, yb_ref, wa_ref, wb_ref, res_ref, g1_ref, b1_ref, w1_ref, w3_ref, w2_ref, g2_ref, b2_ref,
                    o_ref, ob_ref):
    x1 = _outproj_ln_value(ya_ref, yb_ref, wa_ref, wb_ref, res_ref, g1_ref, b1_ref)
    xb = x1.astype(BF16)
    h1 = jnp.dot(xb, w1_ref[...], preferred_element_type=F32)
    h3 = jnp.dot(xb, w3_ref[...], preferred_element_type=F32)
    ff = jnp.dot((_silu(h1) * h3).astype(BF16), w2_ref[...], preferred_element_type=F32)
    y = _layer_norm(DEEPNORM_ALPHA * x1 + ff, g2_ref[...], b2_ref[...])
    o_ref[...] = y
    ob_ref[...] = y.astype(BF16)


def _mix_ffn(ya, yb, wa, wb, res, g1, b1, w1, w3, w2, g2, b2, tm):
    m, d = res.shape
    row = lambda i: (i, 0)
    consts = (wa, wb, None, g1, b1, w1, w3, w2, g2, b2)
    return pl.pallas_call(
        _mix_ffn_kernel,
        grid=(m // tm,),
        in_specs=[pl.BlockSpec((tm, ya.shape[1]), row), pl.BlockSpec((tm, yb.shape[1]), row)]
        + [pl.BlockSpec((tm, d), row) if c is None else _resident(c.shape) for c in consts],
        out_specs=[pl.BlockSpec((tm, d), row), pl.BlockSpec((tm, d), row)],
        out_shape=[jax.ShapeDtypeStruct((m, d), F32), jax.ShapeDtypeStruct((m, d), BF16)],
        compiler_params=_params("parallel"),
        name="out_proj_ffn_ln",
    )(ya, yb, wa, wb, res, g1, b1, w1, w3, w2, g2, b2)


def _mix_router_kernel(ya_ref, yb_ref, wa_ref, wb_ref, res_ref, g_ref, b_ref, w_ref, before_ref, o_ref, ob_ref,
                       gates_ref, rank_ref, rank_t_ref, cnt_ref, carry_ref):
    @pl.when(pl.program_id(0) == 0)
    def _():
        carry_ref[...] = jnp.zeros_like(carry_ref)

    x1 = _outproj_ln_value(ya_ref, yb_ref, wa_ref, wb_ref, res_ref, g_ref, b_ref)
    o_ref[...] = x1
    ob_ref[...] = x1.astype(BF16)
    logits = _mm_split(w_ref[...], x1, _NT)
    sub = lax.broadcasted_iota(jnp.int32, logits.shape, 0)
    neg = jnp.float32(-jnp.inf)
    l1 = jnp.where(sub < N_EXPERTS, logits, neg)
    m1 = jnp.max(l1, axis=0, keepdims=True)
    i1 = jnp.min(jnp.where(l1 == m1, sub, ROUTER_ROWS), axis=0, keepdims=True)
    l2 = jnp.where(sub == i1, neg, l1)
    m2 = jnp.max(l2, axis=0, keepdims=True)
    i2 = jnp.min(jnp.where(l2 == m2, sub, ROUTER_ROWS), axis=0, keepdims=True)
    ex = jnp.exp(m2 - m1)
    w_top = 1.0 / (1.0 + ex)
    gates_t = jnp.where(sub == i1, w_top, 0.0) + jnp.where(sub == i2, ex * w_top, 0.0)

    sel = jnp.where(sub == i1, 1.0, 0.0) + jnp.where(sub == i2, 1.0, 0.0)
    before = _mm(sel, before_ref[...])
    carry = carry_ref[...]
    rank_t = jnp.where(sel > 0.0, carry[:, :1] + before, -1.0)
    rank_t_ref[...] = rank_t[:SUBLANES, :]
    carry = carry + jnp.sum(sel, axis=1, keepdims=True)
    carry_ref[...] = carry
    cnt_ref[...] = carry[:SUBLANES, :]
    pad = jnp.zeros((LANES - ROUTER_ROWS, sel.shape[1]), F32)
    gates_ref[...] = jnp.concatenate([gates_t, pad], axis=0).T
    rank_ref[...] = jnp.concatenate([rank_t, pad], axis=0).T


def _mix_router(ya, yb, wa, wb, res, g, b, router):
    m, d = res.shape
    nblk = m // MOE_TB
    row = lambda i: (i, 0)
    w = jnp.pad(router.T, ((0, ROUTER_ROWS - N_EXPERTS), (0, 0)))
    earlier = (jnp.arange(MOE_TB)[:, None] < jnp.arange(MOE_TB)[None, :]).astype(BF16)
    return pl.pallas_call(
        _mix_router_kernel,
        grid=(nblk,),
        in_specs=[pl.BlockSpec((MOE_TB, ya.shape[1]), row), pl.BlockSpec((MOE_TB, yb.shape[1]), row),
                  _resident(wa.shape), _resident(wb.shape), pl.BlockSpec((MOE_TB, d), row),
                  _resident((1, d)), _resident((1, d)), _resident(w.shape), _resident(earlier.shape)],
        out_specs=[pl.BlockSpec((MOE_TB, d), row), pl.BlockSpec((MOE_TB, d), row),
                   pl.BlockSpec((MOE_TB, LANES), row), pl.BlockSpec((MOE_TB, LANES), row),
                   pl.BlockSpec((SUBLANES, MOE_TB), lambda i: (0, i)), pl.BlockSpec((SUBLANES, LANES), row)],
        out_shape=[jax.ShapeDtypeStruct((m, d), F32), jax.ShapeDtypeStruct((m, d), BF16),
                   jax.ShapeDtypeStruct((m, LANES), F32), jax.ShapeDtypeStruct((m, LANES), F32),
                   jax.ShapeDtypeStruct((SUBLANES, m), F32), jax.ShapeDtypeStruct((nblk * SUBLANES, LANES), F32)],
        scratch_shapes=[pltpu.VMEM((ROUTER_ROWS, LANES), F32)],
        compiler_params=_params("arbitrary"),
        name="out_proj_ln_router",
    )(ya, yb, wa, wb, res, g, b, w, earlier)


def _moe_plan(cnt, n_tok):
    i32 = jnp.int32
    nblk = n_tok // MOE_TB
    c_inc = cnt.reshape(nblk, SUBLANES, LANES)[:, :N_EXPERTS, 0].astype(i32)
    c_exc = jnp.concatenate([jnp.zeros((1, N_EXPERTS), i32), c_inc[:-1]], axis=0)
    gsz = (c_inc[-1] + MOE_TE - 1) // MOE_TE * MOE_TE
    gend = jnp.cumsum(gsz)
    start = gend - gsz
    n_et = _moe_rows(n_tok) // MOE_TE
    n_act = gend[-1] // MOE_TE
    et = jnp.minimum(jnp.arange(n_et, dtype=i32), jnp.maximum(n_act - 1, 0))
    et_expert = jnp.sum((gend[None, :] <= (et * MOE_TE)[:, None]).astype(i32), axis=1)
    et_active = (jnp.arange(n_et, dtype=i32) < n_act).astype(i32)

    lo = start[None, :] + c_exc
    hi = start[None, :] + c_inc
    first_tile = lo // MOE_TR
    n_items = jnp.where(hi > lo, (hi - 1) // MOE_TR - first_tile + 1, 0)
    wmax = _moe_rows(n_tok) // MOE_TR + N_EXPERTS * nblk
    exp_id = jnp.broadcast_to(jnp.arange(N_EXPERTS, dtype=i32)[None, :], (nblk, N_EXPERTS))

    n, ft, exp = (a.reshape(-1) for a in (n_items, first_tile, exp_id))
    inc = jnp.cumsum(n)
    wc = jnp.minimum(jnp.arange(wmax, dtype=i32), jnp.maximum(inc[-1] - 1, 0))
    idx = jnp.sum((inc[None, :] <= wc[:, None]).astype(i32), axis=1)
    per_block = jnp.sum(n_items, axis=1)
    by_block = (ft[idx] + wc - (inc - n)[idx], exp[idx], jnp.cumsum(per_block) - per_block, per_block)

    row0 = jnp.arange(_moe_rows(n_tok) // MOE_TR, dtype=i32) * MOE_TR
    group = jnp.minimum(jnp.sum((gend[None, :] <= row0[:, None]).astype(i32), axis=1), N_EXPERTS - 1)
    rank_lo = row0 - start[group]
    count = c_inc[-1][group]
    rank_hi = jnp.minimum(rank_lo + MOE_TR, count)
    holds = (rank_lo < count) & (row0 < gend[-1])
    blk_lo = jnp.sum((c_inc[:, group].T <= rank_lo[:, None]).astype(i32), axis=1)
    blk_hi = jnp.sum((c_exc[:, group].T < rank_hi[:, None]).astype(i32), axis=1)
    by_tile = (group, jnp.minimum(blk_lo, nblk - 1), jnp.where(holds, blk_hi - blk_lo, 0))

    return start, (et, et_expert, et_active), by_tile, by_block


def _moe_rows(n_tok):
    return 2 * n_tok + N_EXPERTS * MOE_TE


def _one_hot_rows(dest, tile):
    row = lax.broadcasted_iota(jnp.int32, (MOE_TR, MOE_TB), 0) + tile * MOE_TR
    return jnp.where(dest == row.astype(F32), 1.0, 0.0).astype(BF16)


def _dispatch_kernel(exp_ref, blk0_ref, nblk_ref, x_ref, dest_ref, xs_ref):
    j = pl.program_id(0)
    xs_ref[...] = jnp.zeros_like(xs_ref)

    def block(i, carry):
        b = blk0_ref[j] + i
        onehot = _one_hot_rows(dest_ref[b, pl.ds(exp_ref[j], 1), :], j)
        xblk = x_ref[pl.ds(pl.multiple_of(b * MOE_TB, MOE_TB), MOE_TB), :]
        xs_ref[...] += jnp.dot(onehot, xblk, preferred_element_type=F32).astype(BF16)
        return carry

    lax.fori_loop(0, nblk_ref[j], block, 0)


def _dispatch(plan, xb, dest_t):
    exp, blk0, nblk = plan
    n, d = xb.shape
    rows = _moe_rows(n)
    dest3 = dest_t.reshape(N_EXPERTS, n // MOE_TB, MOE_TB).transpose(1, 0, 2)
    grid_spec = pltpu.PrefetchScalarGridSpec(
        num_scalar_prefetch=3,
        grid=(rows // MOE_TR,),
        in_specs=[_resident(xb.shape), _resident(dest3.shape)],
        out_specs=pl.BlockSpec((MOE_TR, d), lambda j, e, b, c: (j, 0)),
    )
    return pl.pallas_call(
        _dispatch_kernel,
        grid_spec=grid_spec,
        out_shape=jax.ShapeDtypeStruct((rows, d), BF16),
        compiler_params=_params("arbitrary"),
        name="moe_dispatch",
    )(exp, blk0, nblk, xb, dest3)


def _expert_kernel(et_ref, ee_ref, ea_ref, xs_ref, w1_ref, w3_ref, w2_ref, ys_ref, acc_ref):
    j = pl.program_id(0)
    f = pl.program_id(1)

    @pl.when((ea_ref[j] == 0) & (f == 0))
    def _():
        ys_ref[...] = jnp.zeros_like(ys_ref)

    @pl.when(ea_ref[j] == 1)
    def _():
        @pl.when(f == 0)
        def _():
            acc_ref[...] = jnp.zeros_like(acc_ref)

        x = xs_ref[...]
        h1 = jnp.dot(x, w1_ref[...], preferred_element_type=F32)
        h3 = jnp.dot(x, w3_ref[...], preferred_element_type=F32)
        acc_ref[...] += jnp.dot((_silu(h1) * h3).astype(BF16), w2_ref[...], preferred_element_type=F32)

        @pl.when(f == pl.num_programs(1) - 1)
        def _():
            ys_ref[...] = acc_ref[...].astype(BF16)


def _experts(plan, xs, w1, w3, w2, tf):
    et, ee, ea = plan
    rows, d = xs.shape
    nf = w1.shape[2] // tf
    fidx = lambda f, a, j: f * a[j] + (nf - 1) * (1 - a[j])
    grid_spec = pltpu.PrefetchScalarGridSpec(
        num_scalar_prefetch=3,
        grid=(et.shape[0], nf),
        in_specs=[pl.BlockSpec((MOE_TE, d), lambda j, f, t, e, a: (t[j], 0)),
                  pl.BlockSpec((None, d, tf), lambda j, f, t, e, a: (e[j], 0, fidx(f, a, j))),
                  pl.BlockSpec((None, d, tf), lambda j, f, t, e, a: (e[j], 0, fidx(f, a, j))),
                  pl.BlockSpec((None, tf, d), lambda j, f, t, e, a: (e[j], fidx(f, a, j), 0))],
        out_specs=pl.BlockSpec((MOE_TE, d), lambda j, f, t, e, a: (j, 0)),
        scratch_shapes=[pltpu.VMEM((MOE_TE, d), F32)],
    )
    return pl.pallas_call(
        _expert_kernel,
        grid_spec=grid_spec,
        out_shape=jax.ShapeDtypeStruct((rows, d), BF16),
        compiler_params=_params("arbitrary", "arbitrary"),
        name="moe_experts",
    )(et, ee, ea, xs, w1, w3, w2)


def _combine_kernel(tile_ref, exp_ref, off_ref, cnt_ref, ys_hbm, dest_ref, gates_ref, res_ref, g_ref, b_ref, o_ref,
                    buf_ref, sem_ref, acc_ref):
    j = pl.program_id(0)
    base = off_ref[j]
    n = cnt_ref[j]

    def fetch(i, slot, first=base):
        row0 = pl.multiple_of(tile_ref[first + i] * MOE_TR, MOE_TR)
        return pltpu.make_async_copy(ys_hbm.at[pl.ds(row0, MOE_TR), :], buf_ref.at[slot], sem_ref.at[slot])

    def prime(blk):
        for i0 in range(MOE_COMBINE_BUFS - 1):
            @pl.when(i0 < cnt_ref[blk])
            def _(i0=i0):
                fetch(i0, i0, off_ref[blk]).start()

    @pl.when(j == 0)
    def _():
        prime(0)

    acc_ref[...] = jnp.zeros_like(acc_ref)
    dest = dest_ref[...]
    gates = gates_ref[...]
    lane = lax.broadcasted_iota(jnp.int32, dest.shape, 1)

    def item(i, carry):
        slot = lax.rem(i, MOE_COMBINE_BUFS)
        ahead = i + (MOE_COMBINE_BUFS - 1)

        @pl.when(ahead < n)
        def _():
            fetch(ahead, lax.rem(ahead, MOE_COMBINE_BUFS)).start()

        mine = lane == exp_ref[base + i]
        dcol = jnp.sum(jnp.where(mine, dest, 0.0), axis=1, keepdims=True)
        gcol = jnp.sum(jnp.where(mine, gates, 0.0), axis=1, keepdims=True)
        col = lax.broadcasted_iota(jnp.int32, (MOE_TB, MOE_TR), 1) + tile_ref[base + i] * MOE_TR
        onehot = jnp.where(dcol == col.astype(F32), 1.0, 0.0).astype(BF16)
        fetch(i, slot).wait()
        acc_ref[...] += jnp.dot(onehot, buf_ref[slot], preferred_element_type=F32) * gcol
        return carry

    lax.fori_loop(0, n, item, 0)

    @pl.when(j + 1 < pl.num_programs(0))
    def _():
        prime(j + 1)

    o_ref[...] = _layer_norm(DEEPNORM_ALPHA * res_ref[...] + acc_ref[...], g_ref[...], b_ref[...])


def _combine_ln(plan, ys, dest, gates, res, g, b):
    tile, exp, off, cnt = plan
    n, d = res.shape
    tok = lambda j, *_: (j, 0)
    grid_spec = pltpu.PrefetchScalarGridSpec(
        num_scalar_prefetch=4,
        grid=(n // MOE_TB,),
        in_specs=[pl.BlockSpec(memory_space=pl.ANY),
                  pl.BlockSpec((MOE_TB, LANES), tok), pl.BlockSpec((MOE_TB, LANES), tok), pl.BlockSpec((MOE_TB, d), tok),
                  _resident((1, d)), _resident((1, d))],
        out_specs=pl.BlockSpec((MOE_TB, d), tok),
        scratch_shapes=[pltpu.VMEM((MOE_COMBINE_BUFS, MOE_TR, d), BF16), pltpu.SemaphoreType.DMA((MOE_COMBINE_BUFS,)),
                        pltpu.VMEM((MOE_TB, d), F32)],
    )
    return pl.pallas_call(
        _combine_kernel,
        grid_spec=grid_spec,
        out_shape=jax.ShapeDtypeStruct((n, d), F32),
        compiler_params=_params("arbitrary"),
        name="moe_combine_ln",
    )(tile, exp, off, cnt, ys, dest, gates, res, g, b)


def _moe_ln(x, xb, routing, w1, w3, w2, g, b):
    n = x.shape[0]
    gates, rank, rank_t, cnt = routing
    start, expert_tiles, by_tile, by_block = _moe_plan(cnt, n)
    startf = start.astype(F32)
    dest = jnp.where(rank >= 0.0, rank + jnp.pad(startf, (0, LANES - N_EXPERTS))[None, :], -1.0)
    dest_t = jnp.where(rank_t >= 0.0, rank_t + startf[:, None], -1.0)
    xs = _dispatch(by_tile, xb, dest_t)
    ys = _experts(expert_tiles, xs, w1, w3, w2, tf=EXPERT_F_TILE)
    return _combine_ln(by_block, ys, dest, gates, x, g, b)


def _gla_kernel(q_ref, k_ref, v_ref, g_ref, gk_ref, gkw_ref, gkb_ref, nw_ref, sel_ref, o_ref, st_ref, bc_scr,
                qd_scr, kd_scr):
    @pl.when(pl.program_id(1) == 0)
    def _():
        st_ref[...] = jnp.zeros_like(st_ref)

    tt = q_ref.shape[0]
    tri = _tri(CHUNK)
    nw = nw_ref[...]

    pre = _mm_split(gk_ref[...], gkw_ref[...]) + gkb_ref[...]
    gk = (jnp.minimum(pre, 0.0) - jnp.log(1.0 + jnp.exp(-jnp.abs(pre)))) * (1.0 / GLA_GATE_NORM)
    bc_all = _mm_01_left(sel_ref[...], gk)
    bc_scr[...] = bc_all
    qd_scr[...] = q_ref[...] * (GLA_DK ** -0.5) * jnp.exp(bc_all)
    kd_scr[...] = k_ref[...] * jnp.exp(-bc_all)

    def chunk(c, carry):
        rows = pl.ds(pl.multiple_of(c * CHUNK, CHUNK), CHUNK)
        bc = bc_scr[rows, :]
        bl = bc[CHUNK - 1:CHUNK, :]
        qd = qd_scr[rows, :]
        kd = kd_scr[rows, :]
        kl = k_ref[rows, :] * jnp.exp(bl - bc)
        dec = jnp.exp(bl)
        heads = range(GLA_HEADS)
        sks = [slice(h * GLA_DK, (h + 1) * GLA_DK) for h in heads]
        svs = [slice(h * GLA_DV, (h + 1) * GLA_DV) for h in heads]
        att = [jnp.where(tri, _mm(qd[:, sk], kd[:, sk], _NT), 0.0) for sk in sks]
        vs = [v_ref[rows, sv] for sv in svs]
        sts = [st_ref[h] for h in heads]
        os_ = [_mm(att[h], vs[h]) + _mm(qd[:, sks[h]], sts[h], _NT) for h in heads]
        for h in heads:
            st_ref[h] = sts[h] * dec[:, sks[h]] + _mm(vs[h], kl[:, sks[h]], _TN)
        for h in heads:
            o = os_[h]
            ms = jnp.mean(o * o, axis=-1, keepdims=True)
            y = o * lax.rsqrt(ms + GLA_NORM_EPS) * nw * _silu(g_ref[rows, svs[h]])
            o_ref[rows, svs[h]] = y.astype(o_ref.dtype)
        return carry

    lax.fori_loop(0, tt // CHUNK, chunk, 0, unroll=2)


def _gla(p, gkw, gkb, nw, tt):
    bsz, t, _ = p.shape

    def col(name):
        off, w = EVEN_COLS[name]
        return pl.BlockSpec((None, tt, w), lambda b, i, j=off // w: (b, i, j))

    fix = lambda b, i: (0, 0)
    return pl.pallas_call(
        _gla_kernel,
        grid=(bsz, t // tt),
        in_specs=[col("q"), col("k"), col("v"), col("g"), col("gk"),
                  pl.BlockSpec(gkw.shape, fix), pl.BlockSpec(gkb.shape, fix), pl.BlockSpec(nw.shape, fix),
                  pl.BlockSpec((tt, tt), fix)],
        out_specs=pl.BlockSpec((None, tt, GLA_VW), lambda b, i: (b, i, 0)),
        out_shape=jax.ShapeDtypeStruct((bsz, t, GLA_VW), BF16),
        scratch_shapes=[pltpu.VMEM((GLA_HEADS, GLA_DV, GLA_DK), F32)] + [pltpu.VMEM((tt, GLA_QK), F32)] * 3,
        compiler_params=_params("parallel", "arbitrary"),
        name="gla",
    )(p, p, p, p, p, gkw, gkb, nw, _chunk_sel(tt, CHUNK))


def _rglru_kernel(x_ref, gate_ref, cw_ref, cb_ref, wa_ref, ba_ref, wx_ref, bx_ref, lam_ref, o_ref,
                  tail_ref, h_ref, a_scr, u_scr):
    @pl.when(pl.program_id(1) == 0)
    def _():
        tail_ref[...] = jnp.zeros_like(tail_ref)
        h_ref[...] = jnp.zeros_like(h_ref)

    tt = x_ref.shape[0]
    x = x_ref[...]
    cw = cw_ref[...]
    xc = x * cw[CONV_K - 1:CONV_K, :] + cb_ref[...]
    for s in range(1, CONV_K):
        xc += _shift_rows(x, tail_ref, s) * cw[CONV_K - 1 - s:CONV_K - s, :]
    tail_ref[...] = x[tt - SUBLANES:, :]

    r = _sigmoid(_mm(xc, wa_ref[...]) + ba_ref[...])
    i = _sigmoid(_mm(xc, wx_ref[...]) + bx_ref[...])
    log_a = -LRU_C * r * _softplus(-lam_ref[...])
    a_scr[...] = jnp.exp(log_a)
    u_scr[...] = jnp.sqrt(1.0 - jnp.exp(2.0 * log_a)) * (i * xc)

    row = lax.broadcasted_iota(jnp.int32, (SUBLANES, x.shape[1]), 0)

    def step(i, h):
        rows = pl.ds(pl.multiple_of(i * SUBLANES, SUBLANES), SUBLANES)
        a = a_scr[rows, :]
        u = u_scr[rows, :]
        for s in (1, 2, 4):
            u = u + a * jnp.where(row >= s, pltpu.roll(u, s, axis=0), 0.0)
            a = a * jnp.where(row >= s, pltpu.roll(a, s, axis=0), 1.0)
        hb = u + a * h
        u_scr[rows, :] = hb
        return hb[SUBLANES - 1:, :]

    h_ref[...] = lax.fori_loop(0, tt // SUBLANES, step, h_ref[...], unroll=4)
    gate = gate_ref[...]
    gelu = 0.5 * gate * (1.0 + jnp.tanh(math.sqrt(2.0 / math.pi) * (gate + 0.044715 * gate * gate * gate)))
    o_ref[...] = (u_scr[...] * gelu).astype(o_ref.dtype)


def _rglru(p, cw, cb, wa, ba, wx, bx, lam, tt):
    bsz, t, _ = p.shape
    w = LRU_WIDTH

    def col(name):
        off, _ = EVEN_COLS[name]
        return pl.BlockSpec((None, tt, w), lambda b, i, j=off // w: (b, i, j))

    fix = lambda b, i: (0, 0)
    vec = pl.BlockSpec((1, w), fix)
    return pl.pallas_call(
        _rglru_kernel,
        grid=(bsz, t // tt),
        in_specs=[col("xb"), col("gate"), pl.BlockSpec((CONV_K, w), fix), vec,
                  pl.BlockSpec((w, w), fix), vec, pl.BlockSpec((w, w), fix), vec, vec],
        out_specs=pl.BlockSpec((None, tt, w), lambda b, i: (b, i, 0)),
        out_shape=jax.ShapeDtypeStruct((bsz, t, w), BF16),
        scratch_shapes=[pltpu.VMEM((SUBLANES, w), F32), pltpu.VMEM((1, w), F32),
                        pltpu.VMEM((tt, w), F32), pltpu.VMEM((tt, w), F32)],
        compiler_params=_params("parallel", "arbitrary"),
        name="rglru",
    )(p, p, cw, cb, wa, ba, wx, bx, lam)


def _ssd_kernel(z_ref, x_ref, bm_ref, cm_ref, dt_ref, cwx_ref, cbx_ref, cwb_ref, cbb_ref, cwc_ref, cbc_ref,
                dtb_ref, alog_ref, dskip_ref, nw_ref, expand_ref, sel_ref, o_ref,
                tx_ref, tb_ref, tc_ref, st_ref, xs_scr, bs_scr, cs_scr, acs_scr, xdt_scr, xdec_scr, ea_scr):
    @pl.when(pl.program_id(1) == 0)
    def _():
        tx_ref[...] = jnp.zeros_like(tx_ref)
        tb_ref[...] = jnp.zeros_like(tb_ref)
        tc_ref[...] = jnp.zeros_like(tc_ref)
        st_ref[...] = jnp.zeros_like(st_ref)

    tt = x_ref.shape[0]

    def conv_silu(src_ref, tail_ref, cw_ref, cb_ref, dst_ref):
        x = src_ref[...]
        cw = cw_ref[...]
        y = x * cw[CONV_K - 1:CONV_K, :] + cb_ref[...]
        for s in range(1, CONV_K):
            y += _shift_rows(x, tail_ref, s) * cw[CONV_K - 1 - s:CONV_K - s, :]
        tail_ref[...] = x[tt - SUBLANES:, :]
        dst_ref[...] = _silu(y)

    conv_silu(x_ref, tx_ref, cwx_ref, cbx_ref, xs_scr)
    conv_silu(bm_ref, tb_ref, cwb_ref, cbb_ref, bs_scr)
    conv_silu(cm_ref, tc_ref, cwc_ref, cbc_ref, cs_scr)

    tri = _tri(CHUNK)
    expand = expand_ref[...]
    dskip = dskip_ref[...]
    nw = nw_ref[...]
    hpg = SSD_HEADS // SSD_GROUPS

    dtc = _softplus(dt_ref[...] + dtb_ref[...])
    da = dtc * -jnp.exp(alog_ref[...])
    sums = _mm_01_left(sel_ref[...], da)
    acs_all = sums[:tt]
    tot = sums[tt:]
    acs_scr[...] = acs_all
    xdt_all = xs_scr[...] * _mm_01(dtc, expand)
    xdt_scr[...] = xdt_all
    xdec_scr[...] = xdt_all * _mm_01(jnp.exp(tot - acs_all), expand)
    ea_scr[...] = _mm_01(jnp.exp(acs_all), expand)

    def chunk(c, carry):
        rows = pl.ds(pl.multiple_of(c * CHUNK, CHUNK), CHUNK)
        acs = acs_scr[rows, :]
        acs_t = acs.T
        ea_x = ea_scr[rows, :]
        x = xs_scr[rows, :]
        xdt = xdt_scr[rows, :]
        xdec = xdec_scr[rows, :]
        cd_x = ea_x[CHUNK - 1:CHUNK, :]
        ys = []
        for g in range(SSD_GROUPS):
            sg = slice(g * SSD_GROUP_WIDTH, (g + 1) * SSD_GROUP_WIDTH)
            ss = slice(g * SSD_STATE, (g + 1) * SSD_STATE)
            bg = bs_scr[rows, ss]
            cg = cs_scr[rows, ss]
            cb = _mm(cg, bg, _NT)
            st = st_ref[g]
            yg = _mm(cg, st) * ea_x[:, sg]
            st_ref[g] = st * cd_x[:, sg] + _mm(bg, xdec[:, sg], _TN)
            yh = []
            for j in range(hpg):
                h = g * hpg + j
                seg = acs[:, h:h + 1] - acs_t[h:h + 1, :]
                lmat = jnp.exp(jnp.where(tri, seg, -jnp.inf))
                sh = slice(h * SSD_HEADDIM, (h + 1) * SSD_HEADDIM)
                yh.append(_mm(cb * lmat, xdt[:, sh]))
            ys.append(yg + jnp.concatenate(yh, axis=1))
        y = jnp.concatenate(ys, axis=1) + x * dskip
        y = y * _silu(z_ref[rows, :])
        outs = []
        for g in range(SSD_GROUPS):
            sg = slice(g * SSD_GROUP_WIDTH, (g + 1) * SSD_GROUP_WIDTH)
            yg = y[:, sg]
            ms = jnp.mean(yg * yg, axis=-1, keepdims=True)
            outs.append(yg * lax.rsqrt(ms + SSD_NORM_EPS))
        o_ref[rows, :] = (jnp.concatenate(outs, axis=1) * nw).astype(o_ref.dtype)
        return carry

    lax.fori_loop(0, tt // CHUNK, chunk, 0, unroll=2)


def _ssd(p, cw, cb, dtb, alog, dskip_x, nw, expand, tt):
    bsz, t, _ = p.shape

    def col(name):
        off, w = ODD_COLS[name]
        return pl.BlockSpec((None, tt, w), lambda b, i, j=off // w: (b, i, j))

    fix = lambda b, i: (0, 0)
    full = lambda a: pl.BlockSpec(a.shape, fix)
    gs = SSD_GROUPS * SSD_STATE
    cwx, cwb, cwc = cw[:, :SSD_INNER], cw[:, SSD_INNER:SSD_INNER + gs], cw[:, SSD_INNER + gs:]
    cbx, cbb, cbc = cb[:, :SSD_INNER], cb[:, SSD_INNER:SSD_INNER + gs], cb[:, SSD_INNER + gs:]
    sel = jnp.concatenate([_chunk_sel(tt, CHUNK), _chunk_sel(tt, CHUNK, "all")], axis=0)
    args = (cwx, cbx, cwb, cbb, cwc, cbc, dtb, alog, dskip_x, nw, expand, sel)
    return pl.pallas_call(
        _ssd_kernel,
        grid=(bsz, t // tt),
        in_specs=[col("z"), col("x"), col("bm"), col("cm"), col("dt")] + [full(a) for a in args],
        out_specs=pl.BlockSpec((None, tt, SSD_INNER), lambda b, i: (b, i, 0)),
        out_shape=jax.ShapeDtypeStruct((bsz, t, SSD_INNER), BF16),
        scratch_shapes=[pltpu.VMEM((SUBLANES, SSD_INNER), F32), pltpu.VMEM((SUBLANES, gs), F32),
                        pltpu.VMEM((SUBLANES, gs), F32),
                        pltpu.VMEM((SSD_GROUPS, SSD_STATE, SSD_GROUP_WIDTH), F32),
                        pltpu.VMEM((tt, SSD_INNER), F32), pltpu.VMEM((tt, gs), F32), pltpu.VMEM((tt, gs), F32),
                        pltpu.VMEM((tt, LANES), F32)] + [pltpu.VMEM((tt, SSD_INNER), F32)] * 3,
        compiler_params=_params("parallel", "arbitrary"),
        name="ssd",
    )(p, p, p, p, p, *args)


def _rwkv_kernel(r_ref, k_ref, v_ref, wa_ref, xg_ref, mur_ref, muk_ref, muv_ref, muwa_ref, mug_ref,
                 w0_ref, w2_ref, a0_ref, a2_ref, g2_ref, kk_ref, ka_ref, rk_ref, lng_ref, lnb_ref, ones_ref, sel_ref,
                 o_ref, tr_ref, tk_ref, tv_ref, twa_ref, tg_ref, st_ref, y_scr,
                 at_scr, rt_scr, bt_scr, kt_scr, bh_scr, kh_scr, v_scr, wc_scr):
    @pl.when(pl.program_id(1) == 0)
    def _():
        for ref in (tr_ref, tk_ref, tv_ref, twa_ref, tg_ref, st_ref):
            ref[...] = jnp.zeros_like(ref)

    tt = r_ref.shape[0]
    hd = RWKV_HEADDIM

    def mix(p_ref, tail_ref, mu_ref):
        p = p_ref[...]
        prev = _shift_rows(p, tail_ref, 1)
        tail_ref[...] = p[tt - SUBLANES:, :]
        return p + (prev - p) * mu_ref[...]

    r = mix(r_ref, tr_ref, mur_ref)
    k = mix(k_ref, tk_ref, muk_ref)
    v = mix(v_ref, tv_ref, muv_ref)
    xwa = mix(wa_ref, twa_ref, muwa_ref)
    xg = mix(xg_ref, tg_ref, mug_ref)

    lane = lax.broadcasted_iota(jnp.int32, xwa.shape, 1)
    lora_in = jnp.where(lane < RWKV_DECAY_LORA, jnp.tanh(xwa), xwa)
    w_log = -_softplus(-(w0_ref[...] + _mm_split(lora_in, w2_ref[...]))) - 0.5
    lw = -jnp.exp(w_log)
    a_sig = _sigmoid(a0_ref[...] + _mm_split(lora_in, a2_ref[...]))
    gate = _mm_split(_sigmoid(xg), g2_ref[...])
    ones_bd = ones_ref[...]
    kk = k * kk_ref[...]
    kk = kk / jnp.maximum(jnp.sqrt(_mm_01(kk * kk, ones_bd)), 1e-12)
    k = k * (1.0 + (a_sig - 1.0) * ka_ref[...])

    cs = RWKV_CHUNK
    tri_incl = _tri(cs)
    tri_strict = _tri(cs, strict=True)
    tri_col = jnp.concatenate([tri_strict, tri_incl], axis=0)
    eye = (tri_incl & ~tri_strict).astype(F32)

    sums = _mm_01_left(sel_ref[...], lw)
    cum = sums[:tt]
    tot = sums[tt:]
    bvec = kk * a_sig
    e_neg = jnp.exp(-cum)
    e_last = jnp.exp(tot - cum)
    for ref, val in ((at_scr, -kk * jnp.exp(cum - lw)), (rt_scr, r * jnp.exp(cum)), (bt_scr, bvec * e_neg),
                     (kt_scr, k * e_neg), (bh_scr, bvec * e_last), (kh_scr, k * e_last), (v_scr, v),
                     (wc_scr, jnp.exp(tot))):
        ref[...] = val

    def chunk(c, carry):
        rows = pl.ds(pl.multiple_of(c * cs, cs), cs)
        at = at_scr[rows, :]
        rt = rt_scr[rows, :]
        bt = bt_scr[rows, :]
        kt = kt_scr[rows, :]
        bh = bh_scr[rows, :]
        kh = kh_scr[rows, :]
        vc = v_scr[rows, :]
        wc = wc_scr[pl.ds(pl.multiple_of(c * cs, cs), 1), :]
        heads = range(RWKV_HEADS)
        sls = [slice(h * hd, (h + 1) * hd) for h in heads]
        amat = [_mm(jnp.concatenate([at[:, sl], rt[:, sl]], axis=0),
                    jnp.concatenate([bt[:, sl], kt[:, sl]], axis=0), _NT) for sl in sls]
        a_ab = [jnp.where(tri_strict, m[:cs, :cs], 0.0) for m in amat]
        a_rb = [jnp.where(tri_incl, m[cs:, :cs], 0.0) for m in amat]
        a_xk = [jnp.where(tri_col, m[:, cs:], 0.0) for m in amat]
        avrv = [_mm(a_xk[h], vc[:, sls[h]]) for h in heads]
        inv = [eye + m for m in a_ab]
        pw = [_mm(m, m) for m in a_ab]
        for _ in range(int(math.log2(cs)) - 2):
            both = [_mm(jnp.concatenate([inv[h], pw[h]], axis=0), pw[h]) for h in heads]
            inv = [inv[h] + both[h][:cs] for h in heads]
            pw = [m[cs:] for m in both]
        inv = [inv[h] + _mm(inv[h], pw[h]) for h in heads]
        tu = [_mm(inv[h], jnp.concatenate([at[:, sls[h]], avrv[h][:cs]], axis=1)) for h in heads]
        qy = [_mm(a_rb[h], tu[h]) + jnp.concatenate([rt[:, sls[h]], avrv[h][cs:]], axis=1)
              for h in heads]
        zb =[_mm(tu[h], bh[:, sls[h]], _TN) for h in heads]
        vk = [_mm(vc[:, sls[h]], kh[:, sls[h]], _TN) for h in heads]
        for h in heads:
            st = st_ref[h]
            y_scr[rows, sls[h]] = _mm(qy[h][:, :hd], st, _NT) + qy[h][:, hd:]
            st_ref[h] = st * wc[:, sls[h]] + _mm(st, zb[h][:hd]) + zb[h][hd:] + vk[h]
        return carry

    lax.fori_loop(0, tt // cs, chunk, 0, unroll=2)

    y = y_scr[...]
    inv_n = 1.0 / hd
    mu_y = _mm_01(y, ones_bd) * inv_n
    dy = y - mu_y
    var_y = _mm_01(dy * dy, ones_bd) * inv_n
    yn = dy * lax.rsqrt(var_y + RWKV_GN_EPS) * lng_ref[...] + lnb_ref[...]
    bonus = _mm_01(r * k * rk_ref[...], ones_bd) * v
    o_ref[...] = ((yn + bonus) * gate).astype(o_ref.dtype)


def _rwkv(p, mu, w0, w2p, a0, a2p, g2, k_k, k_a, r_k, ln_g, ln_b, ones_bd, tt):
    bsz, t, _ = p.shape
    w = RWKV_WIDTH

    def col(name):
        off, wd = ODD_COLS[name]
        return pl.BlockSpec((None, tt, wd), lambda b, i, j=off // wd: (b, i, j))

    fix = lambda b, i: (0, 0)
    full = lambda a: pl.BlockSpec(a.shape, fix)
    mur, muk, muv = mu[:, :w], mu[:, w:2 * w], mu[:, 2 * w:3 * w]
    muwa, mug = mu[:, 3 * w:3 * w + LANES], mu[:, 3 * w + LANES:]
    sel = jnp.concatenate([_chunk_sel(tt, RWKV_CHUNK), _chunk_sel(tt, RWKV_CHUNK, "all")], axis=0)
    args = (mur, muk, muv, muwa, mug, w0, w2p, a0, a2p, g2, k_k, k_a, r_k, ln_g, ln_b, ones_bd, sel)
    return pl.pallas_call(
        _rwkv_kernel,
        grid=(bsz, t // tt),
        in_specs=[col("r"), col("k"), col("v"), col("wa"), col("xg")] + [full(a) for a in args],
        out_specs=pl.BlockSpec((None, tt, w), lambda b, i: (b, i, 0)),
        out_shape=jax.ShapeDtypeStruct((bsz, t, w), BF16),
        scratch_shapes=[pltpu.VMEM((SUBLANES, w), F32), pltpu.VMEM((SUBLANES, w), F32),
                        pltpu.VMEM((SUBLANES, w), F32), pltpu.VMEM((SUBLANES, LANES), F32),
                        pltpu.VMEM((SUBLANES, LANES), F32),
                        pltpu.VMEM((RWKV_HEADS, RWKV_HEADDIM, RWKV_HEADDIM), F32)]
        + [pltpu.VMEM((tt, w), F32)] * 9,
        compiler_params=_params("parallel", "arbitrary"),
        name="rwkv7",
    )(p, p, p, p, p, *args)


def _reorder_cols(w, pieces, order, width):
    parts, pos = [], 0
    for name, (off, slot) in order.items():
        assert off == pos, "slots must be listed in order and contiguous"
        start, size = pieces[name]
        parts.append(w[:, start:start + size])
        if size < slot:
            parts.append(jnp.zeros((w.shape[0], slot - size), w.dtype))
        pos += slot
    assert pos == width
    return jnp.concatenate(parts, axis=1)


def _block_diag(w):
    n, i, j = w.shape
    eye = jnp.eye(n, dtype=w.dtype)
    return (eye[:, None, :, None] * w[:, :, None, :]).reshape(n * i, n * j)


def _row(v, width=None):
    v = v.reshape(1, -1).astype(F32)
    if width is not None and v.shape[1] < width:
        v = jnp.pad(v, ((0, 0), (0, width - v.shape[1])))
    return v


def _even_layer(x, x_in, w_in, gk_w2, gk_b, gla_norm, conv_w, conv_b, wa, ba, wx, bx, lam,
                w_out, ln1_g, ln1_b, f_w1, f_w3, f_w2, ln2_g, ln2_b, bsz, t):
    qk, vw, lw = GLA_QK, GLA_VW, LRU_WIDTH
    pieces = dict(q=(0, qk), k=(qk, qk), v=(2 * qk, vw), g=(2 * qk + vw, vw), gk=(2 * qk + 2 * vw, GLA_GK_RANK),
                  xb=(2 * qk + 2 * vw + GLA_GK_RANK, lw), gate=(2 * qk + 2 * vw + GLA_GK_RANK + lw, lw))
    w_in_r = _reorder_cols(w_in, pieces, EVEN_COLS, EVEN_WIDTH).astype(BF16)
    p = _proj(x_in, w_in_r, tm=min(ROW_TILE, x.shape[0])).reshape(bsz, t, EVEN_WIDTH)
    tt = min(GLA_RGLRU_TIME_TILE, t)
    gkw = jnp.pad(gk_w2, ((0, LANES - GLA_GK_RANK), (0, 0)))
    y_a = _gla(p, gkw, _row(gk_b), _row(gla_norm), tt)
    y_b = _rglru(p, conv_w, _row(conv_b), _block_diag(wa).astype(BF16), _row(ba), _block_diag(wx).astype(BF16),
                 _row(bx), _row(lam), tt)
    n = bsz * t
    tm = min(ROW_TILE, n)
    return _mix_ffn(y_a.reshape(n, vw), y_b.reshape(n, lw), w_out[:vw].astype(BF16), w_out[vw:].astype(BF16), x,
                    _row(ln1_g), _row(ln1_b), f_w1.astype(BF16), f_w3.astype(BF16), f_w2.astype(BF16),
                    _row(ln2_g), _row(ln2_b), tm)


def _odd_layer(x, xb, w_in, conv_w, conv_b, dt_bias, a_log, d_skip, ssd_norm, mu, w0, w2, a0, a2, g2, k_k, k_a,
               r_k, rln_g, rln_b, w_out, ln1_g, ln1_b, router, ew1, ew3, ew2, ln2_g, ln2_b, bsz, t):
    si, gs, rw = SSD_INNER, SSD_GROUPS * SSD_STATE, RWKV_WIDTH
    o = 2 * si + 2 * gs + SSD_HEADS
    pieces = dict(z=(0, si), x=(si, si), bm=(2 * si, gs), cm=(2 * si + gs, gs), dt=(2 * si + 2 * gs, SSD_HEADS),
                  r=(o, rw), k=(o + rw, rw), v=(o + 2 * rw, rw),
                  wa=(o + 3 * rw, RWKV_DECAY_LORA + RWKV_AAA_LORA), xg=(o + 3 * rw + LANES, RWKV_GATE_LORA))
    w_in_r = _reorder_cols(w_in, pieces, ODD_COLS, ODD_WIDTH).astype(BF16)
    n = bsz * t
    p = _proj(xb, w_in_r, tm=min(ROW_TILE, n)).reshape(bsz, t, ODD_WIDTH)
    expand = jnp.pad(jnp.repeat(jnp.eye(SSD_HEADS, dtype=BF16), SSD_HEADDIM, axis=1), ((0, LANES - SSD_HEADS), (0, 0)))
    y_c = _ssd(p, conv_w, _row(conv_b), _row(dt_bias, LANES), _row(a_log, LANES),
               _row(jnp.repeat(d_skip, SSD_HEADDIM)), _row(ssd_norm), expand, tt=min(SSD_RWKV_TIME_TILE, t))
    zeros = jnp.zeros((RWKV_DECAY_LORA, rw), F32)
    ones_bd = _block_diag(jnp.ones((RWKV_HEADS, RWKV_HEADDIM, RWKV_HEADDIM), BF16))
    y_d = _rwkv(p, _row(mu), _row(w0), jnp.concatenate([w2, zeros]), _row(a0), jnp.concatenate([zeros, a2]), g2,
                _row(k_k), _row(k_a), _row(r_k), _row(rln_g), _row(rln_b), ones_bd, tt=min(SSD_RWKV_TIME_TILE, t))
    x1, x1b, *routing = _mix_router(y_c.reshape(n, si), y_d.reshape(n, rw), w_out[:si].astype(BF16),
                                    w_out[si:].astype(BF16), x, _row(ln1_g), _row(ln1_b), router)
    y = _moe_ln(x1, x1b, routing, ew1.astype(BF16), ew3.astype(BF16), ew2.astype(BF16), _row(ln2_g), _row(ln2_b))
    return y, None


def kernel(x, e_w_in, e_gk_w2, e_gk_b, e_gla_norm, e_conv_w, e_conv_b, e_lru_wa, e_lru_ba, e_lru_wx, e_lru_bx, e_lru_lambda, e_w_out, e_ln1_g, e_ln1_b, e_ffn_w1, e_ffn_w3, e_ffn_w2, e_ln2_g, e_ln2_b, o_w_in, o_conv_w, o_conv_b, o_dt_bias, o_a_log, o_d_skip, o_ssd_norm, o_rwkv_mu, o_rwkv_w0, o_rwkv_w2, o_rwkv_a0, o_rwkv_a2, o_rwkv_g2, o_rwkv_k_k, o_rwkv_k_a, o_rwkv_r_k, o_rwkv_ln_g, o_rwkv_ln_b, o_w_out, o_ln1_g, o_ln1_b, o_router, o_exp_w1, o_exp_w3, o_exp_w2, o_ln2_g, o_ln2_b):
    bsz, t, d = x.shape
    h = x.reshape(bsz * t, d)
    hb = h
    for i in range(DEPTH):
        j = i // 2
        if i % 2 == 0:
            h, hb = _even_layer(h, hb, e_w_in[j], e_gk_w2[j], e_gk_b[j], e_gla_norm[j], e_conv_w[j], e_conv_b[j],
                                e_lru_wa[j], e_lru_ba[j], e_lru_wx[j], e_lru_bx[j], e_lru_lambda[j], e_w_out[j],
                                e_ln1_g[j], e_ln1_b[j], e_ffn_w1[j], e_ffn_w3[j], e_ffn_w2[j], e_ln2_g[j],
                                e_ln2_b[j], bsz, t)
        else:
            h, hb = _odd_layer(h, hb, o_w_in[j], o_conv_w[j], o_conv_b[j], o_dt_bias[j], o_a_log[j], o_d_skip[j],
                               o_ssd_norm[j], o_rwkv_mu[j], o_rwkv_w0[j], o_rwkv_w2[j], o_rwkv_a0[j],
                               o_rwkv_a2[j], o_rwkv_g2[j], o_rwkv_k_k[j], o_rwkv_k_a[j], o_rwkv_r_k[j],
                               o_rwkv_ln_g[j], o_rwkv_ln_b[j], o_w_out[j], o_ln1_g[j], o_ln1_b[j], o_router[j],
                               o_exp_w1[j], o_exp_w3[j], o_exp_w2[j], o_ln2_g[j], o_ln2_b[j], bsz, t)
    return h.reshape(bsz, t, d)
```

```python
import math

import jax
import jax.numpy as jnp
from jax import lax
from jax.experimental import pallas as pl
from jax.experimental.pallas import tpu as pltpu

F32 = jnp.float32
BF16 = jnp.bfloat16

D_MODEL = 1024
DEPTH = 2
DEEPNORM_ALPHA = (2 * DEPTH) ** 0.25
LN_EPS = 1e-5
CONV_K = 4
CHUNK = 64

GLA_HEADS = 4
GLA_DK = 64
GLA_DV = 128
GLA_QK = GLA_HEADS * GLA_DK
GLA_VW = GLA_HEADS * GLA_DV
GLA_GK_RANK = 16
GLA_GATE_NORM = 16.0
GLA_NORM_EPS = 1e-5

LRU_WIDTH = 512
LRU_BLOCKS = 8
LRU_C = 8.0

SSD_HEADS = 16
SSD_HEADDIM = 64
SSD_INNER = SSD_HEADS * SSD_HEADDIM
SSD_GROUPS = 2
SSD_STATE = 128
SSD_GROUP_WIDTH = SSD_INNER // SSD_GROUPS
SSD_NORM_EPS = 1e-5

RWKV_HEADS = 8
RWKV_HEADDIM = 64
RWKV_WIDTH = RWKV_HEADS * RWKV_HEADDIM
RWKV_DECAY_LORA = 64
RWKV_AAA_LORA = 64
RWKV_GATE_LORA = 128
RWKV_GN_EPS = 64e-5
RWKV_CHUNK = 128

N_EXPERTS = 8
LANES = 128
SUBLANES = 8
assert N_EXPERTS == SUBLANES
ROUTER_ROWS = 16
MOE_TB = 512
MOE_TR = 256
MOE_TE = 512
assert MOE_TE % MOE_TR == 0
MOE_COMBINE_BUFS = 4
VMEM_LIMIT = 56 * 1024 * 1024

ROW_TILE = 512
GLA_RGLRU_TIME_TILE = 512
SSD_RWKV_TIME_TILE = 256
EXPERT_F_TILE = 1792

EVEN_COLS = dict(v=(0, 512), g=(512, 512), xb=(1024, 512), gate=(1536, 512), q=(2048, 256), k=(2304, 256),
                 gk=(2560, 128))
EVEN_WIDTH = 2688
ODD_COLS = dict(z=(0, 1024), x=(1024, 1024), r=(2048, 512), k=(2560, 512), v=(3072, 512), bm=(3584, 256),
                cm=(3840, 256), wa=(4096, 128), xg=(4224, 128), dt=(4352, 128))
ODD_WIDTH = 4480


def _mm(a, b, dims=((1,), (0,))):
    return lax.dot_general(a.astype(BF16), b.astype(BF16), (dims, ((), ())), preferred_element_type=F32)


def _mm_split(a, b, dims=((1,), (0,))):
    a_hi = a.astype(BF16)
    b_hi = b.astype(BF16)
    a_lo = (a - a_hi.astype(F32)).astype(BF16)
    b_lo = (b - b_hi.astype(F32)).astype(BF16)
    return _mm(a_hi, b_hi, dims) + _mm(a_lo, b_hi, dims) + _mm(a_hi, b_lo, dims)


_NT = ((1,), (1,))
_TN = ((0,), (0,))


def _mm_01(x, sel):
    hi = x.astype(BF16)
    lo = (x - hi.astype(F32)).astype(BF16)
    return (jnp.dot(hi, sel, preferred_element_type=F32) + jnp.dot(lo, sel, preferred_element_type=F32))


def _sigmoid(x):
    return 1.0 / (1.0 + jnp.exp(-x))


def _softplus(x):
    return jnp.maximum(x, 0.0) + jnp.log(1.0 + jnp.exp(-jnp.abs(x)))


def _silu(x):
    return x * _sigmoid(x)


def _layer_norm(h, g, b):
    mu = jnp.mean(h, axis=-1, keepdims=True)
    d = h - mu
    var = jnp.mean(d * d, axis=-1, keepdims=True)
    return d * lax.rsqrt(var + LN_EPS) * g + b


def _params(*sem):
    return pltpu.CompilerParams(dimension_semantics=sem, vmem_limit_bytes=VMEM_LIMIT)


def _tri(n, strict=False):
    r = lax.broadcasted_iota(jnp.int32, (n, n), 0)
    c = lax.broadcasted_iota(jnp.int32, (n, n), 1)
    return (r > c) if strict else (r >= c)


def _chunk_sel(n, chunk, kind="incl"):
    r = jnp.arange(n)[:, None]
    c = jnp.arange(n)[None, :]
    first = r - r % chunk
    upper = r if kind == "incl" else first + (chunk - 1)
    return ((c >= first) & (c <= upper)).astype(BF16)


def _mm_01_left(sel, x):
    hi = x.astype(BF16)
    lo = (x - hi.astype(F32)).astype(BF16)
    return jnp.dot(sel, hi, preferred_element_type=F32) + jnp.dot(sel, lo, preferred_element_type=F32)


def _shift_rows(p, tail_ref, s):
    n = p.shape[0]
    rolled = pltpu.roll(p, s, axis=0)
    head = pltpu.roll(tail_ref[...], s, axis=0)
    row = lax.broadcasted_iota(jnp.int32, (SUBLANES, p.shape[1]), 0)
    fixed = jnp.where(row < s, head, rolled[:SUBLANES])
    return jnp.concatenate([fixed, rolled[SUBLANES:]], axis=0) if n > SUBLANES else fixed


def _proj_kernel(x_ref, w_ref, o_ref):
    o_ref[...] = jnp.dot(x_ref[...].astype(BF16), w_ref[...], preferred_element_type=F32)


def _resident(shape):
    return pl.BlockSpec(shape, lambda *_: (0,) * len(shape), pipeline_mode=pl.Buffered(1))


def _proj(x, w, tm):
    m, k = x.shape
    n = w.shape[1]
    return pl.pallas_call(
        _proj_kernel,
        grid=(m // tm,),
        in_specs=[pl.BlockSpec((tm, k), lambda i: (i, 0)), _resident((k, n))],
        out_specs=pl.BlockSpec((tm, n), lambda i: (i, 0)),
        out_shape=jax.ShapeDtypeStruct((m, n), F32),
        compiler_params=_params("parallel"),
        name="in_proj",
    )(x, w)


def _outproj_ln_value(ya_ref, yb_ref, wa_ref, wb_ref, res_ref, g_ref, b_ref):
    acc = jnp.dot(ya_ref[...], wa_ref[...], preferred_element_type=F32)
    acc += jnp.dot(yb_ref[...], wb_ref[...], preferred_element_type=F32)
    return _layer_norm(DEEPNORM_ALPHA * res_ref[...] + acc, g_ref[...], b_ref[...])


def _mix_ffn_kernel(ya_ref, yb_ref, wa_ref, wb_ref, res_ref, g1_ref, b1_ref, w1_ref, w3_ref, w2_ref, g2_ref, b2_ref,
                    o_ref, ob_ref):
    x1 = _outproj_ln_value(ya_ref, yb_ref, wa_ref, wb_ref, res_ref, g1_ref, b1_ref)
    xb = x1.astype(BF16)
    h1 = jnp.dot(xb, w1_ref[...], preferred_element_type=F32)
    h3 = jnp.dot(xb, w3_ref[...], preferred_element_type=F32)
    ff = jnp.dot((_silu(h1) * h3).astype(BF16), w2_ref[...], preferred_element_type=F32)
    y = _layer_norm(DEEPNORM_ALPHA * x1 + ff, g2_ref[...], b2_ref[...])
    o_ref[...] = y
    ob_ref[...] = y.astype(BF16)


def _mix_ffn(ya, yb, wa, wb, res, g1, b1, w1, w3, w2, g2, b2, tm):
    m, d = res.shape
    row = lambda i: (i, 0)
    consts = (wa, wb, None, g1, b1, w1, w3, w2, g2, b2)
    return pl.pallas_call(
        _mix_ffn_kernel,
        grid=(m // tm,),
        in_specs=[pl.BlockSpec((tm, ya.shape[1]), row), pl.BlockSpec((tm, yb.shape[1]), row)]
        + [pl.BlockSpec((tm, d), row) if c is None else _resident(c.shape) for c in consts],
        out_specs=[pl.BlockSpec((tm, d), row), pl.BlockSpec((tm, d), row)],
        out_shape=[jax.ShapeDtypeStruct((m, d), F32), jax.ShapeDtypeStruct((m, d), BF16)],
        compiler_params=_params("parallel"),
        name="out_proj_ffn_ln",
    )(ya, yb, wa, wb, res, g1, b1, w1, w3, w2, g2, b2)


def _mix_router_kernel(ya_ref, yb_ref, wa_ref, wb_ref, res_ref, g_ref, b_ref, w_ref, before_ref, o_ref, ob_ref,
                       gates_ref, rank_ref, rank_t_ref, cnt_ref, carry_ref):
    @pl.when(pl.program_id(0) == 0)
    def _():
        carry_ref[...] = jnp.zeros_like(carry_ref)

    x1 = _outproj_ln_value(ya_ref, yb_ref, wa_ref, wb_ref, res_ref, g_ref, b_ref)
    o_ref[...] = x1
    ob_ref[...] = x1.astype(BF16)
    logits = _mm_split(w_ref[...], x1, _NT)
    sub = lax.broadcasted_iota(jnp.int32, logits.shape, 0)
    neg = jnp.float32(-jnp.inf)
    l1 = jnp.where(sub < N_EXPERTS, logits, neg)
    m1 = jnp.max(l1, axis=0, keepdims=True)
    i1 = jnp.min(jnp.where(l1 == m1, sub, ROUTER_ROWS), axis=0, keepdims=True)
    l2 = jnp.where(sub == i1, neg, l1)
    m2 = jnp.max(l2, axis=0, keepdims=True)
    i2 = jnp.min(jnp.where(l2 == m2, sub, ROUTER_ROWS), axis=0, keepdims=True)
    ex = jnp.exp(m2 - m1)
    w_top = 1.0 / (1.0 + ex)
    gates_t = jnp.where(sub == i1, w_top, 0.0) + jnp.where(sub == i2, ex * w_top, 0.0)

    sel = jnp.where(sub == i1, 1.0, 0.0) + jnp.where(sub == i2, 1.0, 0.0)
    before = _mm(sel, before_ref[...])
    carry = carry_ref[...]
    rank_t = jnp.where(sel > 0.0, carry[:, :1] + before, -1.0)
    rank_t_ref[...] = rank_t[:SUBLANES, :]
    carry = carry + jnp.sum(sel, axis=1, keepdims=True)
    carry_ref[...] = carry
    cnt_ref[...] = carry[:SUBLANES, :]
    pad = jnp.zeros((LANES - ROUTER_ROWS, sel.shape[1]), F32)
    gates_ref[...] = jnp.concatenate([gates_t, pad], axis=0).T
    rank_ref[...] = jnp.concatenate([rank_t, pad], axis=0).T


def _mix_router(ya, yb, wa, wb, res, g, b, router):
    m, d = res.shape
    nblk = m // MOE_TB
    row = lambda i: (i, 0)
    w = jnp.pad(router.T, ((0, ROUTER_ROWS - N_EXPERTS), (0, 0)))
    earlier = (jnp.arange(MOE_TB)[:, None] < jnp.arange(MOE_TB)[None, :]).astype(BF16)
    return pl.pallas_call(
        _mix_router_kernel,
        grid=(nblk,),
        in_specs=[pl.BlockSpec((MOE_TB, ya.shape[1]), row), pl.BlockSpec((MOE_TB, yb.shape[1]), row),
                  _resident(wa.shape), _resident(wb.shape), pl.BlockSpec((MOE_TB, d), row),
                  _resident((1, d)), _resident((1, d)), _resident(w.shape), _resident(earlier.shape)],
        out_specs=[pl.BlockSpec((MOE_TB, d), row), pl.BlockSpec((MOE_TB, d), row),
                   pl.BlockSpec((MOE_TB, LANES), row), pl.BlockSpec((MOE_TB, LANES), row),
                   pl.BlockSpec((SUBLANES, MOE_TB), lambda i: (0, i)), pl.BlockSpec((SUBLANES, LANES), row)],
        out_shape=[jax.ShapeDtypeStruct((m, d), F32), jax.ShapeDtypeStruct((m, d), BF16),
                   jax.ShapeDtypeStruct((m, LANES), F32), jax.ShapeDtypeStruct((m, LANES), F32),
                   jax.ShapeDtypeStruct((SUBLANES, m), F32), jax.ShapeDtypeStruct((nblk * SUBLANES, LANES), F32)],
        scratch_shapes=[pltpu.VMEM((ROUTER_ROWS, LANES), F32)],
        compiler_params=_params("arbitrary"),
        name="out_proj_ln_router",
    )(ya, yb, wa, wb, res, g, b, w, earlier)


def _moe_plan(cnt, n_tok):
    i32 = jnp.int32
    nblk = n_tok // MOE_TB
    c_inc = cnt.reshape(nblk, SUBLANES, LANES)[:, :N_EXPERTS, 0].astype(i32)
    c_exc = jnp.concatenate([jnp.zeros((1, N_EXPERTS), i32), c_inc[:-1]], axis=0)
    gsz = (c_inc[-1] + MOE_TE - 1) // MOE_TE * MOE_TE
    gend = jnp.cumsum(gsz)
    start = gend - gsz
    n_et = _moe_rows(n_tok) // MOE_TE
    n_act = gend[-1] // MOE_TE
    et = jnp.minimum(jnp.arange(n_et, dtype=i32), jnp.maximum(n_act - 1, 0))
    et_expert = jnp.sum((gend[None, :] <= (et * MOE_TE)[:, None]).astype(i32), axis=1)
    et_active = (jnp.arange(n_et, dtype=i32) < n_act).astype(i32)

    lo = start[None, :] + c_exc
    hi = start[None, :] + c_inc
    first_tile = lo // MOE_TR
    n_items = jnp.where(hi > lo, (hi - 1) // MOE_TR - first_tile + 1, 0)
    wmax = _moe_rows(n_tok) // MOE_TR + N_EXPERTS * nblk
    exp_id = jnp.broadcast_to(jnp.arange(N_EXPERTS, dtype=i32)[None, :], (nblk, N_EXPERTS))

    n, ft, exp = (a.reshape(-1) for a in (n_items, first_tile, exp_id))
    inc = jnp.cumsum(n)
    wc = jnp.minimum(jnp.arange(wmax, dtype=i32), jnp.maximum(inc[-1] - 1, 0))
    idx = jnp.sum((inc[None, :] <= wc[:, None]).astype(i32), axis=1)
    per_block = jnp.sum(n_items, axis=1)
    by_block = (ft[idx] + wc - (inc - n)[idx], exp[idx], jnp.cumsum(per_block) - per_block, per_block)

    row0 = jnp.arange(_moe_rows(n_tok) // MOE_TR, dtype=i32) * MOE_TR
    group = jnp.minimum(jnp.sum((gend[None, :] <= row0[:, None]).astype(i32), axis=1), N_EXPERTS - 1)
    rank_lo = row0 - start[group]
    count = c_inc[-1][group]
    rank_hi = jnp.minimum(rank_lo + MOE_TR, count)
    holds = (rank_lo < count) & (row0 < gend[-1])
    blk_lo = jnp.sum((c_inc[:, group].T <= rank_lo[:, None]).astype(i32), axis=1)
    blk_hi = jnp.sum((c_exc[:, group].T < rank_hi[:, None]).astype(i32), axis=1)
    by_tile = (group, jnp.minimum(blk_lo, nblk - 1), jnp.where(holds, blk_hi - blk_lo, 0))

    return start, (et, et_expert, et_active), by_tile, by_block


def _moe_rows(n_tok):
    return 2 * n_tok + N_EXPERTS * MOE_TE


def _one_hot_rows(dest, tile):
    row = lax.broadcasted_iota(jnp.int32, (MOE_TR, MOE_TB), 0) + tile * MOE_TR
    return jnp.where(dest == row.astype(F32), 1.0, 0.0).astype(BF16)


def _dispatch_kernel(exp_ref, blk0_ref, nblk_ref, x_ref, dest_ref, xs_ref):
    j = pl.program_id(0)
    xs_ref[...] = jnp.zeros_like(xs_ref)

    def block(i, carry):
        b = blk0_ref[j] + i
        onehot = _one_hot_rows(dest_ref[b, pl.ds(exp_ref[j], 1), :], j)
        xblk = x_ref[pl.ds(pl.multiple_of(b * MOE_TB, MOE_TB), MOE_TB), :]
        xs_ref[...] += jnp.dot(onehot, xblk, preferred_element_type=F32).astype(BF16)
        return carry

    lax.fori_loop(0, nblk_ref[j], block, 0)


def _dispatch(plan, xb, dest_t):
    exp, blk0, nblk = plan
    n, d = xb.shape
    rows = _moe_rows(n)
    dest3 = dest_t.reshape(N_EXPERTS, n // MOE_TB, MOE_TB).transpose(1, 0, 2)
    grid_spec = pltpu.PrefetchScalarGridSpec(
        num_scalar_prefetch=3,
        grid=(rows // MOE_TR,),
        in_specs=[_resident(xb.shape), _resident(dest3.shape)],
        out_specs=pl.BlockSpec((MOE_TR, d), lambda j, e, b, c: (j, 0)),
    )
    return pl.pallas_call(
        _dispatch_kernel,
        grid_spec=grid_spec,
        out_shape=jax.ShapeDtypeStruct((rows, d), BF16),
        compiler_params=_params("arbitrary"),
        name="moe_dispatch",
    )(exp, blk0, nblk, xb, dest3)


def _expert_kernel(et_ref, ee_ref, ea_ref, xs_ref, w1_ref, w3_ref, w2_ref, ys_ref, acc_ref):
    j = pl.program_id(0)
    f = pl.program_id(1)

    @pl.when((ea_ref[j] == 0) & (f == 0))
    def _():
        ys_ref[...] = jnp.zeros_like(ys_ref)

    @pl.when(ea_ref[j] == 1)
    def _():
        @pl.when(f == 0)
        def _():
            acc_ref[...] = jnp.zeros_like(acc_ref)

        x = xs_ref[...]
        h1 = jnp.dot(x, w1_ref[...], preferred_element_type=F32)
        h3 = jnp.dot(x, w3_ref[...], preferred_element_type=F32)
        acc_ref[...] += jnp.dot((_silu(h1) * h3).astype(BF16), w2_ref[...].astype(BF16),
                                preferred_element_type=F32)

        @pl.when(f == pl.num_programs(1) - 1)
        def _():
            ys_ref[...] = acc_ref[...].astype(BF16)


def _experts(plan, xs, w1, w3, w2, tf):
    et, ee, ea = plan
    rows, d = xs.shape
    nf = w1.shape[2] // tf
    fidx = lambda f, a, j: f * a[j] + (nf - 1) * (1 - a[j])
    grid_spec = pltpu.PrefetchScalarGridSpec(
        num_scalar_prefetch=3,
        grid=(et.shape[0], nf),
        in_specs=[pl.BlockSpec((MOE_TE, d), lambda j, f, t, e, a: (t[j], 0)),
                  pl.BlockSpec((None, d, tf), lambda j, f, t, e, a: (e[j], 0, fidx(f, a, j))),
                  pl.BlockSpec((None, d, tf), lambda j, f, t, e, a: (e[j], 0, fidx(f, a, j))),
                  pl.BlockSpec((None, tf, d), lambda j, f, t, e, a: (e[j], fidx(f, a, j), 0))],
        out_specs=pl.BlockSpec((MOE_TE, d), lambda j, f, t, e, a: (j, 0)),
        scratch_shapes=[pltpu.VMEM((MOE_TE, d), F32)],
    )
    return pl.pallas_call(
        _expert_kernel,
        grid_spec=grid_spec,
        out_shape=jax.ShapeDtypeStruct((rows, d), BF16),
        compiler_params=_params("arbitrary", "arbitrary"),
        name="moe_experts",
    )(et, ee, ea, xs, w1, w3, w2)


def _combine_kernel(tile_ref, exp_ref, off_ref, cnt_ref, ys_hbm, dest_ref, gates_ref, res_ref, g_ref, b_ref, o_ref,
                    buf_ref, sem_ref, acc_ref):
    j = pl.program_id(0)
    base = off_ref[j]
    n = cnt_ref[j]

    def fetch(i, slot, first=base):
        row0 = pl.multiple_of(tile_ref[first + i] * MOE_TR, MOE_TR)
        return pltpu.make_async_copy(ys_hbm.at[pl.ds(row0, MOE_TR), :], buf_ref.at[slot], sem_ref.at[slot])

    def prime(blk):
        for i0 in range(MOE_COMBINE_BUFS - 1):
            @pl.when(i0 < cnt_ref[blk])
            def _(i0=i0):
                fetch(i0, i0, off_ref[blk]).start()

    @pl.when(j == 0)
    def _():
        prime(0)

    acc_ref[...] = jnp.zeros_like(acc_ref)
    dest = dest_ref[...]
    gates = gates_ref[...]
    lane = lax.broadcasted_iota(jnp.int32, dest.shape, 1)

    def item(i, carry):
        slot = lax.rem(i, MOE_COMBINE_BUFS)
        ahead = i + (MOE_COMBINE_BUFS - 1)

        @pl.when(ahead < n)
        def _():
            fetch(ahead, lax.rem(ahead, MOE_COMBINE_BUFS)).start()

        mine = lane == exp_ref[base + i]
        dcol = jnp.sum(jnp.where(mine, dest, 0.0), axis=1, keepdims=True)
        gcol = jnp.sum(jnp.where(mine, gates, 0.0), axis=1, keepdims=True)
        col = lax.broadcasted_iota(jnp.int32, (MOE_TB, MOE_TR), 1) + tile_ref[base + i] * MOE_TR
        onehot = jnp.where(dcol == col.astype(F32), 1.0, 0.0).astype(BF16)
        fetch(i, slot).wait()
        acc_ref[...] += jnp.dot(onehot, buf_ref[slot], preferred_element_type=F32) * gcol
        return carry

    lax.fori_loop(0, n, item, 0)

    @pl.when(j + 1 < pl.num_programs(0))
    def _():
        prime(j + 1)

    o_ref[...] = _layer_norm(DEEPNORM_ALPHA * res_ref[...] + acc_ref[...], g_ref[...], b_ref[...])


def _combine_ln(plan, ys, dest, gates, res, g, b):
    tile, exp, off, cnt = plan
    n, d = res.shape
    tok = lambda j, *_: (j, 0)
    grid_spec = pltpu.PrefetchScalarGridSpec(
        num_scalar_prefetch=4,
        grid=(n // MOE_TB,),
        in_specs=[pl.BlockSpec(memory_space=pl.ANY),
                  pl.BlockSpec((MOE_TB, LANES), tok), pl.BlockSpec((MOE_TB, LANES), tok), pl.BlockSpec((MOE_TB, d), tok),
                  _resident((1, d)), _resident((1, d))],
        out_specs=pl.BlockSpec((MOE_TB, d), tok),
        scratch_shapes=[pltpu.VMEM((MOE_COMBINE_BUFS, MOE_TR, d), BF16), pltpu.SemaphoreType.DMA((MOE_COMBINE_BUFS,)),
                        pltpu.VMEM((MOE_TB, d), F32)],
    )
    return pl.pallas_call(
        _combine_kernel,
        grid_spec=grid_spec,
        out_shape=jax.ShapeDtypeStruct((n, d), F32),
        compiler_params=_params("arbitrary"),
        name="moe_combine_ln",
    )(tile, exp, off, cnt, ys, dest, gates, res, g, b)


def _moe_ln(x, xb, routing, w1, w3, w2, g, b):
    n = x.shape[0]
    gates, rank, rank_t, cnt = routing
    start, expert_tiles, by_tile, by_block = _moe_plan(cnt, n)
    startf = start.astype(F32)
    dest = jnp.where(rank >= 0.0, rank + jnp.pad(startf, (0, LANES - N_EXPERTS))[None, :], -1.0)
    dest_t = jnp.where(rank_t >= 0.0, rank_t + startf[:, None], -1.0)
    xs = _dispatch(by_tile, xb, dest_t)
    ys = _experts(expert_tiles, xs, w1, w3, w2, tf=EXPERT_F_TILE)
    return _combine_ln(by_block, ys, dest, gates, x, g, b)


def _gla_kernel(q_ref, k_ref, v_ref, g_ref, gk_ref, gkw_ref, gkb_ref, nw_ref, sel_ref, o_ref, st_ref, bc_scr,
                qd_scr, kd_scr):
    @pl.when(pl.program_id(1) == 0)
    def _():
        st_ref[...] = jnp.zeros_like(st_ref)

    tt = q_ref.shape[0]
    tri = _tri(CHUNK)
    nw = nw_ref[...]

    pre = _mm_split(gk_ref[...], gkw_ref[...]) + gkb_ref[...]
    gk = (jnp.minimum(pre, 0.0) - jnp.log(1.0 + jnp.exp(-jnp.abs(pre)))) * (1.0 / GLA_GATE_NORM)
    bc_all = _mm_01_left(sel_ref[...], gk)
    bc_scr[...] = bc_all
    qd_scr[...] = q_ref[...] * (GLA_DK ** -0.5) * jnp.exp(bc_all)
    kd_scr[...] = k_ref[...] * jnp.exp(-bc_all)

    def chunk(c, carry):
        rows = pl.ds(pl.multiple_of(c * CHUNK, CHUNK), CHUNK)
        bc = bc_scr[rows, :]
        bl = bc[CHUNK - 1:CHUNK, :]
        qd = qd_scr[rows, :]
        kd = kd_scr[rows, :]
        kl = k_ref[rows, :] * jnp.exp(bl - bc)
        dec = jnp.exp(bl)
        heads = range(GLA_HEADS)
        sks = [slice(h * GLA_DK, (h + 1) * GLA_DK) for h in heads]
        svs = [slice(h * GLA_DV, (h + 1) * GLA_DV) for h in heads]
        att = [jnp.where(tri, _mm(qd[:, sk], kd[:, sk], _NT), 0.0) for sk in sks]
        vs = [v_ref[rows, sv] for sv in svs]
        sts = [st_ref[h] for h in heads]
        os_ = [_mm(att[h], vs[h]) + _mm(qd[:, sks[h]], sts[h], _NT) for h in heads]
        for h in heads:
            st_ref[h] = sts[h] * dec[:, sks[h]] + _mm(vs[h], kl[:, sks[h]], _TN)
        for h in heads:
            o = os_[h]
            ms = jnp.mean(o * o, axis=-1, keepdims=True)
            y = o * lax.rsqrt(ms + GLA_NORM_EPS) * nw * _silu(g_ref[rows, svs[h]])
            o_ref[rows, svs[h]] = y.astype(o_ref.dtype)
        return carry

    lax.fori_loop(0, tt // CHUNK, chunk, 0, unroll=2)


def _gla(p, gkw, gkb, nw, tt):
    bsz, t, _ = p.shape

    def col(name):
        off, w = EVEN_COLS[name]
        return pl.BlockSpec((None, tt, w), lambda b, i, j=off // w: (b, i, j))

    fix = lambda b, i: (0, 0)
    return pl.pallas_call(
        _gla_kernel,
        grid=(bsz, t // tt),
        in_specs=[col("q"), col("k"), col("v"), col("g"), col("gk"),
                  pl.BlockSpec(gkw.shape, fix), pl.BlockSpec(gkb.shape, fix), pl.BlockSpec(nw.shape, fix),
                  pl.BlockSpec((tt, tt), fix)],
        out_specs=pl.BlockSpec((None, tt, GLA_VW), lambda b, i: (b, i, 0)),
        out_shape=jax.ShapeDtypeStruct((bsz, t, GLA_VW), BF16),
        scratch_shapes=[pltpu.VMEM((GLA_HEADS, GLA_DV, GLA_DK), F32)] + [pltpu.VMEM((tt, GLA_QK), F32)] * 3,
        compiler_params=_params("parallel", "arbitrary"),
        name="gla",
    )(p, p, p, p, p, gkw, gkb, nw, _chunk_sel(tt, CHUNK))


def _rglru_kernel(x_ref, gate_ref, cw_ref, cb_ref, wa_ref, ba_ref, wx_ref, bx_ref, lam_ref, o_ref,
                  tail_ref, h_ref, a_scr, u_scr):
    @pl.when(pl.program_id(1) == 0)
    def _():
        tail_ref[...] = jnp.zeros_like(tail_ref)
        h_ref[...] = jnp.zeros_like(h_ref)

    tt = x_ref.shape[0]
    x = x_ref[...]
    cw = cw_ref[...]
    xc = x * cw[CONV_K - 1:CONV_K, :] + cb_ref[...]
    for s in range(1, CONV_K):
        xc += _shift_rows(x, tail_ref, s) * cw[CONV_K - 1 - s:CONV_K - s, :]
    tail_ref[...] = x[tt - SUBLANES:, :]

    r = _sigmoid(_mm(xc, wa_ref[...]) + ba_ref[...])
    i = _sigmoid(_mm(xc, wx_ref[...]) + bx_ref[...])
    log_a = -LRU_C * r * _softplus(-lam_ref[...])
    a_scr[...] = jnp.exp(log_a)
    u_scr[...] = jnp.sqrt(1.0 - jnp.exp(2.0 * log_a)) * (i * xc)

    row = lax.broadcasted_iota(jnp.int32, (SUBLANES, x.shape[1]), 0)

    def step(i, h):
        rows = pl.ds(pl.multiple_of(i * SUBLANES, SUBLANES), SUBLANES)
        a = a_scr[rows, :]
        u = u_scr[rows, :]
        for s in (1, 2, 4):
            u = u + a * jnp.where(row >= s, pltpu.roll(u, s, axis=0), 0.0)
            a = a * jnp.where(row >= s, pltpu.roll(a, s, axis=0), 1.0)
        hb = u + a * h
        u_scr[rows, :] = hb
        return hb[SUBLANES - 1:, :]

    h_ref[...] = lax.fori_loop(0, tt // SUBLANES, step, h_ref[...], unroll=4)
    gate = gate_ref[...]
    gelu = 0.5 * gate * (1.0 + jnp.tanh(math.sqrt(2.0 / math.pi) * (gate + 0.044715 * gate * gate * gate)))
    o_ref[...] = (u_scr[...] * gelu).astype(o_ref.dtype)


def _rglru(p, cw, cb, wa, ba, wx, bx, lam, tt):
    bsz, t, _ = p.shape
    w = LRU_WIDTH

    def col(name):
        off, _ = EVEN_COLS[name]
        return pl.BlockSpec((None, tt, w), lambda b, i, j=off // w: (b, i, j))

    fix = lambda b, i: (0, 0)
    vec = pl.BlockSpec((1, w), fix)
    return pl.pallas_call(
        _rglru_kernel,
        grid=(bsz, t // tt),
        in_specs=[col("xb"), col("gate"), pl.BlockSpec((CONV_K, w), fix), vec,
                  pl.BlockSpec((w, w), fix), vec, pl.BlockSpec((w, w), fix), vec, vec],
        out_specs=pl.BlockSpec((None, tt, w), lambda b, i: (b, i, 0)),
        out_shape=jax.ShapeDtypeStruct((bsz, t, w), BF16),
        scratch_shapes=[pltpu.VMEM((SUBLANES, w), F32), pltpu.VMEM((1, w), F32),
                        pltpu.VMEM((tt, w), F32), pltpu.VMEM((tt, w), F32)],
        compiler_params=_params("parallel", "arbitrary"),
        name="rglru",
    )(p, p, cw, cb, wa, ba, wx, bx, lam)


def _ssd_kernel(z_ref, x_ref, bm_ref, cm_ref, dt_ref, cwx_ref, cbx_ref, cwb_ref, cbb_ref, cwc_ref, cbc_ref,
                dtb_ref, alog_ref, dskip_ref, nw_ref, expand_ref, sel_ref, o_ref,
                tx_ref, tb_ref, tc_ref, st_ref, xs_scr, bs_scr, cs_scr, acs_scr, xdt_scr, xdec_scr, ea_scr):
    @pl.when(pl.program_id(1) == 0)
    def _():
        tx_ref[...] = jnp.zeros_like(tx_ref)
        tb_ref[...] = jnp.zeros_like(tb_ref)
        tc_ref[...] = jnp.zeros_like(tc_ref)
        st_ref[...] = jnp.zeros_like(st_ref)

    tt = x_ref.shape[0]

    def conv_silu(src_ref, tail_ref, cw_ref, cb_ref, dst_ref):
        x = src_ref[...]
        cw = cw_ref[...]
        y = x * cw[CONV_K - 1:CONV_K, :] + cb_ref[...]
        for s in range(1, CONV_K):
            y += _shift_rows(x, tail_ref, s) * cw[CONV_K - 1 - s:CONV_K - s, :]
        tail_ref[...] = x[tt - SUBLANES:, :]
        dst_ref[...] = _silu(y)

    conv_silu(x_ref, tx_ref, cwx_ref, cbx_ref, xs_scr)
    conv_silu(bm_ref, tb_ref, cwb_ref, cbb_ref, bs_scr)
    conv_silu(cm_ref, tc_ref, cwc_ref, cbc_ref, cs_scr)

    tri = _tri(CHUNK)
    expand = expand_ref[...]
    dskip = dskip_ref[...]
    nw = nw_ref[...]
    hpg = SSD_HEADS // SSD_GROUPS

    dtc = _softplus(dt_ref[...] + dtb_ref[...])
    da = dtc * -jnp.exp(alog_ref[...])
    sums = _mm_01_left(sel_ref[...], da)
    acs_all = sums[:tt]
    tot = sums[tt:]
    acs_scr[...] = acs_all
    xdt_all = xs_scr[...] * _mm_01(dtc, expand)
    xdt_scr[...] = xdt_all
    xdec_scr[...] = xdt_all * _mm_01(jnp.exp(tot - acs_all), expand)
    ea_scr[...] = _mm_01(jnp.exp(acs_all), expand)

    def chunk(c, carry):
        rows = pl.ds(pl.multiple_of(c * CHUNK, CHUNK), CHUNK)
        acs = acs_scr[rows, :]
        acs_t = acs.T
        ea_x = ea_scr[rows, :]
        x = xs_scr[rows, :]
        xdt = xdt_scr[rows, :]
        xdec = xdec_scr[rows, :]
        cd_x = ea_x[CHUNK - 1:CHUNK, :]
        ys = []
        for g in range(SSD_GROUPS):
            sg = slice(g * SSD_GROUP_WIDTH, (g + 1) * SSD_GROUP_WIDTH)
            ss = slice(g * SSD_STATE, (g + 1) * SSD_STATE)
            bg = bs_scr[rows, ss]
            cg = cs_scr[rows, ss]
            cb = _mm(cg, bg, _NT)
            st = st_ref[g]
            yg = _mm(cg, st) * ea_x[:, sg]
            st_ref[g] = st * cd_x[:, sg] + _mm(bg, xdec[:, sg], _TN)
            yh = []
            for j in range(hpg):
                h = g * hpg + j
                seg = acs[:, h:h + 1] - acs_t[h:h + 1, :]
                lmat = jnp.exp(jnp.where(tri, seg, -jnp.inf))
                sh = slice(h * SSD_HEADDIM, (h + 1) * SSD_HEADDIM)
                yh.append(_mm(cb * lmat, xdt[:, sh]))
            ys.append(yg + jnp.concatenate(yh, axis=1))
        y = jnp.concatenate(ys, axis=1) + x * dskip
        y = y * _silu(z_ref[rows, :])
        outs = []
        for g in range(SSD_GROUPS):
            sg = slice(g * SSD_GROUP_WIDTH, (g + 1) * SSD_GROUP_WIDTH)
            yg = y[:, sg]
            ms = jnp.mean(yg * yg, axis=-1, keepdims=True)
            outs.append(yg * lax.rsqrt(ms + SSD_NORM_EPS))
        o_ref[rows, :] = (jnp.concatenate(outs, axis=1) * nw).astype(o_ref.dtype)
        return carry

    lax.fori_loop(0, tt // CHUNK, chunk, 0, unroll=2)


def _ssd(p, cw, cb, dtb, alog, dskip_x, nw, expand, tt):
    bsz, t, _ = p.shape

    def col(name):
        off, w = ODD_COLS[name]
        return pl.BlockSpec((None, tt, w), lambda b, i, j=off // w: (b, i, j))

    fix = lambda b, i: (0, 0)
    full = lambda a: pl.BlockSpec(a.shape, fix)
    gs = SSD_GROUPS * SSD_STATE
    cwx, cwb, cwc = cw[:, :SSD_INNER], cw[:, SSD_INNER:SSD_INNER + gs], cw[:, SSD_INNER + gs:]
    cbx, cbb, cbc = cb[:, :SSD_INNER], cb[:, SSD_INNER:SSD_INNER + gs], cb[:, SSD_INNER + gs:]
    sel = jnp.concatenate([_chunk_sel(tt, CHUNK), _chunk_sel(tt, CHUNK, "all")], axis=0)
    args = (cwx, cbx, cwb, cbb, cwc, cbc, dtb, alog, dskip_x, nw, expand, sel)
    return pl.pallas_call(
        _ssd_kernel,
        grid=(bsz, t // tt),
        in_specs=[col("z"), col("x"), col("bm"), col("cm"), col("dt")] + [full(a) for a in args],
        out_specs=pl.BlockSpec((None, tt, SSD_INNER), lambda b, i: (b, i, 0)),
        out_shape=jax.ShapeDtypeStruct((bsz, t, SSD_INNER), BF16),
        scratch_shapes=[pltpu.VMEM((SUBLANES, SSD_INNER), F32), pltpu.VMEM((SUBLANES, gs), F32),
                        pltpu.VMEM((SUBLANES, gs), F32),
                        pltpu.VMEM((SSD_GROUPS, SSD_STATE, SSD_GROUP_WIDTH), F32),
                        pltpu.VMEM((tt, SSD_INNER), F32), pltpu.VMEM((tt, gs), F32), pltpu.VMEM((tt, gs), F32),
                        pltpu.VMEM((tt, LANES), F32)] + [pltpu.VMEM((tt, SSD_INNER), F32)] * 3,
        compiler_params=_params("parallel", "arbitrary"),
        name="ssd",
    )(p, p, p, p, p, *args)


def _rwkv_kernel(r_ref, k_ref, v_ref, wa_ref, xg_ref, mur_ref, muk_ref, muv_ref, muwa_ref, mug_ref,
                 w0_ref, w2_ref, a0_ref, a2_ref, g2_ref, kk_ref, ka_ref, rk_ref, lng_ref, lnb_ref, ones_ref, sel_ref,
                 o_ref, tr_ref, tk_ref, tv_ref, twa_ref, tg_ref, st_ref, y_scr,
                 at_scr, rt_scr, bt_scr, kt_scr, bh_scr, kh_scr, v_scr, wc_scr):
    @pl.when(pl.program_id(1) == 0)
    def _():
        for ref in (tr_ref, tk_ref, tv_ref, twa_ref, tg_ref, st_ref):
            ref[...] = jnp.zeros_like(ref)

    tt = r_ref.shape[0]
    hd = RWKV_HEADDIM

    def mix(p_ref, tail_ref, mu_ref):
        p = p_ref[...]
        prev = _shift_rows(p, tail_ref, 1)
        tail_ref[...] = p[tt - SUBLANES:, :]
        return p + (prev - p) * mu_ref[...]

    r = mix(r_ref, tr_ref, mur_ref)
    k = mix(k_ref, tk_ref, muk_ref)
    v = mix(v_ref, tv_ref, muv_ref)
    xwa = mix(wa_ref, twa_ref, muwa_ref)
    xg = mix(xg_ref, tg_ref, mug_ref)

    lane = lax.broadcasted_iota(jnp.int32, xwa.shape, 1)
    lora_in = jnp.where(lane < RWKV_DECAY_LORA, jnp.tanh(xwa), xwa)
    w_log = -_softplus(-(w0_ref[...] + _mm_split(lora_in, w2_ref[...]))) - 0.5
    lw = -jnp.exp(w_log)
    a_sig = _sigmoid(a0_ref[...] + _mm_split(lora_in, a2_ref[...]))
    gate = _mm_split(_sigmoid(xg), g2_ref[...])
    ones_bd = ones_ref[...]
    kk = k * kk_ref[...]
    kk = kk / jnp.maximum(jnp.sqrt(_mm_01(kk * kk, ones_bd)), 1e-12)
    k = k * (1.0 + (a_sig - 1.0) * ka_ref[...])

    cs = RWKV_CHUNK
    tri_incl = _tri(cs)
    tri_strict = _tri(cs, strict=True)
    tri_col = jnp.concatenate([tri_strict, tri_incl], axis=0)
    eye = (tri_incl & ~tri_strict).astype(F32)

    sums = _mm_01_left(sel_ref[...], lw)
    cum = sums[:tt]
    tot = sums[tt:]
    bvec = kk * a_sig
    e_neg = jnp.exp(-cum)
    e_last = jnp.exp(tot - cum)
    for ref, val in ((at_scr, -kk * jnp.exp(cum - lw)), (rt_scr, r * jnp.exp(cum)), (bt_scr, bvec * e_neg),
                     (kt_scr, k * e_neg), (bh_scr, bvec * e_last), (kh_scr, k * e_last), (v_scr, v),
                     (wc_scr, jnp.exp(tot))):
        ref[...] = val

    def chunk(c, carry):
        rows = pl.ds(pl.multiple_of(c * cs, cs), cs)
        at = at_scr[rows, :]
        rt = rt_scr[rows, :]
        bt = bt_scr[rows, :]
        kt = kt_scr[rows, :]
        bh = bh_scr[rows, :]
        kh = kh_scr[rows, :]
        vc = v_scr[rows, :]
        wc = wc_scr[pl.ds(pl.multiple_of(c * cs, cs), 1), :]
        heads = range(RWKV_HEADS)
        sls = [slice(h * hd, (h + 1) * hd) for h in heads]
        amat = [_mm(jnp.concatenate([at[:, sl], rt[:, sl]], axis=0),
                    jnp.concatenate([bt[:, sl], kt[:, sl]], axis=0), _NT) for sl in sls]
        a_ab = [jnp.where(tri_strict, m[:cs, :cs], 0.0) for m in amat]
        a_rb = [jnp.where(tri_incl, m[cs:, :cs], 0.0) for m in amat]
        a_xk = [jnp.where(tri_col, m[:, cs:], 0.0) for m in amat]
        avrv = [_mm(a_xk[h], vc[:, sls[h]]) for h in heads]
        inv = [eye + m for m in a_ab]
        pw = [_mm(m, m) for m in a_ab]
        for _ in range(int(math.log2(cs)) - 2):
            both = [_mm(jnp.concatenate([inv[h], pw[h]], axis=0), pw[h]) for h in heads]
            inv = [inv[h] + both[h][:cs] for h in heads]
            pw = [m[cs:] for m in both]
        inv = [inv[h] + _mm(inv[h], pw[h]) for h in heads]
        tu = [_mm(inv[h], jnp.concatenate([at[:, sls[h]], avrv[h][:cs]], axis=1)) for h in heads]
        qy = [_mm(a_rb[h], tu[h]) + jnp.concatenate([rt[:, sls[h]], avrv[h][cs:]], axis=1)
              for h in heads]
        zb =[_mm(tu[h], bh[:, sls[h]], _TN) for h in heads]
        vk = [_mm(vc[:, sls[h]], kh[:, sls[h]], _TN) for h in heads]
        for h in heads:
            st = st_ref[h]
            y_scr[rows, sls[h]] = _mm(qy[h][:, :hd], st, _NT) + qy[h][:, hd:]
            st_ref[h] = st * wc[:, sls[h]] + _mm(st, zb[h][:hd]) + zb[h][hd:] + vk[h]
        return carry

    lax.fori_loop(0, tt // cs, chunk, 0, unroll=2)

    y = y_scr[...]
    inv_n = 1.0 / hd
    mu_y = _mm_01(y, ones_bd) * inv_n
    dy = y - mu_y
    var_y = _mm_01(dy * dy, ones_bd) * inv_n
    yn = dy * lax.rsqrt(var_y + RWKV_GN_EPS) * lng_ref[...] + lnb_ref[...]
    bonus = _mm_01(r * k * rk_ref[...], ones_bd) * v
    o_ref[...] = ((yn + bonus) * gate).astype(o_ref.dtype)


def _rwkv(p, mu, w0, w2p, a0, a2p, g2, k_k, k_a, r_k, ln_g, ln_b, ones_bd, tt):
    bsz, t, _ = p.shape
    w = RWKV_WIDTH

    def col(name):
        off, wd = ODD_COLS[name]
        return pl.BlockSpec((None, tt, wd), lambda b, i, j=off // wd: (b, i, j))

    fix = lambda b, i: (0, 0)
    full = lambda a: pl.BlockSpec(a.shape, fix)
    mur, muk, muv = mu[:, :w], mu[:, w:2 * w], mu[:, 2 * w:3 * w]
    muwa, mug = mu[:, 3 * w:3 * w + LANES], mu[:, 3 * w + LANES:]
    sel = jnp.concatenate([_chunk_sel(tt, RWKV_CHUNK), _chunk_sel(tt, RWKV_CHUNK, "all")], axis=0)
    args = (mur, muk, muv, muwa, mug, w0, w2p, a0, a2p, g2, k_k, k_a, r_k, ln_g, ln_b, ones_bd, sel)
    return pl.pallas_call(
        _rwkv_kernel,
        grid=(bsz, t // tt),
        in_specs=[col("r"), col("k"), col("v"), col("wa"), col("xg")] + [full(a) for a in args],
        out_specs=pl.BlockSpec((None, tt, w), lambda b, i: (b, i, 0)),
        out_shape=jax.ShapeDtypeStruct((bsz, t, w), BF16),
        scratch_shapes=[pltpu.VMEM((SUBLANES, w), F32), pltpu.VMEM((SUBLANES, w), F32),
                        pltpu.VMEM((SUBLANES, w), F32), pltpu.VMEM((SUBLANES, LANES), F32),
                        pltpu.VMEM((SUBLANES, LANES), F32),
                        pltpu.VMEM((RWKV_HEADS, RWKV_HEADDIM, RWKV_HEADDIM), F32)]
        + [pltpu.VMEM((tt, w), F32)] * 9,
        compiler_params=_params("parallel", "arbitrary"),
        name="rwkv7",
    )(p, p, p, p, p, *args)


def _reorder_cols(w, pieces, order, width):
    parts, pos = [], 0
    for name, (off, slot) in order.items():
        assert off == pos, "slots must be listed in order and contiguous"
        start, size = pieces[name]
        parts.append(w[:, start:start + size])
        if size < slot:
            parts.append(jnp.zeros((w.shape[0], slot - size), w.dtype))
        pos += slot
    assert pos == width
    return jnp.concatenate(parts, axis=1)


def _block_diag(w):
    n, i, j = w.shape
    eye = jnp.eye(n, dtype=w.dtype)
    return (eye[:, None, :, None] * w[:, :, None, :]).reshape(n * i, n * j)


def _row(v, width=None):
    v = v.reshape(1, -1).astype(F32)
    if width is not None and v.shape[1] < width:
        v = jnp.pad(v, ((0, 0), (0, width - v.shape[1])))
    return v


def _even_layer(x, x_in, w_in, gk_w2, gk_b, gla_norm, conv_w, conv_b, wa, ba, wx, bx, lam,
                w_out, ln1_g, ln1_b, f_w1, f_w3, f_w2, ln2_g, ln2_b, bsz, t):
    qk, vw, lw = GLA_QK, GLA_VW, LRU_WIDTH
    pieces = dict(q=(0, qk), k=(qk, qk), v=(2 * qk, vw), g=(2 * qk + vw, vw), gk=(2 * qk + 2 * vw, GLA_GK_RANK),
                  xb=(2 * qk + 2 * vw + GLA_GK_RANK, lw), gate=(2 * qk + 2 * vw + GLA_GK_RANK + lw, lw))
    w_in_r = _reorder_cols(w_in, pieces, EVEN_COLS, EVEN_WIDTH).astype(BF16)
    p = _proj(x_in, w_in_r, tm=min(ROW_TILE, x.shape[0])).reshape(bsz, t, EVEN_WIDTH)
    tt = min(GLA_RGLRU_TIME_TILE, t)
    gkw = jnp.pad(gk_w2, ((0, LANES - GLA_GK_RANK), (0, 0)))
    y_a = _gla(p, gkw, _row(gk_b), _row(gla_norm), tt)
    y_b = _rglru(p, conv_w, _row(conv_b), _block_diag(wa).astype(BF16), _row(ba), _block_diag(wx).astype(BF16),
                 _row(bx), _row(lam), tt)
    n = bsz * t
    tm = min(ROW_TILE, n)
    return _mix_ffn(y_a.reshape(n, vw), y_b.reshape(n, lw), w_out[:vw].astype(BF16), w_out[vw:].astype(BF16), x,
                    _row(ln1_g), _row(ln1_b), f_w1.astype(BF16), f_w3.astype(BF16), f_w2.astype(BF16),
                    _row(ln2_g), _row(ln2_b), tm)


def _odd_layer(x, xb, w_in, conv_w, conv_b, dt_bias, a_log, d_skip, ssd_norm, mu, w0, w2, a0, a2, g2, k_k, k_a,
               r_k, rln_g, rln_b, w_out, ln1_g, ln1_b, router, ew1, ew3, ew2, ln2_g, ln2_b, bsz, t):
    si, gs, rw = SSD_INNER, SSD_GROUPS * SSD_STATE, RWKV_WIDTH
    o = 2 * si + 2 * gs + SSD_HEADS
    pieces = dict(z=(0, si), x=(si, si), bm=(2 * si, gs), cm=(2 * si + gs, gs), dt=(2 * si + 2 * gs, SSD_HEADS),
                  r=(o, rw), k=(o + rw, rw), v=(o + 2 * rw, rw),
                  wa=(o + 3 * rw, RWKV_DECAY_LORA + RWKV_AAA_LORA), xg=(o + 3 * rw + LANES, RWKV_GATE_LORA))
    w_in_r = _reorder_cols(w_in, pieces, ODD_COLS, ODD_WIDTH).astype(BF16)
    n = bsz * t
    p = _proj(xb, w_in_r, tm=min(ROW_TILE, n)).reshape(bsz, t, ODD_WIDTH)
    expand = jnp.pad(jnp.repeat(jnp.eye(SSD_HEADS, dtype=BF16), SSD_HEADDIM, axis=1), ((0, LANES - SSD_HEADS), (0, 0)))
    y_c = _ssd(p, conv_w, _row(conv_b), _row(dt_bias, LANES), _row(a_log, LANES),
               _row(jnp.repeat(d_skip, SSD_HEADDIM)), _row(ssd_norm), expand, tt=min(SSD_RWKV_TIME_TILE, t))
    zeros = jnp.zeros((RWKV_DECAY_LORA, rw), F32)
    ones_bd = _block_diag(jnp.ones((RWKV_HEADS, RWKV_HEADDIM, RWKV_HEADDIM), BF16))
    y_d = _rwkv(p, _row(mu), _row(w0), jnp.concatenate([w2, zeros]), _row(a0), jnp.concatenate([zeros, a2]), g2,
                _row(k_k), _row(k_a), _row(r_k), _row(rln_g), _row(rln_b), ones_bd, tt=min(SSD_RWKV_TIME_TILE, t))
    x1, x1b, *routing = _mix_router(y_c.reshape(n, si), y_d.reshape(n, rw), w_out[:si].astype(BF16),
                                    w_out[si:].astype(BF16), x, _row(ln1_g), _row(ln1_b), router)
    y = _moe_ln(x1, x1b, routing, ew1.astype(BF16), ew3.astype(BF16), ew2, _row(ln2_g), _row(ln2_b))
    return y, None


def kernel(x, e_w_in, e_gk_w2, e_gk_b, e_gla_norm, e_conv_w, e_conv_b, e_lru_wa, e_lru_ba, e_lru_wx, e_lru_bx, e_lru_lambda, e_w_out, e_ln1_g, e_ln1_b, e_ffn_w1, e_ffn_w3, e_ffn_w2, e_ln2_g, e_ln2_b, o_w_in, o_conv_w, o_conv_b, o_dt_bias, o_a_log, o_d_skip, o_ssd_norm, o_rwkv_mu, o_rwkv_w0, o_rwkv_w2, o_rwkv_a0, o_rwkv_a2, o_rwkv_g2, o_rwkv_k_k, o_rwkv_k_a, o_rwkv_r_k, o_rwkv_ln_g, o_rwkv_ln_b, o_w_out, o_ln1_g, o_ln1_b, o_router, o_exp_w1, o_exp_w3, o_exp_w2, o_ln2_g, o_ln2_b):
    bsz, t, d = x.shape
    h = x.reshape(bsz * t, d)
    hb = h
    for i in range(DEPTH):
        j = i // 2
        if i % 2 == 0:
            h, hb = _even_layer(h, hb, e_w_in[j], e_gk_w2[j], e_gk_b[j], e_gla_norm[j], e_conv_w[j], e_conv_b[j],
                                e_lru_wa[j], e_lru_ba[j], e_lru_wx[j], e_lru_bx[j], e_lru_lambda[j], e_w_out[j],
                                e_ln1_g[j], e_ln1_b[j], e_ffn_w1[j], e_ffn_w3[j], e_ffn_w2[j], e_ln2_g[j],
                                e_ln2_b[j], bsz, t)
        else:
            h, hb = _odd_layer(h, hb, o_w_in[j], o_conv_w[j], o_conv_b[j], o_dt_bias[j], o_a_log[j], o_d_skip[j],
                               o_ssd_norm[j], o_rwkv_mu[j], o_rwkv_w0[j], o_rwkv_w2[j], o_rwkv_a0[j],
                               o_rwkv_a2[j], o_rwkv_g2[j], o_rwkv_k_k[j], o_rwkv_k_a[j], o_rwkv_r_k[j],
                               o_rwkv_ln_g[j], o_rwkv_ln_b[j], o_w_out[j], o_ln1_g[j], o_ln1_b[j], o_router[j],
                               o_exp_w1[j], o_exp_w3[j], o_exp_w2[j], o_ln2_g[j], o_ln2_b[j], bsz, t)
    return h.reshape(bsz, t, d)
```

```python
import math

import jax
import jax.numpy as jnp
from jax import lax
from jax.experimental import pallas as pl
from jax.experimental.pallas import tpu as pltpu

F32 = jnp.float32
BF16 = jnp.bfloat16

D_MODEL = 1024
DEPTH = 2
DEEPNORM_ALPHA = (2 * DEPTH) ** 0.25
LN_EPS = 1e-5
CONV_K = 4
CHUNK = 64

GLA_HEADS = 4
GLA_DK = 64
GLA_DV = 128
GLA_QK = GLA_HEADS * GLA_DK
GLA_VW = GLA_HEADS * GLA_DV
GLA_GK_RANK = 16
GLA_GATE_NORM = 16.0
GLA_NORM_EPS = 1e-5

LRU_WIDTH = 512
LRU_BLOCKS = 8
LRU_C = 8.0

SSD_HEADS = 16
SSD_HEADDIM = 64
SSD_INNER = SSD_HEADS * SSD_HEADDIM
SSD_GROUPS = 2
SSD_STATE = 128
SSD_GROUP_WIDTH = SSD_INNER // SSD_GROUPS
SSD_NORM_EPS = 1e-5

RWKV_HEADS = 8
RWKV_HEADDIM = 64
RWKV_WIDTH = RWKV_HEADS * RWKV_HEADDIM
RWKV_DECAY_LORA = 64
RWKV_AAA_LORA = 64
RWKV_GATE_LORA = 128
RWKV_GN_EPS = 64e-5
RWKV_CHUNK = 128

N_EXPERTS = 8
LANES = 128
SUBLANES = 8
assert N_EXPERTS == SUBLANES
ROUTER_ROWS = 16
MOE_TB = 512
MOE_TR = 256
MOE_TE = 512
assert MOE_TE % MOE_TR == 0
MOE_COMBINE_BUFS = 8
VMEM_LIMIT = 56 * 1024 * 1024

ROW_TILE = 512
GLA_RGLRU_TIME_TILE = 512
SSD_RWKV_TIME_TILE = 256
EXPERT_F_TILE = 1792

EVEN_COLS = dict(v=(0, 512), g=(512, 512), xb=(1024, 512), gate=(1536, 512), q=(2048, 256), k=(2304, 256),
                 gk=(2560, 128))
EVEN_WIDTH = 2688
ODD_COLS = dict(z=(0, 1024), x=(1024, 1024), r=(2048, 512), k=(2560, 512), v=(3072, 512), bm=(3584, 256),
                cm=(3840, 256), wa=(4096, 128), xg=(4224, 128), dt=(4352, 128))
ODD_WIDTH = 4480


def _mm(a, b, dims=((1,), (0,))):
    return lax.dot_general(a.astype(BF16), b.astype(BF16), (dims, ((), ())), preferred_element_type=F32)


def _mm_split(a, b, dims=((1,), (0,))):
    a_hi = a.astype(BF16)
    b_hi = b.astype(BF16)
    a_lo = (a - a_hi.astype(F32)).astype(BF16)
    b_lo = (b - b_hi.astype(F32)).astype(BF16)
    return _mm(a_hi, b_hi, dims) + _mm(a_lo, b_hi, dims) + _mm(a_hi, b_lo, dims)


_NT = ((1,), (1,))
_TN = ((0,), (0,))


def _mm_01(x, sel):
    hi = x.astype(BF16)
    lo = (x - hi.astype(F32)).astype(BF16)
    return (jnp.dot(hi, sel, preferred_element_type=F32) + jnp.dot(lo, sel, preferred_element_type=F32))


def _sigmoid(x):
    return 1.0 / (1.0 + jnp.exp(-x))


def _softplus(x):
    return jnp.maximum(x, 0.0) + jnp.log(1.0 + jnp.exp(-jnp.abs(x)))


def _silu(x):
    return x * _sigmoid(x)


def _layer_norm(h, g, b):
    mu = jnp.mean(h, axis=-1, keepdims=True)
    d = h - mu
    var = jnp.mean(d * d, axis=-1, keepdims=True)
    return d * lax.rsqrt(var + LN_EPS) * g + b


def _params(*sem):
    return pltpu.CompilerParams(dimension_semantics=sem, vmem_limit_bytes=VMEM_LIMIT)


def _tri(n, strict=False):
    r = lax.broadcasted_iota(jnp.int32, (n, n), 0)
    c = lax.broadcasted_iota(jnp.int32, (n, n), 1)
    return (r > c) if strict else (r >= c)


def _chunk_sel(n, chunk, kind="incl"):
    r = jnp.arange(n)[:, None]
    c = jnp.arange(n)[None, :]
    first = r - r % chunk
    upper = r if kind == "incl" else first + (chunk - 1)
    return ((c >= first) & (c <= upper)).astype(BF16)


def _mm_01_left(sel, x):
    hi = x.astype(BF16)
    lo = (x - hi.astype(F32)).astype(BF16)
    return jnp.dot(sel, hi, preferred_element_type=F32) + jnp.dot(sel, lo, preferred_element_type=F32)


def _shift_rows(p, tail_ref, s):
    n = p.shape[0]
    rolled = pltpu.roll(p, s, axis=0)
    head = pltpu.roll(tail_ref[...], s, axis=0)
    row = lax.broadcasted_iota(jnp.int32, (SUBLANES, p.shape[1]), 0)
    fixed = jnp.where(row < s, head, rolled[:SUBLANES])
    return jnp.concatenate([fixed, rolled[SUBLANES:]], axis=0) if n > SUBLANES else fixed


def _proj_kernel(x_ref, w_ref, o_ref):
    o_ref[...] = jnp.dot(x_ref[...].astype(BF16), w_ref[...], preferred_element_type=F32)


def _resident(shape):
    return pl.BlockSpec(shape, lambda *_: (0,) * len(shape), pipeline_mode=pl.Buffered(1))


def _proj(x, w, tm):
    m, k = x.shape
    n = w.shape[1]
    return pl.pallas_call(
        _proj_kernel,
        grid=(m // tm,),
        in_specs=[pl.BlockSpec((tm, k), lambda i: (i, 0)), _resident((k, n))],
        out_specs=pl.BlockSpec((tm, n), lambda i: (i, 0)),
        out_shape=jax.ShapeDtypeStruct((m, n), F32),
        compiler_params=_params("parallel"),
        name="in_proj",
    )(x, w)


def _outproj_ln_value(ya_ref, yb_ref, wa_ref, wb_ref, res_ref, g_ref, b_ref):
    acc = jnp.dot(ya_ref[...], wa_ref[...], preferred_element_type=F32)
    acc += jnp.dot(yb_ref[...], wb_ref[...], preferred_element_type=F32)
    return _layer_norm(DEEPNORM_ALPHA * res_ref[...] + acc, g_ref[...], b_ref[...])


def _mix_ffn_kernel(ya_ref, yb_ref, wa_ref, wb_ref, res_ref, g1_ref, b1_ref, w1_ref, w3_ref, w2_ref, g2_ref, b2_ref,
                    o_ref, ob_ref):
    x1 = _outproj_ln_value(ya_ref, yb_ref, wa_ref, wb_ref, res_ref, g1_ref, b1_ref)
    xb = x1.astype(BF16)
    h1 = jnp.dot(xb, w1_ref[...], preferred_element_type=F32)
    h3 = jnp.dot(xb, w3_ref[...], preferred_element_type=F32)
    ff = jnp.dot((_silu(h1) * h3).astype(BF16), w2_ref[...], preferred_element_type=F32)
    y = _layer_norm(DEEPNORM_ALPHA * x1 + ff, g2_ref[...], b2_ref[...])
    o_ref[...] = y
    ob_ref[...] = y.astype(BF16)


def _mix_ffn(ya, yb, wa, wb, res, g1, b1, w1, w3, w2, g2, b2, tm):
    m, d = res.shape
    row = lambda i: (i, 0)
    consts = (wa, wb, None, g1, b1, w1, w3, w2, g2, b2)
    return pl.pallas_call(
        _mix_ffn_kernel,
        grid=(m // tm,),
        in_specs=[pl.BlockSpec((tm, ya.shape[1]), row), pl.BlockSpec((tm, yb.shape[1]), row)]
        + [pl.BlockSpec((tm, d), row) if c is None else _resident(c.shape) for c in consts],
        out_specs=[pl.BlockSpec((tm, d), row), pl.BlockSpec((tm, d), row)],
        out_shape=[jax.ShapeDtypeStruct((m, d), F32), jax.ShapeDtypeStruct((m, d), BF16)],
        compiler_params=_params("parallel"),
        name="out_proj_ffn_ln",
    )(ya, yb, wa, wb, res, g1, b1, w1, w3, w2, g2, b2)


def _mix_router_kernel(ya_ref, yb_ref, wa_ref, wb_ref, res_ref, g_ref, b_ref, w_ref, before_ref, o_ref, ob_ref,
                       gates_ref, rank_ref, rank_t_ref, cnt_ref, carry_ref):
    @pl.when(pl.program_id(0) == 0)
    def _():
        carry_ref[...] = jnp.zeros_like(carry_ref)

    x1 = _outproj_ln_value(ya_ref, yb_ref, wa_ref, wb_ref, res_ref, g_ref, b_ref)
    o_ref[...] = x1
    ob_ref[...] = x1.astype(BF16)
    logits = _mm_split(w_ref[...], x1, _NT)
    sub = lax.broadcasted_iota(jnp.int32, logits.shape, 0)
    neg = jnp.float32(-jnp.inf)
    l1 = jnp.where(sub < N_EXPERTS, logits, neg)
    m1 = jnp.max(l1, axis=0, keepdims=True)
    i1 = jnp.min(jnp.where(l1 == m1, sub, ROUTER_ROWS), axis=0, keepdims=True)
    l2 = jnp.where(sub == i1, neg, l1)
    m2 = jnp.max(l2, axis=0, keepdims=True)
    i2 = jnp.min(jnp.where(l2 == m2, sub, ROUTER_ROWS), axis=0, keepdims=True)
    ex = jnp.exp(m2 - m1)
    w_top = 1.0 / (1.0 + ex)
    gates_t = jnp.where(sub == i1, w_top, 0.0) + jnp.where(sub == i2, ex * w_top, 0.0)

    sel = jnp.where(sub == i1, 1.0, 0.0) + jnp.where(sub == i2, 1.0, 0.0)
    before = _mm(sel, before_ref[...])
    carry = carry_ref[...]
    rank_t = jnp.where(sel > 0.0, carry[:, :1] + before, -1.0)
    rank_t_ref[...] = rank_t[:SUBLANES, :]
    carry = carry + jnp.sum(sel, axis=1, keepdims=True)
    carry_ref[...] = carry
    cnt_ref[...] = carry[:SUBLANES, :]
    pad = jnp.zeros((LANES - ROUTER_ROWS, sel.shape[1]), F32)
    gates_ref[...] = jnp.concatenate([gates_t, pad], axis=0).T
    rank_ref[...] = jnp.concatenate([rank_t, pad], axis=0).T


def _mix_router(ya, yb, wa, wb, res, g, b, router):
    m, d = res.shape
    nblk = m // MOE_TB
    row = lambda i: (i, 0)
    w = jnp.pad(router.T, ((0, ROUTER_ROWS - N_EXPERTS), (0, 0)))
    earlier = (jnp.arange(MOE_TB)[:, None] < jnp.arange(MOE_TB)[None, :]).astype(BF16)
    return pl.pallas_call(
        _mix_router_kernel,
        grid=(nblk,),
        in_specs=[pl.BlockSpec((MOE_TB, ya.shape[1]), row), pl.BlockSpec((MOE_TB, yb.shape[1]), row),
                  _resident(wa.shape), _resident(wb.shape), pl.BlockSpec((MOE_TB, d), row),
                  _resident((1, d)), _resident((1, d)), _resident(w.shape), _resident(earlier.shape)],
        out_specs=[pl.BlockSpec((MOE_TB, d), row), pl.BlockSpec((MOE_TB, d), row),
                   pl.BlockSpec((MOE_TB, LANES), row), pl.BlockSpec((MOE_TB, LANES), row),
                   pl.BlockSpec((SUBLANES, MOE_TB), lambda i: (0, i)), pl.BlockSpec((SUBLANES, LANES), row)],
        out_shape=[jax.ShapeDtypeStruct((m, d), F32), jax.ShapeDtypeStruct((m, d), BF16),
                   jax.ShapeDtypeStruct((m, LANES), F32), jax.ShapeDtypeStruct((m, LANES), F32),
                   jax.ShapeDtypeStruct((SUBLANES, m), F32), jax.ShapeDtypeStruct((nblk * SUBLANES, LANES), F32)],
        scratch_shapes=[pltpu.VMEM((ROUTER_ROWS, LANES), F32)],
        compiler_params=_params("arbitrary"),
        name="out_proj_ln_router",
    )(ya, yb, wa, wb, res, g, b, w, earlier)


def _moe_plan(cnt, n_tok):
    i32 = jnp.int32
    nblk = n_tok // MOE_TB
    c_inc = cnt.reshape(nblk, SUBLANES, LANES)[:, :N_EXPERTS, 0].astype(i32)
    c_exc = jnp.concatenate([jnp.zeros((1, N_EXPERTS), i32), c_inc[:-1]], axis=0)
    gsz = (c_inc[-1] + MOE_TE - 1) // MOE_TE * MOE_TE
    gend = jnp.cumsum(gsz)
    start = gend - gsz
    n_et = _moe_rows(n_tok) // MOE_TE
    n_act = gend[-1] // MOE_TE
    et = jnp.minimum(jnp.arange(n_et, dtype=i32), jnp.maximum(n_act - 1, 0))
    et_expert = jnp.sum((gend[None, :] <= (et * MOE_TE)[:, None]).astype(i32), axis=1)
    et_active = (jnp.arange(n_et, dtype=i32) < n_act).astype(i32)

    lo = start[None, :] + c_exc
    hi = start[None, :] + c_inc
    first_tile = lo // MOE_TR
    n_items = jnp.where(hi > lo, (hi - 1) // MOE_TR - first_tile + 1, 0)
    wmax = _moe_rows(n_tok) // MOE_TR + N_EXPERTS * nblk
    exp_id = jnp.broadcast_to(jnp.arange(N_EXPERTS, dtype=i32)[None, :], (nblk, N_EXPERTS))

    n, ft, exp = (a.reshape(-1) for a in (n_items, first_tile, exp_id))
    inc = jnp.cumsum(n)
    wc = jnp.minimum(jnp.arange(wmax, dtype=i32), jnp.maximum(inc[-1] - 1, 0))
    idx = jnp.sum((inc[None, :] <= wc[:, None]).astype(i32), axis=1)
    per_block = jnp.sum(n_items, axis=1)
    by_block = (ft[idx] + wc - (inc - n)[idx], exp[idx], jnp.cumsum(per_block) - per_block, per_block)

    row0 = jnp.arange(_moe_rows(n_tok) // MOE_TR, dtype=i32) * MOE_TR
    group = jnp.minimum(jnp.sum((gend[None, :] <= row0[:, None]).astype(i32), axis=1), N_EXPERTS - 1)
    rank_lo = row0 - start[group]
    count = c_inc[-1][group]
    rank_hi = jnp.minimum(rank_lo + MOE_TR, count)
    holds = (rank_lo < count) & (row0 < gend[-1])
    blk_lo = jnp.sum((c_inc[:, group].T <= rank_lo[:, None]).astype(i32), axis=1)
    blk_hi = jnp.sum((c_exc[:, group].T < rank_hi[:, None]).astype(i32), axis=1)
    by_tile = (group, jnp.minimum(blk_lo, nblk - 1), jnp.where(holds, blk_hi - blk_lo, 0))

    return start, (et, et_expert, et_active), by_tile, by_block


def _moe_rows(n_tok):
    return 2 * n_tok + N_EXPERTS * MOE_TE


def _one_hot_rows(dest, tile):
    row = lax.broadcasted_iota(jnp.int32, (MOE_TR, MOE_TB), 0) + tile * MOE_TR
    return jnp.where(dest == row.astype(F32), 1.0, 0.0).astype(BF16)


def _dispatch_kernel(exp_ref, blk0_ref, nblk_ref, x_ref, dest_ref, xs_ref):
    j = pl.program_id(0)
    xs_ref[...] = jnp.zeros_like(xs_ref)

    def block(i, carry):
        b = blk0_ref[j] + i
        onehot = _one_hot_rows(dest_ref[b, pl.ds(exp_ref[j], 1), :], j)
        xblk = x_ref[pl.ds(pl.multiple_of(b * MOE_TB, MOE_TB), MOE_TB), :]
        xs_ref[...] += jnp.dot(onehot, xblk, preferred_element_type=F32).astype(BF16)
        return carry

    lax.fori_loop(0, nblk_ref[j], block, 0)


def _dispatch(plan, xb, dest_t):
    exp, blk0, nblk = plan
    n, d = xb.shape
    rows = _moe_rows(n)
    dest3 = dest_t.reshape(N_EXPERTS, n // MOE_TB, MOE_TB).transpose(1, 0, 2)
    grid_spec = pltpu.PrefetchScalarGridSpec(
        num_scalar_prefetch=3,
        grid=(rows // MOE_TR,),
        in_specs=[_resident(xb.shape), _resident(dest3.shape)],
        out_specs=pl.BlockSpec((MOE_TR, d), lambda j, e, b, c: (j, 0)),
    )
    return pl.pallas_call(
        _dispatch_kernel,
        grid_spec=grid_spec,
        out_shape=jax.ShapeDtypeStruct((rows, d), BF16),
        compiler_params=_params("arbitrary"),
        name="moe_dispatch",
    )(exp, blk0, nblk, xb, dest3)


def _expert_kernel(et_ref, ee_ref, ea_ref, xs_ref, w1_ref, w3_ref, w2_ref, ys_ref, acc_ref):
    j = pl.program_id(0)
    f = pl.program_id(1)

    @pl.when((ea_ref[j] == 0) & (f == 0))
    def _():
        ys_ref[...] = jnp.zeros_like(ys_ref)

    @pl.when(ea_ref[j] == 1)
    def _():
        @pl.when(f == 0)
        def _():
            acc_ref[...] = jnp.zeros_like(acc_ref)

        x = xs_ref[...]
        h1 = jnp.dot(x, w1_ref[...], preferred_element_type=F32)
        h3 = jnp.dot(x, w3_ref[...], preferred_element_type=F32)
        acc_ref[...] += jnp.dot((_silu(h1) * h3).astype(BF16), w2_ref[...].astype(BF16),
                                preferred_element_type=F32)

        @pl.when(f == pl.num_programs(1) - 1)
        def _():
            ys_ref[...] = acc_ref[...].astype(BF16)


def _experts(plan, xs, w1, w3, w2, tf):
    et, ee, ea = plan
    rows, d = xs.shape
    nf = w1.shape[2] // tf
    fidx = lambda f, a, j: f * a[j] + (nf - 1) * (1 - a[j])
    grid_spec = pltpu.PrefetchScalarGridSpec(
        num_scalar_prefetch=3,
        grid=(et.shape[0], nf),
        in_specs=[pl.BlockSpec((MOE_TE, d), lambda j, f, t, e, a: (t[j], 0)),
                  pl.BlockSpec((None, d, tf), lambda j, f, t, e, a: (e[j], 0, fidx(f, a, j))),
                  pl.BlockSpec((None, d, tf), lambda j, f, t, e, a: (e[j], 0, fidx(f, a, j))),
                  pl.BlockSpec((None, tf, d), lambda j, f, t, e, a: (e[j], fidx(f, a, j), 0))],
        out_specs=pl.BlockSpec((MOE_TE, d), lambda j, f, t, e, a: (j, 0)),
        scratch_shapes=[pltpu.VMEM((MOE_TE, d), F32)],
    )
    return pl.pallas_call(
        _expert_kernel,
        grid_spec=grid_spec,
        out_shape=jax.ShapeDtypeStruct((rows, d), BF16),
        compiler_params=_params("arbitrary", "arbitrary"),
        name="moe_experts",
    )(et, ee, ea, xs, w1, w3, w2)


def _combine_kernel(tile_ref, exp_ref, off_ref, cnt_ref, ys_hbm, dest_ref, gates_ref, res_ref, g_ref, b_ref, o_ref,
                    buf_ref, sem_ref, acc_ref):
    j = pl.program_id(0)
    base = off_ref[j]
    n = cnt_ref[j]

    def fetch(i, slot, first=base):
        row0 = pl.multiple_of(tile_ref[first + i] * MOE_TR, MOE_TR)
        return pltpu.make_async_copy(ys_hbm.at[pl.ds(row0, MOE_TR), :], buf_ref.at[slot], sem_ref.at[slot])

    def prime(blk):
        for i0 in range(MOE_COMBINE_BUFS - 1):
            @pl.when(i0 < cnt_ref[blk])
            def _(i0=i0):
                fetch(i0, i0, off_ref[blk]).start()

    @pl.when(j == 0)
    def _():
        prime(0)

    acc_ref[...] = jnp.zeros_like(acc_ref)
    dest = dest_ref[...]
    gates = gates_ref[...]
    lane = lax.broadcasted_iota(jnp.int32, dest.shape, 1)

    def item(i, carry):
        slot = lax.rem(i, MOE_COMBINE_BUFS)
        ahead = i + (MOE_COMBINE_BUFS - 1)

        @pl.when(ahead < n)
        def _():
            fetch(ahead, lax.rem(ahead, MOE_COMBINE_BUFS)).start()

        mine = lane == exp_ref[base + i]
        dcol = jnp.sum(jnp.where(mine, dest, 0.0), axis=1, keepdims=True)
        gcol = jnp.sum(jnp.where(mine, gates, 0.0), axis=1, keepdims=True)
        col = lax.broadcasted_iota(jnp.int32, (MOE_TB, MOE_TR), 1) + tile_ref[base + i] * MOE_TR
        onehot = jnp.where(dcol == col.astype(F32), 1.0, 0.0).astype(BF16)
        fetch(i, slot).wait()
        acc_ref[...] += jnp.dot(onehot, buf_ref[slot], preferred_element_type=F32) * gcol
        return carry

    lax.fori_loop(0, n, item, 0)

    @pl.when(j + 1 < pl.num_programs(0))
    def _():
        prime(j + 1)

    o_ref[...] = _layer_norm(DEEPNORM_ALPHA * res_ref[...] + acc_ref[...], g_ref[...], b_ref[...])


def _combine_ln(plan, ys, dest, gates, res, g, b):
    tile, exp, off, cnt = plan
    n, d = res.shape
    tok = lambda j, *_: (j, 0)
    grid_spec = pltpu.PrefetchScalarGridSpec(
        num_scalar_prefetch=4,
        grid=(n // MOE_TB,),
        in_specs=[pl.BlockSpec(memory_space=pl.ANY),
                  pl.BlockSpec((MOE_TB, LANES), tok), pl.BlockSpec((MOE_TB, LANES), tok), pl.BlockSpec((MOE_TB, d), tok),
                  _resident((1, d)), _resident((1, d))],
        out_specs=pl.BlockSpec((MOE_TB, d), tok),
        scratch_shapes=[pltpu.VMEM((MOE_COMBINE_BUFS, MOE_TR, d), BF16), pltpu.SemaphoreType.DMA((MOE_COMBINE_BUFS,)),
                        pltpu.VMEM((MOE_TB, d), F32)],
    )
    return pl.pallas_call(
        _combine_kernel,
        grid_spec=grid_spec,
        out_shape=jax.ShapeDtypeStruct((n, d), F32),
        compiler_params=_params("arbitrary"),
        name="moe_combine_ln",
    )(tile, exp, off, cnt, ys, dest, gates, res, g, b)


def _moe_ln(x, xb, routing, w1, w3, w2, g, b):
    n = x.shape[0]
    gates, rank, rank_t, cnt = routing
    start, expert_tiles, by_tile, by_block = _moe_plan(cnt, n)
    startf = start.astype(F32)
    dest = jnp.where(rank >= 0.0, rank + jnp.pad(startf, (0, LANES - N_EXPERTS))[None, :], -1.0)
    dest_t = jnp.where(rank_t >= 0.0, rank_t + startf[:, None], -1.0)
    xs = _dispatch(by_tile, xb, dest_t)
    ys = _experts(expert_tiles, xs, w1, w3, w2, tf=EXPERT_F_TILE)
    return _combine_ln(by_block, ys, dest, gates, x, g, b)


def _gla_kernel(q_ref, k_ref, v_ref, g_ref, gk_ref, gkw_ref, gkb_ref, nw_ref, sel_ref, o_ref, st_ref, bc_scr,
                qd_scr, kd_scr):
    @pl.when(pl.program_id(1) == 0)
    def _():
        st_ref[...] = jnp.zeros_like(st_ref)

    tt = q_ref.shape[0]
    tri = _tri(CHUNK)
    nw = nw_ref[...]

    pre = _mm_split(gk_ref[...], gkw_ref[...]) + gkb_ref[...]
    gk = (jnp.minimum(pre, 0.0) - jnp.log(1.0 + jnp.exp(-jnp.abs(pre)))) * (1.0 / GLA_GATE_NORM)
    bc_all = _mm_01_left(sel_ref[...], gk)
    bc_scr[...] = bc_all
    qd_scr[...] = q_ref[...] * (GLA_DK ** -0.5) * jnp.exp(bc_all)
    kd_scr[...] = k_ref[...] * jnp.exp(-bc_all)

    def chunk(c, carry):
        rows = pl.ds(pl.multiple_of(c * CHUNK, CHUNK), CHUNK)
        bc = bc_scr[rows, :]
        bl = bc[CHUNK - 1:CHUNK, :]
        qd = qd_scr[rows, :]
        kd = kd_scr[rows, :]
        kl = k_ref[rows, :] * jnp.exp(bl - bc)
        dec = jnp.exp(bl)
        heads = range(GLA_HEADS)
        sks = [slice(h * GLA_DK, (h + 1) * GLA_DK) for h in heads]
        svs = [slice(h * GLA_DV, (h + 1) * GLA_DV) for h in heads]
        att = [jnp.where(tri, _mm(qd[:, sk], kd[:, sk], _NT), 0.0) for sk in sks]
        vs = [v_ref[rows, sv] for sv in svs]
        sts = [st_ref[h] for h in heads]
        os_ = [_mm(att[h], vs[h]) + _mm(qd[:, sks[h]], sts[h], _NT) for h in heads]
        for h in heads:
            st_ref[h] = sts[h] * dec[:, sks[h]] + _mm(vs[h], kl[:, sks[h]], _TN)
        for h in heads:
            o = os_[h]
            ms = jnp.mean(o * o, axis=-1, keepdims=True)
            y = o * lax.rsqrt(ms + GLA_NORM_EPS) * nw * _silu(g_ref[rows, svs[h]])
            o_ref[rows, svs[h]] = y.astype(o_ref.dtype)
        return carry

    lax.fori_loop(0, tt // CHUNK, chunk, 0, unroll=2)


def _gla(p, gkw, gkb, nw, tt):
    bsz, t, _ = p.shape

    def col(name):
        off, w = EVEN_COLS[name]
        return pl.BlockSpec((None, tt, w), lambda b, i, j=off // w: (b, i, j))

    fix = lambda b, i: (0, 0)
    return pl.pallas_call(
        _gla_kernel,
        grid=(bsz, t // tt),
        in_specs=[col("q"), col("k"), col("v"), col("g"), col("gk"),
                  pl.BlockSpec(gkw.shape, fix), pl.BlockSpec(gkb.shape, fix), pl.BlockSpec(nw.shape, fix),
                  pl.BlockSpec((tt, tt), fix)],
        out_specs=pl.BlockSpec((None, tt, GLA_VW), lambda b, i: (b, i, 0)),
        out_shape=jax.ShapeDtypeStruct((bsz, t, GLA_VW), BF16),
        scratch_shapes=[pltpu.VMEM((GLA_HEADS, GLA_DV, GLA_DK), F32)] + [pltpu.VMEM((tt, GLA_QK), F32)] * 3,
        compiler_params=_params("parallel", "arbitrary"),
        name="gla",
    )(p, p, p, p, p, gkw, gkb, nw, _chunk_sel(tt, CHUNK))


def _rglru_kernel(x_ref, gate_ref, cw_ref, cb_ref, wa_ref, ba_ref, wx_ref, bx_ref, lam_ref, o_ref,
                  tail_ref, h_ref, a_scr, u_scr):
    @pl.when(pl.program_id(1) == 0)
    def _():
        tail_ref[...] = jnp.zeros_like(tail_ref)
        h_ref[...] = jnp.zeros_like(h_ref)

    tt = x_ref.shape[0]
    x = x_ref[...]
    cw = cw_ref[...]
    xc = x * cw[CONV_K - 1:CONV_K, :] + cb_ref[...]
    for s in range(1, CONV_K):
        xc += _shift_rows(x, tail_ref, s) * cw[CONV_K - 1 - s:CONV_K - s, :]
    tail_ref[...] = x[tt - SUBLANES:, :]

    r = _sigmoid(_mm(xc, wa_ref[...]) + ba_ref[...])
    i = _sigmoid(_mm(xc, wx_ref[...]) + bx_ref[...])
    log_a = -LRU_C * r * _softplus(-lam_ref[...])
    a_scr[...] = jnp.exp(log_a)
    u_scr[...] = jnp.sqrt(1.0 - jnp.exp(2.0 * log_a)) * (i * xc)

    row = lax.broadcasted_iota(jnp.int32, (SUBLANES, x.shape[1]), 0)

    def step(i, h):
        rows = pl.ds(pl.multiple_of(i * SUBLANES, SUBLANES), SUBLANES)
        a = a_scr[rows, :]
        u = u_scr[rows, :]
        for s in (1, 2, 4):
            u = u + a * jnp.where(row >= s, pltpu.roll(u, s, axis=0), 0.0)
            a = a * jnp.where(row >= s, pltpu.roll(a, s, axis=0), 1.0)
        hb = u + a * h
        u_scr[rows, :] = hb
        return hb[SUBLANES - 1:, :]

    h_ref[...] = lax.fori_loop(0, tt // SUBLANES, step, h_ref[...], unroll=4)
    gate = gate_ref[...]
    gelu = 0.5 * gate * (1.0 + jnp.tanh(math.sqrt(2.0 / math.pi) * (gate + 0.044715 * gate * gate * gate)))
    o_ref[...] = (u_scr[...] * gelu).astype(o_ref.dtype)


def _rglru(p, cw, cb, wa, ba, wx, bx, lam, tt):
    bsz, t, _ = p.shape
    w = LRU_WIDTH

    def col(name):
        off, _ = EVEN_COLS[name]
        return pl.BlockSpec((None, tt, w), lambda b, i, j=off // w: (b, i, j))

    fix = lambda b, i: (0, 0)
    vec = pl.BlockSpec((1, w), fix)
    return pl.pallas_call(
        _rglru_kernel,
        grid=(bsz, t // tt),
        in_specs=[col("xb"), col("gate"), pl.BlockSpec((CONV_K, w), fix), vec,
                  pl.BlockSpec((w, w), fix), vec, pl.BlockSpec((w, w), fix), vec, vec],
        out_specs=pl.BlockSpec((None, tt, w), lambda b, i: (b, i, 0)),
        out_shape=jax.ShapeDtypeStruct((bsz, t, w), BF16),
        scratch_shapes=[pltpu.VMEM((SUBLANES, w), F32), pltpu.VMEM((1, w), F32),
                        pltpu.VMEM((tt, w), F32), pltpu.VMEM((tt, w), F32)],
        compiler_params=_params("parallel", "arbitrary"),
        name="rglru",
    )(p, p, cw, cb, wa, ba, wx, bx, lam)


def _ssd_kernel(z_ref, x_ref, bm_ref, cm_ref, dt_ref, cwx_ref, cbx_ref, cwb_ref, cbb_ref, cwc_ref, cbc_ref,
                dtb_ref, alog_ref, dskip_ref, nw_ref, expand_ref, sel_ref, o_ref,
                tx_ref, tb_ref, tc_ref, st_ref, xs_scr, bs_scr, cs_scr, acs_scr, xdt_scr, xdec_scr, ea_scr):
    @pl.when(pl.program_id(1) == 0)
    def _():
        tx_ref[...] = jnp.zeros_like(tx_ref)
        tb_ref[...] = jnp.zeros_like(tb_ref)
        tc_ref[...] = jnp.zeros_like(tc_ref)
        st_ref[...] = jnp.zeros_like(st_ref)

    tt = x_ref.shape[0]

    def conv_silu(src_ref, tail_ref, cw_ref, cb_ref, dst_ref):
        x = src_ref[...]
        cw = cw_ref[...]
        y = x * cw[CONV_K - 1:CONV_K, :] + cb_ref[...]
        for s in range(1, CONV_K):
            y += _shift_rows(x, tail_ref, s) * cw[CONV_K - 1 - s:CONV_K - s, :]
        tail_ref[...] = x[tt - SUBLANES:, :]
        dst_ref[...] = _silu(y)

    conv_silu(x_ref, tx_ref, cwx_ref, cbx_ref, xs_scr)
    conv_silu(bm_ref, tb_ref, cwb_ref, cbb_ref, bs_scr)
    conv_silu(cm_ref, tc_ref, cwc_ref, cbc_ref, cs_scr)

    tri = _tri(CHUNK)
    expand = expand_ref[...]
    dskip = dskip_ref[...]
    nw = nw_ref[...]
    hpg = SSD_HEADS // SSD_GROUPS

    dtc = _softplus(dt_ref[...] + dtb_ref[...])
    da = dtc * -jnp.exp(alog_ref[...])
    sums = _mm_01_left(sel_ref[...], da)
    acs_all = sums[:tt]
    tot = sums[tt:]
    acs_scr[...] = acs_all
    xdt_all = xs_scr[...] * _mm_01(dtc, expand)
    xdt_scr[...] = xdt_all
    xdec_scr[...] = xdt_all * _mm_01(jnp.exp(tot - acs_all), expand)
    ea_scr[...] = _mm_01(jnp.exp(acs_all), expand)

    def chunk(c, carry):
        rows = pl.ds(pl.multiple_of(c * CHUNK, CHUNK), CHUNK)
        acs = acs_scr[rows, :]
        acs_t = acs.T
        ea_x = ea_scr[rows, :]
        x = xs_scr[rows, :]
        xdt = xdt_scr[rows, :]
        xdec = xdec_scr[rows, :]
        cd_x = ea_x[CHUNK - 1:CHUNK, :]
        ys = []
        for g in range(SSD_GROUPS):
            sg = slice(g * SSD_GROUP_WIDTH, (g + 1) * SSD_GROUP_WIDTH)
            ss = slice(g * SSD_STATE, (g + 1) * SSD_STATE)
            bg = bs_scr[rows, ss]
            cg = cs_scr[rows, ss]
            cb = _mm(cg, bg, _NT)
            st = st_ref[g]
            yg = _mm(cg, st) * ea_x[:, sg]
            st_ref[g] = st * cd_x[:, sg] + _mm(bg, xdec[:, sg], _TN)
            yh = []
            for j in range(hpg):
                h = g * hpg + j
                seg = acs[:, h:h + 1] - acs_t[h:h + 1, :]
                lmat = jnp.exp(jnp.where(tri, seg, -jnp.inf))
                sh = slice(h * SSD_HEADDIM, (h + 1) * SSD_HEADDIM)
                yh.append(_mm(cb * lmat, xdt[:, sh]))
            ys.append(yg + jnp.concatenate(yh, axis=1))
        y = jnp.concatenate(ys, axis=1) + x * dskip
        y = y * _silu(z_ref[rows, :])
        outs = []
        for g in range(SSD_GROUPS):
            sg = slice(g * SSD_GROUP_WIDTH, (g + 1) * SSD_GROUP_WIDTH)
            yg = y[:, sg]
            ms = jnp.mean(yg * yg, axis=-1, keepdims=True)
            outs.append(yg * lax.rsqrt(ms + SSD_NORM_EPS))
        o_ref[rows, :] = (jnp.concatenate(outs, axis=1) * nw).astype(o_ref.dtype)
        return carry

    lax.fori_loop(0, tt // CHUNK, chunk, 0, unroll=2)


def _ssd(p, cw, cb, dtb, alog, dskip_x, nw, expand, tt):
    bsz, t, _ = p.shape

    def col(name):
        off, w = ODD_COLS[name]
        return pl.BlockSpec((None, tt, w), lambda b, i, j=off // w: (b, i, j))

    fix = lambda b, i: (0, 0)
    full = lambda a: pl.BlockSpec(a.shape, fix)
    gs = SSD_GROUPS * SSD_STATE
    cwx, cwb, cwc = cw[:, :SSD_INNER], cw[:, SSD_INNER:SSD_INNER + gs], cw[:, SSD_INNER + gs:]
    cbx, cbb, cbc = cb[:, :SSD_INNER], cb[:, SSD_INNER:SSD_INNER + gs], cb[:, SSD_INNER + gs:]
    sel = jnp.concatenate([_chunk_sel(tt, CHUNK), _chunk_sel(tt, CHUNK, "all")], axis=0)
    args = (cwx, cbx, cwb, cbb, cwc, cbc, dtb, alog, dskip_x, nw, expand, sel)
    return pl.pallas_call(
        _ssd_kernel,
        grid=(bsz, t // tt),
        in_specs=[col("z"), col("x"), col("bm"), col("cm"), col("dt")] + [full(a) for a in args],
        out_specs=pl.BlockSpec((None, tt, SSD_INNER), lambda b, i: (b, i, 0)),
        out_shape=jax.ShapeDtypeStruct((bsz, t, SSD_INNER), BF16),
        scratch_shapes=[pltpu.VMEM((SUBLANES, SSD_INNER), F32), pltpu.VMEM((SUBLANES, gs), F32),
                        pltpu.VMEM((SUBLANES, gs), F32),
                        pltpu.VMEM((SSD_GROUPS, SSD_STATE, SSD_GROUP_WIDTH), F32),
                        pltpu.VMEM((tt, SSD_INNER), F32), pltpu.VMEM((tt, gs), F32), pltpu.VMEM((tt, gs), F32),
                        pltpu.VMEM((tt, LANES), F32)] + [pltpu.VMEM((tt, SSD_INNER), F32)] * 3,
        compiler_params=_params("parallel", "arbitrary"),
        name="ssd",
    )(p, p, p, p, p, *args)


def _rwkv_kernel(r_ref, k_ref, v_ref, wa_ref, xg_ref, mur_ref, muk_ref, muv_ref, muwa_ref, mug_ref,
                 w0_ref, w2_ref, a0_ref, a2_ref, g2_ref, kk_ref, ka_ref, rk_ref, lng_ref, lnb_ref, ones_ref, sel_ref,
                 o_ref, tr_ref, tk_ref, tv_ref, twa_ref, tg_ref, st_ref, y_scr,
                 at_scr, rt_scr, bt_scr, kt_scr, bh_scr, kh_scr, v_scr, wc_scr):
    @pl.when(pl.program_id(1) == 0)
    def _():
        for ref in (tr_ref, tk_ref, tv_ref, twa_ref, tg_ref, st_ref):
            ref[...] = jnp.zeros_like(ref)

    tt = r_ref.shape[0]
    hd = RWKV_HEADDIM

    def mix(p_ref, tail_ref, mu_ref):
        p = p_ref[...]
        prev = _shift_rows(p, tail_ref, 1)
        tail_ref[...] = p[tt - SUBLANES:, :]
        return p + (prev - p) * mu_ref[...]

    r = mix(r_ref, tr_ref, mur_ref)
    k = mix(k_ref, tk_ref, muk_ref)
    v = mix(v_ref, tv_ref, muv_ref)
    xwa = mix(wa_ref, twa_ref, muwa_ref)
    xg = mix(xg_ref, tg_ref, mug_ref)

    lane = lax.broadcasted_iota(jnp.int32, xwa.shape, 1)
    lora_in = jnp.where(lane < RWKV_DECAY_LORA, jnp.tanh(xwa), xwa)
    w_log = -_softplus(-(w0_ref[...] + _mm_split(lora_in, w2_ref[...]))) - 0.5
    lw = -jnp.exp(w_log)
    a_sig = _sigmoid(a0_ref[...] + _mm_split(lora_in, a2_ref[...]))
    gate = _mm_split(_sigmoid(xg), g2_ref[...])
    ones_bd = ones_ref[...]
    kk = k * kk_ref[...]
    kk = kk / jnp.maximum(jnp.sqrt(_mm_01(kk * kk, ones_bd)), 1e-12)
    k = k * (1.0 + (a_sig - 1.0) * ka_ref[...])

    cs = RWKV_CHUNK
    tri_incl = _tri(cs)
    tri_strict = _tri(cs, strict=True)
    tri_col = jnp.concatenate([tri_strict, tri_incl], axis=0)
    eye = (tri_incl & ~tri_strict).astype(F32)

    sums = _mm_01_left(sel_ref[...], lw)
    cum = sums[:tt]
    tot = sums[tt:]
    bvec = kk * a_sig
    e_neg = jnp.exp(-cum)
    e_last = jnp.exp(tot - cum)
    for ref, val in ((at_scr, -kk * jnp.exp(cum - lw)), (rt_scr, r * jnp.exp(cum)), (bt_scr, bvec * e_neg),
                     (kt_scr, k * e_neg), (bh_scr, bvec * e_last), (kh_scr, k * e_last), (v_scr, v),
                     (wc_scr, jnp.exp(tot))):
        ref[...] = val

    def chunk(c, carry):
        rows = pl.ds(pl.multiple_of(c * cs, cs), cs)
        at = at_scr[rows, :]
        rt = rt_scr[rows, :]
        bt = bt_scr[rows, :]
        kt = kt_scr[rows, :]
        bh = bh_scr[rows, :]
        kh = kh_scr[rows, :]
        vc = v_scr[rows, :]
        wc = wc_scr[pl.ds(pl.multiple_of(c * cs, cs), 1), :]
        heads = range(RWKV_HEADS)
        sls = [slice(h * hd, (h + 1) * hd) for h in heads]
        amat = [_mm(jnp.concatenate([at[:, sl], rt[:, sl]], axis=0),
                    jnp.concatenate([bt[:, sl], kt[:, sl]], axis=0), _NT) for sl in sls]
        a_ab = [jnp.where(tri_strict, m[:cs, :cs], 0.0) for m in amat]
        a_rb = [jnp.where(tri_incl, m[cs:, :cs], 0.0) for m in amat]
        a_xk = [jnp.where(tri_col, m[:, cs:], 0.0) for m in amat]
        avrv = [_mm(a_xk[h], vc[:, sls[h]]) for h in heads]
        inv = [eye + m for m in a_ab]
        pw = [_mm(m, m) for m in a_ab]
        for _ in range(int(math.log2(cs)) - 2):
            both = [_mm(jnp.concatenate([inv[h], pw[h]], axis=0), pw[h]) for h in heads]
            inv = [inv[h] + both[h][:cs] for h in heads]
            pw = [m[cs:] for m in both]
        inv = [inv[h] + _mm(inv[h], pw[h]) for h in heads]
        tu = [_mm(inv[h], jnp.concatenate([at[:, sls[h]], avrv[h][:cs]], axis=1)) for h in heads]
        qy = [_mm(a_rb[h], tu[h]) + jnp.concatenate([rt[:, sls[h]], avrv[h][cs:]], axis=1)
              for h in heads]
        zb =[_mm(tu[h], bh[:, sls[h]], _TN) for h in heads]
        vk = [_mm(vc[:, sls[h]], kh[:, sls[h]], _TN) for h in heads]
        for h in heads:
            st = st_ref[h]
            y_scr[rows, sls[h]] = _mm(qy[h][:, :hd], st, _NT) + qy[h][:, hd:]
            st_ref[h] = st * wc[:, sls[h]] + _mm(st, zb[h][:hd]) + zb[h][hd:] + vk[h]
        return carry

    lax.fori_loop(0, tt // cs, chunk, 0, unroll=2)

    y = y_scr[...]
    inv_n = 1.0 / hd
    mu_y = _mm_01(y, ones_bd) * inv_n
    dy = y - mu_y
    var_y = _mm_01(dy * dy, ones_bd) * inv_n
    yn = dy * lax.rsqrt(var_y + RWKV_GN_EPS) * lng_ref[...] + lnb_ref[...]
    bonus = _mm_01(r * k * rk_ref[...], ones_bd) * v
    o_ref[...] = ((yn + bonus) * gate).astype(o_ref.dtype)


def _rwkv(p, mu, w0, w2p, a0, a2p, g2, k_k, k_a, r_k, ln_g, ln_b, ones_bd, tt):
    bsz, t, _ = p.shape
    w = RWKV_WIDTH

    def col(name):
        off, wd = ODD_COLS[name]
        return pl.BlockSpec((None, tt, wd), lambda b, i, j=off // wd: (b, i, j))

    fix = lambda b, i: (0, 0)
    full = lambda a: pl.BlockSpec(a.shape, fix)
    mur, muk, muv = mu[:, :w], mu[:, w:2 * w], mu[:, 2 * w:3 * w]
    muwa, mug = mu[:, 3 * w:3 * w + LANES], mu[:, 3 * w + LANES:]
    sel = jnp.concatenate([_chunk_sel(tt, RWKV_CHUNK), _chunk_sel(tt, RWKV_CHUNK, "all")], axis=0)
    args = (mur, muk, muv, muwa, mug, w0, w2p, a0, a2p, g2, k_k, k_a, r_k, ln_g, ln_b, ones_bd, sel)
    return pl.pallas_call(
        _rwkv_kernel,
        grid=(bsz, t // tt),
        in_specs=[col("r"), col("k"), col("v"), col("wa"), col("xg")] + [full(a) for a in args],
        out_specs=pl.BlockSpec((None, tt, w), lambda b, i: (b, i, 0)),
        out_shape=jax.ShapeDtypeStruct((bsz, t, w), BF16),
        scratch_shapes=[pltpu.VMEM((SUBLANES, w), F32), pltpu.VMEM((SUBLANES, w), F32),
                        pltpu.VMEM((SUBLANES, w), F32), pltpu.VMEM((SUBLANES, LANES), F32),
                        pltpu.VMEM((SUBLANES, LANES), F32),
                        pltpu.VMEM((RWKV_HEADS, RWKV_HEADDIM, RWKV_HEADDIM), F32)]
        + [pltpu.VMEM((tt, w), F32)] * 9,
        compiler_params=_params("parallel", "arbitrary"),
        name="rwkv7",
    )(p, p, p, p, p, *args)


def _reorder_cols(w, pieces, order, width):
    parts, pos = [], 0
    for name, (off, slot) in order.items():
        assert off == pos, "slots must be listed in order and contiguous"
        start, size = pieces[name]
        parts.append(w[:, start:start + size])
        if size < slot:
            parts.append(jnp.zeros((w.shape[0], slot - size), w.dtype))
        pos += slot
    assert pos == width
    return jnp.concatenate(parts, axis=1)


def _block_diag(w):
    n, i, j = w.shape
    eye = jnp.eye(n, dtype=w.dtype)
    return (eye[:, None, :, None] * w[:, :, None, :]).reshape(n * i, n * j)


def _row(v, width=None):
    v = v.reshape(1, -1).astype(F32)
    if width is not None and v.shape[1] < width:
        v = jnp.pad(v, ((0, 0), (0, width - v.shape[1])))
    return v


def _even_layer(x, x_in, w_in, gk_w2, gk_b, gla_norm, conv_w, conv_b, wa, ba, wx, bx, lam,
                w_out, ln1_g, ln1_b, f_w1, f_w3, f_w2, ln2_g, ln2_b, bsz, t):
    qk, vw, lw = GLA_QK, GLA_VW, LRU_WIDTH
    pieces = dict(q=(0, qk), k=(qk, qk), v=(2 * qk, vw), g=(2 * qk + vw, vw), gk=(2 * qk + 2 * vw, GLA_GK_RANK),
                  xb=(2 * qk + 2 * vw + GLA_GK_RANK, lw), gate=(2 * qk + 2 * vw + GLA_GK_RANK + lw, lw))
    w_in_r = _reorder_cols(w_in, pieces, EVEN_COLS, EVEN_WIDTH).astype(BF16)
    p = _proj(x_in, w_in_r, tm=min(ROW_TILE, x.shape[0])).reshape(bsz, t, EVEN_WIDTH)
    tt = min(GLA_RGLRU_TIME_TILE, t)
    gkw = jnp.pad(gk_w2, ((0, LANES - GLA_GK_RANK), (0, 0)))
    y_a = _gla(p, gkw, _row(gk_b), _row(gla_norm), tt)
    y_b = _rglru(p, conv_w, _row(conv_b), _block_diag(wa).astype(BF16), _row(ba), _block_diag(wx).astype(BF16),
                 _row(bx), _row(lam), tt)
    n = bsz * t
    tm = min(ROW_TILE, n)
    return _mix_ffn(y_a.reshape(n, vw), y_b.reshape(n, lw), w_out[:vw].astype(BF16), w_out[vw:].astype(BF16), x,
                    _row(ln1_g), _row(ln1_b), f_w1.astype(BF16), f_w3.astype(BF16), f_w2.astype(BF16),
                    _row(ln2_g), _row(ln2_b), tm)


def _odd_layer(x, xb, w_in, conv_w, conv_b, dt_bias, a_log, d_skip, ssd_norm, mu, w0, w2, a0, a2, g2, k_k, k_a,
               r_k, rln_g, rln_b, w_out, ln1_g, ln1_b, router, ew1, ew3, ew2, ln2_g, ln2_b, bsz, t):
    si, gs, rw = SSD_INNER, SSD_GROUPS * SSD_STATE, RWKV_WIDTH
    o = 2 * si + 2 * gs + SSD_HEADS
    pieces = dict(z=(0, si), x=(si, si), bm=(2 * si, gs), cm=(2 * si + gs, gs), dt=(2 * si + 2 * gs, SSD_HEADS),
                  r=(o, rw), k=(o + rw, rw), v=(o + 2 * rw, rw),
                  wa=(o + 3 * rw, RWKV_DECAY_LORA + RWKV_AAA_LORA), xg=(o + 3 * rw + LANES, RWKV_GATE_LORA))
    w_in_r = _reorder_cols(w_in, pieces, ODD_COLS, ODD_WIDTH).astype(BF16)
    n = bsz * t
    p = _proj(xb, w_in_r, tm=min(ROW_TILE, n)).reshape(bsz, t, ODD_WIDTH)
    expand = jnp.pad(jnp.repeat(jnp.eye(SSD_HEADS, dtype=BF16), SSD_HEADDIM, axis=1), ((0, LANES - SSD_HEADS), (0, 0)))
    y_c = _ssd(p, conv_w, _row(conv_b), _row(dt_bias, LANES), _row(a_log, LANES),
               _row(jnp.repeat(d_skip, SSD_HEADDIM)), _row(ssd_norm), expand, tt=min(SSD_RWKV_TIME_TILE, t))
    zeros = jnp.zeros((RWKV_DECAY_LORA, rw), F32)
    ones_bd = _block_diag(jnp.ones((RWKV_HEADS, RWKV_HEADDIM, RWKV_HEADDIM), BF16))
    y_d = _rwkv(p, _row(mu), _row(w0), jnp.concatenate([w2, zeros]), _row(a0), jnp.concatenate([zeros, a2]), g2,
                _row(k_k), _row(k_a), _row(r_k), _row(rln_g), _row(rln_b), ones_bd, tt=min(SSD_RWKV_TIME_TILE, t))
    x1, x1b, *routing = _mix_router(y_c.reshape(n, si), y_d.reshape(n, rw), w_out[:si].astype(BF16),
                                    w_out[si:].astype(BF16), x, _row(ln1_g), _row(ln1_b), router)
    y = _moe_ln(x1, x1b, routing, ew1.astype(BF16), ew3.astype(BF16), ew2, _row(ln2_g), _row(ln2_b))
    return y, None


def kernel(x, e_w_in, e_gk_w2, e_gk_b, e_gla_norm, e_conv_w, e_conv_b, e_lru_wa, e_lru_ba, e_lru_wx, e_lru_bx, e_lru_lambda, e_w_out, e_ln1_g, e_ln1_b, e_ffn_w1, e_ffn_w3, e_ffn_w2, e_ln2_g, e_ln2_b, o_w_in, o_conv_w, o_conv_b, o_dt_bias, o_a_log, o_d_skip, o_ssd_norm, o_rwkv_mu, o_rwkv_w0, o_rwkv_w2, o_rwkv_a0, o_rwkv_a2, o_rwkv_g2, o_rwkv_k_k, o_rwkv_k_a, o_rwkv_r_k, o_rwkv_ln_g, o_rwkv_ln_b, o_w_out, o_ln1_g, o_ln1_b, o_router, o_exp_w1, o_exp_w3, o_exp_w2, o_ln2_g, o_ln2_b):
    bsz, t, d = x.shape
    h = x.reshape(bsz * t, d)
    hb = h
    for i in range(DEPTH):
        j = i // 2
        if i % 2 == 0:
            h, hb = _even_layer(h, hb, e_w_in[j], e_gk_w2[j], e_gk_b[j], e_gla_norm[j], e_conv_w[j], e_conv_b[j],
                                e_lru_wa[j], e_lru_ba[j], e_lru_wx[j], e_lru_bx[j], e_lru_lambda[j], e_w_out[j],
                                e_ln1_g[j], e_ln1_b[j], e_ffn_w1[j], e_ffn_w3[j], e_ffn_w2[j], e_ln2_g[j],
                                e_ln2_b[j], bsz, t)
        else:
            h, hb = _odd_layer(h, hb, o_w_in[j], o_conv_w[j], o_conv_b[j], o_dt_bias[j], o_a_log[j], o_d_skip[j],
                               o_ssd_norm[j], o_rwkv_mu[j], o_rwkv_w0[j], o_rwkv_w2[j], o_rwkv_a0[j],
                               o_rwkv_a2[j], o_rwkv_g2[j], o_rwkv_k_k[j], o_rwkv_k_a[j], o_rwkv_r_k[j],
                               o_rwkv_ln_g[j], o_rwkv_ln_b[j], o_w_out[j], o_ln1_g[j], o_ln1_b[j], o_router[j],
                               o_exp_w1[j], o_exp_w3[j], o_exp_w2[j], o_ln2_g[j], o_ln2_b[j], bsz, t)
    return h.reshape(bsz, t, d)
```

```python
import math

import jax
import jax.numpy as jnp
from jax import lax
from jax.experimental import pallas as pl
from jax.experimental.pallas import tpu as pltpu

F32 = jnp.float32
BF16 = jnp.bfloat16

D_MODEL = 1024
DEPTH = 2
DEEPNORM_ALPHA = (2 * DEPTH) ** 0.25
LN_EPS = 1e-5
CONV_K = 4
CHUNK = 64

GLA_HEADS = 4
GLA_DK = 64
GLA_DV = 128
GLA_QK = GLA_HEADS * GLA_DK
GLA_VW = GLA_HEADS * GLA_DV
GLA_GK_RANK = 16
GLA_GATE_NORM = 16.0
GLA_NORM_EPS = 1e-5

LRU_WIDTH = 512
LRU_BLOCKS = 8
LRU_C = 8.0

SSD_HEADS = 16
SSD_HEADDIM = 64
SSD_INNER = SSD_HEADS * SSD_HEADDIM
SSD_GROUPS = 2
SSD_STATE = 128
SSD_GROUP_WIDTH = SSD_INNER // SSD_GROUPS
SSD_NORM_EPS = 1e-5

RWKV_HEADS = 8
RWKV_HEADDIM = 64
RWKV_WIDTH = RWKV_HEADS * RWKV_HEADDIM
RWKV_DECAY_LORA = 64
RWKV_AAA_LORA = 64
RWKV_GATE_LORA = 128
RWKV_GN_EPS = 64e-5
RWKV_CHUNK = 128

N_EXPERTS = 8
LANES = 128
SUBLANES = 8
assert N_EXPERTS == SUBLANES
ROUTER_ROWS = 16
MOE_TB = 512
MOE_TR = 256
MOE_TE = 512
assert MOE_TE % MOE_TR == 0
MOE_COMBINE_BUFS = 4
VMEM_LIMIT = 56 * 1024 * 1024

ROW_TILE = 512
GLA_RGLRU_TIME_TILE = 512
SSD_RWKV_TIME_TILE = 256
EXPERT_F_TILE = 1792

EVEN_COLS = dict(v=(0, 512), g=(512, 512), xb=(1024, 512), gate=(1536, 512), q=(2048, 256), k=(2304, 256),
                 gk=(2560, 128))
EVEN_WIDTH = 2688
ODD_COLS = dict(z=(0, 1024), x=(1024, 1024), r=(2048, 512), k=(2560, 512), v=(3072, 512), bm=(3584, 256),
                cm=(3840, 256), wa=(4096, 128), xg=(4224, 128), dt=(4352, 128))
ODD_WIDTH = 4480


def _mm(a, b, dims=((1,), (0,))):
    return lax.dot_general(a.astype(BF16), b.astype(BF16), (dims, ((), ())), preferred_element_type=F32)


def _mm_split(a, b, dims=((1,), (0,))):
    a_hi = a.astype(BF16)
    b_hi = b.astype(BF16)
    a_lo = (a - a_hi.astype(F32)).astype(BF16)
    b_lo = (b - b_hi.astype(F32)).astype(BF16)
    return _mm(a_hi, b_hi, dims) + _mm(a_lo, b_hi, dims) + _mm(a_hi, b_lo, dims)


_NT = ((1,), (1,))
_TN = ((0,), (0,))


def _mm_01(x, sel):
    hi = x.astype(BF16)
    lo = (x - hi.astype(F32)).astype(BF16)
    return (jnp.dot(hi, sel, preferred_element_type=F32) + jnp.dot(lo, sel, preferred_element_type=F32))


def _sigmoid(x):
    return 1.0 / (1.0 + jnp.exp(-x))


def _softplus(x):
    return jnp.maximum(x, 0.0) + jnp.log(1.0 + jnp.exp(-jnp.abs(x)))


def _silu(x):
    return x * _sigmoid(x)


def _layer_norm(h, g, b):
    mu = jnp.mean(h, axis=-1, keepdims=True)
    d = h - mu
    var = jnp.mean(d * d, axis=-1, keepdims=True)
    return d * lax.rsqrt(var + LN_EPS) * g + b


def _params(*sem):
    return pltpu.CompilerParams(dimension_semantics=sem, vmem_limit_bytes=VMEM_LIMIT)


def _tri(n, strict=False):
    r = lax.broadcasted_iota(jnp.int32, (n, n), 0)
    c = lax.broadcasted_iota(jnp.int32, (n, n), 1)
    return (r > c) if strict else (r >= c)


def _chunk_sel(n, chunk, kind="incl"):
    r = jnp.arange(n)[:, None]
    c = jnp.arange(n)[None, :]
    first = r - r % chunk
    upper = r if kind == "incl" else first + (chunk - 1)
    return ((c >= first) & (c <= upper)).astype(BF16)


def _mm_01_left(sel, x):
    hi = x.astype(BF16)
    lo = (x - hi.astype(F32)).astype(BF16)
    return jnp.dot(sel, hi, preferred_element_type=F32) + jnp.dot(sel, lo, preferred_element_type=F32)


def _shift_rows(p, tail_ref, s):
    n = p.shape[0]
    rolled = pltpu.roll(p, s, axis=0)
    head = pltpu.roll(tail_ref[...], s, axis=0)
    row = lax.broadcasted_iota(jnp.int32, (SUBLANES, p.shape[1]), 0)
    fixed = jnp.where(row < s, head, rolled[:SUBLANES])
    return jnp.concatenate([fixed, rolled[SUBLANES:]], axis=0) if n > SUBLANES else fixed


def _proj_kernel(x_ref, w_ref, o_ref):
    o_ref[...] = jnp.dot(x_ref[...].astype(BF16), w_ref[...], preferred_element_type=F32)


def _resident(shape):
    return pl.BlockSpec(shape, lambda *_: (0,) * len(shape), pipeline_mode=pl.Buffered(1))


def _proj(x, w, tm):
    m, k = x.shape
    n = w.shape[1]
    return pl.pallas_call(
        _proj_kernel,
        grid=(m // tm,),
        in_specs=[pl.BlockSpec((tm, k), lambda i: (i, 0)), _resident((k, n))],
        out_specs=pl.BlockSpec((tm, n), lambda i: (i, 0)),
        out_shape=jax.ShapeDtypeStruct((m, n), F32),
        compiler_params=_params("parallel"),
        name="in_proj",
    )(x, w)


def _outproj_ln_value(ya_ref, yb_ref, wa_ref, wb_ref, res_ref, g_ref, b_ref):
    acc = jnp.dot(ya_ref[...], wa_ref[...], preferred_element_type=F32)
    acc += jnp.dot(yb_ref[...], wb_ref[...], preferred_element_type=F32)
    return _layer_norm(DEEPNORM_ALPHA * res_ref[...] + acc, g_ref[...], b_ref[...])


def _mix_ffn_kernel(ya_ref, yb_ref, wa_ref, wb_ref, res_ref, g1_ref, b1_ref, w1_ref, w3_ref, w2_ref, g2_ref, b2_ref,
                    o_ref, ob_ref):
    x1 = _outproj_ln_value(ya_ref, yb_ref, wa_ref, wb_ref, res_ref, g1_ref, b1_ref)
    xb = x1.astype(BF16)
    h1 = jnp.dot(xb, w1_ref[...], preferred_element_type=F32)
    h3 = jnp.dot(xb, w3_ref[...], preferred_element_type=F32)
    ff = jnp.dot((_silu(h1) * h3).astype(BF16), w2_ref[...], preferred_element_type=F32)
    y = _layer_norm(DEEPNORM_ALPHA * x1 + ff, g2_ref[...], b2_ref[...])
    o_ref[...] = y
    ob_ref[...] = y.astype(BF16)


def _mix_ffn(ya, yb, wa, wb, res, g1, b1, w1, w3, w2, g2, b2, tm):
    m, d = res.shape
    row = lambda i: (i, 0)
    consts = (wa, wb, None, g1, b1, w1, w3, w2, g2, b2)
    return pl.pallas_call(
        _mix_ffn_kernel,
        grid=(m // tm,),
        in_specs=[pl.BlockSpec((tm, ya.shape[1]), row), pl.BlockSpec((tm, yb.shape[1]), row)]
        + [pl.BlockSpec((tm, d), row) if c is None else _resident(c.shape) for c in consts],
        out_specs=[pl.BlockSpec((tm, d), row), pl.BlockSpec((tm, d), row)],
        out_shape=[jax.ShapeDtypeStruct((m, d), F32), jax.ShapeDtypeStruct((m, d), BF16)],
        compiler_params=_params("parallel"),
        name="out_proj_ffn_ln",
    )(ya, yb, wa, wb, res, g1, b1, w1, w3, w2, g2, b2)


def _mix_router_kernel(ya_ref, yb_ref, wa_ref, wb_ref, res_ref, g_ref, b_ref, w_ref, before_ref, o_ref, ob_ref,
                       gates_ref, rank_ref, rank_t_ref, cnt_ref, carry_ref):
    @pl.when(pl.program_id(0) == 0)
    def _():
        carry_ref[...] = jnp.zeros_like(carry_ref)

    x1 = _outproj_ln_value(ya_ref, yb_ref, wa_ref, wb_ref, res_ref, g_ref, b_ref)
    o_ref[...] = x1
    ob_ref[...] = x1.astype(BF16)
    logits = _mm_split(w_ref[...], x1, _NT)
    sub = lax.broadcasted_iota(jnp.int32, logits.shape, 0)
    neg = jnp.float32(-jnp.inf)
    l1 = jnp.where(sub < N_EXPERTS, logits, neg)
    m1 = jnp.max(l1, axis=0, keepdims=True)
    i1 = jnp.min(jnp.where(l1 == m1, sub, ROUTER_ROWS), axis=0, keepdims=True)
    l2 = jnp.where(sub == i1, neg, l1)
    m2 = jnp.max(l2, axis=0, keepdims=True)
    i2 = jnp.min(jnp.where(l2 == m2, sub, ROUTER_ROWS), axis=0, keepdims=True)
    ex = jnp.exp(m2 - m1)
    w_top = 1.0 / (1.0 + ex)
    gates_t = jnp.where(sub == i1, w_top, 0.0) + jnp.where(sub == i2, ex * w_top, 0.0)

    sel = jnp.where(sub == i1, 1.0, 0.0) + jnp.where(sub == i2, 1.0, 0.0)
    before = _mm(sel, before_ref[...])
    carry = carry_ref[...]
    rank_t = jnp.where(sel > 0.0, carry[:, :1] + before, -1.0)
    rank_t_ref[...] = rank_t[:SUBLANES, :]
    carry = carry + jnp.sum(sel, axis=1, keepdims=True)
    carry_ref[...] = carry
    cnt_ref[...] = carry[:SUBLANES, :]
    pad = jnp.zeros((LANES - ROUTER_ROWS, sel.shape[1]), F32)
    gates_ref[...] = jnp.concatenate([gates_t, pad], axis=0).T
    rank_ref[...] = jnp.concatenate([rank_t, pad], axis=0).T


def _mix_router(ya, yb, wa, wb, res, g, b, router):
    m, d = res.shape
    nblk = m // MOE_TB
    row = lambda i: (i, 0)
    w = jnp.pad(router.T, ((0, ROUTER_ROWS - N_EXPERTS), (0, 0)))
    earlier = (jnp.arange(MOE_TB)[:, None] < jnp.arange(MOE_TB)[None, :]).astype(BF16)
    return pl.pallas_call(
        _mix_router_kernel,
        grid=(nblk,),
        in_specs=[pl.BlockSpec((MOE_TB, ya.shape[1]), row), pl.BlockSpec((MOE_TB, yb.shape[1]), row),
                  _resident(wa.shape), _resident(wb.shape), pl.BlockSpec((MOE_TB, d), row),
                  _resident((1, d)), _resident((1, d)), _resident(w.shape), _resident(earlier.shape)],
        out_specs=[pl.BlockSpec((MOE_TB, d), row), pl.BlockSpec((MOE_TB, d), row),
                   pl.BlockSpec((MOE_TB, LANES), row), pl.BlockSpec((MOE_TB, LANES), row),
                   pl.BlockSpec((SUBLANES, MOE_TB), lambda i: (0, i)), pl.BlockSpec((SUBLANES, LANES), row)],
        out_shape=[jax.ShapeDtypeStruct((m, d), F32), jax.ShapeDtypeStruct((m, d), BF16),
                   jax.ShapeDtypeStruct((m, LANES), F32), jax.ShapeDtypeStruct((m, LANES), F32),
                   jax.ShapeDtypeStruct((SUBLANES, m), F32), jax.ShapeDtypeStruct((nblk * SUBLANES, LANES), F32)],
        scratch_shapes=[pltpu.VMEM((ROUTER_ROWS, LANES), F32)],
        compiler_params=_params("arbitrary"),
        name="out_proj_ln_router",
    )(ya, yb, wa, wb, res, g, b, w, earlier)


def _moe_plan(cnt, n_tok):
    i32 = jnp.int32
    nblk = n_tok // MOE_TB
    c_inc = cnt.reshape(nblk, SUBLANES, LANES)[:, :N_EXPERTS, 0].astype(i32)
    c_exc = jnp.concatenate([jnp.zeros((1, N_EXPERTS), i32), c_inc[:-1]], axis=0)
    gsz = (c_inc[-1] + MOE_TE - 1) // MOE_TE * MOE_TE
    gend = jnp.cumsum(gsz)
    start = gend - gsz
    n_et = _moe_rows(n_tok) // MOE_TE
    n_act = gend[-1] // MOE_TE
    et = jnp.minimum(jnp.arange(n_et, dtype=i32), jnp.maximum(n_act - 1, 0))
    et_expert = jnp.sum((gend[None, :] <= (et * MOE_TE)[:, None]).astype(i32), axis=1)
    et_active = (jnp.arange(n_et, dtype=i32) < n_act).astype(i32)

    lo = start[None, :] + c_exc
    hi = start[None, :] + c_inc
    first_tile = lo // MOE_TR
    n_items = jnp.where(hi > lo, (hi - 1) // MOE_TR - first_tile + 1, 0)
    wmax = _moe_rows(n_tok) // MOE_TR + N_EXPERTS * nblk
    exp_id = jnp.broadcast_to(jnp.arange(N_EXPERTS, dtype=i32)[None, :], (nblk, N_EXPERTS))

    n, ft, exp = (a.reshape(-1) for a in (n_items, first_tile, exp_id))
    inc = jnp.cumsum(n)
    wc = jnp.minimum(jnp.arange(wmax, dtype=i32), jnp.maximum(inc[-1] - 1, 0))
    idx = jnp.sum((inc[None, :] <= wc[:, None]).astype(i32), axis=1)
    per_block = jnp.sum(n_items, axis=1)
    by_block = (ft[idx] + wc - (inc - n)[idx], exp[idx], jnp.cumsum(per_block) - per_block, per_block)

    row0 = jnp.arange(_moe_rows(n_tok) // MOE_TR, dtype=i32) * MOE_TR
    group = jnp.minimum(jnp.sum((gend[None, :] <= row0[:, None]).astype(i32), axis=1), N_EXPERTS - 1)
    rank_lo = row0 - start[group]
    count = c_inc[-1][group]
    rank_hi = jnp.minimum(rank_lo + MOE_TR, count)
    holds = (rank_lo < count) & (row0 < gend[-1])
    blk_lo = jnp.sum((c_inc[:, group].T <= rank_lo[:, None]).astype(i32), axis=1)
    blk_hi = jnp.sum((c_exc[:, group].T < rank_hi[:, None]).astype(i32), axis=1)
    by_tile = (group, jnp.minimum(blk_lo, nblk - 1), jnp.where(holds, blk_hi - blk_lo, 0))

    return start, (et, et_expert, et_active), by_tile, by_block


def _moe_rows(n_tok):
    return 2 * n_tok + N_EXPERTS * MOE_TE


def _one_hot_rows(dest, tile):
    row = lax.broadcasted_iota(jnp.int32, (MOE_TR, MOE_TB), 0) + tile * MOE_TR
    return jnp.where(dest == row.astype(F32), 1.0, 0.0).astype(BF16)


def _dispatch_kernel(exp_ref, blk0_ref, nblk_ref, x_ref, dest_ref, xs_ref):
    j = pl.program_id(0)
    xs_ref[...] = jnp.zeros_like(xs_ref)

    def block(i, carry):
        b = blk0_ref[j] + i
        onehot = _one_hot_rows(dest_ref[b, pl.ds(exp_ref[j], 1), :], j)
        xblk = x_ref[pl.ds(pl.multiple_of(b * MOE_TB, MOE_TB), MOE_TB), :]
        xs_ref[...] += jnp.dot(onehot, xblk, preferred_element_type=F32).astype(BF16)
        return carry

    lax.fori_loop(0, nblk_ref[j], block, 0)


def _dispatch(plan, xb, dest_t):
    exp, blk0, nblk = plan
    n, d = xb.shape
    rows = _moe_rows(n)
    dest3 = dest_t.reshape(N_EXPERTS, n // MOE_TB, MOE_TB).transpose(1, 0, 2)
    grid_spec = pltpu.PrefetchScalarGridSpec(
        num_scalar_prefetch=3,
        grid=(rows // MOE_TR,),
        in_specs=[_resident(xb.shape), _resident(dest3.shape)],
        out_specs=pl.BlockSpec((MOE_TR, d), lambda j, e, b, c: (j, 0)),
    )
    return pl.pallas_call(
        _dispatch_kernel,
        grid_spec=grid_spec,
        out_shape=jax.ShapeDtypeStruct((rows, d), BF16),
        compiler_params=_params("arbitrary"),
        name="moe_dispatch",
    )(exp, blk0, nblk, xb, dest3)


def _expert_kernel(et_ref, ee_ref, ea_ref, xs_ref, w1_ref, w3_ref, w2_ref, ys_ref, acc_ref):
    j = pl.program_id(0)
    f = pl.program_id(1)

    @pl.when((ea_ref[j] == 0) & (f == 0))
    def _():
        ys_ref[...] = jnp.zeros_like(ys_ref)

    @pl.when(ea_ref[j] == 1)
    def _():
        @pl.when(f == 0)
        def _():
            acc_ref[...] = jnp.zeros_like(acc_ref)

        x = xs_ref[...]
        h1 = jnp.dot(x, w1_ref[...], preferred_element_type=F32)
        h3 = jnp.dot(x, w3_ref[...], preferred_element_type=F32)
        acc_ref[...] += jnp.dot((_silu(h1) * h3).astype(BF16), w2_ref[...].astype(BF16),
                                preferred_element_type=F32)

        @pl.when(f == pl.num_programs(1) - 1)
        def _():
            ys_ref[...] = acc_ref[...].astype(BF16)


def _experts(plan, xs, w1, w3, w2, tf):
    et, ee, ea = plan
    rows, d = xs.shape
    nf = w1.shape[2] // tf
    fidx = lambda f, a, j: f * a[j] + (nf - 1) * (1 - a[j])
    grid_spec = pltpu.PrefetchScalarGridSpec(
        num_scalar_prefetch=3,
        grid=(et.shape[0], nf),
        in_specs=[pl.BlockSpec((MOE_TE, d), lambda j, f, t, e, a: (t[j], 0)),
                  pl.BlockSpec((None, d, tf), lambda j, f, t, e, a: (e[j], 0, fidx(f, a, j))),
                  pl.BlockSpec((None, d, tf), lambda j, f, t, e, a: (e[j], 0, fidx(f, a, j))),
                  pl.BlockSpec((None, tf, d), lambda j, f, t, e, a: (e[j], fidx(f, a, j), 0))],
        out_specs=pl.BlockSpec((MOE_TE, d), lambda j, f, t, e, a: (j, 0)),
        scratch_shapes=[pltpu.VMEM((MOE_TE, d), F32)],
    )
    return pl.pallas_call(
        _expert_kernel,
        grid_spec=grid_spec,
        out_shape=jax.ShapeDtypeStruct((rows, d), BF16),
        compiler_params=_params("arbitrary", "arbitrary"),
        name="moe_experts",
    )(et, ee, ea, xs, w1, w3, w2)


def _combine_kernel(tile_ref, exp_ref, off_ref, cnt_ref, ys_hbm, dest_ref, gates_ref, res_ref, g_ref, b_ref, o_ref,
                    buf_ref, sem_ref, acc_ref):
    j = pl.program_id(0)
    base = off_ref[j]
    n = cnt_ref[j]

    def fetch(i, slot, first=base):
        row0 = pl.multiple_of(tile_ref[first + i] * MOE_TR, MOE_TR)
        return pltpu.make_async_copy(ys_hbm.at[pl.ds(row0, MOE_TR), :], buf_ref.at[slot], sem_ref.at[slot])

    def prime(blk):
        for i0 in range(MOE_COMBINE_BUFS - 1):
            @pl.when(i0 < cnt_ref[blk])
            def _(i0=i0):
                fetch(i0, i0, off_ref[blk]).start()

    @pl.when(j == 0)
    def _():
        prime(0)

    acc_ref[...] = jnp.zeros_like(acc_ref)
    dest = dest_ref[...]
    gates = gates_ref[...]
    lane = lax.broadcasted_iota(jnp.int32, dest.shape, 1)

    def item(i, carry):
        slot = lax.rem(i, MOE_COMBINE_BUFS)
        ahead = i + (MOE_COMBINE_BUFS - 1)

        @pl.when(ahead < n)
        def _():
            fetch(ahead, lax.rem(ahead, MOE_COMBINE_BUFS)).start()

        mine = lane == exp_ref[base + i]
        dcol = jnp.sum(jnp.where(mine, dest, 0.0), axis=1, keepdims=True)
        gcol = jnp.sum(jnp.where(mine, gates, 0.0), axis=1, keepdims=True)
        col = lax.broadcasted_iota(jnp.int32, (MOE_TB, MOE_TR), 1) + tile_ref[base + i] * MOE_TR
        onehot = jnp.where(dcol == col.astype(F32), 1.0, 0.0).astype(BF16)
        fetch(i, slot).wait()
        acc_ref[...] += jnp.dot(onehot, buf_ref[slot], preferred_element_type=F32) * gcol
        return carry

    lax.fori_loop(0, n, item, 0)

    @pl.when(j + 1 < pl.num_programs(0))
    def _():
        prime(j + 1)

    o_ref[...] = _layer_norm(DEEPNORM_ALPHA * res_ref[...] + acc_ref[...], g_ref[...], b_ref[...])


def _combine_ln(plan, ys, dest, gates, res, g, b):
    tile, exp, off, cnt = plan
    n, d = res.shape
    tok = lambda j, *_: (j, 0)
    grid_spec = pltpu.PrefetchScalarGridSpec(
        num_scalar_prefetch=4,
        grid=(n // MOE_TB,),
        in_specs=[pl.BlockSpec(memory_space=pl.ANY),
                  pl.BlockSpec((MOE_TB, LANES), tok), pl.BlockSpec((MOE_TB, LANES), tok), pl.BlockSpec((MOE_TB, d), tok),
                  _resident((1, d)), _resident((1, d))],
        out_specs=pl.BlockSpec((MOE_TB, d), tok),
        scratch_shapes=[pltpu.VMEM((MOE_COMBINE_BUFS, MOE_TR, d), BF16), pltpu.SemaphoreType.DMA((MOE_COMBINE_BUFS,)),
                        pltpu.VMEM((MOE_TB, d), F32)],
    )
    return pl.pallas_call(
        _combine_kernel,
        grid_spec=grid_spec,
        out_shape=jax.ShapeDtypeStruct((n, d), F32),
        compiler_params=_params("arbitrary"),
        name="moe_combine_ln",
    )(tile, exp, off, cnt, ys, dest, gates, res, g, b)


def _moe_ln(x, xb, routing, w1, w3, w2, g, b):
    n = x.shape[0]
    gates, rank, rank_t, cnt = routing
    start, expert_tiles, by_tile, by_block = _moe_plan(cnt, n)
    startf = start.astype(F32)
    dest = jnp.where(rank >= 0.0, rank + jnp.pad(startf, (0, LANES - N_EXPERTS))[None, :], -1.0)
    dest_t = jnp.where(rank_t >= 0.0, rank_t + startf[:, None], -1.0)
    xs = _dispatch(by_tile, xb, dest_t)
    ys = _experts(expert_tiles, xs, w1, w3, w2, tf=EXPERT_F_TILE)
    return _combine_ln(by_block, ys, dest, gates, x, g, b)


def _gla_kernel(q_ref, k_ref, v_ref, g_ref, gk_ref, gkw_ref, gkb_ref, nw_ref, sel_ref, o_ref, st_ref, bc_scr,
                qd_scr, kd_scr):
    @pl.when(pl.program_id(1) == 0)
    def _():
        st_ref[...] = jnp.zeros_like(st_ref)

    tt = q_ref.shape[0]
    tri = _tri(CHUNK)
    nw = nw_ref[...]

    pre = _mm_split(gk_ref[...], gkw_ref[...]) + gkb_ref[...]
    gk = (jnp.minimum(pre, 0.0) - jnp.log(1.0 + jnp.exp(-jnp.abs(pre)))) * (1.0 / GLA_GATE_NORM)
    bc_all = _mm_01_left(sel_ref[...], gk)
    bc_scr[...] = bc_all
    qd_scr[...] = q_ref[...] * (GLA_DK ** -0.5) * jnp.exp(bc_all)
    kd_scr[...] = k_ref[...] * jnp.exp(-bc_all)

    def chunk(c, carry):
        rows = pl.ds(pl.multiple_of(c * CHUNK, CHUNK), CHUNK)
        bc = bc_scr[rows, :]
        bl = bc[CHUNK - 1:CHUNK, :]
        qd = qd_scr[rows, :]
        kd = kd_scr[rows, :]
        kl = k_ref[rows, :] * jnp.exp(bl - bc)
        dec = jnp.exp(bl)
        heads = range(GLA_HEADS)
        sks = [slice(h * GLA_DK, (h + 1) * GLA_DK) for h in heads]
        svs = [slice(h * GLA_DV, (h + 1) * GLA_DV) for h in heads]
        att = [jnp.where(tri, _mm(qd[:, sk], kd[:, sk], _NT), 0.0) for sk in sks]
        vs = [v_ref[rows, sv] for sv in svs]
        sts = [st_ref[h] for h in heads]
        os_ = [_mm(att[h], vs[h]) + _mm(qd[:, sks[h]], sts[h], _NT) for h in heads]
        for h in heads:
            st_ref[h] = sts[h] * dec[:, sks[h]] + _mm(vs[h], kl[:, sks[h]], _TN)
        for h in heads:
            o = os_[h]
            ms = jnp.mean(o * o, axis=-1, keepdims=True)
            y = o * lax.rsqrt(ms + GLA_NORM_EPS) * nw * _silu(g_ref[rows, svs[h]])
            o_ref[rows, svs[h]] = y.astype(o_ref.dtype)
        return carry

    lax.fori_loop(0, tt // CHUNK, chunk, 0, unroll=2)


def _gla(p, gkw, gkb, nw, tt):
    bsz, t, _ = p.shape

    def col(name):
        off, w = EVEN_COLS[name]
        return pl.BlockSpec((None, tt, w), lambda b, i, j=off // w: (b, i, j))

    fix = lambda b, i: (0, 0)
    return pl.pallas_call(
        _gla_kernel,
        grid=(bsz, t // tt),
        in_specs=[col("q"), col("k"), col("v"), col("g"), col("gk"),
                  pl.BlockSpec(gkw.shape, fix), pl.BlockSpec(gkb.shape, fix), pl.BlockSpec(nw.shape, fix),
                  pl.BlockSpec((tt, tt), fix)],
        out_specs=pl.BlockSpec((None, tt, GLA_VW), lambda b, i: (b, i, 0)),
        out_shape=jax.ShapeDtypeStruct((bsz, t, GLA_VW), BF16),
        scratch_shapes=[pltpu.VMEM((GLA_HEADS, GLA_DV, GLA_DK), F32)] + [pltpu.VMEM((tt, GLA_QK), F32)] * 3,
        compiler_params=_params("parallel", "arbitrary"),
        name="gla",
    )(p, p, p, p, p, gkw, gkb, nw, _chunk_sel(tt, CHUNK))


def _rglru_kernel(x_ref, gate_ref, cw_ref, cb_ref, wa_ref, ba_ref, wx_ref, bx_ref, lam_ref, o_ref,
                  tail_ref, h_ref, a_scr, u_scr):
    @pl.when(pl.program_id(1) == 0)
    def _():
        tail_ref[...] = jnp.zeros_like(tail_ref)
        h_ref[...] = jnp.zeros_like(h_ref)

    tt = x_ref.shape[0]
    x = x_ref[...]
    cw = cw_ref[...]
    xc = x * cw[CONV_K - 1:CONV_K, :] + cb_ref[...]
    for s in range(1, CONV_K):
        xc += _shift_rows(x, tail_ref, s) * cw[CONV_K - 1 - s:CONV_K - s, :]
    tail_ref[...] = x[tt - SUBLANES:, :]

    r = _sigmoid(_mm(xc, wa_ref[...]) + ba_ref[...])
    i = _sigmoid(_mm(xc, wx_ref[...]) + bx_ref[...])
    log_a = -LRU_C * r * _softplus(-lam_ref[...])
    a_scr[...] = jnp.exp(log_a)
    u_scr[...] = jnp.sqrt(1.0 - jnp.exp(2.0 * log_a)) * (i * xc)

    row = lax.broadcasted_iota(jnp.int32, (SUBLANES, x.shape[1]), 0)

    def step(i, h):
        rows = pl.ds(pl.multiple_of(i * SUBLANES, SUBLANES), SUBLANES)
        a = a_scr[rows, :]
        u = u_scr[rows, :]
        for s in (1, 2, 4):
            u = u + a * jnp.where(row >= s, pltpu.roll(u, s, axis=0), 0.0)
            a = a * jnp.where(row >= s, pltpu.roll(a, s, axis=0), 1.0)
        hb = u + a * h
        u_scr[rows, :] = hb
        return hb[SUBLANES - 1:, :]

    h_ref[...] = lax.fori_loop(0, tt // SUBLANES, step, h_ref[...], unroll=4)
    gate = gate_ref[...]
    gelu = 0.5 * gate * (1.0 + jnp.tanh(math.sqrt(2.0 / math.pi) * (gate + 0.044715 * gate * gate * gate)))
    o_ref[...] = (u_scr[...] * gelu).astype(o_ref.dtype)


def _rglru(p, cw, cb, wa, ba, wx, bx, lam, tt):
    bsz, t, _ = p.shape
    w = LRU_WIDTH

    def col(name):
        off, _ = EVEN_COLS[name]
        return pl.BlockSpec((None, tt, w), lambda b, i, j=off // w: (b, i, j))

    fix = lambda b, i: (0, 0)
    vec = pl.BlockSpec((1, w), fix)
    return pl.pallas_call(
        _rglru_kernel,
        grid=(bsz, t // tt),
        in_specs=[col("xb"), col("gate"), pl.BlockSpec((CONV_K, w), fix), vec,
                  pl.BlockSpec((w, w), fix), vec, pl.BlockSpec((w, w), fix), vec, vec],
        out_specs=pl.BlockSpec((None, tt, w), lambda b, i: (b, i, 0)),
        out_shape=jax.ShapeDtypeStruct((bsz, t, w), BF16),
        scratch_shapes=[pltpu.VMEM((SUBLANES, w), F32), pltpu.VMEM((1, w), F32),
                        pltpu.VMEM((tt, w), F32), pltpu.VMEM((tt, w), F32)],
        compiler_params=_params("parallel", "arbitrary"),
        name="rglru",
    )(p, p, cw, cb, wa, ba, wx, bx, lam)


def _ssd_kernel(z_ref, x_ref, bm_ref, cm_ref, dt_ref, cwx_ref, cbx_ref, cwb_ref, cbb_ref, cwc_ref, cbc_ref,
                dtb_ref, alog_ref, dskip_ref, nw_ref, expand_ref, sel_ref, o_ref,
                tx_ref, tb_ref, tc_ref, st_ref, xs_scr, bs_scr, cs_scr, acs_scr, xdt_scr, xdec_scr, ea_scr):
    @pl.when(pl.program_id(1) == 0)
    def _():
        tx_ref[...] = jnp.zeros_like(tx_ref)
        tb_ref[...] = jnp.zeros_like(tb_ref)
        tc_ref[...] = jnp.zeros_like(tc_ref)
        st_ref[...] = jnp.zeros_like(st_ref)

    tt = x_ref.shape[0]

    def conv_silu(src_ref, tail_ref, cw_ref, cb_ref, dst_ref):
        x = src_ref[...]
        cw = cw_ref[...]
        y = x * cw[CONV_K - 1:CONV_K, :] + cb_ref[...]
        for s in range(1, CONV_K):
            y += _shift_rows(x, tail_ref, s) * cw[CONV_K - 1 - s:CONV_K - s, :]
        tail_ref[...] = x[tt - SUBLANES:, :]
        dst_ref[...] = _silu(y)

    conv_silu(x_ref, tx_ref, cwx_ref, cbx_ref, xs_scr)
    conv_silu(bm_ref, tb_ref, cwb_ref, cbb_ref, bs_scr)
    conv_silu(cm_ref, tc_ref, cwc_ref, cbc_ref, cs_scr)

    tri = _tri(CHUNK)
    expand = expand_ref[...]
    dskip = dskip_ref[...]
    nw = nw_ref[...]
    hpg = SSD_HEADS // SSD_GROUPS

    dtc = _softplus(dt_ref[...] + dtb_ref[...])
    da = dtc * -jnp.exp(alog_ref[...])
    sums = _mm_01_left(sel_ref[...], da)
    acs_all = sums[:tt]
    tot = sums[tt:]
    acs_scr[...] = acs_all
    xdt_all = xs_scr[...] * _mm_01(dtc, expand)
    xdt_scr[...] = xdt_all
    xdec_scr[...] = xdt_all * _mm_01(jnp.exp(tot - acs_all), expand)
    ea_scr[...] = _mm_01(jnp.exp(acs_all), expand)

    def chunk(c, carry):
        rows = pl.ds(pl.multiple_of(c * CHUNK, CHUNK), CHUNK)
        acs = acs_scr[rows, :]
        acs_t = acs.T
        ea_x = ea_scr[rows, :]
        x = xs_scr[rows, :]
        xdt = xdt_scr[rows, :]
        xdec = xdec_scr[rows, :]
        cd_x = ea_x[CHUNK - 1:CHUNK, :]
        ys = []
        for g in range(SSD_GROUPS):
            sg = slice(g * SSD_GROUP_WIDTH, (g + 1) * SSD_GROUP_WIDTH)
            ss = slice(g * SSD_STATE, (g + 1) * SSD_STATE)
            bg = bs_scr[rows, ss]
            cg = cs_scr[rows, ss]
            cb = _mm(cg, bg, _NT)
            st = st_ref[g]
            yg = _mm(cg, st) * ea_x[:, sg]
            st_ref[g] = st * cd_x[:, sg] + _mm(bg, xdec[:, sg], _TN)
            yh = []
            for j in range(hpg):
                h = g * hpg + j
                seg = acs[:, h:h + 1] - acs_t[h:h + 1, :]
                lmat = jnp.exp(jnp.where(tri, seg, -jnp.inf))
                sh = slice(h * SSD_HEADDIM, (h + 1) * SSD_HEADDIM)
                yh.append(_mm(cb * lmat, xdt[:, sh]))
            ys.append(yg + jnp.concatenate(yh, axis=1))
        y = jnp.concatenate(ys, axis=1) + x * dskip
        y = y * _silu(z_ref[rows, :])
        outs = []
        for g in range(SSD_GROUPS):
            sg = slice(g * SSD_GROUP_WIDTH, (g + 1) * SSD_GROUP_WIDTH)
            yg = y[:, sg]
            ms = jnp.mean(yg * yg, axis=-1, keepdims=True)
            outs.append(yg * lax.rsqrt(ms + SSD_NORM_EPS))
        o_ref[rows, :] = (jnp.concatenate(outs, axis=1) * nw).astype(o_ref.dtype)
        return carry

    lax.fori_loop(0, tt // CHUNK, chunk, 0, unroll=2)


def _ssd(p, cw, cb, dtb, alog, dskip_x, nw, expand, tt):
    bsz, t, _ = p.shape

    def col(name):
        off, w = ODD_COLS[name]
        return pl.BlockSpec((None, tt, w), lambda b, i, j=off // w: (b, i, j))

    fix = lambda b, i: (0, 0)
    full = lambda a: pl.BlockSpec(a.shape, fix)
    gs = SSD_GROUPS * SSD_STATE
    cwx, cwb, cwc = cw[:, :SSD_INNER], cw[:, SSD_INNER:SSD_INNER + gs], cw[:, SSD_INNER + gs:]
    cbx, cbb, cbc = cb[:, :SSD_INNER], cb[:, SSD_INNER:SSD_INNER + gs], cb[:, SSD_INNER + gs:]
    sel = jnp.concatenate([_chunk_sel(tt, CHUNK), _chunk_sel(tt, CHUNK, "all")], axis=0)
    args = (cwx, cbx, cwb, cbb, cwc, cbc, dtb, alog, dskip_x, nw, expand, sel)
    return pl.pallas_call(
        _ssd_kernel,
        grid=(bsz, t // tt),
        in_specs=[col("z"), col("x"), col("bm"), col("cm"), col("dt")] + [full(a) for a in args],
        out_specs=pl.BlockSpec((None, tt, SSD_INNER), lambda b, i: (b, i, 0)),
        out_shape=jax.ShapeDtypeStruct((bsz, t, SSD_INNER), BF16),
        scratch_shapes=[pltpu.VMEM((SUBLANES, SSD_INNER), F32), pltpu.VMEM((SUBLANES, gs), F32),
                        pltpu.VMEM((SUBLANES, gs), F32),
                        pltpu.VMEM((SSD_GROUPS, SSD_STATE, SSD_GROUP_WIDTH), F32),
                        pltpu.VMEM((tt, SSD_INNER), F32), pltpu.VMEM((tt, gs), F32), pltpu.VMEM((tt, gs), F32),
                        pltpu.VMEM((tt, LANES), F32)] + [pltpu.VMEM((tt, SSD_INNER), F32)] * 3,
        compiler_params=_params("parallel", "arbitrary"),
        name="ssd",
    )(p, p, p, p, p, *args)


def _rwkv_kernel(r_ref, k_ref, v_ref, wa_ref, xg_ref, mur_ref, muk_ref, muv_ref, muwa_ref, mug_ref,
                 w0_ref, w2_ref, a0_ref, a2_ref, g2_ref, kk_ref, ka_ref, rk_ref, lng_ref, lnb_ref, ones_ref, sel_ref,
                 o_ref, tr_ref, tk_ref, tv_ref, twa_ref, tg_ref, st_ref, y_scr,
                 at_scr, rt_scr, bt_scr, kt_scr, bh_scr, kh_scr, v_scr, wc_scr):
    @pl.when(pl.program_id(1) == 0)
    def _():
        for ref in (tr_ref, tk_ref, tv_ref, twa_ref, tg_ref, st_ref):
            ref[...] = jnp.zeros_like(ref)

    tt = r_ref.shape[0]
    hd = RWKV_HEADDIM

    def mix(p_ref, tail_ref, mu_ref):
        p = p_ref[...]
        prev = _shift_rows(p, tail_ref, 1)
        tail_ref[...] = p[tt - SUBLANES:, :]
        return p + (prev - p) * mu_ref[...]

    r = mix(r_ref, tr_ref, mur_ref)
    k = mix(k_ref, tk_ref, muk_ref)
    v = mix(v_ref, tv_ref, muv_ref)
    xwa = mix(wa_ref, twa_ref, muwa_ref)
    xg = mix(xg_ref, tg_ref, mug_ref)

    lane = lax.broadcasted_iota(jnp.int32, xwa.shape, 1)
    lora_in = jnp.where(lane < RWKV_DECAY_LORA, jnp.tanh(xwa), xwa)
    w_log = -_softplus(-(w0_ref[...] + _mm_split(lora_in, w2_ref[...]))) - 0.5
    lw = -jnp.exp(w_log)
    a_sig = _sigmoid(a0_ref[...] + _mm_split(lora_in, a2_ref[...]))
    gate = _mm_split(_sigmoid(xg), g2_ref[...])
    ones_bd = ones_ref[...]
    kk = k * kk_ref[...]
    kk = kk / jnp.maximum(jnp.sqrt(_mm_01(kk * kk, ones_bd)), 1e-12)
    k = k * (1.0 + (a_sig - 1.0) * ka_ref[...])

    cs = RWKV_CHUNK
    tri_incl = _tri(cs)
    tri_strict = _tri(cs, strict=True)
    tri_col = jnp.concatenate([tri_strict, tri_incl], axis=0)
    eye = (tri_incl & ~tri_strict).astype(F32)

    sums = _mm_01_left(sel_ref[...], lw)
    cum = sums[:tt]
    tot = sums[tt:]
    bvec = kk * a_sig
    e_neg = jnp.exp(-cum)
    e_last = jnp.exp(tot - cum)
    for ref, val in ((at_scr, -kk * jnp.exp(cum - lw)), (rt_scr, r * jnp.exp(cum)), (bt_scr, bvec * e_neg),
                     (kt_scr, k * e_neg), (bh_scr, bvec * e_last), (kh_scr, k * e_last), (v_scr, v),
                     (wc_scr, jnp.exp(tot))):
        ref[...] = val

    sls = [slice(h * hd, (h + 1) * hd) for h in range(RWKV_HEADS)]
    chunks = range(tt // cs)
    rows = [slice(c * cs, (c + 1) * cs) for c in chunks]
    units = [(c, h) for c in chunks for h in range(RWKV_HEADS)]
    at, rt, bt, kt, bh, kh, vc = ({u: ref[rows[u[0]], sls[u[1]]] for u in units}
                                  for ref in (at_scr, rt_scr, bt_scr, kt_scr, bh_scr, kh_scr, v_scr))
    amat = {u: _mm(jnp.concatenate([at[u], rt[u]], axis=0), jnp.concatenate([bt[u], kt[u]], axis=0), _NT)
            for u in units}
    a_ab = {u: jnp.where(tri_strict, amat[u][:cs, :cs], 0.0) for u in units}
    a_rb = {u: jnp.where(tri_incl, amat[u][cs:, :cs], 0.0) for u in units}
    a_xk = {u: jnp.where(tri_col, amat[u][:, cs:], 0.0) for u in units}
    avrv = {u: _mm(a_xk[u], vc[u]) for u in units}
    inv = {u: eye + a_ab[u] for u in units}
    pw = {u: _mm(a_ab[u], a_ab[u]) for u in units}
    for _ in range(int(math.log2(cs)) - 2):
        both = {u: _mm(jnp.concatenate([inv[u], pw[u]], axis=0), pw[u]) for u in units}
        inv = {u: inv[u] + both[u][:cs] for u in units}
        pw = {u: both[u][cs:] for u in units}
    inv = {u: inv[u] + _mm(inv[u], pw[u]) for u in units}
    tu = {u: _mm(inv[u], jnp.concatenate([at[u], avrv[u][:cs]], axis=1)) for u in units}
    qy = {u: _mm(a_rb[u], tu[u]) + jnp.concatenate([rt[u], avrv[u][cs:]], axis=1) for u in units}
    zb = {u: _mm(tu[u], bh[u], _TN) for u in units}
    vk = {u: _mm(vc[u], kh[u], _TN) for u in units}
    for c, h in units:
        st = st_ref[h]
        y_scr[rows[c], sls[h]] = _mm(qy[c, h][:, :hd], st, _NT) + qy[c, h][:, hd:]
        wc = wc_scr[c * cs:c * cs + 1, sls[h]]
        st_ref[h] = st * wc + _mm(st, zb[c, h][:hd]) + zb[c, h][hd:] + vk[c, h]

    y = y_scr[...]
    inv_n = 1.0 / hd
    mu_y = _mm_01(y, ones_bd) * inv_n
    dy = y - mu_y
    var_y = _mm_01(dy * dy, ones_bd) * inv_n
    yn = dy * lax.rsqrt(var_y + RWKV_GN_EPS) * lng_ref[...] + lnb_ref[...]
    bonus = _mm_01(r * k * rk_ref[...], ones_bd) * v
    o_ref[...] = ((yn + bonus) * gate).astype(o_ref.dtype)


def _rwkv(p, mu, w0, w2p, a0, a2p, g2, k_k, k_a, r_k, ln_g, ln_b, ones_bd, tt):
    bsz, t, _ = p.shape
    w = RWKV_WIDTH

    def col(name):
        off, wd = ODD_COLS[name]
        return pl.BlockSpec((None, tt, wd), lambda b, i, j=off // wd: (b, i, j))

    fix = lambda b, i: (0, 0)
    full = lambda a: pl.BlockSpec(a.shape, fix)
    mur, muk, muv = mu[:, :w], mu[:, w:2 * w], mu[:, 2 * w:3 * w]
    muwa, mug = mu[:, 3 * w:3 * w + LANES], mu[:, 3 * w + LANES:]
    sel = jnp.concatenate([_chunk_sel(tt, RWKV_CHUNK), _chunk_sel(tt, RWKV_CHUNK, "all")], axis=0)
    args = (mur, muk, muv, muwa, mug, w0, w2p, a0, a2p, g2, k_k, k_a, r_k, ln_g, ln_b, ones_bd, sel)
    return pl.pallas_call(
        _rwkv_kernel,
        grid=(bsz, t // tt),
        in_specs=[col("r"), col("k"), col("v"), col("wa"), col("xg")] + [full(a) for a in args],
        out_specs=pl.BlockSpec((None, tt, w), lambda b, i: (b, i, 0)),
        out_shape=jax.ShapeDtypeStruct((bsz, t, w), BF16),
        scratch_shapes=[pltpu.VMEM((SUBLANES, w), F32), pltpu.VMEM((SUBLANES, w), F32),
                        pltpu.VMEM((SUBLANES, w), F32), pltpu.VMEM((SUBLANES, LANES), F32),
                        pltpu.VMEM((SUBLANES, LANES), F32),
                        pltpu.VMEM((RWKV_HEADS, RWKV_HEADDIM, RWKV_HEADDIM), F32)]
        + [pltpu.VMEM((tt, w), F32)] * 9,
        compiler_params=_params("parallel", "arbitrary"),
        name="rwkv7",
    )(p, p, p, p, p, *args)


def _reorder_cols(w, pieces, order, width):
    parts, pos = [], 0
    for name, (off, slot) in order.items():
        assert off == pos, "slots must be listed in order and contiguous"
        start, size = pieces[name]
        parts.append(w[:, start:start + size])
        if size < slot:
            parts.append(jnp.zeros((w.shape[0], slot - size), w.dtype))
        pos += slot
    assert pos == width
    return jnp.concatenate(parts, axis=1)


def _block_diag(w):
    n, i, j = w.shape
    eye = jnp.eye(n, dtype=w.dtype)
    return (eye[:, None, :, None] * w[:, :, None, :]).reshape(n * i, n * j)


def _row(v, width=None):
    v = v.reshape(1, -1).astype(F32)
    if width is not None and v.shape[1] < width:
        v = jnp.pad(v, ((0, 0), (0, width - v.shape[1])))
    return v


def _even_layer(x, x_in, w_in, gk_w2, gk_b, gla_norm, conv_w, conv_b, wa, ba, wx, bx, lam,
                w_out, ln1_g, ln1_b, f_w1, f_w3, f_w2, ln2_g, ln2_b, bsz, t):
    qk, vw, lw = GLA_QK, GLA_VW, LRU_WIDTH
    pieces = dict(q=(0, qk), k=(qk, qk), v=(2 * qk, vw), g=(2 * qk + vw, vw), gk=(2 * qk + 2 * vw, GLA_GK_RANK),
                  xb=(2 * qk + 2 * vw + GLA_GK_RANK, lw), gate=(2 * qk + 2 * vw + GLA_GK_RANK + lw, lw))
    w_in_r = _reorder_cols(w_in, pieces, EVEN_COLS, EVEN_WIDTH).astype(BF16)
    p = _proj(x_in, w_in_r, tm=min(ROW_TILE, x.shape[0])).reshape(bsz, t, EVEN_WIDTH)
    tt = min(GLA_RGLRU_TIME_TILE, t)
    gkw = jnp.pad(gk_w2, ((0, LANES - GLA_GK_RANK), (0, 0)))
    y_a = _gla(p, gkw, _row(gk_b), _row(gla_norm), tt)
    y_b = _rglru(p, conv_w, _row(conv_b), _block_diag(wa).astype(BF16), _row(ba), _block_diag(wx).astype(BF16),
                 _row(bx), _row(lam), tt)
    n = bsz * t
    tm = min(ROW_TILE, n)
    return _mix_ffn(y_a.reshape(n, vw), y_b.reshape(n, lw), w_out[:vw].astype(BF16), w_out[vw:].astype(BF16), x,
                    _row(ln1_g), _row(ln1_b), f_w1.astype(BF16), f_w3.astype(BF16), f_w2.astype(BF16),
                    _row(ln2_g), _row(ln2_b), tm)


def _odd_layer(x, xb, w_in, conv_w, conv_b, dt_bias, a_log, d_skip, ssd_norm, mu, w0, w2, a0, a2, g2, k_k, k_a,
               r_k, rln_g, rln_b, w_out, ln1_g, ln1_b, router, ew1, ew3, ew2, ln2_g, ln2_b, bsz, t):
    si, gs, rw = SSD_INNER, SSD_GROUPS * SSD_STATE, RWKV_WIDTH
    o = 2 * si + 2 * gs + SSD_HEADS
    pieces = dict(z=(0, si), x=(si, si), bm=(2 * si, gs), cm=(2 * si + gs, gs), dt=(2 * si + 2 * gs, SSD_HEADS),
                  r=(o, rw), k=(o + rw, rw), v=(o + 2 * rw, rw),
                  wa=(o + 3 * rw, RWKV_DECAY_LORA + RWKV_AAA_LORA), xg=(o + 3 * rw + LANES, RWKV_GATE_LORA))
    w_in_r = _reorder_cols(w_in, pieces, ODD_COLS, ODD_WIDTH).astype(BF16)
    n = bsz * t
    p = _proj(xb, w_in_r, tm=min(ROW_TILE, n)).reshape(bsz, t, ODD_WIDTH)
    expand = jnp.pad(jnp.repeat(jnp.eye(SSD_HEADS, dtype=BF16), SSD_HEADDIM, axis=1), ((0, LANES - SSD_HEADS), (0, 0)))
    y_c = _ssd(p, conv_w, _row(conv_b), _row(dt_bias, LANES), _row(a_log, LANES),
               _row(jnp.repeat(d_skip, SSD_HEADDIM)), _row(ssd_norm), expand, tt=min(SSD_RWKV_TIME_TILE, t))
    zeros = jnp.zeros((RWKV_DECAY_LORA, rw), F32)
    ones_bd = _block_diag(jnp.ones((RWKV_HEADS, RWKV_HEADDIM, RWKV_HEADDIM), BF16))
    y_d = _rwkv(p, _row(mu), _row(w0), jnp.concatenate([w2, zeros]), _row(a0), jnp.concatenate([zeros, a2]), g2,
                _row(k_k), _row(k_a), _row(r_k), _row(rln_g), _row(rln_b), ones_bd, tt=min(SSD_RWKV_TIME_TILE, t))
    x1, x1b, *routing = _mix_router(y_c.reshape(n, si), y_d.reshape(n, rw), w_out[:si].astype(BF16),
                                    w_out[si:].astype(BF16), x, _row(ln1_g), _row(ln1_b), router)
    y = _moe_ln(x1, x1b, routing, ew1.astype(BF16), ew3.astype(BF16), ew2, _row(ln2_g), _row(ln2_b))
    return y, None


def kernel(x, e_w_in, e_gk_w2, e_gk_b, e_gla_norm, e_conv_w, e_conv_b, e_lru_wa, e_lru_ba, e_lru_wx, e_lru_bx, e_lru_lambda, e_w_out, e_ln1_g, e_ln1_b, e_ffn_w1, e_ffn_w3, e_ffn_w2, e_ln2_g, e_ln2_b, o_w_in, o_conv_w, o_conv_b, o_dt_bias, o_a_log, o_d_skip, o_ssd_norm, o_rwkv_mu, o_rwkv_w0, o_rwkv_w2, o_rwkv_a0, o_rwkv_a2, o_rwkv_g2, o_rwkv_k_k, o_rwkv_k_a, o_rwkv_r_k, o_rwkv_ln_g, o_rwkv_ln_b, o_w_out, o_ln1_g, o_ln1_b, o_router, o_exp_w1, o_exp_w3, o_exp_w2, o_ln2_g, o_ln2_b):
    bsz, t, d = x.shape
    h = x.reshape(bsz * t, d)
    hb = h
    for i in range(DEPTH):
        j = i // 2
        if i % 2 == 0:
            h, hb = _even_layer(h, hb, e_w_in[j], e_gk_w2[j], e_gk_b[j], e_gla_norm[j], e_conv_w[j], e_conv_b[j],
                                e_lru_wa[j], e_lru_ba[j], e_lru_wx[j], e_lru_bx[j], e_lru_lambda[j], e_w_out[j],
                                e_ln1_g[j], e_ln1_b[j], e_ffn_w1[j], e_ffn_w3[j], e_ffn_w2[j], e_ln2_g[j],
                                e_ln2_b[j], bsz, t)
        else:
            h, hb = _odd_layer(h, hb, o_w_in[j], o_conv_w[j], o_conv_b[j], o_dt_bias[j], o_a_log[j], o_d_skip[j],
                               o_ssd_norm[j], o_rwkv_mu[j], o_rwkv_w0[j], o_rwkv_w2[j], o_rwkv_a0[j],
                               o_rwkv_a2[j], o_rwkv_g2[j], o_rwkv_k_k[j], o_rwkv_k_a[j], o_rwkv_r_k[j],
                               o_rwkv_ln_g[j], o_rwkv_ln_b[j], o_w_out[j], o_ln1_g[j], o_ln1_b[j], o_router[j],
                               o_exp_w1[j], o_exp_w3[j], o_exp_w2[j], o_ln2_g[j], o_ln2_b[j], bsz, t)
    return h.reshape(bsz, t, d)
```
